```python
import jax
import jax.numpy as jnp
from jax import lax
import numpy as np

D_MODEL = 1024
BATCH = 8
SEQ = 2048
DEPTH = 2

DN_ALPHA = (2.0 * DEPTH) ** 0.25
DN_BETA = (8.0 * DEPTH) ** -0.25
LN_EPS = 1e-5

RWKV_WIDTH = D_MODEL // 2
RWKV_HEAD = 64
RWKV_HEADS = RWKV_WIDTH // RWKV_HEAD
RWKV_W_RANK = 64
RWKV_A_RANK = 64
RWKV_G_RANK = 128
RWKV_GN_EPS = 1e-5 * RWKV_HEAD
RWKV_COLS = 3 * RWKV_WIDTH + RWKV_W_RANK + RWKV_A_RANK + RWKV_G_RANK

MLSTM_WIDTH = D_MODEL // 2
MLSTM_HEAD = 128
MLSTM_HEADS = MLSTM_WIDTH // MLSTM_HEAD
MLSTM_CONV = 4
MLSTM_CHUNK = 64
MLSTM_COLS = 4 * MLSTM_WIDTH + 2 * MLSTM_HEADS

FOX_HEAD = 64
FOX_HEADS = D_MODEL // FOX_HEAD
FOX_WIDTH = FOX_HEADS * FOX_HEAD
FOX_QBLOCK = 128
FOX_COLS = 4 * FOX_WIDTH + FOX_HEADS

PEER_HEADS = 8
PEER_NKEYS = 128
PEER_EXPERTS = PEER_NKEYS * PEER_NKEYS
PEER_TOPK = 16
PEER_DKEY = 256
PEER_DHALF = PEER_DKEY // 2
PEER_TOKBLOCK = 128

kernel_name = 'hybrid_rwkv7_mlstm_fox_peer'


def split_cols(p, sizes):
    out, start = [], 0
    for s in sizes:
        out.append(p[..., start:start + s])
        start += s
    return out


def layer_norm(x, g, b):
    xf = x.astype(jnp.float32)
    mu = jnp.mean(xf, -1, keepdims=True)
    var = jnp.mean(jnp.square(xf - mu), -1, keepdims=True)
    return ((xf - mu) * lax.rsqrt(var + LN_EPS) * g + b).astype(x.dtype)


def head_norm(y, g, b, eps):
    yf = y.astype(jnp.float32)
    mu = jnp.mean(yf, -1, keepdims=True)
    var = jnp.mean(jnp.square(yf - mu), -1, keepdims=True)
    return ((yf - mu) * lax.rsqrt(var + eps) * g + b).astype(y.dtype)


def rms_norm(y, g):
    yf = y.astype(jnp.float32)
    return (yf * lax.rsqrt(jnp.mean(yf * yf, -1, keepdims=True) + 1e-6) * g).astype(y.dtype)


def token_shift(z):
    return jnp.pad(z, ((0, 0), (1, 0), (0, 0)))[:, :-1]


def causal_conv(z, w, b):
    c = z.shape[-1]
    out = lax.conv_general_dilated(z, w[:, None, :].astype(z.dtype), window_strides=(1,),
                                   padding=((w.shape[0] - 1, 0),),
                                   dimension_numbers=('NWC', 'WIO', 'NWC'),
                                   feature_group_count=c)
    return out + b


def rwkv7_mix(p, mu, w0, w_up, a0, a_up, g_up, k_k, k_a, r_k, ln_g, ln_b):
    B, S, _ = p.shape
    H, N = RWKV_HEADS, RWKV_HEAD
    f32 = jnp.float32
    p = p + (token_shift(p) - p) * mu
    r, k, v, xw, xa, xg = split_cols(p, (RWKV_WIDTH, RWKV_WIDTH, RWKV_WIDTH,
                                         RWKV_W_RANK, RWKV_A_RANK, RWKV_G_RANK))
    log_w = -jnp.exp(-jax.nn.softplus(-(w0 + jnp.tanh(xw) @ w_up).astype(f32)) - 0.5)
    a = jax.nn.sigmoid((a0 + xa @ a_up).astype(f32))
    g = jax.nn.sigmoid(xg) @ g_up
    heads = lambda t: t.reshape(B, S, H, N)
    kk = heads(k * k_k).astype(f32)
    kk = kk / jnp.maximum(jnp.sqrt(jnp.sum(kk * kk, -1, keepdims=True)), 1e-12)
    k = k.astype(f32) * (1.0 + (a - 1.0) * k_a)
    r_h = heads(r).astype(f32)
    k_h = heads(k)
    v_h = heads(v).astype(f32)
    a_h = heads(a)
    w_h = jnp.exp(heads(log_w))

    def step(state, inp):
        r_t, w_t, k_t, v_t, kk_t, a_t = inp
        s_kk = jnp.einsum('bhvk,bhk->bhv', state, kk_t)
        state = (state * w_t[:, :, None, :]
                 - s_kk[..., None] * (kk_t * a_t)[:, :, None, :]
                 + v_t[..., None] * k_t[:, :, None, :])
        return state, jnp.einsum('bhvk,bhk->bhv', state, r_t)

    seq_first = lambda t: jnp.moveaxis(t, 1, 0)
    state0 = jnp.zeros((B, H, N, N), f32)
    _, y = lax.scan(step, state0, (seq_first(r_h), seq_first(w_h), seq_first(k_h),
                                   seq_first(v_h), seq_first(kk), seq_first(a_h)))
    y = jnp.moveaxis(y, 0, 1)
    y = head_norm(y, ln_g, ln_b, RWKV_GN_EPS)
    y = y + jnp.sum(r_h * k_h * r_k, -1, keepdims=True) * v_h
    return (y.reshape(B, S, RWKV_WIDTH) * g).astype(p.dtype)


def mlstm_chunkwise(q, k, v, ig, lf):
    B, S, H, D = q.shape
    L = MLSTM_CHUNK
    NC = S // L

    def to_chunks(t):
        t = t.reshape((B, NC, L, H) + t.shape[3:])
        return jnp.moveaxis(t, (1, 3), (0, 2))

    causal = jnp.tril(jnp.ones((L, L), dtype=bool))

    def body(carry, inp):
        C, n, m = carry
        qc, kc, vc, igc, lfc = inp
        b = jnp.cumsum(lfc, axis=-1)
        dmat = jnp.where(causal, b[..., :, None] - b[..., None, :] + igc[..., None, :], -jnp.inf)
        inter = b + m[..., None]
        m_t = jnp.maximum(inter, jnp.max(dmat, -1))
        weights = jnp.exp(dmat - m_t[..., None])
        sc = jnp.einsum('bhtd,bhsd->bhts', qc, kc) * weights
        carry_in = jnp.exp(inter - m_t)
        num = (jnp.einsum('bhts,bhsd->bhtd', sc, vc)
               + carry_in[..., None] * jnp.einsum('bhtk,bhkv->bhtv', qc, C))
        den = jnp.sum(sc, -1) + carry_in * jnp.einsum('bhtk,bhk->bht', qc, n)
        h = num / jnp.maximum(jnp.abs(den), jnp.exp(-m_t))[..., None]
        b_last = b[..., -1]
        gs = b_last[..., None] - b + igc
        m_new = jnp.maximum(b_last + m, jnp.max(gs, -1))
        ws = jnp.exp(gs - m_new[..., None])
        keep = jnp.exp(b_last + m - m_new)
        C = keep[..., None, None] * C + jnp.einsum('bhs,bhsk,bhsv->bhkv', ws, kc, vc)
        n = keep[..., None] * n + jnp.einsum('bhs,bhsk->bhk', ws, kc)
        return (C, n, m_new), h

    f32 = jnp.float32
    init = (jnp.zeros((B, H, D, D), f32), jnp.zeros((B, H, D), f32), jnp.zeros((B, H), f32))
    _, h = lax.scan(body, init, (to_chunks(q), to_chunks(k), to_chunks(v),
                                 to_chunks(ig), to_chunks(lf)))
    return jnp.moveaxis(h, (0, 2), (1, 3)).reshape(B, S, H, D)


def mlstm_mix(p, conv_w, conv_b, ig_b, fg_b, hn_g, hn_b):
    B, S, _ = p.shape
    H, N = MLSTM_HEADS, MLSTM_HEAD
    f32 = jnp.float32
    q, k, v, o, ig, fg = split_cols(p, (MLSTM_WIDTH, MLSTM_WIDTH, MLSTM_WIDTH, MLSTM_WIDTH, H, H))
    qk = jax.nn.silu(causal_conv(jnp.concatenate([q, k], -1), conv_w, conv_b))
    q, k = qk[..., :MLSTM_WIDTH], qk[..., MLSTM_WIDTH:]
    heads = lambda t: t.reshape(B, S, H, N).astype(f32)
    q = heads(q) * (N ** -0.5)
    ig = (ig + ig_b).astype(f32)
    lf = jax.nn.log_sigmoid((fg + fg_b).astype(f32))
    h = mlstm_chunkwise(q, heads(k), heads(v), ig, lf)
    h = head_norm(h, hn_g, hn_b, LN_EPS).reshape(B, S, MLSTM_WIDTH).astype(p.dtype)
    return jax.nn.sigmoid(o) * h


def even_mixer(x, w_in, mu, w0, w_up, a0, a_up, g_up, k_k, k_a, r_k, rln_g, rln_b,
               conv_w, conv_b, ig_b, fg_b, hn_g, hn_b, w_out):
    p = x @ w_in
    y_a = rwkv7_mix(p[..., :RWKV_COLS], mu, w0, w_up, a0, a_up, g_up, k_k, k_a, r_k, rln_g, rln_b)
    y_b = mlstm_mix(p[..., RWKV_COLS:], conv_w, conv_b, ig_b, fg_b, hn_g, hn_b)
    return jnp.concatenate([y_a, y_b], -1) @ w_out


def fox_attention(q, k, v, logf):
    B, S, H, D = q.shape
    NB = S // FOX_QBLOCK
    c = jnp.cumsum(logf, axis=1)
    c_key = jnp.transpose(c, (0, 2, 1))[:, :, None, :]
    q_blocks = jnp.moveaxis(q.reshape(B, NB, FOX_QBLOCK, H, D), 1, 0)
    c_blocks = jnp.moveaxis(c.reshape(B, NB, FOX_QBLOCK, H), 1, 0)
    key_pos = jnp.arange(S)
    scale = D ** -0.5

    def one_block(args):
        qb, cb, blk = args
        q_pos = blk * FOX_QBLOCK + jnp.arange(FOX_QBLOCK)
        logits = jnp.einsum('bqhd,bkhd->bhqk', qb, k).astype(jnp.float32) * scale
        logits = logits + jnp.transpose(cb, (0, 2, 1))[..., None] - c_key
        logits = jnp.where(key_pos[None, :] <= q_pos[:, None], logits, -jnp.inf)
        probs = jax.nn.softmax(logits, axis=-1).astype(v.dtype)
        return jnp.einsum('bhqk,bkhd->bqhd', probs, v)

    out = lax.map(one_block, (q_blocks, c_blocks, jnp.arange(NB)))
    return jnp.moveaxis(out, 0, 1).reshape(B, S, H, D)


def odd_mixer(x, w_in, qn_g, kn_g, f_b, w_out):
    B, S, _ = x.shape
    p = x @ w_in
    q, k, v, og, fl = split_cols(p, (FOX_WIDTH, FOX_WIDTH, FOX_WIDTH, FOX_WIDTH, FOX_HEADS))
    heads = lambda t: t.reshape(B, S, FOX_HEADS, FOX_HEAD)
    q = rms_norm(heads(q), qn_g)
    k = rms_norm(heads(k), kn_g)
    logf = jax.nn.log_sigmoid((fl + f_b).astype(jnp.float32))
    o = fox_attention(q, k, heads(v), logf).reshape(B, S, FOX_WIDTH)
    return (jax.nn.sigmoid(og) * o) @ w_out


def peer_ffn(x, w_q, sub_k1, sub_k2, u_tab, v_tab):
    B, S, Dm = x.shape
    K = PEER_TOPK
    xt = x.reshape((B * S) // PEER_TOKBLOCK, PEER_TOKBLOCK, Dm)

    def block(xb):
        T = xb.shape[0]
        q = (xb @ w_q).reshape(T, PEER_HEADS, 2, PEER_DHALF)
        s1 = jnp.einsum('thd,hnd->thn', q[:, :, 0], sub_k1)
        s2 = jnp.einsum('thd,hnd->thn', q[:, :, 1], sub_k2)
        v1, i1 = lax.top_k(s1, K)
        v2, i2 = lax.top_k(s2, K)
        cand = (v1[..., :, None] + v2[..., None, :]).reshape(T, PEER_HEADS, K * K)
        sc, ci = lax.top_k(cand, K)
        e1 = jnp.take_along_axis(i1, ci // K, axis=-1)
        e2 = jnp.take_along_axis(i2, ci % K, axis=-1)
        eid = e1 * PEER_NKEYS + e2
        gate = jax.nn.softmax(sc.astype(jnp.float32), axis=-1).astype(xb.dtype)
        act = jax.nn.gelu(jnp.einsum('td,thkd->thk', xb, u_tab[eid]), approximate=False)
        return jnp.einsum('thk,thkd->td', gate * act, v_tab[eid])

    return lax.map(block, xt).reshape(B, S, Dm)


def setup_inputs(seed: int = 0) -> dict:
    key = jax.random.key(seed)
    keys = iter(jax.random.split(key, 128))
    nk = lambda: next(keys)
    D = D_MODEL
    s_in = D ** -0.5
    nrm = lambda shape, scale: jax.random.normal(nk(), shape, jnp.float32) * scale
    uni = lambda shape, lo, hi: jax.random.uniform(nk(), shape, jnp.float32, minval=lo, maxval=hi)
    gain = lambda shape: 1.0 + 0.1 * jax.random.normal(nk(), shape, jnp.float32)
    bias = lambda shape: 0.01 * jax.random.normal(nk(), shape, jnp.float32)

    def peer_params():
        return (nrm((D, PEER_HEADS * PEER_DKEY), s_in),
                nrm((PEER_HEADS, PEER_NKEYS, PEER_DHALF), PEER_DHALF ** -0.5),
                nrm((PEER_HEADS, PEER_NKEYS, PEER_DHALF), PEER_DHALF ** -0.5),
                nrm((PEER_EXPERTS, D), s_in),
                nrm((PEER_EXPERTS, D), DN_BETA))

    x = jax.random.normal(nk(), (BATCH, SEQ, D), jnp.float32)
    l0_w_in = jnp.concatenate([
        nrm((D, RWKV_WIDTH), s_in), nrm((D, RWKV_WIDTH), s_in), nrm((D, RWKV_WIDTH), s_in * DN_BETA),
        nrm((D, RWKV_W_RANK), s_in), nrm((D, RWKV_A_RANK), s_in), nrm((D, RWKV_G_RANK), s_in),
        nrm((D, MLSTM_WIDTH), s_in), nrm((D, MLSTM_WIDTH), s_in), nrm((D, MLSTM_WIDTH), s_in * DN_BETA),
        nrm((D, MLSTM_WIDTH), s_in), nrm((D, MLSTM_HEADS), s_in), nrm((D, MLSTM_HEADS), s_in)], axis=1)
    l0_rwkv_mu = uni((RWKV_COLS,), 0.0, 1.0)
    l0_rwkv_w0 = uni((RWKV_WIDTH,), -6.0, 1.0)
    l0_rwkv_w_up = nrm((RWKV_W_RANK, RWKV_WIDTH), 0.1)
    l0_rwkv_a0 = nrm((RWKV_WIDTH,), 0.1)
    l0_rwkv_a_up = nrm((RWKV_A_RANK, RWKV_WIDTH), 0.1)
    l0_rwkv_g_up = nrm((RWKV_G_RANK, RWKV_WIDTH), RWKV_G_RANK ** -0.5)
    l0_rwkv_k_k = 0.85 + 0.05 * jax.random.normal(nk(), (RWKV_WIDTH,), jnp.float32)
    l0_rwkv_k_a = gain((RWKV_WIDTH,))
    l0_rwkv_r_k = nrm((RWKV_HEADS, RWKV_HEAD), 0.1)
    l0_rwkv_ln_g = gain((RWKV_HEADS, RWKV_HEAD))
    l0_rwkv_ln_b = bias((RWKV_HEADS, RWKV_HEAD))
    l0_mlstm_conv_w = nrm((MLSTM_CONV, 2 * MLSTM_WIDTH), MLSTM_CONV ** -0.5)
    l0_mlstm_conv_b = bias((2 * MLSTM_WIDTH,))
    l0_mlstm_ig_b = nrm((MLSTM_HEADS,), 0.1)
    l0_mlstm_fg_b = uni((MLSTM_HEADS,), 3.0, 6.0)
    l0_mlstm_hn_g = gain((MLSTM_HEADS, MLSTM_HEAD))
    l0_mlstm_hn_b = bias((MLSTM_HEADS, MLSTM_HEAD))
    l0_w_out = nrm((RWKV_WIDTH + MLSTM_WIDTH, D), (RWKV_WIDTH + MLSTM_WIDTH) ** -0.5 * DN_BETA)
    l0_ln1_g, l0_ln1_b = gain((D,)), bias((D,))
    l0_peer_wq, l0_peer_k1, l0_peer_k2, l0_peer_u, l0_peer_v = peer_params()
    l0_ln2_g, l0_ln2_b = gain((D,)), bias((D,))

    l1_w_in = jnp.concatenate([
        nrm((D, FOX_WIDTH), s_in), nrm((D, FOX_WIDTH), s_in), nrm((D, FOX_WIDTH), s_in * DN_BETA),
        nrm((D, FOX_WIDTH), s_in), nrm((D, FOX_HEADS), s_in)], axis=1)
    l1_fox_qn_g = gain((FOX_HEADS, FOX_HEAD))
    l1_fox_kn_g = gain((FOX_HEADS, FOX_HEAD))
    l1_fox_f_b = uni((FOX_HEADS,), 1.0, 6.0)
    l1_w_out = nrm((FOX_WIDTH, D), FOX_WIDTH ** -0.5 * DN_BETA)
    l1_ln1_g, l1_ln1_b = gain((D,)), bias((D,))
    l1_peer_wq, l1_peer_k1, l1_peer_k2, l1_peer_u, l1_peer_v = peer_params()
    l1_ln2_g, l1_ln2_b = gain((D,)), bias((D,))

    return {
        'x': x,
        'l0_w_in': l0_w_in, 'l0_rwkv_mu': l0_rwkv_mu, 'l0_rwkv_w0': l0_rwkv_w0,
        'l0_rwkv_w_up': l0_rwkv_w_up, 'l0_rwkv_a0': l0_rwkv_a0, 'l0_rwkv_a_up': l0_rwkv_a_up,
        'l0_rwkv_g_up': l0_rwkv_g_up, 'l0_rwkv_k_k': l0_rwkv_k_k, 'l0_rwkv_k_a': l0_rwkv_k_a,
        'l0_rwkv_r_k': l0_rwkv_r_k, 'l0_rwkv_ln_g': l0_rwkv_ln_g, 'l0_rwkv_ln_b': l0_rwkv_ln_b,
        'l0_mlstm_conv_w': l0_mlstm_conv_w, 'l0_mlstm_conv_b': l0_mlstm_conv_b,
        'l0_mlstm_ig_b': l0_mlstm_ig_b, 'l0_mlstm_fg_b': l0_mlstm_fg_b,
        'l0_mlstm_hn_g': l0_mlstm_hn_g, 'l0_mlstm_hn_b': l0_mlstm_hn_b,
        'l0_w_out': l0_w_out, 'l0_ln1_g': l0_ln1_g, 'l0_ln1_b': l0_ln1_b,
        'l0_peer_wq': l0_peer_wq, 'l0_peer_k1': l0_peer_k1, 'l0_peer_k2': l0_peer_k2,
        'l0_peer_u': l0_peer_u, 'l0_peer_v': l0_peer_v, 'l0_ln2_g': l0_ln2_g, 'l0_ln2_b': l0_ln2_b,
        'l1_w_in': l1_w_in, 'l1_fox_qn_g': l1_fox_qn_g, 'l1_fox_kn_g': l1_fox_kn_g,
        'l1_fox_f_b': l1_fox_f_b, 'l1_w_out': l1_w_out, 'l1_ln1_g': l1_ln1_g, 'l1_ln1_b': l1_ln1_b,
        'l1_peer_wq': l1_peer_wq, 'l1_peer_k1': l1_peer_k1, 'l1_peer_k2': l1_peer_k2,
        'l1_peer_u': l1_peer_u, 'l1_peer_v': l1_peer_v, 'l1_ln2_g': l1_ln2_g, 'l1_ln2_b': l1_ln2_b,
    }


def reference(x, l0_w_in, l0_rwkv_mu, l0_rwkv_w0, l0_rwkv_w_up, l0_rwkv_a0, l0_rwkv_a_up,
              l0_rwkv_g_up, l0_rwkv_k_k, l0_rwkv_k_a, l0_rwkv_r_k, l0_rwkv_ln_g, l0_rwkv_ln_b,
              l0_mlstm_conv_w, l0_mlstm_conv_b, l0_mlstm_ig_b, l0_mlstm_fg_b,
              l0_mlstm_hn_g, l0_mlstm_hn_b, l0_w_out, l0_ln1_g, l0_ln1_b,
              l0_peer_wq, l0_peer_k1, l0_peer_k2, l0_peer_u, l0_peer_v, l0_ln2_g, l0_ln2_b,
              l1_w_in, l1_fox_qn_g, l1_fox_kn_g, l1_fox_f_b, l1_w_out, l1_ln1_g, l1_ln1_b,
              l1_peer_wq, l1_peer_k1, l1_peer_k2, l1_peer_u, l1_peer_v, l1_ln2_g, l1_ln2_b):
    mixer_params = (
        (l0_w_in, l0_rwkv_mu, l0_rwkv_w0, l0_rwkv_w_up, l0_rwkv_a0, l0_rwkv_a_up, l0_rwkv_g_up,
         l0_rwkv_k_k, l0_rwkv_k_a, l0_rwkv_r_k, l0_rwkv_ln_g, l0_rwkv_ln_b,
         l0_mlstm_conv_w, l0_mlstm_conv_b, l0_mlstm_ig_b, l0_mlstm_fg_b,
         l0_mlstm_hn_g, l0_mlstm_hn_b, l0_w_out),
        (l1_w_in, l1_fox_qn_g, l1_fox_kn_g, l1_fox_f_b, l1_w_out),
    )
    norm_params = ((l0_ln1_g, l0_ln1_b, l0_ln2_g, l0_ln2_b),
                   (l1_ln1_g, l1_ln1_b, l1_ln2_g, l1_ln2_b))
    peer_params = ((l0_peer_wq, l0_peer_k1, l0_peer_k2, l0_peer_u, l0_peer_v),
                   (l1_peer_wq, l1_peer_k1, l1_peer_k2, l1_peer_u, l1_peer_v))
    for layer in range(DEPTH):
        mixer = even_mixer if layer % 2 == 0 else odd_mixer
        ln1_g, ln1_b, ln2_g, ln2_b = norm_params[layer]
        x = layer_norm(DN_ALPHA * x + mixer(x, *mixer_params[layer]), ln1_g, ln1_b)
        x = layer_norm(DN_ALPHA * x + peer_ffn(x, *peer_params[layer]), ln2_g, ln2_b)
    return x
```

```python
import functools

import jax
import jax.numpy as jnp
from jax import lax
from jax.experimental import pallas as pl
from jax.experimental.pallas import tpu as pltpu

D_MODEL = 1024
DEPTH = 2
DN_ALPHA = (2.0 * DEPTH) ** 0.25
LN_EPS = 1e-5

RWKV_WIDTH = D_MODEL // 2
RWKV_HEAD = 64
RWKV_HEADS = RWKV_WIDTH // RWKV_HEAD
RWKV_W_RANK = 64
RWKV_A_RANK = 64
RWKV_G_RANK = 128
RWKV_GN_EPS = 1e-5 * RWKV_HEAD
RWKV_COLS = 3 * RWKV_WIDTH + RWKV_W_RANK + RWKV_A_RANK + RWKV_G_RANK

MLSTM_WIDTH = D_MODEL // 2
MLSTM_HEAD = 128
MLSTM_HEADS = MLSTM_WIDTH // MLSTM_HEAD
MLSTM_CHUNK = 64

FOX_HEAD = 64
FOX_HEADS = D_MODEL // FOX_HEAD
FOX_WIDTH = FOX_HEADS * FOX_HEAD
FOX_QBLOCK = 128

PEER_HEADS = 8
PEER_NKEYS = 128
PEER_TOPK = 16
PEER_DKEY = 256
PEER_DHALF = PEER_DKEY // 2
PEER_TOKBLOCK = 128


def _split_cols(p, sizes):
    out, start = [], 0
    for s in sizes:
        out.append(p[..., start:start + s])
        start += s
    return out


def _resid_ln_kernel(x_ref, y_ref, g_ref, b_ref, o_ref):
    z = DN_ALPHA * x_ref[...] + y_ref[...]
    mu = jnp.mean(z, axis=-1, keepdims=True)
    zc = z - mu
    var = jnp.mean(zc * zc, axis=-1, keepdims=True)
    o_ref[...] = zc * lax.rsqrt(var + LN_EPS) * g_ref[...] + b_ref[...]


def _resid_ln(x2, y2, g, b, tm=512):
    n, d = x2.shape
    row = pl.BlockSpec((tm, d), lambda i: (i, 0))
    vec = pl.BlockSpec((1, d), lambda i: (0, 0))
    return pl.pallas_call(
        _resid_ln_kernel,
        grid=(n // tm,),
        in_specs=[row, row, vec, vec],
        out_specs=row,
        out_shape=jax.ShapeDtypeStruct((n, d), jnp.float32),
        name="resid_ln",
    )(x2, y2, g.reshape(1, d), b.reshape(1, d))


def _head_norm(y, g, b, eps):
    mu = jnp.mean(y, -1, keepdims=True)
    var = jnp.mean(jnp.square(y - mu), -1, keepdims=True)
    return (y - mu) * lax.rsqrt(var + eps) * g + b


def _rms_norm(y, g):
    return y * lax.rsqrt(jnp.mean(y * y, -1, keepdims=True) + 1e-6) * g


def _token_shift(z):
    return jnp.pad(z, ((0, 0), (1, 0), (0, 0)))[:, :-1]


def _causal_conv(z, w, b):
    c = z.shape[-1]
    out = lax.conv_general_dilated(z, w[:, None, :], window_strides=(1,),
                                   padding=((w.shape[0] - 1, 0),),
                                   dimension_numbers=('NWC', 'WIO', 'NWC'),
                                   feature_group_count=c)
    return out + b


def _rwkv7_mix(p, mu, w0, w_up, a0, a_up, g_up, k_k, k_a, r_k, ln_g, ln_b):
    B, S, _ = p.shape
    H, N = RWKV_HEADS, RWKV_HEAD
    p = p + (_token_shift(p) - p) * mu
    r, k, v, xw, xa, xg = _split_cols(p, (RWKV_WIDTH, RWKV_WIDTH, RWKV_WIDTH,
                                          RWKV_W_RANK, RWKV_A_RANK, RWKV_G_RANK))
    log_w = -jnp.exp(-jax.nn.softplus(-(w0 + jnp.tanh(xw) @ w_up)) - 0.5)
    a = jax.nn.sigmoid(a0 + xa @ a_up)
    g = jax.nn.sigmoid(xg) @ g_up
    heads = lambda t: t.reshape(B, S, H, N)
    kk = heads(k * k_k)
    kk = kk / jnp.maximum(jnp.sqrt(jnp.sum(kk * kk, -1, keepdims=True)), 1e-12)
    k = k * (1.0 + (a - 1.0) * k_a)
    r_h, k_h, v_h, a_h = heads(r), heads(k), heads(v), heads(a)
    w_h = jnp.exp(heads(log_w))

    def step(state, inp):
        r_t, w_t, k_t, v_t, kk_t, a_t = inp
        s_kk = jnp.einsum('bhvk,bhk->bhv', state, kk_t)
        state = (state * w_t[:, :, None, :]
                 - s_kk[..., None] * (kk_t * a_t)[:, :, None, :]
                 + v_t[..., None] * k_t[:, :, None, :])
        return state, jnp.einsum('bhvk,bhk->bhv', state, r_t)

    seq_first = lambda t: jnp.moveaxis(t, 1, 0)
    state0 = jnp.zeros((B, H, N, N), jnp.float32)
    _, y = lax.scan(step, state0, (seq_first(r_h), seq_first(w_h), seq_first(k_h),
                                   seq_first(v_h), seq_first(kk), seq_first(a_h)))
    y = jnp.moveaxis(y, 0, 1)
    y = _head_norm(y, ln_g, ln_b, RWKV_GN_EPS)
    y = y + jnp.sum(r_h * k_h * r_k, -1, keepdims=True) * v_h
    return y.reshape(B, S, RWKV_WIDTH) * g


def _mlstm_chunkwise(q, k, v, ig, lf):
    B, S, H, D = q.shape
    L = MLSTM_CHUNK
    NC = S // L

    def to_chunks(t):
        t = t.reshape((B, NC, L, H) + t.shape[3:])
        return jnp.moveaxis(t, (1, 3), (0, 2))

    causal = jnp.tril(jnp.ones((L, L), dtype=bool))

    def body(carry, inp):
        C, n, m = carry
        qc, kc, vc, igc, lfc = inp
        b = jnp.cumsum(lfc, axis=-1)
        dmat = jnp.where(causal, b[..., :, None] - b[..., None, :] + igc[..., None, :], -jnp.inf)
        inter = b + m[..., None]
        m_t = jnp.maximum(inter, jnp.max(dmat, -1))
        weights = jnp.exp(dmat - m_t[..., None])
        sc = jnp.einsum('bhtd,bhsd->bhts', qc, kc) * weights
        carry_in = jnp.exp(inter - m_t)
        num = (jnp.einsum('bhts,bhsd->bhtd', sc, vc)
               + carry_in[..., None] * jnp.einsum('bhtk,bhkv->bhtv', qc, C))
        den = jnp.sum(sc, -1) + carry_in * jnp.einsum('bhtk,bhk->bht', qc, n)
        h = num / jnp.maximum(jnp.abs(den), jnp.exp(-m_t))[..., None]
        b_last = b[..., -1]
        gs = b_last[..., None] - b + igc
        m_new = jnp.maximum(b_last + m, jnp.max(gs, -1))
        ws = jnp.exp(gs - m_new[..., None])
        keep = jnp.exp(b_last + m - m_new)
        C = keep[..., None, None] * C + jnp.einsum('bhs,bhsk,bhsv->bhkv', ws, kc, vc)
        n = keep[..., None] * n + jnp.einsum('bhs,bhsk->bhk', ws, kc)
        return (C, n, m_new), h

    f32 = jnp.float32
    init = (jnp.zeros((B, H, D, D), f32), jnp.zeros((B, H, D), f32), jnp.zeros((B, H), f32))
    _, h = lax.scan(body, init, (to_chunks(q), to_chunks(k), to_chunks(v),
                                 to_chunks(ig), to_chunks(lf)))
    return jnp.moveaxis(h, (0, 2), (1, 3)).reshape(B, S, H, D)


def _mlstm_mix(p, conv_w, conv_b, ig_b, fg_b, hn_g, hn_b):
    B, S, _ = p.shape
    H, N = MLSTM_HEADS, MLSTM_HEAD
    q, k, v, o, ig, fg = _split_cols(p, (MLSTM_WIDTH, MLSTM_WIDTH, MLSTM_WIDTH, MLSTM_WIDTH, H, H))
    qk = jax.nn.silu(_causal_conv(jnp.concatenate([q, k], -1), conv_w, conv_b))
    q, k = qk[..., :MLSTM_WIDTH], qk[..., MLSTM_WIDTH:]
    heads = lambda t: t.reshape(B, S, H, N)
    q = heads(q) * (N ** -0.5)
    ig = ig + ig_b
    lf = jax.nn.log_sigmoid(fg + fg_b)
    h = _mlstm_chunkwise(q, heads(k), heads(v), ig, lf)
    h = _head_norm(h, hn_g, hn_b, LN_EPS).reshape(B, S, MLSTM_WIDTH)
    return jax.nn.sigmoid(o) * h


def _even_mixer(x, w_in, mu, w0, w_up, a0, a_up, g_up, k_k, k_a, r_k, rln_g, rln_b,
                conv_w, conv_b, ig_b, fg_b, hn_g, hn_b, w_out):
    p = x @ w_in
    y_a = _rwkv7_mix(p[..., :RWKV_COLS], mu, w0, w_up, a0, a_up, g_up, k_k, k_a, r_k, rln_g, rln_b)
    y_b = _mlstm_mix(p[..., RWKV_COLS:], conv_w, conv_b, ig_b, fg_b, hn_g, hn_b)
    return jnp.concatenate([y_a, y_b], -1) @ w_out


def _fox_attention(q, k, v, logf):
    B, S, H, D = q.shape
    NB = S // FOX_QBLOCK
    c = jnp.cumsum(logf, axis=1)
    c_key = jnp.transpose(c, (0, 2, 1))[:, :, None, :]
    q_blocks = jnp.moveaxis(q.reshape(B, NB, FOX_QBLOCK, H, D), 1, 0)
    c_blocks = jnp.moveaxis(c.reshape(B, NB, FOX_QBLOCK, H), 1, 0)
    key_pos = jnp.arange(S)
    scale = D ** -0.5

    def one_block(args):
        qb, cb, blk = args
        q_pos = blk * FOX_QBLOCK + jnp.arange(FOX_QBLOCK)
        logits = jnp.einsum('bqhd,bkhd->bhqk', qb, k) * scale
        logits = logits + jnp.transpose(cb, (0, 2, 1))[..., None] - c_key
        logits = jnp.where(key_pos[None, :] <= q_pos[:, None], logits, -jnp.inf)
        probs = jax.nn.softmax(logits, axis=-1)
        return jnp.einsum('bhqk,bkhd->bqhd', probs, v)

    out = lax.map(one_block, (q_blocks, c_blocks, jnp.arange(NB)))
    return jnp.moveaxis(out, 0, 1).reshape(B, S, H, D)


def _odd_mixer(x, w_in, qn_g, kn_g, f_b, w_out):
    B, S, _ = x.shape
    p = x @ w_in
    q, k, v, og, fl = _split_cols(p, (FOX_WIDTH, FOX_WIDTH, FOX_WIDTH, FOX_WIDTH, FOX_HEADS))
    heads = lambda t: t.reshape(B, S, FOX_HEADS, FOX_HEAD)
    q = _rms_norm(heads(q), qn_g)
    k = _rms_norm(heads(k), kn_g)
    logf = jax.nn.log_sigmoid(fl + f_b)
    o = _fox_attention(q, k, heads(v), logf).reshape(B, S, FOX_WIDTH)
    return (jax.nn.sigmoid(og) * o) @ w_out


def _peer_ffn(x, w_q, sub_k1, sub_k2, u_tab, v_tab):
    B, S, Dm = x.shape
    K = PEER_TOPK
    xt = x.reshape((B * S) // PEER_TOKBLOCK, PEER_TOKBLOCK, Dm)

    def block(xb):
        T = xb.shape[0]
        q = (xb @ w_q).reshape(T, PEER_HEADS, 2, PEER_DHALF)
        s1 = jnp.einsum('thd,hnd->thn', q[:, :, 0], sub_k1)
        s2 = jnp.einsum('thd,hnd->thn', q[:, :, 1], sub_k2)
        v1, i1 = lax.top_k(s1, K)
        v2, i2 = lax.top_k(s2, K)
        cand = (v1[..., :, None] + v2[..., None, :]).reshape(T, PEER_HEADS, K * K)
        sc, ci = lax.top_k(cand, K)
        e1 = jnp.take_along_axis(i1, ci // K, axis=-1)
        e2 = jnp.take_along_axis(i2, ci % K, axis=-1)
        eid = e1 * PEER_NKEYS + e2
        gate = jax.nn.softmax(sc, axis=-1)
        act = jax.nn.gelu(jnp.einsum('td,thkd->thk', xb, u_tab[eid]), approximate=False)
        return jnp.einsum('thk,thkd->td', gate * act, v_tab[eid])

    return lax.map(block, xt).reshape(B, S, Dm)


def kernel(x, l0_w_in, l0_rwkv_mu, l0_rwkv_w0, l0_rwkv_w_up, l0_rwkv_a0, l0_rwkv_a_up,
           l0_rwkv_g_up, l0_rwkv_k_k, l0_rwkv_k_a, l0_rwkv_r_k, l0_rwkv_ln_g, l0_rwkv_ln_b,
           l0_mlstm_conv_w, l0_mlstm_conv_b, l0_mlstm_ig_b, l0_mlstm_fg_b,
           l0_mlstm_hn_g, l0_mlstm_hn_b, l0_w_out, l0_ln1_g, l0_ln1_b,
           l0_peer_wq, l0_peer_k1, l0_peer_k2, l0_peer_u, l0_peer_v, l0_ln2_g, l0_ln2_b,
           l1_w_in, l1_fox_qn_g, l1_fox_kn_g, l1_fox_f_b, l1_w_out, l1_ln1_g, l1_ln1_b,
           l1_peer_wq, l1_peer_k1, l1_peer_k2, l1_peer_u, l1_peer_v, l1_ln2_g, l1_ln2_b):
    B, S, D = x.shape
    n = B * S
    as2 = lambda t: t.reshape(n, D)
    as3 = lambda t: t.reshape(B, S, D)

    y = _even_mixer(x, l0_w_in, l0_rwkv_mu, l0_rwkv_w0, l0_rwkv_w_up, l0_rwkv_a0, l0_rwkv_a_up,
                    l0_rwkv_g_up, l0_rwkv_k_k, l0_rwkv_k_a, l0_rwkv_r_k, l0_rwkv_ln_g,
                    l0_rwkv_ln_b, l0_mlstm_conv_w, l0_mlstm_conv_b, l0_mlstm_ig_b,
                    l0_mlstm_fg_b, l0_mlstm_hn_g, l0_mlstm_hn_b, l0_w_out)
    x = as3(_resid_ln(as2(x), as2(y), l0_ln1_g, l0_ln1_b))
    y = _peer_ffn(x, l0_peer_wq, l0_peer_k1, l0_peer_k2, l0_peer_u, l0_peer_v)
    x = as3(_resid_ln(as2(x), as2(y), l0_ln2_g, l0_ln2_b))
    y = _odd_mixer(x, l1_w_in, l1_fox_qn_g, l1_fox_kn_g, l1_fox_f_b, l1_w_out)
    x = as3(_resid_ln(as2(x), as2(y), l1_ln1_g, l1_ln1_b))
    y = _peer_ffn(x, l1_peer_wq, l1_peer_k1, l1_peer_k2, l1_peer_u, l1_peer_v)
    x = as3(_resid_ln(as2(x), as2(y), l1_ln2_g, l1_ln2_b))
    return x
```

```python
import functools

import jax
import jax.numpy as jnp
from jax import lax
from jax.experimental import pallas as pl
from jax.experimental.pallas import tpu as pltpu

D_MODEL = 1024
DEPTH = 2
DN_ALPHA = (2.0 * DEPTH) ** 0.25
LN_EPS = 1e-5

RWKV_WIDTH = D_MODEL // 2
RWKV_HEAD = 64
RWKV_HEADS = RWKV_WIDTH // RWKV_HEAD
RWKV_W_RANK = 64
RWKV_A_RANK = 64
RWKV_G_RANK = 128
RWKV_GN_EPS = 1e-5 * RWKV_HEAD
RWKV_COLS = 3 * RWKV_WIDTH + RWKV_W_RANK + RWKV_A_RANK + RWKV_G_RANK

MLSTM_WIDTH = D_MODEL // 2
MLSTM_HEAD = 128
MLSTM_HEADS = MLSTM_WIDTH // MLSTM_HEAD
MLSTM_CHUNK = 64

FOX_HEAD = 64
FOX_HEADS = D_MODEL // FOX_HEAD
FOX_WIDTH = FOX_HEADS * FOX_HEAD
FOX_QBLOCK = 128

PEER_HEADS = 8
PEER_NKEYS = 128
PEER_TOPK = 16
PEER_DKEY = 256
PEER_DHALF = PEER_DKEY // 2
PEER_TOKBLOCK = 128


def _split_cols(p, sizes):
    out, start = [], 0
    for s in sizes:
        out.append(p[..., start:start + s])
        start += s
    return out


def _resid_ln_kernel(x_ref, y_ref, g_ref, b_ref, o_ref):
    z = DN_ALPHA * x_ref[...] + y_ref[...]
    mu = jnp.mean(z, axis=-1, keepdims=True)
    zc = z - mu
    var = jnp.mean(zc * zc, axis=-1, keepdims=True)
    o_ref[...] = zc * lax.rsqrt(var + LN_EPS) * g_ref[...] + b_ref[...]


def _resid_ln(x2, y2, g, b, tm=512):
    n, d = x2.shape
    row = pl.BlockSpec((tm, d), lambda i: (i, 0))
    vec = pl.BlockSpec((1, d), lambda i: (0, 0))
    return pl.pallas_call(
        _resid_ln_kernel,
        grid=(n // tm,),
        in_specs=[row, row, vec, vec],
        out_specs=row,
        out_shape=jax.ShapeDtypeStruct((n, d), jnp.float32),
        name="resid_ln",
    )(x2, y2, g.reshape(1, d), b.reshape(1, d))


def _head_norm(y, g, b, eps):
    mu = jnp.mean(y, -1, keepdims=True)
    var = jnp.mean(jnp.square(y - mu), -1, keepdims=True)
    return (y - mu) * lax.rsqrt(var + eps) * g + b


def _rms_norm(y, g):
    return y * lax.rsqrt(jnp.mean(y * y, -1, keepdims=True) + 1e-6) * g


def _token_shift(z):
    return jnp.pad(z, ((0, 0), (1, 0), (0, 0)))[:, :-1]


def _causal_conv(z, w, b):
    c = z.shape[-1]
    out = lax.conv_general_dilated(z, w[:, None, :], window_strides=(1,),
                                   padding=((w.shape[0] - 1, 0),),
                                   dimension_numbers=('NWC', 'WIO', 'NWC'),
                                   feature_group_count=c)
    return out + b


def _rwkv7_mix(p, mu, w0, w_up, a0, a_up, g_up, k_k, k_a, r_k, ln_g, ln_b):
    B, S, _ = p.shape
    H, N = RWKV_HEADS, RWKV_HEAD
    p = p + (_token_shift(p) - p) * mu
    r, k, v, xw, xa, xg = _split_cols(p, (RWKV_WIDTH, RWKV_WIDTH, RWKV_WIDTH,
                                          RWKV_W_RANK, RWKV_A_RANK, RWKV_G_RANK))
    log_w = -jnp.exp(-jax.nn.softplus(-(w0 + jnp.tanh(xw) @ w_up)) - 0.5)
    a = jax.nn.sigmoid(a0 + xa @ a_up)
    g = jax.nn.sigmoid(xg) @ g_up
    heads = lambda t: t.reshape(B, S, H, N)
    kk = heads(k * k_k)
    kk = kk / jnp.maximum(jnp.sqrt(jnp.sum(kk * kk, -1, keepdims=True)), 1e-12)
    k = k * (1.0 + (a - 1.0) * k_a)
    r_h, k_h, v_h, a_h = heads(r), heads(k), heads(v), heads(a)
    w_h = jnp.exp(heads(log_w))

    def step(state, inp):
        r_t, w_t, k_t, v_t, kk_t, a_t = inp
        s_kk = jnp.einsum('bhvk,bhk->bhv', state, kk_t)
        state = (state * w_t[:, :, None, :]
                 - s_kk[..., None] * (kk_t * a_t)[:, :, None, :]
                 + v_t[..., None] * k_t[:, :, None, :])
        return state, jnp.einsum('bhvk,bhk->bhv', state, r_t)

    seq_first = lambda t: jnp.moveaxis(t, 1, 0)
    state0 = jnp.zeros((B, H, N, N), jnp.float32)
    _, y = lax.scan(step, state0, (seq_first(r_h), seq_first(w_h), seq_first(k_h),
                                   seq_first(v_h), seq_first(kk), seq_first(a_h)))
    y = jnp.moveaxis(y, 0, 1)
    y = _head_norm(y, ln_g, ln_b, RWKV_GN_EPS)
    y = y + jnp.sum(r_h * k_h * r_k, -1, keepdims=True) * v_h
    return y.reshape(B, S, RWKV_WIDTH) * g


def _mlstm_chunkwise(q, k, v, ig, lf):
    B, S, H, D = q.shape
    L = MLSTM_CHUNK
    NC = S // L

    def to_chunks(t):
        t = t.reshape((B, NC, L, H) + t.shape[3:])
        return jnp.moveaxis(t, (1, 3), (0, 2))

    causal = jnp.tril(jnp.ones((L, L), dtype=bool))

    def body(carry, inp):
        C, n, m = carry
        qc, kc, vc, igc, lfc = inp
        b = jnp.cumsum(lfc, axis=-1)
        dmat = jnp.where(causal, b[..., :, None] - b[..., None, :] + igc[..., None, :], -jnp.inf)
        inter = b + m[..., None]
        m_t = jnp.maximum(inter, jnp.max(dmat, -1))
        weights = jnp.exp(dmat - m_t[..., None])
        sc = jnp.einsum('bhtd,bhsd->bhts', qc, kc) * weights
        carry_in = jnp.exp(inter - m_t)
        num = (jnp.einsum('bhts,bhsd->bhtd', sc, vc)
               + carry_in[..., None] * jnp.einsum('bhtk,bhkv->bhtv', qc, C))
        den = jnp.sum(sc, -1) + carry_in * jnp.einsum('bhtk,bhk->bht', qc, n)
        h = num / jnp.maximum(jnp.abs(den), jnp.exp(-m_t))[..., None]
        b_last = b[..., -1]
        gs = b_last[..., None] - b + igc
        m_new = jnp.maximum(b_last + m, jnp.max(gs, -1))
        ws = jnp.exp(gs - m_new[..., None])
        keep = jnp.exp(b_last + m - m_new)
        C = keep[..., None, None] * C + jnp.einsum('bhs,bhsk,bhsv->bhkv', ws, kc, vc)
        n = keep[..., None] * n + jnp.einsum('bhs,bhsk->bhk', ws, kc)
        return (C, n, m_new), h

    f32 = jnp.float32
    init = (jnp.zeros((B, H, D, D), f32), jnp.zeros((B, H, D), f32), jnp.zeros((B, H), f32))
    _, h = lax.scan(body, init, (to_chunks(q), to_chunks(k), to_chunks(v),
                                 to_chunks(ig), to_chunks(lf)))
    return jnp.moveaxis(h, (0, 2), (1, 3)).reshape(B, S, H, D)


def _mlstm_mix(p, conv_w, conv_b, ig_b, fg_b, hn_g, hn_b):
    B, S, _ = p.shape
    H, N = MLSTM_HEADS, MLSTM_HEAD
    q, k, v, o, ig, fg = _split_cols(p, (MLSTM_WIDTH, MLSTM_WIDTH, MLSTM_WIDTH, MLSTM_WIDTH, H, H))
    qk = jax.nn.silu(_causal_conv(jnp.concatenate([q, k], -1), conv_w, conv_b))
    q, k = qk[..., :MLSTM_WIDTH], qk[..., MLSTM_WIDTH:]
    heads = lambda t: t.reshape(B, S, H, N)
    q = heads(q) * (N ** -0.5)
    ig = ig + ig_b
    lf = jax.nn.log_sigmoid(fg + fg_b)
    h = _mlstm_chunkwise(q, heads(k), heads(v), ig, lf)
    h = _head_norm(h, hn_g, hn_b, LN_EPS).reshape(B, S, MLSTM_WIDTH)
    return jax.nn.sigmoid(o) * h


def _even_mixer(x, w_in, mu, w0, w_up, a0, a_up, g_up, k_k, k_a, r_k, rln_g, rln_b,
                conv_w, conv_b, ig_b, fg_b, hn_g, hn_b, w_out):
    p = x @ w_in
    y_a = _rwkv7_mix(p[..., :RWKV_COLS], mu, w0, w_up, a0, a_up, g_up, k_k, k_a, r_k, rln_g, rln_b)
    y_b = _mlstm_mix(p[..., RWKV_COLS:], conv_w, conv_b, ig_b, fg_b, hn_g, hn_b)
    return jnp.concatenate([y_a, y_b], -1) @ w_out


def _fox_attention(q, k, v, logf):
    B, S, H, D = q.shape
    NB = S // FOX_QBLOCK
    c = jnp.cumsum(logf, axis=1)
    c_key = jnp.transpose(c, (0, 2, 1))[:, :, None, :]
    q_blocks = jnp.moveaxis(q.reshape(B, NB, FOX_QBLOCK, H, D), 1, 0)
    c_blocks = jnp.moveaxis(c.reshape(B, NB, FOX_QBLOCK, H), 1, 0)
    key_pos = jnp.arange(S)
    scale = D ** -0.5

    def one_block(args):
        qb, cb, blk = args
        q_pos = blk * FOX_QBLOCK + jnp.arange(FOX_QBLOCK)
        logits = jnp.einsum('bqhd,bkhd->bhqk', qb, k) * scale
        logits = logits + jnp.transpose(cb, (0, 2, 1))[..., None] - c_key
        logits = jnp.where(key_pos[None, :] <= q_pos[:, None], logits, -jnp.inf)
        probs = jax.nn.softmax(logits, axis=-1)
        return jnp.einsum('bhqk,bkhd->bqhd', probs, v)

    out = lax.map(one_block, (q_blocks, c_blocks, jnp.arange(NB)))
    return jnp.moveaxis(out, 0, 1).reshape(B, S, H, D)


def _odd_mixer(x, w_in, qn_g, kn_g, f_b, w_out):
    B, S, _ = x.shape
    p = x @ w_in
    q, k, v, og, fl = _split_cols(p, (FOX_WIDTH, FOX_WIDTH, FOX_WIDTH, FOX_WIDTH, FOX_HEADS))
    heads = lambda t: t.reshape(B, S, FOX_HEADS, FOX_HEAD)
    q = _rms_norm(heads(q), qn_g)
    k = _rms_norm(heads(k), kn_g)
    logf = jax.nn.log_sigmoid(fl + f_b)
    o = _fox_attention(q, k, heads(v), logf).reshape(B, S, FOX_WIDTH)
    return (jax.nn.sigmoid(og) * o) @ w_out


def _peer_ffn(x, w_q, sub_k1, sub_k2, u_tab, v_tab):
    B, S, Dm = x.shape
    K = PEER_TOPK
    xt = x.reshape((B * S) // PEER_TOKBLOCK, PEER_TOKBLOCK, Dm)

    def block(xb):
        T = xb.shape[0]
        q = (xb @ w_q).reshape(T, PEER_HEADS, 2, PEER_DHALF)
        s1 = jnp.einsum('thd,hnd->thn', q[:, :, 0], sub_k1)
        s2 = jnp.einsum('thd,hnd->thn', q[:, :, 1], sub_k2)
        v1, i1 = lax.top_k(s1, K)
        v2, i2 = lax.top_k(s2, K)
        cand = (v1[..., :, None] + v2[..., None, :]).reshape(T, PEER_HEADS, K * K)
        sc, ci = lax.top_k(cand, K)
        e1 = jnp.take_along_axis(i1, ci // K, axis=-1)
        e2 = jnp.take_along_axis(i2, ci % K, axis=-1)
        eid = e1 * PEER_NKEYS + e2
        gate = jax.nn.softmax(sc, axis=-1)
        act = jax.nn.gelu(jnp.einsum('td,thkd->thk', xb, u_tab[eid]), approximate=False)
        return jnp.einsum('thk,thkd->td', gate * act, v_tab[eid])

    return lax.map(block, xt).reshape(B, S, Dm)


PEER_EXPERTS = PEER_NKEYS * PEER_NKEYS
PEER_TOK_TILE = 256
PEER_EXPERT_CHUNK = 1024
LANES = 128
MXU_DIM = 256
INV_SQRT2 = 0.7071067811865476
_NT = (((1,), (1,)), ((), ()))


def _topk_desc(work, k):
    vals = []
    for _ in range(k):
        m = jnp.max(work, axis=0, keepdims=True)
        vals.append(m)
        work = jnp.where(work >= m, -jnp.inf, work)
    return vals


def _peer_select_kernel(x_ref, wqt_ref, k1_ref, k2_ref,
                        s1m_ref, a1_ref, s2m_ref, a2_ref, tau_ref, qt_ref):
    f32, bf16 = jnp.float32, jnp.bfloat16
    K = PEER_TOPK
    qt_ref[...] = lax.dot_general(wqt_ref[...], x_ref[...], _NT, preferred_element_type=f32)
    n_groups = x_ref.shape[0] // LANES

    def head(h, carry):
        base = pl.multiple_of(h * PEER_DKEY, PEER_DKEY)
        q1 = qt_ref[pl.ds(base, PEER_DHALF), :].astype(bf16)
        q2 = qt_ref[pl.ds(base + PEER_DHALF, PEER_DHALF), :].astype(bf16)
        s1 = jnp.dot(k1_ref[h], q1, preferred_element_type=f32)
        s2 = jnp.dot(k2_ref[h], q2, preferred_element_type=f32)
        for c in range(n_groups):
            sl = slice(c * LANES, (c + 1) * LANES)
            s1c, s2c = s1[:, sl], s2[:, sl]
            v1 = _topk_desc(s1c, K)
            v2 = _topk_desc(s2c, K)
            v2s = jnp.concatenate(v2, axis=0)
            cand = jnp.concatenate([v1[a] + v2s for a in range(K)], axis=0)
            vc = _topk_desc(cand, K)
            tau, cmax = vc[K - 1], vc[0]
            z = jnp.sum(jnp.where(cand >= tau, jnp.exp(cand - cmax), 0.0), axis=0, keepdims=True)
            s1m = jnp.where(s1c >= v1[K - 1], s1c, -jnp.inf)
            s2m = jnp.where(s2c >= v2[K - 1], s2c, -jnp.inf)
            s1m_ref[h, :, sl] = s1m
            a1_ref[h, :, sl] = jnp.exp(s1m - v1[0])
            s2m_ref[h, :, sl] = s2m
            a2_ref[h, :, sl] = jnp.exp(s2m - v2[0]) / z
            tau_ref[h, :, sl] = tau
        return carry

    lax.fori_loop(0, PEER_HEADS, head, 0)


def _peer_select(xb, wqt, k1, k2):
    n, d = xb.shape
    T = PEER_TOK_TILE
    H, NK = PEER_HEADS, PEER_NKEYS
    full = lambda shape: pl.BlockSpec(shape, lambda i: (0,) * len(shape))
    tok3 = lambda rows: pl.BlockSpec((H, rows, T), lambda i: (0, 0, i))
    f32 = jnp.float32
    return pl.pallas_call(
        _peer_select_kernel,
        grid=(n // T,),
        in_specs=[pl.BlockSpec((T, d), lambda i: (i, 0)), full(wqt.shape), full(k1.shape), full(k2.shape)],
        out_specs=[tok3(NK), tok3(NK), tok3(NK), tok3(NK), tok3(1)],
        out_shape=[jax.ShapeDtypeStruct((H, NK, n), f32)] * 4 + [jax.ShapeDtypeStruct((H, 1, n), f32)],
        scratch_shapes=[pltpu.VMEM((H * PEER_DKEY, T), f32)],
        name="peer_select",
    )(xb, wqt, k1, k2)


def _peer_expert_kernel(x_ref, u_ref, vt_ref, s1m_ref, a1_ref, s2m_ref, a2_ref, tau_ref,
                        o_ref, acc_ref):
    f32, bf16 = jnp.float32, jnp.bfloat16
    j = pl.program_id(1)

    @pl.when(j == 0)
    def _():
        acc_ref[...] = jnp.zeros_like(acc_ref)

    xb = x_ref[...]
    T = xb.shape[0]
    for p in range(PEER_EXPERT_CHUNK // MXU_DIM):
        rows = slice(p * MXU_DIM, (p + 1) * MXU_DIM)
        h = lax.dot_general(u_ref[rows, :], xb, _NT, preferred_element_type=f32)
        ws = []
        for half in range(MXU_DIM // PEER_NKEYS):
            r = p * (MXU_DIM // PEER_NKEYS) + half
            g = jnp.zeros((PEER_NKEYS, T), f32)
            for hd in range(PEER_HEADS):
                t = s1m_ref[hd, r:r + 1, :] + s2m_ref[hd]
                g = g + jnp.where(t >= tau_ref[hd], a1_ref[hd, r:r + 1, :] * a2_ref[hd], 0.0)
            hh = h[half * PEER_NKEYS:(half + 1) * PEER_NKEYS]
            ws.append((g * (0.5 * hh * (1.0 + lax.erf(hh * INV_SQRT2)))).astype(bf16))
        w = jnp.concatenate(ws, axis=0)
        acc_ref[...] += jnp.dot(vt_ref[:, rows], w, preferred_element_type=f32)

    @pl.when(j == pl.num_programs(1) - 1)
    def _():
        o_ref[...] = acc_ref[...].T


def _peer_experts(xb, ub, vtb, s1m, a1, s2m, a2, tau):
    n, d = xb.shape
    T, CE = PEER_TOK_TILE, PEER_EXPERT_CHUNK
    H, NK = PEER_HEADS, PEER_NKEYS
    e1_per_chunk = CE // NK
    return pl.pallas_call(
        _peer_expert_kernel,
        grid=(n // T, PEER_EXPERTS // CE),
        in_specs=[
            pl.BlockSpec((T, d), lambda i, j: (i, 0)),
            pl.BlockSpec((CE, d), lambda i, j: (j, 0)),
            pl.BlockSpec((d, CE), lambda i, j: (0, j)),
            pl.BlockSpec((H, e1_per_chunk, T), lambda i, j: (0, j, i)),
            pl.BlockSpec((H, e1_per_chunk, T), lambda i, j: (0, j, i)),
            pl.BlockSpec((H, NK, T), lambda i, j: (0, 0, i)),
            pl.BlockSpec((H, NK, T), lambda i, j: (0, 0, i)),
            pl.BlockSpec((H, 1, T), lambda i, j: (0, 0, i)),
        ],
        out_specs=pl.BlockSpec((T, d), lambda i, j: (i, 0)),
        out_shape=jax.ShapeDtypeStruct((n, d), jnp.float32),
        scratch_shapes=[pltpu.VMEM((d, T), jnp.float32)],
        compiler_params=pltpu.CompilerParams(dimension_semantics=("arbitrary", "arbitrary")),
        name="peer_experts",
    )(xb, ub, vtb, s1m, a1, s2m, a2, tau)


def _peer_pallas(x2, w_q, sub_k1, sub_k2, u_tab, v_tab):
    bf16 = jnp.bfloat16
    xb = x2.astype(bf16)
    sel = _peer_select(xb, w_q.T.astype(bf16), sub_k1.astype(bf16), sub_k2.astype(bf16))
    return _peer_experts(xb, u_tab.astype(bf16), v_tab.T.astype(bf16), *sel)


def kernel(x, l0_w_in, l0_rwkv_mu, l0_rwkv_w0, l0_rwkv_w_up, l0_rwkv_a0, l0_rwkv_a_up,
           l0_rwkv_g_up, l0_rwkv_k_k, l0_rwkv_k_a, l0_rwkv_r_k, l0_rwkv_ln_g, l0_rwkv_ln_b,
           l0_mlstm_conv_w, l0_mlstm_conv_b, l0_mlstm_ig_b, l0_mlstm_fg_b,
           l0_mlstm_hn_g, l0_mlstm_hn_b, l0_w_out, l0_ln1_g, l0_ln1_b,
           l0_peer_wq, l0_peer_k1, l0_peer_k2, l0_peer_u, l0_peer_v, l0_ln2_g, l0_ln2_b,
           l1_w_in, l1_fox_qn_g, l1_fox_kn_g, l1_fox_f_b, l1_w_out, l1_ln1_g, l1_ln1_b,
           l1_peer_wq, l1_peer_k1, l1_peer_k2, l1_peer_u, l1_peer_v, l1_ln2_g, l1_ln2_b):
    B, S, D = x.shape
    n = B * S
    as2 = lambda t: t.reshape(n, D)
    as3 = lambda t: t.reshape(B, S, D)

    y = _even_mixer(x, l0_w_in, l0_rwkv_mu, l0_rwkv_w0, l0_rwkv_w_up, l0_rwkv_a0, l0_rwkv_a_up,
                    l0_rwkv_g_up, l0_rwkv_k_k, l0_rwkv_k_a, l0_rwkv_r_k, l0_rwkv_ln_g,
                    l0_rwkv_ln_b, l0_mlstm_conv_w, l0_mlstm_conv_b, l0_mlstm_ig_b,
                    l0_mlstm_fg_b, l0_mlstm_hn_g, l0_mlstm_hn_b, l0_w_out)
    x = as3(_resid_ln(as2(x), as2(y), l0_ln1_g, l0_ln1_b))
    y = _peer_pallas(as2(x), l0_peer_wq, l0_peer_k1, l0_peer_k2, l0_peer_u, l0_peer_v)
    x = as3(_resid_ln(as2(x), y, l0_ln2_g, l0_ln2_b))
    y = _odd_mixer(x, l1_w_in, l1_fox_qn_g, l1_fox_kn_g, l1_fox_f_b, l1_w_out)
    x = as3(_resid_ln(as2(x), as2(y), l1_ln1_g, l1_ln1_b))
    y = _peer_pallas(as2(x), l1_peer_wq, l1_peer_k1, l1_peer_k2, l1_peer_u, l1_peer_v)
    x = as3(_resid_ln(as2(x), y, l1_ln2_g, l1_ln2_b))
    return x
```

```python
import functools

import jax
import jax.numpy as jnp
from jax import lax
from jax.experimental import pallas as pl
from jax.experimental.pallas import tpu as pltpu

D_MODEL = 1024
DEPTH = 2
DN_ALPHA = (2.0 * DEPTH) ** 0.25
LN_EPS = 1e-5

RWKV_WIDTH = D_MODEL // 2
RWKV_HEAD = 64
RWKV_HEADS = RWKV_WIDTH // RWKV_HEAD
RWKV_W_RANK = 64
RWKV_A_RANK = 64
RWKV_G_RANK = 128
RWKV_GN_EPS = 1e-5 * RWKV_HEAD
RWKV_COLS = 3 * RWKV_WIDTH + RWKV_W_RANK + RWKV_A_RANK + RWKV_G_RANK

MLSTM_WIDTH = D_MODEL // 2
MLSTM_HEAD = 128
MLSTM_HEADS = MLSTM_WIDTH // MLSTM_HEAD
MLSTM_CHUNK = 64

FOX_HEAD = 64
FOX_HEADS = D_MODEL // FOX_HEAD
FOX_WIDTH = FOX_HEADS * FOX_HEAD
FOX_QBLOCK = 128

PEER_HEADS = 8
PEER_NKEYS = 128
PEER_TOPK = 16
PEER_DKEY = 256
PEER_DHALF = PEER_DKEY // 2
PEER_TOKBLOCK = 128


def _split_cols(p, sizes):
    out, start = [], 0
    for s in sizes:
        out.append(p[..., start:start + s])
        start += s
    return out


def _resid_ln_kernel(x_ref, y_ref, g_ref, b_ref, o_ref):
    z = DN_ALPHA * x_ref[...] + y_ref[...]
    mu = jnp.mean(z, axis=-1, keepdims=True)
    zc = z - mu
    var = jnp.mean(zc * zc, axis=-1, keepdims=True)
    o_ref[...] = zc * lax.rsqrt(var + LN_EPS) * g_ref[...] + b_ref[...]


def _resid_ln(x2, y2, g, b, tm=512):
    n, d = x2.shape
    row = pl.BlockSpec((tm, d), lambda i: (i, 0))
    vec = pl.BlockSpec((1, d), lambda i: (0, 0))
    return pl.pallas_call(
        _resid_ln_kernel,
        grid=(n // tm,),
        in_specs=[row, row, vec, vec],
        out_specs=row,
        out_shape=jax.ShapeDtypeStruct((n, d), jnp.float32),
        name="resid_ln",
    )(x2, y2, g.reshape(1, d), b.reshape(1, d))


def _head_norm(y, g, b, eps):
    mu = jnp.mean(y, -1, keepdims=True)
    var = jnp.mean(jnp.square(y - mu), -1, keepdims=True)
    return (y - mu) * lax.rsqrt(var + eps) * g + b


def _rms_norm(y, g):
    return y * lax.rsqrt(jnp.mean(y * y, -1, keepdims=True) + 1e-6) * g


def _token_shift(z):
    return jnp.pad(z, ((0, 0), (1, 0), (0, 0)))[:, :-1]


def _causal_conv(z, w, b):
    c = z.shape[-1]
    out = lax.conv_general_dilated(z, w[:, None, :], window_strides=(1,),
                                   padding=((w.shape[0] - 1, 0),),
                                   dimension_numbers=('NWC', 'WIO', 'NWC'),
                                   feature_group_count=c)
    return out + b


def _rwkv7_mix(p, mu, w0, w_up, a0, a_up, g_up, k_k, k_a, r_k, ln_g, ln_b):
    B, S, _ = p.shape
    H, N = RWKV_HEADS, RWKV_HEAD
    p = p + (_token_shift(p) - p) * mu
    r, k, v, xw, xa, xg = _split_cols(p, (RWKV_WIDTH, RWKV_WIDTH, RWKV_WIDTH,
                                          RWKV_W_RANK, RWKV_A_RANK, RWKV_G_RANK))
    log_w = -jnp.exp(-jax.nn.softplus(-(w0 + jnp.tanh(xw) @ w_up)) - 0.5)
    a = jax.nn.sigmoid(a0 + xa @ a_up)
    g = jax.nn.sigmoid(xg) @ g_up
    heads = lambda t: t.reshape(B, S, H, N)
    kk = heads(k * k_k)
    kk = kk / jnp.maximum(jnp.sqrt(jnp.sum(kk * kk, -1, keepdims=True)), 1e-12)
    k = k * (1.0 + (a - 1.0) * k_a)
    r_h, k_h, v_h, a_h = heads(r), heads(k), heads(v), heads(a)
    w_h = jnp.exp(heads(log_w))

    def step(state, inp):
        r_t, w_t, k_t, v_t, kk_t, a_t = inp
        s_kk = jnp.einsum('bhvk,bhk->bhv', state, kk_t)
        state = (state * w_t[:, :, None, :]
                 - s_kk[..., None] * (kk_t * a_t)[:, :, None, :]
                 + v_t[..., None] * k_t[:, :, None, :])
        return state, jnp.einsum('bhvk,bhk->bhv', state, r_t)

    seq_first = lambda t: jnp.moveaxis(t, 1, 0)
    state0 = jnp.zeros((B, H, N, N), jnp.float32)
    _, y = lax.scan(step, state0, (seq_first(r_h), seq_first(w_h), seq_first(k_h),
                                   seq_first(v_h), seq_first(kk), seq_first(a_h)))
    y = jnp.moveaxis(y, 0, 1)
    y = _head_norm(y, ln_g, ln_b, RWKV_GN_EPS)
    y = y + jnp.sum(r_h * k_h * r_k, -1, keepdims=True) * v_h
    return y.reshape(B, S, RWKV_WIDTH) * g


def _mlstm_chunkwise(q, k, v, ig, lf):
    B, S, H, D = q.shape
    L = MLSTM_CHUNK
    NC = S // L

    def to_chunks(t):
        t = t.reshape((B, NC, L, H) + t.shape[3:])
        return jnp.moveaxis(t, (1, 3), (0, 2))

    causal = jnp.tril(jnp.ones((L, L), dtype=bool))

    def body(carry, inp):
        C, n, m = carry
        qc, kc, vc, igc, lfc = inp
        b = jnp.cumsum(lfc, axis=-1)
        dmat = jnp.where(causal, b[..., :, None] - b[..., None, :] + igc[..., None, :], -jnp.inf)
        inter = b + m[..., None]
        m_t = jnp.maximum(inter, jnp.max(dmat, -1))
        weights = jnp.exp(dmat - m_t[..., None])
        sc = jnp.einsum('bhtd,bhsd->bhts', qc, kc) * weights
        carry_in = jnp.exp(inter - m_t)
        num = (jnp.einsum('bhts,bhsd->bhtd', sc, vc)
               + carry_in[..., None] * jnp.einsum('bhtk,bhkv->bhtv', qc, C))
        den = jnp.sum(sc, -1) + carry_in * jnp.einsum('bhtk,bhk->bht', qc, n)
        h = num / jnp.maximum(jnp.abs(den), jnp.exp(-m_t))[..., None]
        b_last = b[..., -1]
        gs = b_last[..., None] - b + igc
        m_new = jnp.maximum(b_last + m, jnp.max(gs, -1))
        ws = jnp.exp(gs - m_new[..., None])
        keep = jnp.exp(b_last + m - m_new)
        C = keep[..., None, None] * C + jnp.einsum('bhs,bhsk,bhsv->bhkv', ws, kc, vc)
        n = keep[..., None] * n + jnp.einsum('bhs,bhsk->bhk', ws, kc)
        return (C, n, m_new), h

    f32 = jnp.float32
    init = (jnp.zeros((B, H, D, D), f32), jnp.zeros((B, H, D), f32), jnp.zeros((B, H), f32))
    _, h = lax.scan(body, init, (to_chunks(q), to_chunks(k), to_chunks(v),
                                 to_chunks(ig), to_chunks(lf)))
    return jnp.moveaxis(h, (0, 2), (1, 3)).reshape(B, S, H, D)


def _mlstm_mix(p, conv_w, conv_b, ig_b, fg_b, hn_g, hn_b):
    B, S, _ = p.shape
    H, N = MLSTM_HEADS, MLSTM_HEAD
    q, k, v, o, ig, fg = _split_cols(p, (MLSTM_WIDTH, MLSTM_WIDTH, MLSTM_WIDTH, MLSTM_WIDTH, H, H))
    qk = jax.nn.silu(_causal_conv(jnp.concatenate([q, k], -1), conv_w, conv_b))
    q, k = qk[..., :MLSTM_WIDTH], qk[..., MLSTM_WIDTH:]
    heads = lambda t: t.reshape(B, S, H, N)
    q = heads(q) * (N ** -0.5)
    ig = ig + ig_b
    lf = jax.nn.log_sigmoid(fg + fg_b)
    h = _mlstm_chunkwise(q, heads(k), heads(v), ig, lf)
    h = _head_norm(h, hn_g, hn_b, LN_EPS).reshape(B, S, MLSTM_WIDTH)
    return jax.nn.sigmoid(o) * h


def _even_mixer(x, w_in, mu, w0, w_up, a0, a_up, g_up, k_k, k_a, r_k, rln_g, rln_b,
                conv_w, conv_b, ig_b, fg_b, hn_g, hn_b, w_out):
    p = x @ w_in
    y_a = _rwkv_pallas(p[..., :RWKV_COLS], mu, w0, w_up, a0, a_up, g_up, k_k, k_a, r_k, rln_g, rln_b)
    y_b = _mlstm_mix(p[..., RWKV_COLS:], conv_w, conv_b, ig_b, fg_b, hn_g, hn_b)
    return jnp.concatenate([y_a, y_b], -1) @ w_out


def _fox_attention(q, k, v, logf):
    B, S, H, D = q.shape
    NB = S // FOX_QBLOCK
    c = jnp.cumsum(logf, axis=1)
    c_key = jnp.transpose(c, (0, 2, 1))[:, :, None, :]
    q_blocks = jnp.moveaxis(q.reshape(B, NB, FOX_QBLOCK, H, D), 1, 0)
    c_blocks = jnp.moveaxis(c.reshape(B, NB, FOX_QBLOCK, H), 1, 0)
    key_pos = jnp.arange(S)
    scale = D ** -0.5

    def one_block(args):
        qb, cb, blk = args
        q_pos = blk * FOX_QBLOCK + jnp.arange(FOX_QBLOCK)
        logits = jnp.einsum('bqhd,bkhd->bhqk', qb, k) * scale
        logits = logits + jnp.transpose(cb, (0, 2, 1))[..., None] - c_key
        logits = jnp.where(key_pos[None, :] <= q_pos[:, None], logits, -jnp.inf)
        probs = jax.nn.softmax(logits, axis=-1)
        return jnp.einsum('bhqk,bkhd->bqhd', probs, v)

    out = lax.map(one_block, (q_blocks, c_blocks, jnp.arange(NB)))
    return jnp.moveaxis(out, 0, 1).reshape(B, S, H, D)


def _odd_mixer(x, w_in, qn_g, kn_g, f_b, w_out):
    B, S, _ = x.shape
    p = x @ w_in
    q, k, v, og, fl = _split_cols(p, (FOX_WIDTH, FOX_WIDTH, FOX_WIDTH, FOX_WIDTH, FOX_HEADS))
    heads = lambda t: t.reshape(B, S, FOX_HEADS, FOX_HEAD)
    q = _rms_norm(heads(q), qn_g)
    k = _rms_norm(heads(k), kn_g)
    logf = jax.nn.log_sigmoid(fl + f_b)
    o = _fox_attention(q, k, heads(v), logf).reshape(B, S, FOX_WIDTH)
    return (jax.nn.sigmoid(og) * o) @ w_out


def _peer_ffn(x, w_q, sub_k1, sub_k2, u_tab, v_tab):
    B, S, Dm = x.shape
    K = PEER_TOPK
    xt = x.reshape((B * S) // PEER_TOKBLOCK, PEER_TOKBLOCK, Dm)

    def block(xb):
        T = xb.shape[0]
        q = (xb @ w_q).reshape(T, PEER_HEADS, 2, PEER_DHALF)
        s1 = jnp.einsum('thd,hnd->thn', q[:, :, 0], sub_k1)
        s2 = jnp.einsum('thd,hnd->thn', q[:, :, 1], sub_k2)
        v1, i1 = lax.top_k(s1, K)
        v2, i2 = lax.top_k(s2, K)
        cand = (v1[..., :, None] + v2[..., None, :]).reshape(T, PEER_HEADS, K * K)
        sc, ci = lax.top_k(cand, K)
        e1 = jnp.take_along_axis(i1, ci // K, axis=-1)
        e2 = jnp.take_along_axis(i2, ci % K, axis=-1)
        eid = e1 * PEER_NKEYS + e2
        gate = jax.nn.softmax(sc, axis=-1)
        act = jax.nn.gelu(jnp.einsum('td,thkd->thk', xb, u_tab[eid]), approximate=False)
        return jnp.einsum('thk,thkd->td', gate * act, v_tab[eid])

    return lax.map(block, xt).reshape(B, S, Dm)


PEER_EXPERTS = PEER_NKEYS * PEER_NKEYS
PEER_TOK_TILE = 256
PEER_EXPERT_CHUNK = 1024
LANES = 128
MXU_DIM = 256
INV_SQRT2 = 0.7071067811865476
_NT = (((1,), (1,)), ((), ()))


def _topk_desc(work, k):
    vals = []
    for _ in range(k):
        m = jnp.max(work, axis=0, keepdims=True)
        vals.append(m)
        work = jnp.where(work >= m, -jnp.inf, work)
    return vals


def _peer_select_kernel(x_ref, wqt_ref, k1_ref, k2_ref,
                        s1m_ref, a1_ref, s2m_ref, a2_ref, tau_ref, qt_ref):
    f32, bf16 = jnp.float32, jnp.bfloat16
    K = PEER_TOPK
    qt_ref[...] = lax.dot_general(wqt_ref[...], x_ref[...], _NT, preferred_element_type=f32)
    n_groups = x_ref.shape[0] // LANES

    def head(h, carry):
        base = pl.multiple_of(h * PEER_DKEY, PEER_DKEY)
        q1 = qt_ref[pl.ds(base, PEER_DHALF), :].astype(bf16)
        q2 = qt_ref[pl.ds(base + PEER_DHALF, PEER_DHALF), :].astype(bf16)
        s1 = jnp.dot(k1_ref[h], q1, preferred_element_type=f32)
        s2 = jnp.dot(k2_ref[h], q2, preferred_element_type=f32)
        for c in range(n_groups):
            sl = slice(c * LANES, (c + 1) * LANES)
            s1c, s2c = s1[:, sl], s2[:, sl]
            v1 = _topk_desc(s1c, K)
            v2 = _topk_desc(s2c, K)
            v2s = jnp.concatenate(v2, axis=0)
            cand = jnp.concatenate([v1[a] + v2s for a in range(K)], axis=0)
            vc = _topk_desc(cand, K)
            tau, cmax = vc[K - 1], vc[0]
            z = jnp.sum(jnp.where(cand >= tau, jnp.exp(cand - cmax), 0.0), axis=0, keepdims=True)
            s1m = jnp.where(s1c >= v1[K - 1], s1c, -jnp.inf)
            s2m = jnp.where(s2c >= v2[K - 1], s2c, -jnp.inf)
            s1m_ref[h, :, sl] = s1m
            a1_ref[h, :, sl] = jnp.exp(s1m - v1[0])
            s2m_ref[h, :, sl] = s2m
            a2_ref[h, :, sl] = jnp.exp(s2m - v2[0]) / z
            tau_ref[h, :, sl] = tau
        return carry

    lax.fori_loop(0, PEER_HEADS, head, 0)


def _peer_select(xb, wqt, k1, k2):
    n, d = xb.shape
    T = PEER_TOK_TILE
    H, NK = PEER_HEADS, PEER_NKEYS
    full = lambda shape: pl.BlockSpec(shape, lambda i: (0,) * len(shape))
    tok3 = lambda rows: pl.BlockSpec((H, rows, T), lambda i: (0, 0, i))
    f32 = jnp.float32
    return pl.pallas_call(
        _peer_select_kernel,
        grid=(n // T,),
        in_specs=[pl.BlockSpec((T, d), lambda i: (i, 0)), full(wqt.shape), full(k1.shape), full(k2.shape)],
        out_specs=[tok3(NK), tok3(NK), tok3(NK), tok3(NK), tok3(1)],
        out_shape=[jax.ShapeDtypeStruct((H, NK, n), f32)] * 4 + [jax.ShapeDtypeStruct((H, 1, n), f32)],
        scratch_shapes=[pltpu.VMEM((H * PEER_DKEY, T), f32)],
        name="peer_select",
    )(xb, wqt, k1, k2)


def _peer_expert_kernel(x_ref, u_ref, vt_ref, s1m_ref, a1_ref, s2m_ref, a2_ref, tau_ref,
                        o_ref, acc_ref):
    f32, bf16 = jnp.float32, jnp.bfloat16
    j = pl.program_id(1)

    @pl.when(j == 0)
    def _():
        acc_ref[...] = jnp.zeros_like(acc_ref)

    xb = x_ref[...]
    T = xb.shape[0]
    for p in range(PEER_EXPERT_CHUNK // MXU_DIM):
        rows = slice(p * MXU_DIM, (p + 1) * MXU_DIM)
        h = lax.dot_general(u_ref[rows, :], xb, _NT, preferred_element_type=f32)
        ws = []
        for half in range(MXU_DIM // PEER_NKEYS):
            r = p * (MXU_DIM // PEER_NKEYS) + half
            g = jnp.zeros((PEER_NKEYS, T), f32)
            for hd in range(PEER_HEADS):
                t = s1m_ref[hd, r:r + 1, :] + s2m_ref[hd]
                g = g + jnp.where(t >= tau_ref[hd], a1_ref[hd, r:r + 1, :] * a2_ref[hd], 0.0)
            hh = h[half * PEER_NKEYS:(half + 1) * PEER_NKEYS]
            ws.append((g * (0.5 * hh * (1.0 + lax.erf(hh * INV_SQRT2)))).astype(bf16))
        w = jnp.concatenate(ws, axis=0)
        acc_ref[...] += jnp.dot(vt_ref[:, rows], w, preferred_element_type=f32)

    @pl.when(j == pl.num_programs(1) - 1)
    def _():
        o_ref[...] = acc_ref[...].T


def _peer_experts(xb, ub, vtb, s1m, a1, s2m, a2, tau):
    n, d = xb.shape
    T, CE = PEER_TOK_TILE, PEER_EXPERT_CHUNK
    H, NK = PEER_HEADS, PEER_NKEYS
    e1_per_chunk = CE // NK
    return pl.pallas_call(
        _peer_expert_kernel,
        grid=(n // T, PEER_EXPERTS // CE),
        in_specs=[
            pl.BlockSpec((T, d), lambda i, j: (i, 0)),
            pl.BlockSpec((CE, d), lambda i, j: (j, 0)),
            pl.BlockSpec((d, CE), lambda i, j: (0, j)),
            pl.BlockSpec((H, e1_per_chunk, T), lambda i, j: (0, j, i)),
            pl.BlockSpec((H, e1_per_chunk, T), lambda i, j: (0, j, i)),
            pl.BlockSpec((H, NK, T), lambda i, j: (0, 0, i)),
            pl.BlockSpec((H, NK, T), lambda i, j: (0, 0, i)),
            pl.BlockSpec((H, 1, T), lambda i, j: (0, 0, i)),
        ],
        out_specs=pl.BlockSpec((T, d), lambda i, j: (i, 0)),
        out_shape=jax.ShapeDtypeStruct((n, d), jnp.float32),
        scratch_shapes=[pltpu.VMEM((d, T), jnp.float32)],
        compiler_params=pltpu.CompilerParams(dimension_semantics=("arbitrary", "arbitrary")),
        name="peer_experts",
    )(xb, ub, vtb, s1m, a1, s2m, a2, tau)


def _peer_pallas(x2, w_q, sub_k1, sub_k2, u_tab, v_tab):
    bf16 = jnp.bfloat16
    xb = x2.astype(bf16)
    sel = _peer_select(xb, w_q.T.astype(bf16), sub_k1.astype(bf16), sub_k2.astype(bf16))
    return _peer_experts(xb, u_tab.astype(bf16), v_tab.T.astype(bf16), *sel)


RWKV_CHUNK = 64
RWKV_PAIR = LANES // RWKV_HEAD
RWKV_PAIRS = RWKV_HEADS // RWKV_PAIR
_NN = (((1,), (0,)), ((), ()))
_TN = (((0,), (0,)), ((), ()))


def _split_bf16(a):
    hi = a.astype(jnp.bfloat16)
    lo = (a - hi.astype(jnp.float32)).astype(jnp.bfloat16)
    return hi, lo


def _mm2(a, b, dims):
    d = lambda u, v: lax.dot_general(u, v, dims, preferred_element_type=jnp.float32)
    (ah, al), (bh, bl) = _split_bf16(a), _split_bf16(b)
    return d(ah, bh) + d(ah, bl) + d(al, bh)


def _mm_exact_rhs(a, b_exact, dims, terms=2):
    d = lambda u: lax.dot_general(u, b_exact, dims, preferred_element_type=jnp.float32)
    out, rem = None, a
    for _ in range(terms):
        piece = rem.astype(jnp.bfloat16)
        rem = rem - piece.astype(jnp.float32)
        out = d(piece) if out is None else out + d(piece)
    return out


def _mm_exact_lhs(a_exact, b, dims, terms=3):
    d = lambda v: lax.dot_general(a_exact, v, dims, preferred_element_type=jnp.float32)
    out, rem = None, b
    for _ in range(terms):
        piece = rem.astype(jnp.bfloat16)
        rem = rem - piece.astype(jnp.float32)
        out = d(piece) if out is None else out + d(piece)
    return out


def _softplus(z):
    return jnp.maximum(z, 0.0) + jnp.log1p(jnp.exp(-jnp.abs(z)))


def _rwkv_kernel(p_ref, prev_ref, mu_ref, w0_ref, wup_ref, a0_ref, aup_ref, gup_ref,
                 kk_ref, ka_ref, rk_ref, lng_ref, lnb_ref, ones_ref,
                 o_ref, state_ref):
    f32, bf16 = jnp.float32, jnp.bfloat16
    L, W = RWKV_CHUNK, RWKV_WIDTH
    c_idx = pl.program_id(1)

    @pl.when(c_idx == 0)
    def _():
        state_ref[...] = jnp.zeros_like(state_ref)

    p = p_ref[0]
    row = lax.broadcasted_iota(jnp.int32, p.shape, 0)
    prev_row = jnp.where(c_idx == 0, 0.0, prev_ref[0, 7:8, :])
    shifted = jnp.where(row == 0, prev_row, pltpu.roll(p, 1, axis=0))
    p = p + (shifted - p) * mu_ref[...]
    r, k, v = p[:, 0:W], p[:, W:2 * W], p[:, 2 * W:3 * W]
    o = 3 * W
    xw = p[:, o:o + RWKV_W_RANK]
    xa = p[:, o + RWKV_W_RANK:o + RWKV_W_RANK + RWKV_A_RANK]
    xg = p[:, o + RWKV_W_RANK + RWKV_A_RANK:]
    dotd = lambda u, m: jnp.dot(u.astype(bf16), m, preferred_element_type=f32)
    lw = -jnp.exp(-_softplus(-(w0_ref[...] + dotd(jnp.tanh(xw), wup_ref[...]))) - 0.5)
    a = jax.nn.sigmoid(a0_ref[...] + dotd(xa, aup_ref[...]))
    g = dotd(jax.nn.sigmoid(xg), gup_ref[...])
    ones_bd = ones_ref[...]
    head_sum = lambda t: _mm_exact_rhs(t, ones_bd, _NN)
    kk = k * kk_ref[...]
    kk = kk / jnp.maximum(jnp.sqrt(head_sum(kk * kk)), 1e-12)
    k = k * (1.0 + (a - 1.0) * ka_ref[...])

    lane = lax.broadcasted_iota(jnp.int32, (L, LANES), 1)
    trow = lax.broadcasted_iota(jnp.int32, (L, LANES), 0)
    s_in = lane % RWKV_HEAD
    strict, incl = s_in < trow, s_in <= trow
    eye_pair = (s_in == trow).astype(f32)
    m0 = lane < RWKV_HEAD
    tril = (lax.broadcasted_iota(jnp.int32, (L, L), 1)
            <= lax.broadcasted_iota(jnp.int32, (L, L), 0)).astype(bf16)
    r2 = lax.broadcasted_iota(jnp.int32, (LANES, LANES), 0) // RWKV_HEAD
    c2 = lax.broadcasted_iota(jnp.int32, (LANES, LANES), 1) // RWKV_HEAD
    bd_mask = r2 == c2

    def bd(t):
        return jnp.concatenate([jnp.where(m0, t, 0.0), jnp.where(m0, 0.0, t)], axis=0)

    ys = []
    for pr in range(RWKV_PAIRS):
        sl = slice(pr * LANES, (pr + 1) * LANES)
        rp, kp, vp, kkp, ap, lwp = r[:, sl], k[:, sl], v[:, sl], kk[:, sl], a[:, sl], lw[:, sl]
        c = _mm_exact_lhs(tril, lwp, _NN)
        eg, egp, eni = jnp.exp(c), jnp.exp(c - lwp), jnp.exp(-c)
        at, bt, kt, rt = -kkp * egp, kkp * ap * eni, kp * eni, rp * eg
        g_last = jnp.exp(c[L - 1:L, :])
        gram = _mm2(jnp.concatenate([at, rt], axis=0),
                    jnp.concatenate([bd(bt), bd(kt)], axis=0), _NT)
        n_ab = jnp.where(strict, gram[0:L, 0:LANES], 0.0)
        a_ak = jnp.where(strict, gram[0:L, LANES:], 0.0)
        a_rb = jnp.where(incl, gram[L:, 0:LANES], 0.0)
        a_rk = jnp.where(incl, gram[L:, LANES:], 0.0)
        tinv, m = eye_pair + n_ab, n_ab
        for _ in range(5):
            m = _mm2(m, bd(m), _NN)
            tinv = tinv + _mm2(m, bd(tinv), _NN)
        s0 = state_ref[pr]
        pq = _mm2(at, s0, _NT) + _mm2(a_ak, bd(vp), _NN)
        u = _mm2(tinv, bd(pq), _NN)
        y = (_mm2(rt, s0, _NT)
             + _mm2(jnp.concatenate([a_rb, a_rk], axis=1),
                    jnp.concatenate([bd(u), bd(vp)], axis=0), _NN))
        upd = _mm2(jnp.concatenate([u, vp], axis=0), jnp.concatenate([bt, kt], axis=0), _TN)
        state_ref[pr] = (s0 + jnp.where(bd_mask, upd, 0.0)) * g_last
        ys.append(y)
    y = jnp.concatenate(ys, axis=1)

    inv_n = 1.0 / RWKV_HEAD
    mean = head_sum(y) * inv_n
    yc = y - mean
    var = head_sum(yc * yc) * inv_n
    y = yc * lax.rsqrt(var + RWKV_GN_EPS) * lng_ref[...] + lnb_ref[...]
    y = y + head_sum(r * k * rk_ref[...]) * v
    o_ref[0] = y * g


def _rwkv_pallas(p_r, mu, w0, w_up, a0, a_up, g_up, k_k, k_a, r_k, ln_g, ln_b):
    B, S, C = p_r.shape
    L, W = RWKV_CHUNK, RWKV_WIDTH
    bf16 = jnp.bfloat16
    vecw = lambda t: t.reshape(1, W)
    ones_bd = jnp.kron(jnp.eye(RWKV_HEADS, dtype=bf16), jnp.ones((RWKV_HEAD, RWKV_HEAD), bf16))
    full = lambda a: pl.BlockSpec(a.shape, lambda b, c: (0,) * a.ndim)
    args = [mu.reshape(1, C), vecw(w0), w_up.astype(bf16), vecw(a0), a_up.astype(bf16),
            g_up.astype(bf16), vecw(k_k), vecw(k_a), vecw(r_k), vecw(ln_g), vecw(ln_b), ones_bd]
    return pl.pallas_call(
        _rwkv_kernel,
        grid=(B, S // L),
        in_specs=[pl.BlockSpec((1, L, C), lambda b, c: (b, c, 0)),
                  pl.BlockSpec((1, 8, C), lambda b, c: (b, jnp.maximum(c * (L // 8) - 1, 0), 0))]
                 + [full(a) for a in args],
        out_specs=pl.BlockSpec((1, L, W), lambda b, c: (b, c, 0)),
        out_shape=jax.ShapeDtypeStruct((B, S, W), jnp.float32),
        scratch_shapes=[pltpu.VMEM((RWKV_PAIRS, LANES, LANES), jnp.float32)],
        compiler_params=pltpu.CompilerParams(dimension_semantics=("arbitrary", "arbitrary")),
        name="rwkv7",
    )(p_r, p_r, *args)


def kernel(x, l0_w_in, l0_rwkv_mu, l0_rwkv_w0, l0_rwkv_w_up, l0_rwkv_a0, l0_rwkv_a_up,
           l0_rwkv_g_up, l0_rwkv_k_k, l0_rwkv_k_a, l0_rwkv_r_k, l0_rwkv_ln_g, l0_rwkv_ln_b,
           l0_mlstm_conv_w, l0_mlstm_conv_b, l0_mlstm_ig_b, l0_mlstm_fg_b,
           l0_mlstm_hn_g, l0_mlstm_hn_b, l0_w_out, l0_ln1_g, l0_ln1_b,
           l0_peer_wq, l0_peer_k1, l0_peer_k2, l0_peer_u, l0_peer_v, l0_ln2_g, l0_ln2_b,
           l1_w_in, l1_fox_qn_g, l1_fox_kn_g, l1_fox_f_b, l1_w_out, l1_ln1_g, l1_ln1_b,
           l1_peer_wq, l1_peer_k1, l1_peer_k2, l1_peer_u, l1_peer_v, l1_ln2_g, l1_ln2_b):
    B, S, D = x.shape
    n = B * S
    as2 = lambda t: t.reshape(n, D)
    as3 = lambda t: t.reshape(B, S, D)

    y = _even_mixer(x, l0_w_in, l0_rwkv_mu, l0_rwkv_w0, l0_rwkv_w_up, l0_rwkv_a0, l0_rwkv_a_up,
                    l0_rwkv_g_up, l0_rwkv_k_k, l0_rwkv_k_a, l0_rwkv_r_k, l0_rwkv_ln_g,
                    l0_rwkv_ln_b, l0_mlstm_conv_w, l0_mlstm_conv_b, l0_mlstm_ig_b,
                    l0_mlstm_fg_b, l0_mlstm_hn_g, l0_mlstm_hn_b, l0_w_out)
    x = as3(_resid_ln(as2(x), as2(y), l0_ln1_g, l0_ln1_b))
    y = _peer_pallas(as2(x), l0_peer_wq, l0_peer_k1, l0_peer_k2, l0_peer_u, l0_peer_v)
    x = as3(_resid_ln(as2(x), y, l0_ln2_g, l0_ln2_b))
    y = _odd_mixer(x, l1_w_in, l1_fox_qn_g, l1_fox_kn_g, l1_fox_f_b, l1_w_out)
    x = as3(_resid_ln(as2(x), as2(y), l1_ln1_g, l1_ln1_b))
    y = _peer_pallas(as2(x), l1_peer_wq, l1_peer_k1, l1_peer_k2, l1_peer_u, l1_peer_v)
    x = as3(_resid_ln(as2(x), y, l1_ln2_g, l1_ln2_b))
    return x
```

```python
import functools

import jax
import jax.numpy as jnp
from jax import lax
from jax.experimental import pallas as pl
from jax.experimental.pallas import tpu as pltpu

D_MODEL = 1024
DEPTH = 2
DN_ALPHA = (2.0 * DEPTH) ** 0.25
LN_EPS = 1e-5

RWKV_WIDTH = D_MODEL // 2
RWKV_HEAD = 64
RWKV_HEADS = RWKV_WIDTH // RWKV_HEAD
RWKV_W_RANK = 64
RWKV_A_RANK = 64
RWKV_G_RANK = 128
RWKV_GN_EPS = 1e-5 * RWKV_HEAD
RWKV_COLS = 3 * RWKV_WIDTH + RWKV_W_RANK + RWKV_A_RANK + RWKV_G_RANK

MLSTM_WIDTH = D_MODEL // 2
MLSTM_HEAD = 128
MLSTM_HEADS = MLSTM_WIDTH // MLSTM_HEAD
MLSTM_CHUNK = 64

FOX_HEAD = 64
FOX_HEADS = D_MODEL // FOX_HEAD
FOX_WIDTH = FOX_HEADS * FOX_HEAD
FOX_QBLOCK = 128

PEER_HEADS = 8
PEER_NKEYS = 128
PEER_TOPK = 16
PEER_DKEY = 256
PEER_DHALF = PEER_DKEY // 2
PEER_TOKBLOCK = 128


def _split_cols(p, sizes):
    out, start = [], 0
    for s in sizes:
        out.append(p[..., start:start + s])
        start += s
    return out


def _resid_ln_kernel(x_ref, y_ref, g_ref, b_ref, o_ref):
    z = DN_ALPHA * x_ref[...] + y_ref[...]
    mu = jnp.mean(z, axis=-1, keepdims=True)
    zc = z - mu
    var = jnp.mean(zc * zc, axis=-1, keepdims=True)
    o_ref[...] = zc * lax.rsqrt(var + LN_EPS) * g_ref[...] + b_ref[...]


def _resid_ln(x2, y2, g, b, tm=512):
    n, d = x2.shape
    row = pl.BlockSpec((tm, d), lambda i: (i, 0))
    vec = pl.BlockSpec((1, d), lambda i: (0, 0))
    return pl.pallas_call(
        _resid_ln_kernel,
        grid=(n // tm,),
        in_specs=[row, row, vec, vec],
        out_specs=row,
        out_shape=jax.ShapeDtypeStruct((n, d), jnp.float32),
        name="resid_ln",
    )(x2, y2, g.reshape(1, d), b.reshape(1, d))


VMEM_LIMIT_BYTES = 56 * 1024 * 1024


def _proj_kernel(x_ref, w_ref, *o_refs):
    xb = x_ref[...].astype(jnp.bfloat16)
    start = 0
    for o_ref in o_refs:
        width = o_ref.shape[1]
        o_ref[...] = jnp.dot(xb, w_ref[:, start:start + width], preferred_element_type=jnp.float32)
        start += width


def _proj(x2, w_bf16, widths, tm=256):
    n, d = x2.shape
    assert sum(widths) == w_bf16.shape[1]
    return pl.pallas_call(
        _proj_kernel,
        grid=(n // tm,),
        in_specs=[pl.BlockSpec((tm, d), lambda i: (i, 0)),
                  pl.BlockSpec(w_bf16.shape, lambda i: (0, 0))],
        out_specs=[pl.BlockSpec((tm, w), lambda i: (i, 0)) for w in widths],
        out_shape=[jax.ShapeDtypeStruct((n, w), jnp.float32) for w in widths],
        compiler_params=pltpu.CompilerParams(vmem_limit_bytes=VMEM_LIMIT_BYTES),
        name="in_proj",
    )(x2, w_bf16)


def _out_proj_ln_kernel(*refs, n_parts):
    y_refs, (w_ref, x_ref, g_ref, b_ref, o_ref) = refs[:n_parts], refs[n_parts:]
    acc, start = None, 0
    for y_ref in y_refs:
        width = y_ref.shape[1]
        part = jnp.dot(y_ref[...].astype(jnp.bfloat16), w_ref[start:start + width, :],
                       preferred_element_type=jnp.float32)
        acc = part if acc is None else acc + part
        start += width
    z = DN_ALPHA * x_ref[...] + acc
    mu = jnp.mean(z, axis=-1, keepdims=True)
    zc = z - mu
    var = jnp.mean(zc * zc, axis=-1, keepdims=True)
    o_ref[...] = zc * lax.rsqrt(var + LN_EPS) * g_ref[...] + b_ref[...]


def _out_proj_ln(ys, w_bf16, x2, g, b, tm=256):
    n, d = x2.shape
    row = lambda width: pl.BlockSpec((tm, width), lambda i: (i, 0))
    vec = pl.BlockSpec((1, d), lambda i: (0, 0))
    return pl.pallas_call(
        functools.partial(_out_proj_ln_kernel, n_parts=len(ys)),
        grid=(n // tm,),
        in_specs=[row(y.shape[1]) for y in ys]
                 + [pl.BlockSpec(w_bf16.shape, lambda i: (0, 0)), row(d), vec, vec],
        out_specs=row(d),
        out_shape=jax.ShapeDtypeStruct((n, d), jnp.float32),
        name="out_proj_ln",
    )(*ys, w_bf16, x2, g.reshape(1, d), b.reshape(1, d))


def _head_norm(y, g, b, eps):
    mu = jnp.mean(y, -1, keepdims=True)
    var = jnp.mean(jnp.square(y - mu), -1, keepdims=True)
    return (y - mu) * lax.rsqrt(var + eps) * g + b


def _rms_norm(y, g):
    return y * lax.rsqrt(jnp.mean(y * y, -1, keepdims=True) + 1e-6) * g


def _token_shift(z):
    return jnp.pad(z, ((0, 0), (1, 0), (0, 0)))[:, :-1]


def _causal_conv(z, w, b):
    c = z.shape[-1]
    out = lax.conv_general_dilated(z, w[:, None, :], window_strides=(1,),
                                   padding=((w.shape[0] - 1, 0),),
                                   dimension_numbers=('NWC', 'WIO', 'NWC'),
                                   feature_group_count=c)
    return out + b


def _rwkv7_mix(p, mu, w0, w_up, a0, a_up, g_up, k_k, k_a, r_k, ln_g, ln_b):
    B, S, _ = p.shape
    H, N = RWKV_HEADS, RWKV_HEAD
    p = p + (_token_shift(p) - p) * mu
    r, k, v, xw, xa, xg = _split_cols(p, (RWKV_WIDTH, RWKV_WIDTH, RWKV_WIDTH,
                                          RWKV_W_RANK, RWKV_A_RANK, RWKV_G_RANK))
    log_w = -jnp.exp(-jax.nn.softplus(-(w0 + jnp.tanh(xw) @ w_up)) - 0.5)
    a = jax.nn.sigmoid(a0 + xa @ a_up)
    g = jax.nn.sigmoid(xg) @ g_up
    heads = lambda t: t.reshape(B, S, H, N)
    kk = heads(k * k_k)
    kk = kk / jnp.maximum(jnp.sqrt(jnp.sum(kk * kk, -1, keepdims=True)), 1e-12)
    k = k * (1.0 + (a - 1.0) * k_a)
    r_h, k_h, v_h, a_h = heads(r), heads(k), heads(v), heads(a)
    w_h = jnp.exp(heads(log_w))

    def step(state, inp):
        r_t, w_t, k_t, v_t, kk_t, a_t = inp
        s_kk = jnp.einsum('bhvk,bhk->bhv', state, kk_t)
        state = (state * w_t[:, :, None, :]
                 - s_kk[..., None] * (kk_t * a_t)[:, :, None, :]
                 + v_t[..., None] * k_t[:, :, None, :])
        return state, jnp.einsum('bhvk,bhk->bhv', state, r_t)

    seq_first = lambda t: jnp.moveaxis(t, 1, 0)
    state0 = jnp.zeros((B, H, N, N), jnp.float32)
    _, y = lax.scan(step, state0, (seq_first(r_h), seq_first(w_h), seq_first(k_h),
                                   seq_first(v_h), seq_first(kk), seq_first(a_h)))
    y = jnp.moveaxis(y, 0, 1)
    y = _head_norm(y, ln_g, ln_b, RWKV_GN_EPS)
    y = y + jnp.sum(r_h * k_h * r_k, -1, keepdims=True) * v_h
    return y.reshape(B, S, RWKV_WIDTH) * g


def _mlstm_chunkwise(q, k, v, ig, lf):
    B, S, H, D = q.shape
    L = MLSTM_CHUNK
    NC = S // L

    def to_chunks(t):
        t = t.reshape((B, NC, L, H) + t.shape[3:])
        return jnp.moveaxis(t, (1, 3), (0, 2))

    causal = jnp.tril(jnp.ones((L, L), dtype=bool))

    def body(carry, inp):
        C, n, m = carry
        qc, kc, vc, igc, lfc = inp
        b = jnp.cumsum(lfc, axis=-1)
        dmat = jnp.where(causal, b[..., :, None] - b[..., None, :] + igc[..., None, :], -jnp.inf)
        inter = b + m[..., None]
        m_t = jnp.maximum(inter, jnp.max(dmat, -1))
        weights = jnp.exp(dmat - m_t[..., None])
        sc = jnp.einsum('bhtd,bhsd->bhts', qc, kc) * weights
        carry_in = jnp.exp(inter - m_t)
        num = (jnp.einsum('bhts,bhsd->bhtd', sc, vc)
               + carry_in[..., None] * jnp.einsum('bhtk,bhkv->bhtv', qc, C))
        den = jnp.sum(sc, -1) + carry_in * jnp.einsum('bhtk,bhk->bht', qc, n)
        h = num / jnp.maximum(jnp.abs(den), jnp.exp(-m_t))[..., None]
        b_last = b[..., -1]
        gs = b_last[..., None] - b + igc
        m_new = jnp.maximum(b_last + m, jnp.max(gs, -1))
        ws = jnp.exp(gs - m_new[..., None])
        keep = jnp.exp(b_last + m - m_new)
        C = keep[..., None, None] * C + jnp.einsum('bhs,bhsk,bhsv->bhkv', ws, kc, vc)
        n = keep[..., None] * n + jnp.einsum('bhs,bhsk->bhk', ws, kc)
        return (C, n, m_new), h

    f32 = jnp.float32
    init = (jnp.zeros((B, H, D, D), f32), jnp.zeros((B, H, D), f32), jnp.zeros((B, H), f32))
    _, h = lax.scan(body, init, (to_chunks(q), to_chunks(k), to_chunks(v),
                                 to_chunks(ig), to_chunks(lf)))
    return jnp.moveaxis(h, (0, 2), (1, 3)).reshape(B, S, H, D)


def _mlstm_mix(p, conv_w, conv_b, ig_b, fg_b, hn_g, hn_b):
    B, S, _ = p.shape
    H, N = MLSTM_HEADS, MLSTM_HEAD
    q, k, v, o, ig, fg = _split_cols(p, (MLSTM_WIDTH, MLSTM_WIDTH, MLSTM_WIDTH, MLSTM_WIDTH, H, H))
    qk = jax.nn.silu(_causal_conv(jnp.concatenate([q, k], -1), conv_w, conv_b))
    q, k = qk[..., :MLSTM_WIDTH], qk[..., MLSTM_WIDTH:]
    heads = lambda t: t.reshape(B, S, H, N)
    q = heads(q) * (N ** -0.5)
    ig = ig + ig_b
    lf = jax.nn.log_sigmoid(fg + fg_b)
    h = _mlstm_chunkwise(q, heads(k), heads(v), ig, lf)
    h = _head_norm(h, hn_g, hn_b, LN_EPS).reshape(B, S, MLSTM_WIDTH)
    return jax.nn.sigmoid(o) * h


def _even_mixer(x, w_in, mu, w0, w_up, a0, a_up, g_up, k_k, k_a, r_k, rln_g, rln_b,
                conv_w, conv_b, ig_b, fg_b, hn_g, hn_b, w_out):
    p = x @ w_in
    y_a = _rwkv_pallas(p[..., :RWKV_COLS], mu, w0, w_up, a0, a_up, g_up, k_k, k_a, r_k, rln_g, rln_b)
    y_b = _mlstm_mix(p[..., RWKV_COLS:], conv_w, conv_b, ig_b, fg_b, hn_g, hn_b)
    return jnp.concatenate([y_a, y_b], -1) @ w_out


def _fox_attention(q, k, v, logf):
    B, S, H, D = q.shape
    NB = S // FOX_QBLOCK
    c = jnp.cumsum(logf, axis=1)
    c_key = jnp.transpose(c, (0, 2, 1))[:, :, None, :]
    q_blocks = jnp.moveaxis(q.reshape(B, NB, FOX_QBLOCK, H, D), 1, 0)
    c_blocks = jnp.moveaxis(c.reshape(B, NB, FOX_QBLOCK, H), 1, 0)
    key_pos = jnp.arange(S)
    scale = D ** -0.5

    def one_block(args):
        qb, cb, blk = args
        q_pos = blk * FOX_QBLOCK + jnp.arange(FOX_QBLOCK)
        logits = jnp.einsum('bqhd,bkhd->bhqk', qb, k) * scale
        logits = logits + jnp.transpose(cb, (0, 2, 1))[..., None] - c_key
        logits = jnp.where(key_pos[None, :] <= q_pos[:, None], logits, -jnp.inf)
        probs = jax.nn.softmax(logits, axis=-1)
        return jnp.einsum('bhqk,bkhd->bqhd', probs, v)

    out = lax.map(one_block, (q_blocks, c_blocks, jnp.arange(NB)))
    return jnp.moveaxis(out, 0, 1).reshape(B, S, H, D)


def _odd_mixer(x, w_in, qn_g, kn_g, f_b, w_out):
    B, S, _ = x.shape
    p = x @ w_in
    q, k, v, og, fl = _split_cols(p, (FOX_WIDTH, FOX_WIDTH, FOX_WIDTH, FOX_WIDTH, FOX_HEADS))
    heads = lambda t: t.reshape(B, S, FOX_HEADS, FOX_HEAD)
    q = _rms_norm(heads(q), qn_g)
    k = _rms_norm(heads(k), kn_g)
    logf = jax.nn.log_sigmoid(fl + f_b)
    o = _fox_attention(q, k, heads(v), logf).reshape(B, S, FOX_WIDTH)
    return (jax.nn.sigmoid(og) * o) @ w_out


def _peer_ffn(x, w_q, sub_k1, sub_k2, u_tab, v_tab):
    B, S, Dm = x.shape
    K = PEER_TOPK
    xt = x.reshape((B * S) // PEER_TOKBLOCK, PEER_TOKBLOCK, Dm)

    def block(xb):
        T = xb.shape[0]
        q = (xb @ w_q).reshape(T, PEER_HEADS, 2, PEER_DHALF)
        s1 = jnp.einsum('thd,hnd->thn', q[:, :, 0], sub_k1)
        s2 = jnp.einsum('thd,hnd->thn', q[:, :, 1], sub_k2)
        v1, i1 = lax.top_k(s1, K)
        v2, i2 = lax.top_k(s2, K)
        cand = (v1[..., :, None] + v2[..., None, :]).reshape(T, PEER_HEADS, K * K)
        sc, ci = lax.top_k(cand, K)
        e1 = jnp.take_along_axis(i1, ci // K, axis=-1)
        e2 = jnp.take_along_axis(i2, ci % K, axis=-1)
        eid = e1 * PEER_NKEYS + e2
        gate = jax.nn.softmax(sc, axis=-1)
        act = jax.nn.gelu(jnp.einsum('td,thkd->thk', xb, u_tab[eid]), approximate=False)
        return jnp.einsum('thk,thkd->td', gate * act, v_tab[eid])

    return lax.map(block, xt).reshape(B, S, Dm)


PEER_EXPERTS = PEER_NKEYS * PEER_NKEYS
PEER_TOK_TILE = 256
PEER_EXPERT_CHUNK = 1024
LANES = 128
MXU_DIM = 256
INV_SQRT2 = 0.7071067811865476
_NT = (((1,), (1,)), ((), ()))


def _topk_desc(work, k):
    vals = []
    for _ in range(k):
        m = jnp.max(work, axis=0, keepdims=True)
        vals.append(m)
        work = jnp.where(work >= m, -jnp.inf, work)
    return vals


def _peer_select_kernel(x_ref, wqt_ref, k1_ref, k2_ref,
                        s1m_ref, a1_ref, s2m_ref, a2_ref, tau_ref, qt_ref):
    f32, bf16 = jnp.float32, jnp.bfloat16
    K = PEER_TOPK
    qt_ref[...] = lax.dot_general(wqt_ref[...], x_ref[...], _NT, preferred_element_type=f32)
    n_groups = x_ref.shape[0] // LANES

    def head(h, carry):
        base = pl.multiple_of(h * PEER_DKEY, PEER_DKEY)
        q1 = qt_ref[pl.ds(base, PEER_DHALF), :].astype(bf16)
        q2 = qt_ref[pl.ds(base + PEER_DHALF, PEER_DHALF), :].astype(bf16)
        s1 = jnp.dot(k1_ref[h], q1, preferred_element_type=f32)
        s2 = jnp.dot(k2_ref[h], q2, preferred_element_type=f32)
        for c in range(n_groups):
            sl = slice(c * LANES, (c + 1) * LANES)
            s1c, s2c = s1[:, sl], s2[:, sl]
            v1 = _topk_desc(s1c, K)
            v2 = _topk_desc(s2c, K)
            v2s = jnp.concatenate(v2, axis=0)
            cand = jnp.concatenate([v1[a] + v2s for a in range(K)], axis=0)
            vc = _topk_desc(cand, K)
            tau, cmax = vc[K - 1], vc[0]
            z = jnp.sum(jnp.where(cand >= tau, jnp.exp(cand - cmax), 0.0), axis=0, keepdims=True)
            s1m = jnp.where(s1c >= v1[K - 1], s1c, -jnp.inf)
            s2m = jnp.where(s2c >= v2[K - 1], s2c, -jnp.inf)
            s1m_ref[h, :, sl] = s1m
            a1_ref[h, :, sl] = jnp.exp(s1m - v1[0])
            s2m_ref[h, :, sl] = s2m
            a2_ref[h, :, sl] = jnp.exp(s2m - v2[0]) / z
            tau_ref[h, :, sl] = tau
        return carry

    lax.fori_loop(0, PEER_HEADS, head, 0)


def _peer_select(xb, wqt, k1, k2):
    n, d = xb.shape
    T = PEER_TOK_TILE
    H, NK = PEER_HEADS, PEER_NKEYS
    full = lambda shape: pl.BlockSpec(shape, lambda i: (0,) * len(shape))
    tok3 = lambda rows: pl.BlockSpec((H, rows, T), lambda i: (0, 0, i))
    f32 = jnp.float32
    return pl.pallas_call(
        _peer_select_kernel,
        grid=(n // T,),
        in_specs=[pl.BlockSpec((T, d), lambda i: (i, 0)), full(wqt.shape), full(k1.shape), full(k2.shape)],
        out_specs=[tok3(NK), tok3(NK), tok3(NK), tok3(NK), tok3(1)],
        out_shape=[jax.ShapeDtypeStruct((H, NK, n), f32)] * 4 + [jax.ShapeDtypeStruct((H, 1, n), f32)],
        scratch_shapes=[pltpu.VMEM((H * PEER_DKEY, T), f32)],
        name="peer_select",
    )(xb, wqt, k1, k2)


def _peer_expert_kernel(x_ref, u_ref, vt_ref, s1m_ref, a1_ref, s2m_ref, a2_ref, tau_ref,
                        o_ref, acc_ref):
    f32, bf16 = jnp.float32, jnp.bfloat16
    j = pl.program_id(1)

    @pl.when(j == 0)
    def _():
        acc_ref[...] = jnp.zeros_like(acc_ref)

    xb = x_ref[...]
    T = xb.shape[0]
    for p in range(PEER_EXPERT_CHUNK // MXU_DIM):
        rows = slice(p * MXU_DIM, (p + 1) * MXU_DIM)
        h = lax.dot_general(u_ref[rows, :], xb, _NT, preferred_element_type=f32)
        ws = []
        for half in range(MXU_DIM // PEER_NKEYS):
            r = p * (MXU_DIM // PEER_NKEYS) + half
            g = jnp.zeros((PEER_NKEYS, T), f32)
            for hd in range(PEER_HEADS):
                t = s1m_ref[hd, r:r + 1, :] + s2m_ref[hd]
                g = g + jnp.where(t >= tau_ref[hd], a1_ref[hd, r:r + 1, :] * a2_ref[hd], 0.0)
            hh = h[half * PEER_NKEYS:(half + 1) * PEER_NKEYS]
            ws.append((g * (0.5 * hh * (1.0 + lax.erf(hh * INV_SQRT2)))).astype(bf16))
        w = jnp.concatenate(ws, axis=0)
        acc_ref[...] += jnp.dot(vt_ref[:, rows], w, preferred_element_type=f32)

    @pl.when(j == pl.num_programs(1) - 1)
    def _():
        o_ref[...] = acc_ref[...].T


def _peer_experts(xb, ub, vtb, s1m, a1, s2m, a2, tau):
    n, d = xb.shape
    T, CE = PEER_TOK_TILE, PEER_EXPERT_CHUNK
    H, NK = PEER_HEADS, PEER_NKEYS
    e1_per_chunk = CE // NK
    return pl.pallas_call(
        _peer_expert_kernel,
        grid=(n // T, PEER_EXPERTS // CE),
        in_specs=[
            pl.BlockSpec((T, d), lambda i, j: (i, 0)),
            pl.BlockSpec((CE, d), lambda i, j: (j, 0)),
            pl.BlockSpec((d, CE), lambda i, j: (0, j)),
            pl.BlockSpec((H, e1_per_chunk, T), lambda i, j: (0, j, i)),
            pl.BlockSpec((H, e1_per_chunk, T), lambda i, j: (0, j, i)),
            pl.BlockSpec((H, NK, T), lambda i, j: (0, 0, i)),
            pl.BlockSpec((H, NK, T), lambda i, j: (0, 0, i)),
            pl.BlockSpec((H, 1, T), lambda i, j: (0, 0, i)),
        ],
        out_specs=pl.BlockSpec((T, d), lambda i, j: (i, 0)),
        out_shape=jax.ShapeDtypeStruct((n, d), jnp.float32),
        scratch_shapes=[pltpu.VMEM((d, T), jnp.float32)],
        compiler_params=pltpu.CompilerParams(dimension_semantics=("arbitrary", "arbitrary")),
        name="peer_experts",
    )(xb, ub, vtb, s1m, a1, s2m, a2, tau)


def _peer_pallas(x2, w_q, sub_k1, sub_k2, u_tab, v_tab):
    bf16 = jnp.bfloat16
    xb = x2.astype(bf16)
    sel = _peer_select(xb, w_q.T.astype(bf16), sub_k1.astype(bf16), sub_k2.astype(bf16))
    return _peer_experts(xb, u_tab.astype(bf16), v_tab.T.astype(bf16), *sel)


RWKV_CHUNK = 64
RWKV_PAIR = LANES // RWKV_HEAD
RWKV_PAIRS = RWKV_HEADS // RWKV_PAIR
_NN = (((1,), (0,)), ((), ()))
_TN = (((0,), (0,)), ((), ()))


def _split_bf16(a):
    hi = a.astype(jnp.bfloat16)
    lo = (a - hi.astype(jnp.float32)).astype(jnp.bfloat16)
    return hi, lo


def _mm2(a, b, dims):
    d = lambda u, v: lax.dot_general(u, v, dims, preferred_element_type=jnp.float32)
    (ah, al), (bh, bl) = _split_bf16(a), _split_bf16(b)
    return d(ah, bh) + d(ah, bl) + d(al, bh)


def _mm_exact_rhs(a, b_exact, dims, terms=2):
    d = lambda u: lax.dot_general(u, b_exact, dims, preferred_element_type=jnp.float32)
    out, rem = None, a
    for _ in range(terms):
        piece = rem.astype(jnp.bfloat16)
        rem = rem - piece.astype(jnp.float32)
        out = d(piece) if out is None else out + d(piece)
    return out


def _mm_exact_lhs(a_exact, b, dims, terms=3):
    d = lambda v: lax.dot_general(a_exact, v, dims, preferred_element_type=jnp.float32)
    out, rem = None, b
    for _ in range(terms):
        piece = rem.astype(jnp.bfloat16)
        rem = rem - piece.astype(jnp.float32)
        out = d(piece) if out is None else out + d(piece)
    return out


def _softplus(z):
    return jnp.maximum(z, 0.0) + jnp.log1p(jnp.exp(-jnp.abs(z)))


def _rwkv_kernel(p_ref, prev_ref, mu_ref, w0_ref, wup_ref, a0_ref, aup_ref, gup_ref,
                 kk_ref, ka_ref, rk_ref, lng_ref, lnb_ref, ones_ref,
                 o_ref, state_ref):
    f32, bf16 = jnp.float32, jnp.bfloat16
    L, W = RWKV_CHUNK, RWKV_WIDTH
    c_idx = pl.program_id(1)

    @pl.when(c_idx == 0)
    def _():
        state_ref[...] = jnp.zeros_like(state_ref)

    p = p_ref[0]
    row = lax.broadcasted_iota(jnp.int32, p.shape, 0)
    prev_row = jnp.where(c_idx == 0, 0.0, prev_ref[0, 7:8, :])
    shifted = jnp.where(row == 0, prev_row, pltpu.roll(p, 1, axis=0))
    p = p + (shifted - p) * mu_ref[...]
    r, k, v = p[:, 0:W], p[:, W:2 * W], p[:, 2 * W:3 * W]
    o = 3 * W
    xw = p[:, o:o + RWKV_W_RANK]
    xa = p[:, o + RWKV_W_RANK:o + RWKV_W_RANK + RWKV_A_RANK]
    xg = p[:, o + RWKV_W_RANK + RWKV_A_RANK:]
    dotd = lambda u, m: jnp.dot(u.astype(bf16), m, preferred_element_type=f32)
    lw = -jnp.exp(-_softplus(-(w0_ref[...] + dotd(jnp.tanh(xw), wup_ref[...]))) - 0.5)
    a = jax.nn.sigmoid(a0_ref[...] + dotd(xa, aup_ref[...]))
    g = dotd(jax.nn.sigmoid(xg), gup_ref[...])
    ones_bd = ones_ref[...]
    head_sum = lambda t: _mm_exact_rhs(t, ones_bd, _NN)
    kk = k * kk_ref[...]
    kk = kk / jnp.maximum(jnp.sqrt(head_sum(kk * kk)), 1e-12)
    k = k * (1.0 + (a - 1.0) * ka_ref[...])

    lane = lax.broadcasted_iota(jnp.int32, (L, LANES), 1)
    trow = lax.broadcasted_iota(jnp.int32, (L, LANES), 0)
    s_in = lane % RWKV_HEAD
    strict, incl = s_in < trow, s_in <= trow
    eye_pair = (s_in == trow).astype(f32)
    m0 = lane < RWKV_HEAD
    tril = (lax.broadcasted_iota(jnp.int32, (L, L), 1)
            <= lax.broadcasted_iota(jnp.int32, (L, L), 0)).astype(bf16)
    r2 = lax.broadcasted_iota(jnp.int32, (LANES, LANES), 0) // RWKV_HEAD
    c2 = lax.broadcasted_iota(jnp.int32, (LANES, LANES), 1) // RWKV_HEAD
    bd_mask = r2 == c2

    def bd(t):
        return jnp.concatenate([jnp.where(m0, t, 0.0), jnp.where(m0, 0.0, t)], axis=0)

    ys = []
    for pr in range(RWKV_PAIRS):
        sl = slice(pr * LANES, (pr + 1) * LANES)
        rp, kp, vp, kkp, ap, lwp = r[:, sl], k[:, sl], v[:, sl], kk[:, sl], a[:, sl], lw[:, sl]
        c = _mm_exact_lhs(tril, lwp, _NN)
        eg, egp, eni = jnp.exp(c), jnp.exp(c - lwp), jnp.exp(-c)
        at, bt, kt, rt = -kkp * egp, kkp * ap * eni, kp * eni, rp * eg
        g_last = jnp.exp(c[L - 1:L, :])
        gram = _mm2(jnp.concatenate([at, rt], axis=0),
                    jnp.concatenate([bd(bt), bd(kt)], axis=0), _NT)
        n_ab = jnp.where(strict, gram[0:L, 0:LANES], 0.0)
        a_ak = jnp.where(strict, gram[0:L, LANES:], 0.0)
        a_rb = jnp.where(incl, gram[L:, 0:LANES], 0.0)
        a_rk = jnp.where(incl, gram[L:, LANES:], 0.0)
        tinv, m = eye_pair + n_ab, n_ab
        for _ in range(5):
            m = _mm2(m, bd(m), _NN)
            tinv = tinv + _mm2(m, bd(tinv), _NN)
        s0 = state_ref[pr]
        pq = _mm2(at, s0, _NT) + _mm2(a_ak, bd(vp), _NN)
        u = _mm2(tinv, bd(pq), _NN)
        y = (_mm2(rt, s0, _NT)
             + _mm2(jnp.concatenate([a_rb, a_rk], axis=1),
                    jnp.concatenate([bd(u), bd(vp)], axis=0), _NN))
        upd = _mm2(jnp.concatenate([u, vp], axis=0), jnp.concatenate([bt, kt], axis=0), _TN)
        state_ref[pr] = (s0 + jnp.where(bd_mask, upd, 0.0)) * g_last
        ys.append(y)
    y = jnp.concatenate(ys, axis=1)

    inv_n = 1.0 / RWKV_HEAD
    mean = head_sum(y) * inv_n
    yc = y - mean
    var = head_sum(yc * yc) * inv_n
    y = yc * lax.rsqrt(var + RWKV_GN_EPS) * lng_ref[...] + lnb_ref[...]
    y = y + head_sum(r * k * rk_ref[...]) * v
    o_ref[0] = y * g


def _rwkv_pallas(p_r, mu, w0, w_up, a0, a_up, g_up, k_k, k_a, r_k, ln_g, ln_b):
    B, S, C = p_r.shape
    L, W = RWKV_CHUNK, RWKV_WIDTH
    bf16 = jnp.bfloat16
    vecw = lambda t: t.reshape(1, W)
    ones_bd = jnp.kron(jnp.eye(RWKV_HEADS, dtype=bf16), jnp.ones((RWKV_HEAD, RWKV_HEAD), bf16))
    full = lambda a: pl.BlockSpec(a.shape, lambda b, c: (0,) * a.ndim)
    args = [mu.reshape(1, C), vecw(w0), w_up.astype(bf16), vecw(a0), a_up.astype(bf16),
            g_up.astype(bf16), vecw(k_k), vecw(k_a), vecw(r_k), vecw(ln_g), vecw(ln_b), ones_bd]
    return pl.pallas_call(
        _rwkv_kernel,
        grid=(B, S // L),
        in_specs=[pl.BlockSpec((1, L, C), lambda b, c: (b, c, 0)),
                  pl.BlockSpec((1, 8, C), lambda b, c: (b, jnp.maximum(c * (L // 8) - 1, 0), 0))]
                 + [full(a) for a in args],
        out_specs=pl.BlockSpec((1, L, W), lambda b, c: (b, c, 0)),
        out_shape=jax.ShapeDtypeStruct((B, S, W), jnp.float32),
        scratch_shapes=[pltpu.VMEM((RWKV_PAIRS, LANES, LANES), jnp.float32)],
        compiler_params=pltpu.CompilerParams(dimension_semantics=("arbitrary", "arbitrary")),
        name="rwkv7",
    )(p_r, p_r, *args)


MLSTM_CONV = 4
MLSTM_GATE_LANES = LANES


def _mlstm_kernel(p_ref, prev_ref, cw_ref, cb_ref, igb_ref, fgb_ref, hng_ref, hnb_ref,
                  o_ref, c_ref, n_ref, m_ref):
    f32, bf16 = jnp.float32, jnp.bfloat16
    L, W, H, D = MLSTM_CHUNK, MLSTM_WIDTH, MLSTM_HEADS, MLSTM_HEAD
    c_idx = pl.program_id(1)

    @pl.when(c_idx == 0)
    def _():
        c_ref[...] = jnp.zeros_like(c_ref)
        n_ref[...] = jnp.zeros_like(n_ref)
        m_ref[...] = jnp.zeros_like(m_ref)

    p = p_ref[0]
    z = p[:, 0:2 * W]
    prev = jnp.where(c_idx == 0, 0.0, prev_ref[0, :, 0:2 * W])
    ext = jnp.concatenate([prev, z], axis=0)
    conv = cb_ref[...] + cw_ref[MLSTM_CONV - 1:MLSTM_CONV, :] * z
    for j in range(MLSTM_CONV - 1):
        d = MLSTM_CONV - 1 - j
        conv = conv + cw_ref[j:j + 1, :] * ext[8 - d:8 - d + L, :]
    qk = conv * jax.nn.sigmoid(conv)
    q_all, k_all = qk[:, 0:W] * (D ** -0.5), qk[:, W:2 * W]
    v_all, o_all = p[:, 2 * W:3 * W], p[:, 3 * W:4 * W]
    gates = p[:, 4 * W:]
    igl = gates + igb_ref[...]
    lfl = -_softplus(-(gates + fgb_ref[...]))
    tril = (lax.broadcasted_iota(jnp.int32, (L, L), 1)
            <= lax.broadcasted_iota(jnp.int32, (L, L), 0))
    bcum = _mm_exact_lhs(tril.astype(bf16), lfl, _NN)
    e_all = igl - pltpu.roll(bcum, LANES - H, axis=1)
    lane = lax.broadcasted_iota(jnp.int32, (L, LANES), 1)
    outs = []
    for h in range(H):
        hs = slice(h * D, (h + 1) * D)
        q, k, v = q_all[:, hs], k_all[:, hs], v_all[:, hs]
        b_col = bcum[:, H + h:H + h + 1]
        ig_col = igl[:, h:h + 1]
        onehot = (lane == h).astype(bf16)
        e_row = _mm_exact_lhs(onehot, e_all, _NT)
        m_prev = m_ref[h, 0:1, 0:1]
        dmat = jnp.where(tril, b_col + e_row, -jnp.inf)
        inter = b_col + m_prev
        m_t = jnp.maximum(inter, jnp.max(dmat, axis=-1, keepdims=True))
        weights = jnp.exp(dmat - m_t)
        sc = lax.dot_general(q.astype(bf16), k.astype(bf16), _NT, preferred_element_type=f32) * weights
        carry_in = jnp.exp(inter - m_t)
        c_prev, n_prev = c_ref[h], n_ref[h, 0:1, :]
        num = (jnp.dot(sc.astype(bf16), v.astype(bf16), preferred_element_type=f32)
               + carry_in * jnp.dot(q.astype(bf16), c_prev.astype(bf16), preferred_element_type=f32))
        den = (jnp.sum(sc, axis=-1, keepdims=True)
               + carry_in * jnp.sum(q * n_prev, axis=-1, keepdims=True))
        hval = num / jnp.maximum(jnp.abs(den), jnp.exp(-m_t))
        b_last = b_col[L - 1:L, :]
        gs = b_last - b_col + ig_col
        m_new = jnp.maximum(b_last + m_prev, jnp.max(gs, axis=0, keepdims=True))
        ws = jnp.exp(gs - m_new)
        keep = jnp.exp(b_last + m_prev - m_new)
        wk = ws * k
        c_ref[h] = keep * c_prev + lax.dot_general(wk.astype(bf16), v.astype(bf16), _TN,
                                                   preferred_element_type=f32)
        n_ref[h] = jnp.broadcast_to(keep * n_prev + jnp.sum(wk, axis=0, keepdims=True), (8, D))
        m_ref[h] = jnp.broadcast_to(m_new, (8, LANES))
        mu = jnp.mean(hval, axis=-1, keepdims=True)
        hc = hval - mu
        var = jnp.mean(hc * hc, axis=-1, keepdims=True)
        hn = hc * lax.rsqrt(var + LN_EPS) * hng_ref[:, hs] + hnb_ref[:, hs]
        outs.append(jax.nn.sigmoid(o_all[:, hs]) * hn)
    o_ref[0] = jnp.concatenate(outs, axis=1)


def _mlstm_pallas(p_m, conv_w, conv_b, ig_b, fg_b, hn_g, hn_b):
    B, S, C = p_m.shape
    L, W, H, D = MLSTM_CHUNK, MLSTM_WIDTH, MLSTM_HEADS, MLSTM_HEAD
    pad = lambda t, off: jnp.zeros((1, LANES), jnp.float32).at[0, off:off + H].set(t)
    args = [conv_w, conv_b.reshape(1, 2 * W), pad(ig_b, 0), pad(fg_b, H),
            hn_g.reshape(1, W), hn_b.reshape(1, W)]
    full = lambda a: pl.BlockSpec(a.shape, lambda b, c: (0,) * a.ndim)
    return pl.pallas_call(
        _mlstm_kernel,
        grid=(B, S // L),
        in_specs=[pl.BlockSpec((1, L, C), lambda b, c: (b, c, 0)),
                  pl.BlockSpec((1, 8, C), lambda b, c: (b, jnp.maximum(c * (L // 8) - 1, 0), 0))]
                 + [full(a) for a in args],
        out_specs=pl.BlockSpec((1, L, W), lambda b, c: (b, c, 0)),
        out_shape=jax.ShapeDtypeStruct((B, S, W), jnp.float32),
        scratch_shapes=[pltpu.VMEM((H, D, D), jnp.float32),
                        pltpu.VMEM((H, 8, D), jnp.float32),
                        pltpu.VMEM((H, 8, LANES), jnp.float32)],
        compiler_params=pltpu.CompilerParams(dimension_semantics=("arbitrary", "arbitrary")),
        name="mlstm",
    )(p_m, p_m, *args)


FOX_TILE = 256
FOX_AUG = LANES
FOX_GATE_LANES = LANES


def _fox_prep_kernel(p_ref, qg_ref, kg_ref, fb_ref, ones_ref, qa_ref, ka_ref, vt_ref, carry_ref):
    f32, bf16 = jnp.float32, jnp.bfloat16
    T, W, H, D = FOX_TILE, FOX_WIDTH, FOX_HEADS, FOX_HEAD
    i = pl.program_id(1)

    @pl.when(i == 0)
    def _():
        carry_ref[...] = jnp.zeros_like(carry_ref)

    p = p_ref[0]
    ones_bd = ones_ref[...]
    ms = lambda t: _mm_exact_rhs(t * t, ones_bd, _NN) * (1.0 / D)
    q = p[:, 0:W]
    q = q * lax.rsqrt(ms(q) + 1e-6) * qg_ref[...]
    k = p[:, W:2 * W]
    k = k * lax.rsqrt(ms(k) + 1e-6) * kg_ref[...] * (D ** -0.5)
    vt = p[:, 2 * W:3 * W].T.astype(bf16)
    logf = -_softplus(-(p[:, 4 * W:] + fb_ref[...]))
    tril = (lax.broadcasted_iota(jnp.int32, (T, T), 1)
            <= lax.broadcasted_iota(jnp.int32, (T, T), 0)).astype(bf16)
    c = _mm_exact_lhs(tril, logf, _NN) + carry_ref[0:1, :]
    carry_ref[...] = jnp.broadcast_to(c[T - 1:T, :], carry_ref.shape)
    c1 = c.astype(bf16).astype(f32)
    c2 = (c - c1).astype(bf16).astype(f32)
    c3 = (c - c1 - c2).astype(bf16).astype(f32)
    lane = lax.broadcasted_iota(jnp.int32, (T, D), 1)
    for h in range(H):
        hs = slice(h * D, (h + 1) * D)
        c1h, c2h, c3h = c1[:, h:h + 1], c2[:, h:h + 1], c3[:, h:h + 1]
        pieces = jnp.where(lane % 3 == 0, c1h, jnp.where(lane % 3 == 1, c2h, c3h))
        q_aug = jnp.where(lane < 3, 1.0, jnp.where(lane < 6, pieces, 0.0))
        k_aug = jnp.where(lane < 3, -pieces, jnp.where(lane < 6, 1.0, 0.0))
        qa_ref[0, h] = jnp.concatenate([q[:, hs], q_aug], axis=1).astype(bf16)
        ka_ref[0, h] = jnp.concatenate([k[:, hs], k_aug], axis=1).astype(bf16)
        vt_ref[0, h, 0] = vt[hs, :]


def _fox_prep(p_f, qn_g, kn_g, f_b):
    B, S, C = p_f.shape
    T, W, H, D = FOX_TILE, FOX_WIDTH, FOX_HEADS, FOX_HEAD
    bf16 = jnp.bfloat16
    ones_bd = jnp.kron(jnp.eye(H, dtype=bf16), jnp.ones((D, D), bf16))
    fb = jnp.zeros((1, FOX_GATE_LANES), jnp.float32).at[0, :H].set(f_b)
    args = [qn_g.reshape(1, W), kn_g.reshape(1, W), fb, ones_bd]
    full = lambda a: pl.BlockSpec(a.shape, lambda b, i: (0,) * a.ndim)
    return pl.pallas_call(
        _fox_prep_kernel,
        grid=(B, S // T),
        in_specs=[pl.BlockSpec((1, T, C), lambda b, i: (b, i, 0))] + [full(a) for a in args],
        out_specs=[pl.BlockSpec((1, H, T, FOX_AUG), lambda b, i: (b, 0, i, 0)),
                   pl.BlockSpec((1, H, T, FOX_AUG), lambda b, i: (b, 0, i, 0)),
                   pl.BlockSpec((1, H, 1, D, T), lambda b, i: (b, 0, i, 0, 0))],
        out_shape=[jax.ShapeDtypeStruct((B, H, S, FOX_AUG), bf16),
                   jax.ShapeDtypeStruct((B, H, S, FOX_AUG), bf16),
                   jax.ShapeDtypeStruct((B, H, S // T, D, T), bf16)],
        scratch_shapes=[pltpu.VMEM((8, FOX_GATE_LANES), jnp.float32)],
        compiler_params=pltpu.CompilerParams(dimension_semantics=("arbitrary", "arbitrary"),
                                             vmem_limit_bytes=VMEM_LIMIT_BYTES),
        name="fox_prep",
    )(p_f, *args)


def _fox_attn_kernel(qa_ref, ka_ref, vt_ref, og_ref, o_ref):
    f32, bf16 = jnp.float32, jnp.bfloat16
    T, D = FOX_TILE, FOX_HEAD
    i = pl.program_id(2)
    k_pos = lax.broadcasted_iota(jnp.int32, (T, T), 0)
    q_pos = lax.broadcasted_iota(jnp.int32, (T, T), 1)
    outs = []
    for j in range(LANES // D):
        qa = qa_ref[0, j]

        def kv_step(kb, carry):
            m, l, acc = carry
            ka = ka_ref[0, j, pl.ds(pl.multiple_of(kb * T, T), T), :]
            s = lax.dot_general(ka, qa, _NT, preferred_element_type=f32)
            s = jnp.where(k_pos + kb * T <= q_pos + i * T, s, -jnp.inf)
            m_new = jnp.maximum(m, jnp.max(s, axis=0, keepdims=True))
            alpha = jnp.exp(m - m_new)
            pr = jnp.exp(s - m_new)
            l = l * alpha + jnp.sum(pr, axis=0, keepdims=True)
            acc = acc * alpha + jnp.dot(vt_ref[0, j, kb], pr.astype(bf16), preferred_element_type=f32)
            return m_new, l, acc

        init = (jnp.full((1, T), -jnp.inf, f32), jnp.zeros((1, T), f32), jnp.zeros((D, T), f32))
        _, l, acc = lax.fori_loop(0, i + 1, kv_step, init)
        outs.append((acc / l).T)
    o_ref[0] = jax.nn.sigmoid(og_ref[0]) * jnp.concatenate(outs, axis=1)


def _fox_attn(qa, ka, vt, p_f):
    B, H, S, A = qa.shape
    T, D = FOX_TILE, FOX_HEAD
    hp = LANES // D
    og_block0 = 3 * FOX_WIDTH // LANES
    return pl.pallas_call(
        _fox_attn_kernel,
        grid=(B, H // hp, S // T),
        in_specs=[pl.BlockSpec((1, hp, T, A), lambda b, g, i: (b, g, i, 0)),
                  pl.BlockSpec((1, hp, S, A), lambda b, g, i: (b, g, 0, 0)),
                  pl.BlockSpec((1, hp, S // T, D, T), lambda b, g, i: (b, g, 0, 0, 0)),
                  pl.BlockSpec((1, T, LANES), lambda b, g, i: (b, i, og_block0 + g))],
        out_specs=pl.BlockSpec((1, T, LANES), lambda b, g, i: (b, i, g)),
        out_shape=jax.ShapeDtypeStruct((B, S, FOX_WIDTH), jnp.float32),
        name="fox_attn",
    )(qa, ka, vt, p_f)


def kernel(x, l0_w_in, l0_rwkv_mu, l0_rwkv_w0, l0_rwkv_w_up, l0_rwkv_a0, l0_rwkv_a_up,
           l0_rwkv_g_up, l0_rwkv_k_k, l0_rwkv_k_a, l0_rwkv_r_k, l0_rwkv_ln_g, l0_rwkv_ln_b,
           l0_mlstm_conv_w, l0_mlstm_conv_b, l0_mlstm_ig_b, l0_mlstm_fg_b,
           l0_mlstm_hn_g, l0_mlstm_hn_b, l0_w_out, l0_ln1_g, l0_ln1_b,
           l0_peer_wq, l0_peer_k1, l0_peer_k2, l0_peer_u, l0_peer_v, l0_ln2_g, l0_ln2_b,
           l1_w_in, l1_fox_qn_g, l1_fox_kn_g, l1_fox_f_b, l1_w_out, l1_ln1_g, l1_ln1_b,
           l1_peer_wq, l1_peer_k1, l1_peer_k2, l1_peer_u, l1_peer_v, l1_ln2_g, l1_ln2_b):
    B, S, D = x.shape
    n = B * S
    bf16 = jnp.bfloat16
    x2 = x.reshape(n, D)

    def pad_lanes(w):
        return jnp.pad(w, ((0, 0), (0, LANES - w.shape[1])))

    m_main = 4 * MLSTM_WIDTH
    w0 = jnp.concatenate([l0_w_in[:, :RWKV_COLS + m_main],
                          pad_lanes(l0_w_in[:, RWKV_COLS + m_main:])], axis=1).astype(bf16)
    p_r, p_m = _proj(x2, w0, (RWKV_COLS, m_main + LANES))
    y_a = _rwkv_pallas(p_r.reshape(B, S, -1), l0_rwkv_mu, l0_rwkv_w0, l0_rwkv_w_up, l0_rwkv_a0,
                       l0_rwkv_a_up, l0_rwkv_g_up, l0_rwkv_k_k, l0_rwkv_k_a, l0_rwkv_r_k,
                       l0_rwkv_ln_g, l0_rwkv_ln_b)
    y_b = _mlstm_pallas(p_m.reshape(B, S, -1), l0_mlstm_conv_w, l0_mlstm_conv_b, l0_mlstm_ig_b,
                        l0_mlstm_fg_b, l0_mlstm_hn_g, l0_mlstm_hn_b)
    x2 = _out_proj_ln([y_a.reshape(n, -1), y_b.reshape(n, -1)], l0_w_out.astype(bf16), x2,
                      l0_ln1_g, l0_ln1_b)
    y = _peer_pallas(x2, l0_peer_wq, l0_peer_k1, l0_peer_k2, l0_peer_u, l0_peer_v)
    x2 = _resid_ln(x2, y, l0_ln2_g, l0_ln2_b)

    f_main = 4 * FOX_WIDTH
    w1 = jnp.concatenate([l1_w_in[:, :f_main], pad_lanes(l1_w_in[:, f_main:])], axis=1).astype(bf16)
    (p_f,) = _proj(x2, w1, (f_main + LANES,))
    p_f = p_f.reshape(B, S, -1)
    qa, ka, vt = _fox_prep(p_f, l1_fox_qn_g, l1_fox_kn_g, l1_fox_f_b)
    o = _fox_attn(qa, ka, vt, p_f)
    x2 = _out_proj_ln([o.reshape(n, -1)], l1_w_out.astype(bf16), x2, l1_ln1_g, l1_ln1_b)
    y = _peer_pallas(x2, l1_peer_wq, l1_peer_k1, l1_peer_k2, l1_peer_u, l1_peer_v)
    x2 = _resid_ln(x2, y, l1_ln2_g, l1_ln2_b)
    return x2.reshape(B, S, D)
```

```python
import functools

import jax
import jax.numpy as jnp
from jax import lax
from jax.experimental import pallas as pl
from jax.experimental.pallas import tpu as pltpu

D_MODEL = 1024
DEPTH = 2
DN_ALPHA = (2.0 * DEPTH) ** 0.25
LN_EPS = 1e-5

RWKV_WIDTH = D_MODEL // 2
RWKV_HEAD = 64
RWKV_HEADS = RWKV_WIDTH // RWKV_HEAD
RWKV_W_RANK = 64
RWKV_A_RANK = 64
RWKV_G_RANK = 128
RWKV_GN_EPS = 1e-5 * RWKV_HEAD
RWKV_COLS = 3 * RWKV_WIDTH + RWKV_W_RANK + RWKV_A_RANK + RWKV_G_RANK

MLSTM_WIDTH = D_MODEL // 2
MLSTM_HEAD = 128
MLSTM_HEADS = MLSTM_WIDTH // MLSTM_HEAD
MLSTM_CHUNK = 64

FOX_HEAD = 64
FOX_HEADS = D_MODEL // FOX_HEAD
FOX_WIDTH = FOX_HEADS * FOX_HEAD
FOX_QBLOCK = 128

PEER_HEADS = 8
PEER_NKEYS = 128
PEER_TOPK = 16
PEER_DKEY = 256
PEER_DHALF = PEER_DKEY // 2
PEER_TOKBLOCK = 128


def _split_cols(p, sizes):
    out, start = [], 0
    for s in sizes:
        out.append(p[..., start:start + s])
        start += s
    return out


def _resid_ln_kernel(x_ref, y_ref, g_ref, b_ref, o_ref):
    z = DN_ALPHA * x_ref[...] + y_ref[...]
    mu = jnp.mean(z, axis=-1, keepdims=True)
    zc = z - mu
    var = jnp.mean(zc * zc, axis=-1, keepdims=True)
    o_ref[...] = zc * lax.rsqrt(var + LN_EPS) * g_ref[...] + b_ref[...]


def _resid_ln(x2, y2, g, b, tm=512):
    n, d = x2.shape
    row = pl.BlockSpec((tm, d), lambda i: (i, 0))
    vec = pl.BlockSpec((1, d), lambda i: (0, 0))
    return pl.pallas_call(
        _resid_ln_kernel,
        grid=(n // tm,),
        in_specs=[row, row, vec, vec],
        out_specs=row,
        out_shape=jax.ShapeDtypeStruct((n, d), jnp.float32),
        name="resid_ln",
    )(x2, y2, g.reshape(1, d), b.reshape(1, d))


VMEM_LIMIT_BYTES = 56 * 1024 * 1024


def _proj_kernel(x_ref, w_ref, *o_refs):
    xb = x_ref[...].astype(jnp.bfloat16)
    start = 0
    for o_ref in o_refs:
        width = o_ref.shape[1]
        o_ref[...] = jnp.dot(xb, w_ref[:, start:start + width], preferred_element_type=jnp.float32)
        start += width


def _proj(x2, w_bf16, widths, tm=256):
    n, d = x2.shape
    assert sum(widths) == w_bf16.shape[1]
    return pl.pallas_call(
        _proj_kernel,
        grid=(n // tm,),
        in_specs=[pl.BlockSpec((tm, d), lambda i: (i, 0)),
                  pl.BlockSpec(w_bf16.shape, lambda i: (0, 0))],
        out_specs=[pl.BlockSpec((tm, w), lambda i: (i, 0)) for w in widths],
        out_shape=[jax.ShapeDtypeStruct((n, w), jnp.float32) for w in widths],
        compiler_params=pltpu.CompilerParams(vmem_limit_bytes=VMEM_LIMIT_BYTES),
        name="in_proj",
    )(x2, w_bf16)


def _out_proj_ln_kernel(*refs, n_parts):
    y_refs, (w_ref, x_ref, g_ref, b_ref, o_ref) = refs[:n_parts], refs[n_parts:]
    acc, start = None, 0
    for y_ref in y_refs:
        width = y_ref.shape[1]
        part = jnp.dot(y_ref[...].astype(jnp.bfloat16), w_ref[start:start + width, :],
                       preferred_element_type=jnp.float32)
        acc = part if acc is None else acc + part
        start += width
    z = DN_ALPHA * x_ref[...] + acc
    mu = jnp.mean(z, axis=-1, keepdims=True)
    zc = z - mu
    var = jnp.mean(zc * zc, axis=-1, keepdims=True)
    o_ref[...] = zc * lax.rsqrt(var + LN_EPS) * g_ref[...] + b_ref[...]


def _out_proj_ln(ys, w_bf16, x2, g, b, tm=256):
    n, d = x2.shape
    row = lambda width: pl.BlockSpec((tm, width), lambda i: (i, 0))
    vec = pl.BlockSpec((1, d), lambda i: (0, 0))
    return pl.pallas_call(
        functools.partial(_out_proj_ln_kernel, n_parts=len(ys)),
        grid=(n // tm,),
        in_specs=[row(y.shape[1]) for y in ys]
                 + [pl.BlockSpec(w_bf16.shape, lambda i: (0, 0)), row(d), vec, vec],
        out_specs=row(d),
        out_shape=jax.ShapeDtypeStruct((n, d), jnp.float32),
        name="out_proj_ln",
    )(*ys, w_bf16, x2, g.reshape(1, d), b.reshape(1, d))


def _head_norm(y, g, b, eps):
    mu = jnp.mean(y, -1, keepdims=True)
    var = jnp.mean(jnp.square(y - mu), -1, keepdims=True)
    return (y - mu) * lax.rsqrt(var + eps) * g + b


def _rms_norm(y, g):
    return y * lax.rsqrt(jnp.mean(y * y, -1, keepdims=True) + 1e-6) * g


def _token_shift(z):
    return jnp.pad(z, ((0, 0), (1, 0), (0, 0)))[:, :-1]


def _causal_conv(z, w, b):
    c = z.shape[-1]
    out = lax.conv_general_dilated(z, w[:, None, :], window_strides=(1,),
                                   padding=((w.shape[0] - 1, 0),),
                                   dimension_numbers=('NWC', 'WIO', 'NWC'),
                                   feature_group_count=c)
    return out + b


def _rwkv7_mix(p, mu, w0, w_up, a0, a_up, g_up, k_k, k_a, r_k, ln_g, ln_b):
    B, S, _ = p.shape
    H, N = RWKV_HEADS, RWKV_HEAD
    p = p + (_token_shift(p) - p) * mu
    r, k, v, xw, xa, xg = _split_cols(p, (RWKV_WIDTH, RWKV_WIDTH, RWKV_WIDTH,
                                          RWKV_W_RANK, RWKV_A_RANK, RWKV_G_RANK))
    log_w = -jnp.exp(-jax.nn.softplus(-(w0 + jnp.tanh(xw) @ w_up)) - 0.5)
    a = jax.nn.sigmoid(a0 + xa @ a_up)
    g = jax.nn.sigmoid(xg) @ g_up
    heads = lambda t: t.reshape(B, S, H, N)
    kk = heads(k * k_k)
    kk = kk / jnp.maximum(jnp.sqrt(jnp.sum(kk * kk, -1, keepdims=True)), 1e-12)
    k = k * (1.0 + (a - 1.0) * k_a)
    r_h, k_h, v_h, a_h = heads(r), heads(k), heads(v), heads(a)
    w_h = jnp.exp(heads(log_w))

    def step(state, inp):
        r_t, w_t, k_t, v_t, kk_t, a_t = inp
        s_kk = jnp.einsum('bhvk,bhk->bhv', state, kk_t)
        state = (state * w_t[:, :, None, :]
                 - s_kk[..., None] * (kk_t * a_t)[:, :, None, :]
                 + v_t[..., None] * k_t[:, :, None, :])
        return state, jnp.einsum('bhvk,bhk->bhv', state, r_t)

    seq_first = lambda t: jnp.moveaxis(t, 1, 0)
    state0 = jnp.zeros((B, H, N, N), jnp.float32)
    _, y = lax.scan(step, state0, (seq_first(r_h), seq_first(w_h), seq_first(k_h),
                                   seq_first(v_h), seq_first(kk), seq_first(a_h)))
    y = jnp.moveaxis(y, 0, 1)
    y = _head_norm(y, ln_g, ln_b, RWKV_GN_EPS)
    y = y + jnp.sum(r_h * k_h * r_k, -1, keepdims=True) * v_h
    return y.reshape(B, S, RWKV_WIDTH) * g


def _mlstm_chunkwise(q, k, v, ig, lf):
    B, S, H, D = q.shape
    L = MLSTM_CHUNK
    NC = S // L

    def to_chunks(t):
        t = t.reshape((B, NC, L, H) + t.shape[3:])
        return jnp.moveaxis(t, (1, 3), (0, 2))

    causal = jnp.tril(jnp.ones((L, L), dtype=bool))

    def body(carry, inp):
        C, n, m = carry
        qc, kc, vc, igc, lfc = inp
        b = jnp.cumsum(lfc, axis=-1)
        dmat = jnp.where(causal, b[..., :, None] - b[..., None, :] + igc[..., None, :], -jnp.inf)
        inter = b + m[..., None]
        m_t = jnp.maximum(inter, jnp.max(dmat, -1))
        weights = jnp.exp(dmat - m_t[..., None])
        sc = jnp.einsum('bhtd,bhsd->bhts', qc, kc) * weights
        carry_in = jnp.exp(inter - m_t)
        num = (jnp.einsum('bhts,bhsd->bhtd', sc, vc)
               + carry_in[..., None] * jnp.einsum('bhtk,bhkv->bhtv', qc, C))
        den = jnp.sum(sc, -1) + carry_in * jnp.einsum('bhtk,bhk->bht', qc, n)
        h = num / jnp.maximum(jnp.abs(den), jnp.exp(-m_t))[..., None]
        b_last = b[..., -1]
        gs = b_last[..., None] - b + igc
        m_new = jnp.maximum(b_last + m, jnp.max(gs, -1))
        ws = jnp.exp(gs - m_new[..., None])
        keep = jnp.exp(b_last + m - m_new)
        C = keep[..., None, None] * C + jnp.einsum('bhs,bhsk,bhsv->bhkv', ws, kc, vc)
        n = keep[..., None] * n + jnp.einsum('bhs,bhsk->bhk', ws, kc)
        return (C, n, m_new), h

    f32 = jnp.float32
    init = (jnp.zeros((B, H, D, D), f32), jnp.zeros((B, H, D), f32), jnp.zeros((B, H), f32))
    _, h = lax.scan(body, init, (to_chunks(q), to_chunks(k), to_chunks(v),
                                 to_chunks(ig), to_chunks(lf)))
    return jnp.moveaxis(h, (0, 2), (1, 3)).reshape(B, S, H, D)


def _mlstm_mix(p, conv_w, conv_b, ig_b, fg_b, hn_g, hn_b):
    B, S, _ = p.shape
    H, N = MLSTM_HEADS, MLSTM_HEAD
    q, k, v, o, ig, fg = _split_cols(p, (MLSTM_WIDTH, MLSTM_WIDTH, MLSTM_WIDTH, MLSTM_WIDTH, H, H))
    qk = jax.nn.silu(_causal_conv(jnp.concatenate([q, k], -1), conv_w, conv_b))
    q, k = qk[..., :MLSTM_WIDTH], qk[..., MLSTM_WIDTH:]
    heads = lambda t: t.reshape(B, S, H, N)
    q = heads(q) * (N ** -0.5)
    ig = ig + ig_b
    lf = jax.nn.log_sigmoid(fg + fg_b)
    h = _mlstm_chunkwise(q, heads(k), heads(v), ig, lf)
    h = _head_norm(h, hn_g, hn_b, LN_EPS).reshape(B, S, MLSTM_WIDTH)
    return jax.nn.sigmoid(o) * h


def _even_mixer(x, w_in, mu, w0, w_up, a0, a_up, g_up, k_k, k_a, r_k, rln_g, rln_b,
                conv_w, conv_b, ig_b, fg_b, hn_g, hn_b, w_out):
    p = x @ w_in
    y_a = _rwkv_pallas(p[..., :RWKV_COLS], mu, w0, w_up, a0, a_up, g_up, k_k, k_a, r_k, rln_g, rln_b)
    y_b = _mlstm_mix(p[..., RWKV_COLS:], conv_w, conv_b, ig_b, fg_b, hn_g, hn_b)
    return jnp.concatenate([y_a, y_b], -1) @ w_out


def _fox_attention(q, k, v, logf):
    B, S, H, D = q.shape
    NB = S // FOX_QBLOCK
    c = jnp.cumsum(logf, axis=1)
    c_key = jnp.transpose(c, (0, 2, 1))[:, :, None, :]
    q_blocks = jnp.moveaxis(q.reshape(B, NB, FOX_QBLOCK, H, D), 1, 0)
    c_blocks = jnp.moveaxis(c.reshape(B, NB, FOX_QBLOCK, H), 1, 0)
    key_pos = jnp.arange(S)
    scale = D ** -0.5

    def one_block(args):
        qb, cb, blk = args
        q_pos = blk * FOX_QBLOCK + jnp.arange(FOX_QBLOCK)
        logits = jnp.einsum('bqhd,bkhd->bhqk', qb, k) * scale
        logits = logits + jnp.transpose(cb, (0, 2, 1))[..., None] - c_key
        logits = jnp.where(key_pos[None, :] <= q_pos[:, None], logits, -jnp.inf)
        probs = jax.nn.softmax(logits, axis=-1)
        return jnp.einsum('bhqk,bkhd->bqhd', probs, v)

    out = lax.map(one_block, (q_blocks, c_blocks, jnp.arange(NB)))
    return jnp.moveaxis(out, 0, 1).reshape(B, S, H, D)


def _odd_mixer(x, w_in, qn_g, kn_g, f_b, w_out):
    B, S, _ = x.shape
    p = x @ w_in
    q, k, v, og, fl = _split_cols(p, (FOX_WIDTH, FOX_WIDTH, FOX_WIDTH, FOX_WIDTH, FOX_HEADS))
    heads = lambda t: t.reshape(B, S, FOX_HEADS, FOX_HEAD)
    q = _rms_norm(heads(q), qn_g)
    k = _rms_norm(heads(k), kn_g)
    logf = jax.nn.log_sigmoid(fl + f_b)
    o = _fox_attention(q, k, heads(v), logf).reshape(B, S, FOX_WIDTH)
    return (jax.nn.sigmoid(og) * o) @ w_out


def _peer_ffn(x, w_q, sub_k1, sub_k2, u_tab, v_tab):
    B, S, Dm = x.shape
    K = PEER_TOPK
    xt = x.reshape((B * S) // PEER_TOKBLOCK, PEER_TOKBLOCK, Dm)

    def block(xb):
        T = xb.shape[0]
        q = (xb @ w_q).reshape(T, PEER_HEADS, 2, PEER_DHALF)
        s1 = jnp.einsum('thd,hnd->thn', q[:, :, 0], sub_k1)
        s2 = jnp.einsum('thd,hnd->thn', q[:, :, 1], sub_k2)
        v1, i1 = lax.top_k(s1, K)
        v2, i2 = lax.top_k(s2, K)
        cand = (v1[..., :, None] + v2[..., None, :]).reshape(T, PEER_HEADS, K * K)
        sc, ci = lax.top_k(cand, K)
        e1 = jnp.take_along_axis(i1, ci // K, axis=-1)
        e2 = jnp.take_along_axis(i2, ci % K, axis=-1)
        eid = e1 * PEER_NKEYS + e2
        gate = jax.nn.softmax(sc, axis=-1)
        act = jax.nn.gelu(jnp.einsum('td,thkd->thk', xb, u_tab[eid]), approximate=False)
        return jnp.einsum('thk,thkd->td', gate * act, v_tab[eid])

    return lax.map(block, xt).reshape(B, S, Dm)


PEER_EXPERTS = PEER_NKEYS * PEER_NKEYS
PEER_TOK_TILE = 256
PEER_EXPERT_CHUNK = 1024
PEER_E1_GROUP = 4
PEER_E2_SUB = 32
LANES = 128
MXU_DIM = 256
INV_SQRT2 = 0.7071067811865476
_NT = (((1,), (1,)), ((), ()))


def _topk_desc(work, k):
    vals = []
    for _ in range(k):
        m = jnp.max(work, axis=0, keepdims=True)
        vals.append(m)
        work = jnp.where(work >= m, -jnp.inf, work)
    return vals


def _peer_select_kernel(x_ref, wqt_ref, k1_ref, k2_ref,
                        phi_ref, a1_ref, a2_ref, qt_ref):
    f32, bf16 = jnp.float32, jnp.bfloat16
    K = PEER_TOPK
    qt_ref[...] = lax.dot_general(wqt_ref[...], x_ref[...], _NT, preferred_element_type=f32)
    n_groups = x_ref.shape[0] // LANES

    def head(h, carry):
        base = pl.multiple_of(h * PEER_DKEY, PEER_DKEY)
        q1 = qt_ref[pl.ds(base, PEER_DHALF), :].astype(bf16)
        q2 = qt_ref[pl.ds(base + PEER_DHALF, PEER_DHALF), :].astype(bf16)
        s1 = jnp.dot(k1_ref[h], q1, preferred_element_type=f32)
        s2 = jnp.dot(k2_ref[h], q2, preferred_element_type=f32)
        for c in range(n_groups):
            sl = slice(c * LANES, (c + 1) * LANES)
            s1c, s2c = s1[:, sl], s2[:, sl]
            v1 = _topk_desc(s1c, K)
            v2 = _topk_desc(s2c, K)
            v2s = jnp.concatenate(v2, axis=0)
            cand = jnp.concatenate([v1[a] + v2s for a in range(K)], axis=0)
            vc = _topk_desc(cand, K + 1)
            cmax = vc[0]
            z = jnp.sum(jnp.where(cand >= vc[K - 1], jnp.exp(cand - cmax), 0.0), axis=0, keepdims=True)
            cut = 0.5 * (vc[K - 1] + vc[K])
            s1m = jnp.where(s1c >= v1[K - 1], s1c, -jnp.inf)
            s2m = jnp.where(s2c >= v2[K - 1], s2c, -jnp.inf)
            inv_z = 1.0 / z
            phi_ref[h, :, sl] = jnp.exp(cut - s1m - v2[0]) * inv_z
            a1_ref[h, :, sl] = jnp.exp(s1m - v1[0])
            a2_ref[h, :, sl] = jnp.exp(s2m - v2[0]) * inv_z
        return carry

    lax.fori_loop(0, PEER_HEADS, head, 0)


def _peer_select(xb, wqt, k1, k2):
    n, d = xb.shape
    T = PEER_TOK_TILE
    H, NK = PEER_HEADS, PEER_NKEYS
    full = lambda shape: pl.BlockSpec(shape, lambda i: (0,) * len(shape))
    tok3 = lambda rows: pl.BlockSpec((H, rows, T), lambda i: (0, 0, i))
    f32 = jnp.float32
    return pl.pallas_call(
        _peer_select_kernel,
        grid=(n // T,),
        in_specs=[pl.BlockSpec((T, d), lambda i: (i, 0)), full(wqt.shape), full(k1.shape), full(k2.shape)],
        out_specs=[tok3(NK)] * 3,
        out_shape=[jax.ShapeDtypeStruct((H, NK, n), f32)] * 3,
        scratch_shapes=[pltpu.VMEM((H * PEER_DKEY, T), f32)],
        name="peer_select",
    )(xb, wqt, k1, k2)


def _peer_expert_kernel(xt_ref, u_ref, v_ref, phi_ref, a1_ref, a2_ref,
                        o_ref, acc_ref, hn_ref, hc_ref, wn_ref, wc_ref):
    f32, bf16 = jnp.float32, jnp.bfloat16
    s = pl.program_id(0)
    n_chunks = PEER_EXPERTS // PEER_EXPERT_CHUNK
    c_chunk = jnp.maximum(s - 2, 0) % n_chunks

    @pl.when(s == 0)
    def _():
        hn_ref[...] = jnp.zeros_like(hn_ref)
        wn_ref[...] = jnp.zeros_like(wn_ref)

    @pl.when(c_chunk == 0)
    def _():
        acc_ref[...] = jnp.zeros_like(acc_ref)

    T = xt_ref.shape[1]
    NK, G, SB = PEER_NKEYS, PEER_E1_GROUP, PEER_E2_SUB
    hc_ref[...] = hn_ref[...]
    wc_ref[...] = wn_ref[...]
    u_blk = pltpu.bitcast(u_ref[...], bf16)
    hn_ref[...] = jnp.dot(u_blk, xt_ref[...], preferred_element_type=f32)
    for q in range(PEER_EXPERT_CHUNK // (G * NK)):
        for c in range(T // LANES):
            ts = slice(c * LANES, (c + 1) * LANES)
            for sb in range(NK // SB):
                e2s = slice(sb * SB, (sb + 1) * SB)
                g = [None] * G
                for hd in range(PEER_HEADS):
                    a2 = a2_ref[hd, e2s, ts]
                    for e in range(G):
                        r = q * G + e
                        term = jnp.where(a2 > phi_ref[hd, r:r + 1, ts], a1_ref[hd, r:r + 1, ts] * a2, 0.0)
                        g[e] = term if g[e] is None else g[e] + term
                for e in range(G):
                    lo = (q * G + e) * NK + sb * SB
                    hh = hc_ref[lo:lo + SB, ts]
                    wn_ref[lo:lo + SB, ts] = (
                        g[e] * (0.5 * hh * (1.0 + lax.erf(hh * INV_SQRT2)))).astype(bf16)
    acc_ref[...] += lax.dot_general(wc_ref[...], v_ref[...], _TN, preferred_element_type=f32)

    @pl.when(c_chunk == n_chunks - 1)
    def _():
        o_ref[...] = acc_ref[...]


def _peer_experts(xtb, ub, vb, phi, a1, a2):
    d, n = xtb.shape
    T, CE = PEER_TOK_TILE, PEER_EXPERT_CHUNK
    H, NK = PEER_HEADS, PEER_NKEYS
    e1_per_chunk = CE // NK
    nc = PEER_EXPERTS // CE
    last = (n // T) * nc - 1
    at = lambda s, lag: jnp.clip(s - lag, 0, last)
    return pl.pallas_call(
        _peer_expert_kernel,
        grid=(last + 3,),
        in_specs=[
            pl.BlockSpec((d, T), lambda s: (0, at(s, 0) // nc)),
            pl.BlockSpec((CE // 2, d), lambda s: (at(s, 0) % nc, 0)),
            pl.BlockSpec((CE, d), lambda s: (at(s, 2) % nc, 0)),
            pl.BlockSpec((H, e1_per_chunk, T), lambda s: (0, at(s, 1) % nc, at(s, 1) // nc)),
            pl.BlockSpec((H, e1_per_chunk, T), lambda s: (0, at(s, 1) % nc, at(s, 1) // nc)),
            pl.BlockSpec((H, NK, T), lambda s: (0, 0, at(s, 1) // nc)),
        ],
        out_specs=pl.BlockSpec((T, d), lambda s: (at(s, 2) // nc, 0)),
        out_shape=jax.ShapeDtypeStruct((n, d), jnp.float32),
        scratch_shapes=[pltpu.VMEM((T, d), jnp.float32),
                        pltpu.VMEM((CE, T), jnp.float32), pltpu.VMEM((CE, T), jnp.float32),
                        pltpu.VMEM((CE, T), jnp.bfloat16), pltpu.VMEM((CE, T), jnp.bfloat16)],
        compiler_params=pltpu.CompilerParams(dimension_semantics=("arbitrary",)),
        name="peer_experts",
    )(xtb, ub, vb, phi, a1, a2)


def _pack_row_pairs(t_bf16):
    rows, cols = t_bf16.shape
    pairs = jnp.swapaxes(t_bf16.reshape(rows // 2, 2, cols), -1, -2)
    return lax.bitcast_convert_type(pairs, jnp.uint32)


def _peer_pallas(x2, w_q, sub_k1, sub_k2, u_tab, v_tab):
    bf16 = jnp.bfloat16
    xb = x2.astype(bf16)
    sel = _peer_select(xb, w_q.T.astype(bf16), sub_k1.astype(bf16), sub_k2.astype(bf16))
    return _peer_experts(xb.T, _pack_row_pairs(u_tab.astype(bf16)), v_tab.astype(bf16), *sel)


RWKV_CHUNK = 64
RWKV_PAIR = LANES // RWKV_HEAD
RWKV_PAIRS = RWKV_HEADS // RWKV_PAIR
_NN = (((1,), (0,)), ((), ()))
_TN = (((0,), (0,)), ((), ()))


def _split_bf16(a):
    hi = a.astype(jnp.bfloat16)
    lo = (a - hi.astype(jnp.float32)).astype(jnp.bfloat16)
    return hi, lo


def _mm2(a, b, dims):
    d = lambda u, v: lax.dot_general(u, v, dims, preferred_element_type=jnp.float32)
    (ah, al), (bh, bl) = _split_bf16(a), _split_bf16(b)
    return d(ah, bh) + d(ah, bl) + d(al, bh)


def _mm_exact_rhs(a, b_exact, dims, terms=2):
    d = lambda u: lax.dot_general(u, b_exact, dims, preferred_element_type=jnp.float32)
    out, rem = None, a
    for _ in range(terms):
        piece = rem.astype(jnp.bfloat16)
        rem = rem - piece.astype(jnp.float32)
        out = d(piece) if out is None else out + d(piece)
    return out


def _mm_exact_lhs(a_exact, b, dims, terms=3):
    d = lambda v: lax.dot_general(a_exact, v, dims, preferred_element_type=jnp.float32)
    out, rem = None, b
    for _ in range(terms):
        piece = rem.astype(jnp.bfloat16)
        rem = rem - piece.astype(jnp.float32)
        out = d(piece) if out is None else out + d(piece)
    return out


def _softplus(z):
    return jnp.maximum(z, 0.0) + jnp.log1p(jnp.exp(-jnp.abs(z)))


def _rwkv_kernel(p_ref, prev_ref, mu_ref, w0_ref, wup_ref, a0_ref, aup_ref, gup_ref,
                 kk_ref, ka_ref, rk_ref, lng_ref, lnb_ref, ones_ref,
                 o_ref, state_ref):
    f32, bf16 = jnp.float32, jnp.bfloat16
    L, W = RWKV_CHUNK, RWKV_WIDTH
    c_idx = pl.program_id(1)

    @pl.when(c_idx == 0)
    def _():
        state_ref[...] = jnp.zeros_like(state_ref)

    p = p_ref[0]
    row = lax.broadcasted_iota(jnp.int32, p.shape, 0)
    prev_row = jnp.where(c_idx == 0, 0.0, prev_ref[0, 7:8, :])
    shifted = jnp.where(row == 0, prev_row, pltpu.roll(p, 1, axis=0))
    p = p + (shifted - p) * mu_ref[...]
    r, k, v = p[:, 0:W], p[:, W:2 * W], p[:, 2 * W:3 * W]
    o = 3 * W
    xw = p[:, o:o + RWKV_W_RANK]
    xa = p[:, o + RWKV_W_RANK:o + RWKV_W_RANK + RWKV_A_RANK]
    xg = p[:, o + RWKV_W_RANK + RWKV_A_RANK:]
    dotd = lambda u, m: jnp.dot(u.astype(bf16), m, preferred_element_type=f32)
    lw = -jnp.exp(-_softplus(-(w0_ref[...] + dotd(jnp.tanh(xw), wup_ref[...]))) - 0.5)
    a = jax.nn.sigmoid(a0_ref[...] + dotd(xa, aup_ref[...]))
    g = dotd(jax.nn.sigmoid(xg), gup_ref[...])
    ones_bd = ones_ref[...]
    head_sum = lambda t: _mm_exact_rhs(t, ones_bd, _NN)
    kk = k * kk_ref[...]
    kk = kk / jnp.maximum(jnp.sqrt(head_sum(kk * kk)), 1e-12)
    k = k * (1.0 + (a - 1.0) * ka_ref[...])

    lane = lax.broadcasted_iota(jnp.int32, (L, LANES), 1)
    trow = lax.broadcasted_iota(jnp.int32, (L, LANES), 0)
    s_in = lane % RWKV_HEAD
    strict, incl = s_in < trow, s_in <= trow
    eye_pair = (s_in == trow).astype(f32)
    m0 = lane < RWKV_HEAD
    tril = (lax.broadcasted_iota(jnp.int32, (L, L), 1)
            <= lax.broadcasted_iota(jnp.int32, (L, L), 0)).astype(bf16)
    r2 = lax.broadcasted_iota(jnp.int32, (LANES, LANES), 0) // RWKV_HEAD
    c2 = lax.broadcasted_iota(jnp.int32, (LANES, LANES), 1) // RWKV_HEAD
    bd_mask = r2 == c2

    def bd(t):
        return jnp.concatenate([jnp.where(m0, t, 0.0), jnp.where(m0, 0.0, t)], axis=0)

    ys = []
    for pr in range(RWKV_PAIRS):
        sl = slice(pr * LANES, (pr + 1) * LANES)
        rp, kp, vp, kkp, ap, lwp = r[:, sl], k[:, sl], v[:, sl], kk[:, sl], a[:, sl], lw[:, sl]
        c = _mm_exact_lhs(tril, lwp, _NN)
        eg, egp, eni = jnp.exp(c), jnp.exp(c - lwp), jnp.exp(-c)
        at, bt, kt, rt = -kkp * egp, kkp * ap * eni, kp * eni, rp * eg
        g_last = jnp.exp(c[L - 1:L, :])
        gram = _mm2(jnp.concatenate([at, rt], axis=0),
                    jnp.concatenate([bd(bt), bd(kt)], axis=0), _NT)
        n_ab = jnp.where(strict, gram[0:L, 0:LANES], 0.0)
        a_ak = jnp.where(strict, gram[0:L, LANES:], 0.0)
        a_rb = jnp.where(incl, gram[L:, 0:LANES], 0.0)
        a_rk = jnp.where(incl, gram[L:, LANES:], 0.0)
        tinv, m = eye_pair + n_ab, n_ab
        for _ in range(5):
            m = _mm2(m, bd(m), _NN)
            tinv = tinv + _mm2(m, bd(tinv), _NN)
        s0 = state_ref[pr]
        pq = _mm2(at, s0, _NT) + _mm2(a_ak, bd(vp), _NN)
        u = _mm2(tinv, bd(pq), _NN)
        y = (_mm2(rt, s0, _NT)
             + _mm2(jnp.concatenate([a_rb, a_rk], axis=1),
                    jnp.concatenate([bd(u), bd(vp)], axis=0), _NN))
        upd = _mm2(jnp.concatenate([u, vp], axis=0), jnp.concatenate([bt, kt], axis=0), _TN)
        state_ref[pr] = (s0 + jnp.where(bd_mask, upd, 0.0)) * g_last
        ys.append(y)
    y = jnp.concatenate(ys, axis=1)

    inv_n = 1.0 / RWKV_HEAD
    mean = head_sum(y) * inv_n
    yc = y - mean
    var = head_sum(yc * yc) * inv_n
    y = yc * lax.rsqrt(var + RWKV_GN_EPS) * lng_ref[...] + lnb_ref[...]
    y = y + head_sum(r * k * rk_ref[...]) * v
    o_ref[0] = y * g


def _rwkv_pallas(p_r, mu, w0, w_up, a0, a_up, g_up, k_k, k_a, r_k, ln_g, ln_b):
    B, S, C = p_r.shape
    L, W = RWKV_CHUNK, RWKV_WIDTH
    bf16 = jnp.bfloat16
    vecw = lambda t: t.reshape(1, W)
    ones_bd = jnp.kron(jnp.eye(RWKV_HEADS, dtype=bf16), jnp.ones((RWKV_HEAD, RWKV_HEAD), bf16))
    full = lambda a: pl.BlockSpec(a.shape, lambda b, c: (0,) * a.ndim)
    args = [mu.reshape(1, C), vecw(w0), w_up.astype(bf16), vecw(a0), a_up.astype(bf16),
            g_up.astype(bf16), vecw(k_k), vecw(k_a), vecw(r_k), vecw(ln_g), vecw(ln_b), ones_bd]
    return pl.pallas_call(
        _rwkv_kernel,
        grid=(B, S // L),
        in_specs=[pl.BlockSpec((1, L, C), lambda b, c: (b, c, 0)),
                  pl.BlockSpec((1, 8, C), lambda b, c: (b, jnp.maximum(c * (L // 8) - 1, 0), 0))]
                 + [full(a) for a in args],
        out_specs=pl.BlockSpec((1, L, W), lambda b, c: (b, c, 0)),
        out_shape=jax.ShapeDtypeStruct((B, S, W), jnp.float32),
        scratch_shapes=[pltpu.VMEM((RWKV_PAIRS, LANES, LANES), jnp.float32)],
        compiler_params=pltpu.CompilerParams(dimension_semantics=("arbitrary", "arbitrary")),
        name="rwkv7",
    )(p_r, p_r, *args)


MLSTM_CONV = 4
MLSTM_GATE_LANES = LANES


def _mlstm_kernel(p_ref, prev_ref, cw_ref, cb_ref, igb_ref, fgb_ref, hng_ref, hnb_ref,
                  o_ref, c_ref, n_ref, m_ref):
    f32, bf16 = jnp.float32, jnp.bfloat16
    L, W, H, D = MLSTM_CHUNK, MLSTM_WIDTH, MLSTM_HEADS, MLSTM_HEAD
    c_idx = pl.program_id(1)

    @pl.when(c_idx == 0)
    def _():
        c_ref[...] = jnp.zeros_like(c_ref)
        n_ref[...] = jnp.zeros_like(n_ref)
        m_ref[...] = jnp.zeros_like(m_ref)

    p = p_ref[0]
    z = p[:, 0:2 * W]
    prev = jnp.where(c_idx == 0, 0.0, prev_ref[0, :, 0:2 * W])
    ext = jnp.concatenate([prev, z], axis=0)
    conv = cb_ref[...] + cw_ref[MLSTM_CONV - 1:MLSTM_CONV, :] * z
    for j in range(MLSTM_CONV - 1):
        d = MLSTM_CONV - 1 - j
        conv = conv + cw_ref[j:j + 1, :] * ext[8 - d:8 - d + L, :]
    qk = conv * jax.nn.sigmoid(conv)
    q_all, k_all = qk[:, 0:W] * (D ** -0.5), qk[:, W:2 * W]
    v_all, o_all = p[:, 2 * W:3 * W], p[:, 3 * W:4 * W]
    gates = p[:, 4 * W:]
    igl = gates + igb_ref[...]
    lfl = -_softplus(-(gates + fgb_ref[...]))
    tril = (lax.broadcasted_iota(jnp.int32, (L, L), 1)
            <= lax.broadcasted_iota(jnp.int32, (L, L), 0))
    bcum = _mm_exact_lhs(tril.astype(bf16), lfl, _NN)
    e_all = igl - pltpu.roll(bcum, LANES - H, axis=1)
    lane = lax.broadcasted_iota(jnp.int32, (L, LANES), 1)
    outs = []
    for h in range(H):
        hs = slice(h * D, (h + 1) * D)
        q, k, v = q_all[:, hs], k_all[:, hs], v_all[:, hs]
        b_col = bcum[:, H + h:H + h + 1]
        ig_col = igl[:, h:h + 1]
        onehot = (lane == h).astype(bf16)
        e_row = _mm_exact_lhs(onehot, e_all, _NT)
        m_prev = m_ref[h, 0:1, 0:1]
        dmat = jnp.where(tril, b_col + e_row, -jnp.inf)
        inter = b_col + m_prev
        m_t = jnp.maximum(inter, jnp.max(dmat, axis=-1, keepdims=True))
        weights = jnp.exp(dmat - m_t)
        sc = lax.dot_general(q.astype(bf16), k.astype(bf16), _NT, preferred_element_type=f32) * weights
        carry_in = jnp.exp(inter - m_t)
        c_prev, n_prev = c_ref[h], n_ref[h, 0:1, :]
        num = (jnp.dot(sc.astype(bf16), v.astype(bf16), preferred_element_type=f32)
               + carry_in * jnp.dot(q.astype(bf16), c_prev.astype(bf16), preferred_element_type=f32))
        den = (jnp.sum(sc, axis=-1, keepdims=True)
               + carry_in * jnp.sum(q * n_prev, axis=-1, keepdims=True))
        hval = num / jnp.maximum(jnp.abs(den), jnp.exp(-m_t))
        b_last = b_col[L - 1:L, :]
        gs = b_last - b_col + ig_col
        m_new = jnp.maximum(b_last + m_prev, jnp.max(gs, axis=0, keepdims=True))
        ws = jnp.exp(gs - m_new)
        keep = jnp.exp(b_last + m_prev - m_new)
        wk = ws * k
        c_ref[h] = keep * c_prev + lax.dot_general(wk.astype(bf16), v.astype(bf16), _TN,
                                                   preferred_element_type=f32)
        n_ref[h] = jnp.broadcast_to(keep * n_prev + jnp.sum(wk, axis=0, keepdims=True), (8, D))
        m_ref[h] = jnp.broadcast_to(m_new, (8, LANES))
        mu = jnp.mean(hval, axis=-1, keepdims=True)
        hc = hval - mu
        var = jnp.mean(hc * hc, axis=-1, keepdims=True)
        hn = hc * lax.rsqrt(var + LN_EPS) * hng_ref[:, hs] + hnb_ref[:, hs]
        outs.append(jax.nn.sigmoid(o_all[:, hs]) * hn)
    o_ref[0] = jnp.concatenate(outs, axis=1)


def _mlstm_pallas(p_m, conv_w, conv_b, ig_b, fg_b, hn_g, hn_b):
    B, S, C = p_m.shape
    L, W, H, D = MLSTM_CHUNK, MLSTM_WIDTH, MLSTM_HEADS, MLSTM_HEAD
    pad = lambda t, off: jnp.zeros((1, LANES), jnp.float32).at[0, off:off + H].set(t)
    args = [conv_w, conv_b.reshape(1, 2 * W), pad(ig_b, 0), pad(fg_b, H),
            hn_g.reshape(1, W), hn_b.reshape(1, W)]
    full = lambda a: pl.BlockSpec(a.shape, lambda b, c: (0,) * a.ndim)
    return pl.pallas_call(
        _mlstm_kernel,
        grid=(B, S // L),
        in_specs=[pl.BlockSpec((1, L, C), lambda b, c: (b, c, 0)),
                  pl.BlockSpec((1, 8, C), lambda b, c: (b, jnp.maximum(c * (L // 8) - 1, 0), 0))]
                 + [full(a) for a in args],
        out_specs=pl.BlockSpec((1, L, W), lambda b, c: (b, c, 0)),
        out_shape=jax.ShapeDtypeStruct((B, S, W), jnp.float32),
        scratch_shapes=[pltpu.VMEM((H, D, D), jnp.float32),
                        pltpu.VMEM((H, 8, D), jnp.float32),
                        pltpu.VMEM((H, 8, LANES), jnp.float32)],
        compiler_params=pltpu.CompilerParams(dimension_semantics=("arbitrary", "arbitrary")),
        name="mlstm",
    )(p_m, p_m, *args)


FOX_TILE = 256
FOX_AUG = LANES
FOX_GATE_LANES = LANES


def _fox_prep_kernel(p_ref, qg_ref, kg_ref, fb_ref, ones_ref, qa_ref, ka_ref, vt_ref, carry_ref):
    f32, bf16 = jnp.float32, jnp.bfloat16
    T, W, H, D = FOX_TILE, FOX_WIDTH, FOX_HEADS, FOX_HEAD
    i = pl.program_id(1)

    @pl.when(i == 0)
    def _():
        carry_ref[...] = jnp.zeros_like(carry_ref)

    p = p_ref[0]
    ones_bd = ones_ref[...]
    ms = lambda t: _mm_exact_rhs(t * t, ones_bd, _NN) * (1.0 / D)
    q = p[:, 0:W]
    q = q * lax.rsqrt(ms(q) + 1e-6) * qg_ref[...]
    k = p[:, W:2 * W]
    k = k * lax.rsqrt(ms(k) + 1e-6) * kg_ref[...] * (D ** -0.5)
    vt = p[:, 2 * W:3 * W].T.astype(bf16)
    logf = -_softplus(-(p[:, 4 * W:] + fb_ref[...]))
    tril = (lax.broadcasted_iota(jnp.int32, (T, T), 1)
            <= lax.broadcasted_iota(jnp.int32, (T, T), 0)).astype(bf16)
    c = _mm_exact_lhs(tril, logf, _NN) + carry_ref[0:1, :]
    carry_ref[...] = jnp.broadcast_to(c[T - 1:T, :], carry_ref.shape)
    c1 = c.astype(bf16).astype(f32)
    c2 = (c - c1).astype(bf16).astype(f32)
    c3 = (c - c1 - c2).astype(bf16).astype(f32)
    lane = lax.broadcasted_iota(jnp.int32, (T, D), 1)
    for h in range(H):
        hs = slice(h * D, (h + 1) * D)
        c1h, c2h, c3h = c1[:, h:h + 1], c2[:, h:h + 1], c3[:, h:h + 1]
        pieces = jnp.where(lane % 3 == 0, c1h, jnp.where(lane % 3 == 1, c2h, c3h))
        q_aug = jnp.where(lane < 3, 1.0, jnp.where(lane < 6, pieces, 0.0))
        k_aug = jnp.where(lane < 3, -pieces, jnp.where(lane < 6, 1.0, 0.0))
        qa_ref[0, h] = jnp.concatenate([q[:, hs], q_aug], axis=1).astype(bf16)
        ka_ref[0, h] = jnp.concatenate([k[:, hs], k_aug], axis=1).astype(bf16)
        vt_ref[0, h, 0] = vt[hs, :]


def _fox_prep(p_f, qn_g, kn_g, f_b):
    B, S, C = p_f.shape
    T, W, H, D = FOX_TILE, FOX_WIDTH, FOX_HEADS, FOX_HEAD
    bf16 = jnp.bfloat16
    ones_bd = jnp.kron(jnp.eye(H, dtype=bf16), jnp.ones((D, D), bf16))
    fb = jnp.zeros((1, FOX_GATE_LANES), jnp.float32).at[0, :H].set(f_b)
    args = [qn_g.reshape(1, W), kn_g.reshape(1, W), fb, ones_bd]
    full = lambda a: pl.BlockSpec(a.shape, lambda b, i: (0,) * a.ndim)
    return pl.pallas_call(
        _fox_prep_kernel,
        grid=(B, S // T),
        in_specs=[pl.BlockSpec((1, T, C), lambda b, i: (b, i, 0))] + [full(a) for a in args],
        out_specs=[pl.BlockSpec((1, H, T, FOX_AUG), lambda b, i: (b, 0, i, 0)),
                   pl.BlockSpec((1, H, T, FOX_AUG), lambda b, i: (b, 0, i, 0)),
                   pl.BlockSpec((1, H, 1, D, T), lambda b, i: (b, 0, i, 0, 0))],
        out_shape=[jax.ShapeDtypeStruct((B, H, S, FOX_AUG), bf16),
                   jax.ShapeDtypeStruct((B, H, S, FOX_AUG), bf16),
                   jax.ShapeDtypeStruct((B, H, S // T, D, T), bf16)],
        scratch_shapes=[pltpu.VMEM((8, FOX_GATE_LANES), jnp.float32)],
        compiler_params=pltpu.CompilerParams(dimension_semantics=("arbitrary", "arbitrary"),
                                             vmem_limit_bytes=VMEM_LIMIT_BYTES),
        name="fox_prep",
    )(p_f, *args)


def _fox_attn_kernel(qa_ref, ka_ref, vt_ref, og_ref, o_ref):
    f32, bf16 = jnp.float32, jnp.bfloat16
    T, D = FOX_TILE, FOX_HEAD
    i = pl.program_id(2)
    k_pos = lax.broadcasted_iota(jnp.int32, (T, T), 0)
    q_pos = lax.broadcasted_iota(jnp.int32, (T, T), 1)
    outs = []
    for j in range(LANES // D):
        qa = qa_ref[0, j]

        def kv_step(kb, carry):
            m, l, acc = carry
            ka = ka_ref[0, j, pl.ds(pl.multiple_of(kb * T, T), T), :]
            s = lax.dot_general(ka, qa, _NT, preferred_element_type=f32)
            s = jnp.where(k_pos + kb * T <= q_pos + i * T, s, -jnp.inf)
            m_new = jnp.maximum(m, jnp.max(s, axis=0, keepdims=True))
            alpha = jnp.exp(m - m_new)
            pr = jnp.exp(s - m_new)
            l = l * alpha + jnp.sum(pr, axis=0, keepdims=True)
            acc = acc * alpha + jnp.dot(vt_ref[0, j, kb], pr.astype(bf16), preferred_element_type=f32)
            return m_new, l, acc

        init = (jnp.full((1, T), -jnp.inf, f32), jnp.zeros((1, T), f32), jnp.zeros((D, T), f32))
        _, l, acc = lax.fori_loop(0, i + 1, kv_step, init)
        outs.append((acc / l).T)
    o_ref[0] = jax.nn.sigmoid(og_ref[0]) * jnp.concatenate(outs, axis=1)


def _fox_attn(qa, ka, vt, p_f):
    B, H, S, A = qa.shape
    T, D = FOX_TILE, FOX_HEAD
    hp = LANES // D
    og_block0 = 3 * FOX_WIDTH // LANES
    return pl.pallas_call(
        _fox_attn_kernel,
        grid=(B, H // hp, S // T),
        in_specs=[pl.BlockSpec((1, hp, T, A), lambda b, g, i: (b, g, i, 0)),
                  pl.BlockSpec((1, hp, S, A), lambda b, g, i: (b, g, 0, 0)),
                  pl.BlockSpec((1, hp, S // T, D, T), lambda b, g, i: (b, g, 0, 0, 0)),
                  pl.BlockSpec((1, T, LANES), lambda b, g, i: (b, i, og_block0 + g))],
        out_specs=pl.BlockSpec((1, T, LANES), lambda b, g, i: (b, i, g)),
        out_shape=jax.ShapeDtypeStruct((B, S, FOX_WIDTH), jnp.float32),
        name="fox_attn",
    )(qa, ka, vt, p_f)


def kernel(x, l0_w_in, l0_rwkv_mu, l0_rwkv_w0, l0_rwkv_w_up, l0_rwkv_a0, l0_rwkv_a_up,
           l0_rwkv_g_up, l0_rwkv_k_k, l0_rwkv_k_a, l0_rwkv_r_k, l0_rwkv_ln_g, l0_rwkv_ln_b,
           l0_mlstm_conv_w, l0_mlstm_conv_b, l0_mlstm_ig_b, l0_mlstm_fg_b,
           l0_mlstm_hn_g, l0_mlstm_hn_b, l0_w_out, l0_ln1_g, l0_ln1_b,
           l0_peer_wq, l0_peer_k1, l0_peer_k2, l0_peer_u, l0_peer_v, l0_ln2_g, l0_ln2_b,
           l1_w_in, l1_fox_qn_g, l1_fox_kn_g, l1_fox_f_b, l1_w_out, l1_ln1_g, l1_ln1_b,
           l1_peer_wq, l1_peer_k1, l1_peer_k2, l1_peer_u, l1_peer_v, l1_ln2_g, l1_ln2_b):
    B, S, D = x.shape
    n = B * S
    bf16 = jnp.bfloat16
    x2 = x.reshape(n, D)

    def pad_lanes(w):
        return jnp.pad(w, ((0, 0), (0, LANES - w.shape[1])))

    m_main = 4 * MLSTM_WIDTH
    w0 = jnp.concatenate([l0_w_in[:, :RWKV_COLS + m_main],
                          pad_lanes(l0_w_in[:, RWKV_COLS + m_main:])], axis=1).astype(bf16)
    p_r, p_m = _proj(x2, w0, (RWKV_COLS, m_main + LANES))
    y_a = _rwkv_pallas(p_r.reshape(B, S, -1), l0_rwkv_mu, l0_rwkv_w0, l0_rwkv_w_up, l0_rwkv_a0,
                       l0_rwkv_a_up, l0_rwkv_g_up, l0_rwkv_k_k, l0_rwkv_k_a, l0_rwkv_r_k,
                       l0_rwkv_ln_g, l0_rwkv_ln_b)
    y_b = _mlstm_pallas(p_m.reshape(B, S, -1), l0_mlstm_conv_w, l0_mlstm_conv_b, l0_mlstm_ig_b,
                        l0_mlstm_fg_b, l0_mlstm_hn_g, l0_mlstm_hn_b)
    x2 = _out_proj_ln([y_a.reshape(n, -1), y_b.reshape(n, -1)], l0_w_out.astype(bf16), x2,
                      l0_ln1_g, l0_ln1_b)
    y = _peer_pallas(x2, l0_peer_wq, l0_peer_k1, l0_peer_k2, l0_peer_u, l0_peer_v)
    x2 = _resid_ln(x2, y, l0_ln2_g, l0_ln2_b)

    f_main = 4 * FOX_WIDTH
    w1 = jnp.concatenate([l1_w_in[:, :f_main], pad_lanes(l1_w_in[:, f_main:])], axis=1).astype(bf16)
    (p_f,) = _proj(x2, w1, (f_main + LANES,))
    p_f = p_f.reshape(B, S, -1)
    qa, ka, vt = _fox_prep(p_f, l1_fox_qn_g, l1_fox_kn_g, l1_fox_f_b)
    o = _fox_attn(qa, ka, vt, p_f)
    x2 = _out_proj_ln([o.reshape(n, -1)], l1_w_out.astype(bf16), x2, l1_ln1_g, l1_ln1_b)
    y = _peer_pallas(x2, l1_peer_wq, l1_peer_k1, l1_peer_k2, l1_peer_u, l1_peer_v)
    x2 = _resid_ln(x2, y, l1_ln2_g, l1_ln2_b)
    return x2.reshape(B, S, D)
```

```python
import functools

import jax
import jax.numpy as jnp
from jax import lax
from jax.experimental import pallas as pl
from jax.experimental.pallas import tpu as pltpu

D_MODEL = 1024
DEPTH = 2
DN_ALPHA = (2.0 * DEPTH) ** 0.25
LN_EPS = 1e-5

RWKV_WIDTH = D_MODEL // 2
RWKV_HEAD = 64
RWKV_HEADS = RWKV_WIDTH // RWKV_HEAD
RWKV_W_RANK = 64
RWKV_A_RANK = 64
RWKV_G_RANK = 128
RWKV_GN_EPS = 1e-5 * RWKV_HEAD
RWKV_COLS = 3 * RWKV_WIDTH + RWKV_W_RANK + RWKV_A_RANK + RWKV_G_RANK

MLSTM_WIDTH = D_MODEL // 2
MLSTM_HEAD = 128
MLSTM_HEADS = MLSTM_WIDTH // MLSTM_HEAD
MLSTM_CHUNK = 64

FOX_HEAD = 64
FOX_HEADS = D_MODEL // FOX_HEAD
FOX_WIDTH = FOX_HEADS * FOX_HEAD
FOX_QBLOCK = 128

PEER_HEADS = 8
PEER_NKEYS = 128
PEER_TOPK = 16
PEER_DKEY = 256
PEER_DHALF = PEER_DKEY // 2
PEER_TOKBLOCK = 128


def _split_cols(p, sizes):
    out, start = [], 0
    for s in sizes:
        out.append(p[..., start:start + s])
        start += s
    return out


def _resid_ln_kernel(x_ref, y_ref, g_ref, b_ref, o_ref):
    z = DN_ALPHA * x_ref[...] + y_ref[...]
    mu = jnp.mean(z, axis=-1, keepdims=True)
    zc = z - mu
    var = jnp.mean(zc * zc, axis=-1, keepdims=True)
    o_ref[...] = zc * lax.rsqrt(var + LN_EPS) * g_ref[...] + b_ref[...]


def _resid_ln(x2, y2, g, b, tm=512):
    n, d = x2.shape
    row = pl.BlockSpec((tm, d), lambda i: (i, 0))
    vec = pl.BlockSpec((1, d), lambda i: (0, 0))
    return pl.pallas_call(
        _resid_ln_kernel,
        grid=(n // tm,),
        in_specs=[row, row, vec, vec],
        out_specs=row,
        out_shape=jax.ShapeDtypeStruct((n, d), jnp.float32),
        name="resid_ln",
    )(x2, y2, g.reshape(1, d), b.reshape(1, d))


VMEM_LIMIT_BYTES = 56 * 1024 * 1024


def _proj_kernel(x_ref, w_ref, *o_refs):
    xb = x_ref[...].astype(jnp.bfloat16)
    start = 0
    for o_ref in o_refs:
        width = o_ref.shape[1]
        o_ref[...] = jnp.dot(xb, w_ref[:, start:start + width], preferred_element_type=jnp.float32)
        start += width


def _proj(x2, w_bf16, widths, tm=256):
    n, d = x2.shape
    assert sum(widths) == w_bf16.shape[1]
    return pl.pallas_call(
        _proj_kernel,
        grid=(n // tm,),
        in_specs=[pl.BlockSpec((tm, d), lambda i: (i, 0)),
                  pl.BlockSpec(w_bf16.shape, lambda i: (0, 0))],
        out_specs=[pl.BlockSpec((tm, w), lambda i: (i, 0)) for w in widths],
        out_shape=[jax.ShapeDtypeStruct((n, w), jnp.float32) for w in widths],
        compiler_params=pltpu.CompilerParams(vmem_limit_bytes=VMEM_LIMIT_BYTES),
        name="in_proj",
    )(x2, w_bf16)


def _out_proj_ln_kernel(*refs, n_parts):
    y_refs, (w_ref, x_ref, g_ref, b_ref, o_ref) = refs[:n_parts], refs[n_parts:]
    acc, start = None, 0
    for y_ref in y_refs:
        width = y_ref.shape[1]
        part = jnp.dot(y_ref[...].astype(jnp.bfloat16), w_ref[start:start + width, :],
                       preferred_element_type=jnp.float32)
        acc = part if acc is None else acc + part
        start += width
    z = DN_ALPHA * x_ref[...] + acc
    mu = jnp.mean(z, axis=-1, keepdims=True)
    zc = z - mu
    var = jnp.mean(zc * zc, axis=-1, keepdims=True)
    o_ref[...] = zc * lax.rsqrt(var + LN_EPS) * g_ref[...] + b_ref[...]


def _out_proj_ln(ys, w_bf16, x2, g, b, tm=256):
    n, d = x2.shape
    row = lambda width: pl.BlockSpec((tm, width), lambda i: (i, 0))
    vec = pl.BlockSpec((1, d), lambda i: (0, 0))
    return pl.pallas_call(
        functools.partial(_out_proj_ln_kernel, n_parts=len(ys)),
        grid=(n // tm,),
        in_specs=[row(y.shape[1]) for y in ys]
                 + [pl.BlockSpec(w_bf16.shape, lambda i: (0, 0)), row(d), vec, vec],
        out_specs=row(d),
        out_shape=jax.ShapeDtypeStruct((n, d), jnp.float32),
        name="out_proj_ln",
    )(*ys, w_bf16, x2, g.reshape(1, d), b.reshape(1, d))


def _head_norm(y, g, b, eps):
    mu = jnp.mean(y, -1, keepdims=True)
    var = jnp.mean(jnp.square(y - mu), -1, keepdims=True)
    return (y - mu) * lax.rsqrt(var + eps) * g + b


def _rms_norm(y, g):
    return y * lax.rsqrt(jnp.mean(y * y, -1, keepdims=True) + 1e-6) * g


def _token_shift(z):
    return jnp.pad(z, ((0, 0), (1, 0), (0, 0)))[:, :-1]


def _causal_conv(z, w, b):
    c = z.shape[-1]
    out = lax.conv_general_dilated(z, w[:, None, :], window_strides=(1,),
                                   padding=((w.shape[0] - 1, 0),),
                                   dimension_numbers=('NWC', 'WIO', 'NWC'),
                                   feature_group_count=c)
    return out + b


def _rwkv7_mix(p, mu, w0, w_up, a0, a_up, g_up, k_k, k_a, r_k, ln_g, ln_b):
    B, S, _ = p.shape
    H, N = RWKV_HEADS, RWKV_HEAD
    p = p + (_token_shift(p) - p) * mu
    r, k, v, xw, xa, xg = _split_cols(p, (RWKV_WIDTH, RWKV_WIDTH, RWKV_WIDTH,
                                          RWKV_W_RANK, RWKV_A_RANK, RWKV_G_RANK))
    log_w = -jnp.exp(-jax.nn.softplus(-(w0 + jnp.tanh(xw) @ w_up)) - 0.5)
    a = jax.nn.sigmoid(a0 + xa @ a_up)
    g = jax.nn.sigmoid(xg) @ g_up
    heads = lambda t: t.reshape(B, S, H, N)
    kk = heads(k * k_k)
    kk = kk / jnp.maximum(jnp.sqrt(jnp.sum(kk * kk, -1, keepdims=True)), 1e-12)
    k = k * (1.0 + (a - 1.0) * k_a)
    r_h, k_h, v_h, a_h = heads(r), heads(k), heads(v), heads(a)
    w_h = jnp.exp(heads(log_w))

    def step(state, inp):
        r_t, w_t, k_t, v_t, kk_t, a_t = inp
        s_kk = jnp.einsum('bhvk,bhk->bhv', state, kk_t)
        state = (state * w_t[:, :, None, :]
                 - s_kk[..., None] * (kk_t * a_t)[:, :, None, :]
                 + v_t[..., None] * k_t[:, :, None, :])
        return state, jnp.einsum('bhvk,bhk->bhv', state, r_t)

    seq_first = lambda t: jnp.moveaxis(t, 1, 0)
    state0 = jnp.zeros((B, H, N, N), jnp.float32)
    _, y = lax.scan(step, state0, (seq_first(r_h), seq_first(w_h), seq_first(k_h),
                                   seq_first(v_h), seq_first(kk), seq_first(a_h)))
    y = jnp.moveaxis(y, 0, 1)
    y = _head_norm(y, ln_g, ln_b, RWKV_GN_EPS)
    y = y + jnp.sum(r_h * k_h * r_k, -1, keepdims=True) * v_h
    return y.reshape(B, S, RWKV_WIDTH) * g


def _mlstm_chunkwise(q, k, v, ig, lf):
    B, S, H, D = q.shape
    L = MLSTM_CHUNK
    NC = S // L

    def to_chunks(t):
        t = t.reshape((B, NC, L, H) + t.shape[3:])
        return jnp.moveaxis(t, (1, 3), (0, 2))

    causal = jnp.tril(jnp.ones((L, L), dtype=bool))

    def body(carry, inp):
        C, n, m = carry
        qc, kc, vc, igc, lfc = inp
        b = jnp.cumsum(lfc, axis=-1)
        dmat = jnp.where(causal, b[..., :, None] - b[..., None, :] + igc[..., None, :], -jnp.inf)
        inter = b + m[..., None]
        m_t = jnp.maximum(inter, jnp.max(dmat, -1))
        weights = jnp.exp(dmat - m_t[..., None])
        sc = jnp.einsum('bhtd,bhsd->bhts', qc, kc) * weights
        carry_in = jnp.exp(inter - m_t)
        num = (jnp.einsum('bhts,bhsd->bhtd', sc, vc)
               + carry_in[..., None] * jnp.einsum('bhtk,bhkv->bhtv', qc, C))
        den = jnp.sum(sc, -1) + carry_in * jnp.einsum('bhtk,bhk->bht', qc, n)
        h = num / jnp.maximum(jnp.abs(den), jnp.exp(-m_t))[..., None]
        b_last = b[..., -1]
        gs = b_last[..., None] - b + igc
        m_new = jnp.maximum(b_last + m, jnp.max(gs, -1))
        ws = jnp.exp(gs - m_new[..., None])
        keep = jnp.exp(b_last + m - m_new)
        C = keep[..., None, None] * C + jnp.einsum('bhs,bhsk,bhsv->bhkv', ws, kc, vc)
        n = keep[..., None] * n + jnp.einsum('bhs,bhsk->bhk', ws, kc)
        return (C, n, m_new), h

    f32 = jnp.float32
    init = (jnp.zeros((B, H, D, D), f32), jnp.zeros((B, H, D), f32), jnp.zeros((B, H), f32))
    _, h = lax.scan(body, init, (to_chunks(q), to_chunks(k), to_chunks(v),
                                 to_chunks(ig), to_chunks(lf)))
    return jnp.moveaxis(h, (0, 2), (1, 3)).reshape(B, S, H, D)


def _mlstm_mix(p, conv_w, conv_b, ig_b, fg_b, hn_g, hn_b):
    B, S, _ = p.shape
    H, N = MLSTM_HEADS, MLSTM_HEAD
    q, k, v, o, ig, fg = _split_cols(p, (MLSTM_WIDTH, MLSTM_WIDTH, MLSTM_WIDTH, MLSTM_WIDTH, H, H))
    qk = jax.nn.silu(_causal_conv(jnp.concatenate([q, k], -1), conv_w, conv_b))
    q, k = qk[..., :MLSTM_WIDTH], qk[..., MLSTM_WIDTH:]
    heads = lambda t: t.reshape(B, S, H, N)
    q = heads(q) * (N ** -0.5)
    ig = ig + ig_b
    lf = jax.nn.log_sigmoid(fg + fg_b)
    h = _mlstm_chunkwise(q, heads(k), heads(v), ig, lf)
    h = _head_norm(h, hn_g, hn_b, LN_EPS).reshape(B, S, MLSTM_WIDTH)
    return jax.nn.sigmoid(o) * h


def _even_mixer(x, w_in, mu, w0, w_up, a0, a_up, g_up, k_k, k_a, r_k, rln_g, rln_b,
                conv_w, conv_b, ig_b, fg_b, hn_g, hn_b, w_out):
    p = x @ w_in
    y_a = _rwkv_pallas(p[..., :RWKV_COLS], mu, w0, w_up, a0, a_up, g_up, k_k, k_a, r_k, rln_g, rln_b)
    y_b = _mlstm_mix(p[..., RWKV_COLS:], conv_w, conv_b, ig_b, fg_b, hn_g, hn_b)
    return jnp.concatenate([y_a, y_b], -1) @ w_out


def _fox_attention(q, k, v, logf):
    B, S, H, D = q.shape
    NB = S // FOX_QBLOCK
    c = jnp.cumsum(logf, axis=1)
    c_key = jnp.transpose(c, (0, 2, 1))[:, :, None, :]
    q_blocks = jnp.moveaxis(q.reshape(B, NB, FOX_QBLOCK, H, D), 1, 0)
    c_blocks = jnp.moveaxis(c.reshape(B, NB, FOX_QBLOCK, H), 1, 0)
    key_pos = jnp.arange(S)
    scale = D ** -0.5

    def one_block(args):
        qb, cb, blk = args
        q_pos = blk * FOX_QBLOCK + jnp.arange(FOX_QBLOCK)
        logits = jnp.einsum('bqhd,bkhd->bhqk', qb, k) * scale
        logits = logits + jnp.transpose(cb, (0, 2, 1))[..., None] - c_key
        logits = jnp.where(key_pos[None, :] <= q_pos[:, None], logits, -jnp.inf)
        probs = jax.nn.softmax(logits, axis=-1)
        return jnp.einsum('bhqk,bkhd->bqhd', probs, v)

    out = lax.map(one_block, (q_blocks, c_blocks, jnp.arange(NB)))
    return jnp.moveaxis(out, 0, 1).reshape(B, S, H, D)


def _odd_mixer(x, w_in, qn_g, kn_g, f_b, w_out):
    B, S, _ = x.shape
    p = x @ w_in
    q, k, v, og, fl = _split_cols(p, (FOX_WIDTH, FOX_WIDTH, FOX_WIDTH, FOX_WIDTH, FOX_HEADS))
    heads = lambda t: t.reshape(B, S, FOX_HEADS, FOX_HEAD)
    q = _rms_norm(heads(q), qn_g)
    k = _rms_norm(heads(k), kn_g)
    logf = jax.nn.log_sigmoid(fl + f_b)
    o = _fox_attention(q, k, heads(v), logf).reshape(B, S, FOX_WIDTH)
    return (jax.nn.sigmoid(og) * o) @ w_out


def _peer_ffn(x, w_q, sub_k1, sub_k2, u_tab, v_tab):
    B, S, Dm = x.shape
    K = PEER_TOPK
    xt = x.reshape((B * S) // PEER_TOKBLOCK, PEER_TOKBLOCK, Dm)

    def block(xb):
        T = xb.shape[0]
        q = (xb @ w_q).reshape(T, PEER_HEADS, 2, PEER_DHALF)
        s1 = jnp.einsum('thd,hnd->thn', q[:, :, 0], sub_k1)
        s2 = jnp.einsum('thd,hnd->thn', q[:, :, 1], sub_k2)
        v1, i1 = lax.top_k(s1, K)
        v2, i2 = lax.top_k(s2, K)
        cand = (v1[..., :, None] + v2[..., None, :]).reshape(T, PEER_HEADS, K * K)
        sc, ci = lax.top_k(cand, K)
        e1 = jnp.take_along_axis(i1, ci // K, axis=-1)
        e2 = jnp.take_along_axis(i2, ci % K, axis=-1)
        eid = e1 * PEER_NKEYS + e2
        gate = jax.nn.softmax(sc, axis=-1)
        act = jax.nn.gelu(jnp.einsum('td,thkd->thk', xb, u_tab[eid]), approximate=False)
        return jnp.einsum('thk,thkd->td', gate * act, v_tab[eid])

    return lax.map(block, xt).reshape(B, S, Dm)


PEER_EXPERTS = PEER_NKEYS * PEER_NKEYS
PEER_TOK_TILE = 256
PEER_EXPERT_CHUNK = 1024
PEER_E1_GROUP = 4
PEER_E2_SUB = 32
PEER_MXU_PIECES = 1
LANES = 128
MXU_DIM = 256
INV_SQRT2 = 0.7071067811865476
_NT = (((1,), (1,)), ((), ()))


def _topk_desc(work, k):
    vals = []
    for _ in range(k):
        m = jnp.max(work, axis=0, keepdims=True)
        vals.append(m)
        work = jnp.where(work >= m, -jnp.inf, work)
    return vals


def _peer_select_kernel(x_ref, wqt_ref, k1_ref, k2_ref,
                        phi_ref, a1_ref, a2_ref, qt_ref):
    f32, bf16 = jnp.float32, jnp.bfloat16
    K = PEER_TOPK
    qt_ref[...] = lax.dot_general(wqt_ref[...], x_ref[...], _NT, preferred_element_type=f32)
    n_groups = x_ref.shape[0] // LANES

    def head(h, carry):
        base = pl.multiple_of(h * PEER_DKEY, PEER_DKEY)
        q1 = qt_ref[pl.ds(base, PEER_DHALF), :].astype(bf16)
        q2 = qt_ref[pl.ds(base + PEER_DHALF, PEER_DHALF), :].astype(bf16)
        s1 = jnp.dot(k1_ref[h], q1, preferred_element_type=f32)
        s2 = jnp.dot(k2_ref[h], q2, preferred_element_type=f32)
        for c in range(n_groups):
            sl = slice(c * LANES, (c + 1) * LANES)
            s1c, s2c = s1[:, sl], s2[:, sl]
            v1 = _topk_desc(s1c, K)
            v2 = _topk_desc(s2c, K)
            v2s = jnp.concatenate(v2, axis=0)
            cand = jnp.concatenate([v1[a] + v2s for a in range(K)], axis=0)
            vc = _topk_desc(cand, K + 1)
            cmax = vc[0]
            z = jnp.sum(jnp.where(cand >= vc[K - 1], jnp.exp(cand - cmax), 0.0), axis=0, keepdims=True)
            cut = 0.5 * (vc[K - 1] + vc[K])
            s1m = jnp.where(s1c >= v1[K - 1], s1c, -jnp.inf)
            s2m = jnp.where(s2c >= v2[K - 1], s2c, -jnp.inf)
            inv_z = 1.0 / z
            phi_ref[h, :, sl] = jnp.exp(cut - s1m - v2[0]) * inv_z
            a1_ref[h, :, sl] = jnp.exp(s1m - v1[0])
            a2_ref[h, :, sl] = jnp.exp(s2m - v2[0]) * inv_z
        return carry

    lax.fori_loop(0, PEER_HEADS, head, 0)


def _peer_select(xb, wqt, k1, k2):
    n, d = xb.shape
    T = PEER_TOK_TILE
    H, NK = PEER_HEADS, PEER_NKEYS
    full = lambda shape: pl.BlockSpec(shape, lambda i: (0,) * len(shape))
    tok3 = lambda rows: pl.BlockSpec((H, rows, T), lambda i: (0, 0, i))
    f32 = jnp.float32
    return pl.pallas_call(
        _peer_select_kernel,
        grid=(n // T,),
        in_specs=[pl.BlockSpec((T, d), lambda i: (i, 0)), full(wqt.shape), full(k1.shape), full(k2.shape)],
        out_specs=[tok3(NK)] * 3,
        out_shape=[jax.ShapeDtypeStruct((H, NK, n), f32)] * 3,
        scratch_shapes=[pltpu.VMEM((H * PEER_DKEY, T), f32)],
        name="peer_select",
    )(xb, wqt, k1, k2)


def _peer_expert_kernel(xt_ref, u_ref, v_ref, phi_ref, a1_ref, a2_ref,
                        o_ref, acc_ref, hn_ref, hc_ref, wn_ref, wc_ref):
    f32, bf16 = jnp.float32, jnp.bfloat16
    s = pl.program_id(0)
    n_chunks = PEER_EXPERTS // PEER_EXPERT_CHUNK
    c_chunk = jnp.maximum(s - 2, 0) % n_chunks

    @pl.when(s == 0)
    def _():
        hn_ref[...] = jnp.zeros_like(hn_ref)
        wn_ref[...] = jnp.zeros_like(wn_ref)

    @pl.when(c_chunk == 0)
    def _():
        acc_ref[...] = jnp.zeros_like(acc_ref)

    T = xt_ref.shape[1]
    NK, G, SB = PEER_NKEYS, PEER_E1_GROUP, PEER_E2_SUB
    hc_ref[...] = hn_ref[...]
    wc_ref[...] = wn_ref[...]
    CE, d = PEER_EXPERT_CHUNK, v_ref.shape[1]
    P = PEER_MXU_PIECES

    def stage_a(k):
        rows = CE // P
        u_blk = pltpu.bitcast(u_ref[k * rows // 2:(k + 1) * rows // 2, :], bf16)
        hn_ref[k * rows:(k + 1) * rows, :] = jnp.dot(u_blk, xt_ref[...], preferred_element_type=f32)

    def stage_c(k):
        cols = slice(k * d // P, (k + 1) * d // P)
        acc_ref[:, cols] += lax.dot_general(wc_ref[...], v_ref[:, cols], _TN, preferred_element_type=f32)

    def stage_b(q, c, sb):
        ts = slice(c * LANES, (c + 1) * LANES)
        e2s = slice(sb * SB, (sb + 1) * SB)
        g = [None] * G
        for hd in range(PEER_HEADS):
            a2 = a2_ref[hd, e2s, ts]
            for e in range(G):
                r = q * G + e
                term = jnp.where(a2 > phi_ref[hd, r:r + 1, ts], a1_ref[hd, r:r + 1, ts] * a2, 0.0)
                g[e] = term if g[e] is None else g[e] + term
        for e in range(G):
            lo = (q * G + e) * NK + sb * SB
            hh = hc_ref[lo:lo + SB, ts]
            wn_ref[lo:lo + SB, ts] = (g[e] * (0.5 * hh * (1.0 + lax.erf(hh * INV_SQRT2)))).astype(bf16)

    b_blocks = [(q, c, sb) for q in range(CE // (G * NK)) for c in range(T // LANES)
                for sb in range(NK // SB)]
    stride = len(b_blocks) // P
    for t, blk in enumerate(b_blocks):
        stage_b(*blk)
        if t % stride == 0:
            stage_a(t // stride)
        if t % stride == stride // 2:
            stage_c(t // stride)

    @pl.when(c_chunk == n_chunks - 1)
    def _():
        o_ref[...] = acc_ref[...]


def _peer_experts(xtb, ub, vb, phi, a1, a2):
    d, n = xtb.shape
    T, CE = PEER_TOK_TILE, PEER_EXPERT_CHUNK
    H, NK = PEER_HEADS, PEER_NKEYS
    e1_per_chunk = CE // NK
    nc = PEER_EXPERTS // CE
    last = (n // T) * nc - 1
    at = lambda s, lag: jnp.clip(s - lag, 0, last)
    return pl.pallas_call(
        _peer_expert_kernel,
        grid=(last + 3,),
        in_specs=[
            pl.BlockSpec((d, T), lambda s: (0, at(s, 0) // nc)),
            pl.BlockSpec((CE // 2, d), lambda s: (at(s, 0) % nc, 0)),
            pl.BlockSpec((CE, d), lambda s: (at(s, 2) % nc, 0)),
            pl.BlockSpec((H, e1_per_chunk, T), lambda s: (0, at(s, 1) % nc, at(s, 1) // nc)),
            pl.BlockSpec((H, e1_per_chunk, T), lambda s: (0, at(s, 1) % nc, at(s, 1) // nc)),
            pl.BlockSpec((H, NK, T), lambda s: (0, 0, at(s, 1) // nc)),
        ],
        out_specs=pl.BlockSpec((T, d), lambda s: (at(s, 2) // nc, 0)),
        out_shape=jax.ShapeDtypeStruct((n, d), jnp.float32),
        scratch_shapes=[pltpu.VMEM((T, d), jnp.float32),
                        pltpu.VMEM((CE, T), jnp.float32), pltpu.VMEM((CE, T), jnp.float32),
                        pltpu.VMEM((CE, T), jnp.bfloat16), pltpu.VMEM((CE, T), jnp.bfloat16)],
        compiler_params=pltpu.CompilerParams(dimension_semantics=("arbitrary",)),
        name="peer_experts",
    )(xtb, ub, vb, phi, a1, a2)


def _pack_row_pairs(t_bf16):
    rows, cols = t_bf16.shape
    pairs = jnp.swapaxes(t_bf16.reshape(rows // 2, 2, cols), -1, -2)
    return lax.bitcast_convert_type(pairs, jnp.uint32)


def _peer_pallas(x2, w_q, sub_k1, sub_k2, u_tab, v_tab):
    bf16 = jnp.bfloat16
    xb = x2.astype(bf16)
    sel = _peer_select(xb, w_q.T.astype(bf16), sub_k1.astype(bf16), sub_k2.astype(bf16))
    return _peer_experts(xb.T, _pack_row_pairs(u_tab.astype(bf16)), v_tab.astype(bf16), *sel)


RWKV_CHUNK = 64
RWKV_PAIR = LANES // RWKV_HEAD
RWKV_PAIRS = RWKV_HEADS // RWKV_PAIR
_NN = (((1,), (0,)), ((), ()))
_TN = (((0,), (0,)), ((), ()))


def _split_bf16(a):
    hi = a.astype(jnp.bfloat16)
    lo = (a - hi.astype(jnp.float32)).astype(jnp.bfloat16)
    return hi, lo


def _mm2(a, b, dims):
    d = lambda u, v: lax.dot_general(u, v, dims, preferred_element_type=jnp.float32)
    (ah, al), (bh, bl) = _split_bf16(a), _split_bf16(b)
    return d(ah, bh) + d(ah, bl) + d(al, bh)


def _mm_exact_rhs(a, b_exact, dims, terms=2):
    d = lambda u: lax.dot_general(u, b_exact, dims, preferred_element_type=jnp.float32)
    out, rem = None, a
    for _ in range(terms):
        piece = rem.astype(jnp.bfloat16)
        rem = rem - piece.astype(jnp.float32)
        out = d(piece) if out is None else out + d(piece)
    return out


def _mm_exact_lhs(a_exact, b, dims, terms=3):
    d = lambda v: lax.dot_general(a_exact, v, dims, preferred_element_type=jnp.float32)
    out, rem = None, b
    for _ in range(terms):
        piece = rem.astype(jnp.bfloat16)
        rem = rem - piece.astype(jnp.float32)
        out = d(piece) if out is None else out + d(piece)
    return out


def _softplus(z):
    return jnp.maximum(z, 0.0) + jnp.log1p(jnp.exp(-jnp.abs(z)))


def _rwkv_kernel(p_ref, prev_ref, mu_ref, w0_ref, wup_ref, a0_ref, aup_ref, gup_ref,
                 kk_ref, ka_ref, rk_ref, lng_ref, lnb_ref, ones_ref,
                 o_ref, state_ref):
    f32, bf16 = jnp.float32, jnp.bfloat16
    L, W = RWKV_CHUNK, RWKV_WIDTH
    c_idx = pl.program_id(1)

    @pl.when(c_idx == 0)
    def _():
        state_ref[...] = jnp.zeros_like(state_ref)

    p = p_ref[0]
    row = lax.broadcasted_iota(jnp.int32, p.shape, 0)
    prev_row = jnp.where(c_idx == 0, 0.0, prev_ref[0, 7:8, :])
    shifted = jnp.where(row == 0, prev_row, pltpu.roll(p, 1, axis=0))
    p = p + (shifted - p) * mu_ref[...]
    r, k, v = p[:, 0:W], p[:, W:2 * W], p[:, 2 * W:3 * W]
    o = 3 * W
    xw = p[:, o:o + RWKV_W_RANK]
    xa = p[:, o + RWKV_W_RANK:o + RWKV_W_RANK + RWKV_A_RANK]
    xg = p[:, o + RWKV_W_RANK + RWKV_A_RANK:]
    dotd = lambda u, m: jnp.dot(u.astype(bf16), m, preferred_element_type=f32)
    lw = -jnp.exp(-_softplus(-(w0_ref[...] + dotd(jnp.tanh(xw), wup_ref[...]))) - 0.5)
    a = jax.nn.sigmoid(a0_ref[...] + dotd(xa, aup_ref[...]))
    g = dotd(jax.nn.sigmoid(xg), gup_ref[...])
    ones_bd = ones_ref[...]
    head_sum = lambda t: _mm_exact_rhs(t, ones_bd, _NN)
    kk = k * kk_ref[...]
    kk = kk / jnp.maximum(jnp.sqrt(head_sum(kk * kk)), 1e-12)
    k = k * (1.0 + (a - 1.0) * ka_ref[...])

    lane = lax.broadcasted_iota(jnp.int32, (L, LANES), 1)
    trow = lax.broadcasted_iota(jnp.int32, (L, LANES), 0)
    s_in = lane % RWKV_HEAD
    strict, incl = s_in < trow, s_in <= trow
    eye_pair = (s_in == trow).astype(f32)
    m0 = lane < RWKV_HEAD
    tril = (lax.broadcasted_iota(jnp.int32, (L, L), 1)
            <= lax.broadcasted_iota(jnp.int32, (L, L), 0)).astype(bf16)
    r2 = lax.broadcasted_iota(jnp.int32, (LANES, LANES), 0) // RWKV_HEAD
    c2 = lax.broadcasted_iota(jnp.int32, (LANES, LANES), 1) // RWKV_HEAD
    bd_mask = r2 == c2

    def bd(t):
        return jnp.concatenate([jnp.where(m0, t, 0.0), jnp.where(m0, 0.0, t)], axis=0)

    ys = []
    for pr in range(RWKV_PAIRS):
        sl = slice(pr * LANES, (pr + 1) * LANES)
        rp, kp, vp, kkp, ap, lwp = r[:, sl], k[:, sl], v[:, sl], kk[:, sl], a[:, sl], lw[:, sl]
        c = _mm_exact_lhs(tril, lwp, _NN)
        eg, egp, eni = jnp.exp(c), jnp.exp(c - lwp), jnp.exp(-c)
        at, bt, kt, rt = -kkp * egp, kkp * ap * eni, kp * eni, rp * eg
        g_last = jnp.exp(c[L - 1:L, :])
        gram = _mm2(jnp.concatenate([at, rt], axis=0),
                    jnp.concatenate([bd(bt), bd(kt)], axis=0), _NT)
        n_ab = jnp.where(strict, gram[0:L, 0:LANES], 0.0)
        a_ak = jnp.where(strict, gram[0:L, LANES:], 0.0)
        a_rb = jnp.where(incl, gram[L:, 0:LANES], 0.0)
        a_rk = jnp.where(incl, gram[L:, LANES:], 0.0)
        tinv, m = eye_pair + n_ab, n_ab
        for _ in range(5):
            m = _mm2(m, bd(m), _NN)
            tinv = tinv + _mm2(m, bd(tinv), _NN)
        s0 = state_ref[pr]
        pq = _mm2(at, s0, _NT) + _mm2(a_ak, bd(vp), _NN)
        u = _mm2(tinv, bd(pq), _NN)
        y = (_mm2(rt, s0, _NT)
             + _mm2(jnp.concatenate([a_rb, a_rk], axis=1),
                    jnp.concatenate([bd(u), bd(vp)], axis=0), _NN))
        upd = _mm2(jnp.concatenate([u, vp], axis=0), jnp.concatenate([bt, kt], axis=0), _TN)
        state_ref[pr] = (s0 + jnp.where(bd_mask, upd, 0.0)) * g_last
        ys.append(y)
    y = jnp.concatenate(ys, axis=1)

    inv_n = 1.0 / RWKV_HEAD
    mean = head_sum(y) * inv_n
    yc = y - mean
    var = head_sum(yc * yc) * inv_n
    y = yc * lax.rsqrt(var + RWKV_GN_EPS) * lng_ref[...] + lnb_ref[...]
    y = y + head_sum(r * k * rk_ref[...]) * v
    o_ref[0] = y * g


def _rwkv_pallas(p_r, mu, w0, w_up, a0, a_up, g_up, k_k, k_a, r_k, ln_g, ln_b):
    B, S, C = p_r.shape
    L, W = RWKV_CHUNK, RWKV_WIDTH
    bf16 = jnp.bfloat16
    vecw = lambda t: t.reshape(1, W)
    ones_bd = jnp.kron(jnp.eye(RWKV_HEADS, dtype=bf16), jnp.ones((RWKV_HEAD, RWKV_HEAD), bf16))
    full = lambda a: pl.BlockSpec(a.shape, lambda b, c: (0,) * a.ndim)
    args = [mu.reshape(1, C), vecw(w0), w_up.astype(bf16), vecw(a0), a_up.astype(bf16),
            g_up.astype(bf16), vecw(k_k), vecw(k_a), vecw(r_k), vecw(ln_g), vecw(ln_b), ones_bd]
    return pl.pallas_call(
        _rwkv_kernel,
        grid=(B, S // L),
        in_specs=[pl.BlockSpec((1, L, C), lambda b, c: (b, c, 0)),
                  pl.BlockSpec((1, 8, C), lambda b, c: (b, jnp.maximum(c * (L // 8) - 1, 0), 0))]
                 + [full(a) for a in args],
        out_specs=pl.BlockSpec((1, L, W), lambda b, c: (b, c, 0)),
        out_shape=jax.ShapeDtypeStruct((B, S, W), jnp.float32),
        scratch_shapes=[pltpu.VMEM((RWKV_PAIRS, LANES, LANES), jnp.float32)],
        compiler_params=pltpu.CompilerParams(dimension_semantics=("arbitrary", "arbitrary")),
        name="rwkv7",
    )(p_r, p_r, *args)


MLSTM_CONV = 4
MLSTM_GATE_LANES = LANES


def _mlstm_kernel(p_ref, prev_ref, cw_ref, cb_ref, igb_ref, fgb_ref, hng_ref, hnb_ref,
                  o_ref, c_ref, n_ref, m_ref):
    f32, bf16 = jnp.float32, jnp.bfloat16
    L, W, H, D = MLSTM_CHUNK, MLSTM_WIDTH, MLSTM_HEADS, MLSTM_HEAD
    c_idx = pl.program_id(1)

    @pl.when(c_idx == 0)
    def _():
        c_ref[...] = jnp.zeros_like(c_ref)
        n_ref[...] = jnp.zeros_like(n_ref)
        m_ref[...] = jnp.zeros_like(m_ref)

    p = p_ref[0]
    z = p[:, 0:2 * W]
    prev = jnp.where(c_idx == 0, 0.0, prev_ref[0, :, 0:2 * W])
    ext = jnp.concatenate([prev, z], axis=0)
    conv = cb_ref[...] + cw_ref[MLSTM_CONV - 1:MLSTM_CONV, :] * z
    for j in range(MLSTM_CONV - 1):
        d = MLSTM_CONV - 1 - j
        conv = conv + cw_ref[j:j + 1, :] * ext[8 - d:8 - d + L, :]
    qk = conv * jax.nn.sigmoid(conv)
    q_all, k_all = qk[:, 0:W] * (D ** -0.5), qk[:, W:2 * W]
    v_all, o_all = p[:, 2 * W:3 * W], p[:, 3 * W:4 * W]
    gates = p[:, 4 * W:]
    igl = gates + igb_ref[...]
    lfl = -_softplus(-(gates + fgb_ref[...]))
    tril = (lax.broadcasted_iota(jnp.int32, (L, L), 1)
            <= lax.broadcasted_iota(jnp.int32, (L, L), 0))
    bcum = _mm_exact_lhs(tril.astype(bf16), lfl, _NN)
    e_all = igl - pltpu.roll(bcum, LANES - H, axis=1)
    lane = lax.broadcasted_iota(jnp.int32, (L, LANES), 1)
    outs = []
    for h in range(H):
        hs = slice(h * D, (h + 1) * D)
        q, k, v = q_all[:, hs], k_all[:, hs], v_all[:, hs]
        b_col = bcum[:, H + h:H + h + 1]
        ig_col = igl[:, h:h + 1]
        onehot = (lane == h).astype(bf16)
        e_row = _mm_exact_lhs(onehot, e_all, _NT)
        m_prev = m_ref[h, 0:1, 0:1]
        dmat = jnp.where(tril, b_col + e_row, -jnp.inf)
        inter = b_col + m_prev
        m_t = jnp.maximum(inter, jnp.max(dmat, axis=-1, keepdims=True))
        weights = jnp.exp(dmat - m_t)
        sc = lax.dot_general(q.astype(bf16), k.astype(bf16), _NT, preferred_element_type=f32) * weights
        carry_in = jnp.exp(inter - m_t)
        c_prev, n_prev = c_ref[h], n_ref[h, 0:1, :]
        num = (jnp.dot(sc.astype(bf16), v.astype(bf16), preferred_element_type=f32)
               + carry_in * jnp.dot(q.astype(bf16), c_prev.astype(bf16), preferred_element_type=f32))
        den = (jnp.sum(sc, axis=-1, keepdims=True)
               + carry_in * jnp.sum(q * n_prev, axis=-1, keepdims=True))
        hval = num / jnp.maximum(jnp.abs(den), jnp.exp(-m_t))
        b_last = b_col[L - 1:L, :]
        gs = b_last - b_col + ig_col
        m_new = jnp.maximum(b_last + m_prev, jnp.max(gs, axis=0, keepdims=True))
        ws = jnp.exp(gs - m_new)
        keep = jnp.exp(b_last + m_prev - m_new)
        wk = ws * k
        c_ref[h] = keep * c_prev + lax.dot_general(wk.astype(bf16), v.astype(bf16), _TN,
                                                   preferred_element_type=f32)
        n_ref[h] = jnp.broadcast_to(keep * n_prev + jnp.sum(wk, axis=0, keepdims=True), (8, D))
        m_ref[h] = jnp.broadcast_to(m_new, (8, LANES))
        mu = jnp.mean(hval, axis=-1, keepdims=True)
        hc = hval - mu
        var = jnp.mean(hc * hc, axis=-1, keepdims=True)
        hn = hc * lax.rsqrt(var + LN_EPS) * hng_ref[:, hs] + hnb_ref[:, hs]
        outs.append(jax.nn.sigmoid(o_all[:, hs]) * hn)
    o_ref[0] = jnp.concatenate(outs, axis=1)


def _mlstm_pallas(p_m, conv_w, conv_b, ig_b, fg_b, hn_g, hn_b):
    B, S, C = p_m.shape
    L, W, H, D = MLSTM_CHUNK, MLSTM_WIDTH, MLSTM_HEADS, MLSTM_HEAD
    pad = lambda t, off: jnp.zeros((1, LANES), jnp.float32).at[0, off:off + H].set(t)
    args = [conv_w, conv_b.reshape(1, 2 * W), pad(ig_b, 0), pad(fg_b, H),
            hn_g.reshape(1, W), hn_b.reshape(1, W)]
    full = lambda a: pl.BlockSpec(a.shape, lambda b, c: (0,) * a.ndim)
    return pl.pallas_call(
        _mlstm_kernel,
        grid=(B, S // L),
        in_specs=[pl.BlockSpec((1, L, C), lambda b, c: (b, c, 0)),
                  pl.BlockSpec((1, 8, C), lambda b, c: (b, jnp.maximum(c * (L // 8) - 1, 0), 0))]
                 + [full(a) for a in args],
        out_specs=pl.BlockSpec((1, L, W), lambda b, c: (b, c, 0)),
        out_shape=jax.ShapeDtypeStruct((B, S, W), jnp.float32),
        scratch_shapes=[pltpu.VMEM((H, D, D), jnp.float32),
                        pltpu.VMEM((H, 8, D), jnp.float32),
                        pltpu.VMEM((H, 8, LANES), jnp.float32)],
        compiler_params=pltpu.CompilerParams(dimension_semantics=("arbitrary", "arbitrary")),
        name="mlstm",
    )(p_m, p_m, *args)


FOX_TILE = 256
FOX_KEY_GROUP = 4
FOX_HEADS_PER_STEP = 8
FOX_AUG = LANES
FOX_GATE_LANES = LANES


def _fox_prep_kernel(p_ref, qg_ref, kg_ref, fb_ref, ones_ref, qa_ref, ka_ref, vt_ref, carry_ref):
    f32, bf16 = jnp.float32, jnp.bfloat16
    T, W, H, D = FOX_TILE, FOX_WIDTH, FOX_HEADS, FOX_HEAD
    i = pl.program_id(1)

    @pl.when(i == 0)
    def _():
        carry_ref[...] = jnp.zeros_like(carry_ref)

    p = p_ref[0]
    ones_bd = ones_ref[...]
    ms = lambda t: _mm_exact_rhs(t * t, ones_bd, _NN) * (1.0 / D)
    q = p[:, 0:W]
    q = q * lax.rsqrt(ms(q) + 1e-6) * qg_ref[...]
    k = p[:, W:2 * W]
    k = k * lax.rsqrt(ms(k) + 1e-6) * kg_ref[...] * (D ** -0.5)
    vt = p[:, 2 * W:3 * W].T.astype(bf16)
    logf = -_softplus(-(p[:, 4 * W:] + fb_ref[...]))
    tril = (lax.broadcasted_iota(jnp.int32, (T, T), 1)
            <= lax.broadcasted_iota(jnp.int32, (T, T), 0)).astype(bf16)
    c = _mm_exact_lhs(tril, logf, _NN) + carry_ref[0:1, :]
    carry_ref[...] = jnp.broadcast_to(c[T - 1:T, :], carry_ref.shape)
    c1 = c.astype(bf16).astype(f32)
    c2 = (c - c1).astype(bf16).astype(f32)
    c3 = (c - c1 - c2).astype(bf16).astype(f32)
    lane = lax.broadcasted_iota(jnp.int32, (T, D), 1)
    for h in range(H):
        hs = slice(h * D, (h + 1) * D)
        c1h, c2h, c3h = c1[:, h:h + 1], c2[:, h:h + 1], c3[:, h:h + 1]
        pieces = jnp.where(lane % 3 == 0, c1h, jnp.where(lane % 3 == 1, c2h, c3h))
        q_aug = jnp.where(lane < 3, 1.0, jnp.where(lane < 6, pieces, 0.0))
        k_aug = jnp.where(lane < 3, -pieces, jnp.where(lane < 6, 1.0, 0.0))
        qa_ref[0, h] = jnp.concatenate([q[:, hs], q_aug], axis=1).astype(bf16)
        ka_ref[0, h] = jnp.concatenate([k[:, hs], k_aug], axis=1).astype(bf16)
        vt_ref[0, h, 0] = vt[hs, :]


def _fox_prep(p_f, qn_g, kn_g, f_b):
    B, S, C = p_f.shape
    T, W, H, D = FOX_TILE, FOX_WIDTH, FOX_HEADS, FOX_HEAD
    bf16 = jnp.bfloat16
    ones_bd = jnp.kron(jnp.eye(H, dtype=bf16), jnp.ones((D, D), bf16))
    fb = jnp.zeros((1, FOX_GATE_LANES), jnp.float32).at[0, :H].set(f_b)
    args = [qn_g.reshape(1, W), kn_g.reshape(1, W), fb, ones_bd]
    full = lambda a: pl.BlockSpec(a.shape, lambda b, i: (0,) * a.ndim)
    return pl.pallas_call(
        _fox_prep_kernel,
        grid=(B, S // T),
        in_specs=[pl.BlockSpec((1, T, C), lambda b, i: (b, i, 0))] + [full(a) for a in args],
        out_specs=[pl.BlockSpec((1, H, T, FOX_AUG), lambda b, i: (b, 0, i, 0)),
                   pl.BlockSpec((1, H, T, FOX_AUG), lambda b, i: (b, 0, i, 0)),
                   pl.BlockSpec((1, H, 1, D, T), lambda b, i: (b, 0, i, 0, 0))],
        out_shape=[jax.ShapeDtypeStruct((B, H, S, FOX_AUG), bf16),
                   jax.ShapeDtypeStruct((B, H, S, FOX_AUG), bf16),
                   jax.ShapeDtypeStruct((B, H, S // T, D, T), bf16)],
        scratch_shapes=[pltpu.VMEM((8, FOX_GATE_LANES), jnp.float32)],
        compiler_params=pltpu.CompilerParams(dimension_semantics=("arbitrary", "arbitrary"),
                                             vmem_limit_bytes=VMEM_LIMIT_BYTES),
        name="fox_prep",
    )(p_f, *args)


def _fox_attn_kernel(qa_ref, ka_ref, vt_ref, og_ref, o_ref, m_ref, l_ref, acc_ref):
    f32, bf16 = jnp.float32, jnp.bfloat16
    T, D = FOX_TILE, FOX_HEAD
    i = pl.program_id(2)
    heads = range(FOX_HEADS_PER_STEP)
    n_blocks = ka_ref.shape[2] // T
    GK = FOX_KEY_GROUP
    rel = (lax.broadcasted_iota(jnp.int32, (GK * T, T), 0)
           - lax.broadcasted_iota(jnp.int32, (GK * T, T), 1))
    qas = [qa_ref[0, j] for j in heads]

    def key_group(g, first):
        visible = rel <= (i - g * GK) * T

        def scores(j):
            ka = ka_ref[0, j, g * GK * T:(g + 1) * GK * T, :]
            return lax.dot_general(ka, qas[j], _NT, preferred_element_type=f32)

        nxt = scores(0)
        for j in heads:
            s = nxt
            if j + 1 < len(heads):
                nxt = scores(j + 1)
            s = jnp.where(visible, s, -jnp.inf)
            m_g = jnp.max(s, axis=0, keepdims=True)
            m_new = m_g if first else jnp.maximum(m_ref[j], m_g)
            pr = jnp.exp(s - m_new)
            l_g = jnp.sum(pr, axis=0, keepdims=True)
            vt = jnp.concatenate([vt_ref[0, j, g * GK + b] for b in range(GK)], axis=1)
            pv = jnp.dot(vt, pr.astype(bf16), preferred_element_type=f32)
            if first:
                l_ref[j], acc_ref[j] = l_g, pv
            else:
                alpha = jnp.exp(m_ref[j] - m_new)
                l_ref[j] = l_ref[j] * alpha + l_g
                acc_ref[j] = acc_ref[j] * alpha + pv
            m_ref[j] = m_new

    key_group(0, True)
    for g in range(1, n_blocks // GK):
        pl.when(i >= g * GK)(functools.partial(key_group, g, False))
    outs = [(acc_ref[j] / l_ref[j]).T for j in heads]
    o_ref[0] = jax.nn.sigmoid(og_ref[0]) * jnp.concatenate(outs, axis=1)


def _fox_attn(qa, ka, vt, p_f):
    B, H, S, A = qa.shape
    T, D = FOX_TILE, FOX_HEAD
    hp = FOX_HEADS_PER_STEP
    width = hp * D
    og_block0 = 3 * FOX_WIDTH // width
    return pl.pallas_call(
        _fox_attn_kernel,
        grid=(B, H // hp, S // T),
        in_specs=[pl.BlockSpec((1, hp, T, A), lambda b, g, i: (b, g, i, 0)),
                  pl.BlockSpec((1, hp, S, A), lambda b, g, i: (b, g, 0, 0)),
                  pl.BlockSpec((1, hp, S // T, D, T), lambda b, g, i: (b, g, 0, 0, 0)),
                  pl.BlockSpec((1, T, width), lambda b, g, i: (b, i, og_block0 + g))],
        out_specs=pl.BlockSpec((1, T, width), lambda b, g, i: (b, i, g)),
        out_shape=jax.ShapeDtypeStruct((B, S, FOX_WIDTH), jnp.float32),
        scratch_shapes=[pltpu.VMEM((hp, 1, T), jnp.float32), pltpu.VMEM((hp, 1, T), jnp.float32),
                        pltpu.VMEM((hp, D, T), jnp.float32)],
        name="fox_attn",
    )(qa, ka, vt, p_f)


def kernel(x, l0_w_in, l0_rwkv_mu, l0_rwkv_w0, l0_rwkv_w_up, l0_rwkv_a0, l0_rwkv_a_up,
           l0_rwkv_g_up, l0_rwkv_k_k, l0_rwkv_k_a, l0_rwkv_r_k, l0_rwkv_ln_g, l0_rwkv_ln_b,
           l0_mlstm_conv_w, l0_mlstm_conv_b, l0_mlstm_ig_b, l0_mlstm_fg_b,
           l0_mlstm_hn_g, l0_mlstm_hn_b, l0_w_out, l0_ln1_g, l0_ln1_b,
           l0_peer_wq, l0_peer_k1, l0_peer_k2, l0_peer_u, l0_peer_v, l0_ln2_g, l0_ln2_b,
           l1_w_in, l1_fox_qn_g, l1_fox_kn_g, l1_fox_f_b, l1_w_out, l1_ln1_g, l1_ln1_b,
           l1_peer_wq, l1_peer_k1, l1_peer_k2, l1_peer_u, l1_peer_v, l1_ln2_g, l1_ln2_b):
    B, S, D = x.shape
    n = B * S
    bf16 = jnp.bfloat16
    x2 = x.reshape(n, D)

    def pad_lanes(w):
        return jnp.pad(w, ((0, 0), (0, LANES - w.shape[1])))

    m_main = 4 * MLSTM_WIDTH
    w0 = jnp.concatenate([l0_w_in[:, :RWKV_COLS + m_main],
                          pad_lanes(l0_w_in[:, RWKV_COLS + m_main:])], axis=1).astype(bf16)
    p_r, p_m = _proj(x2, w0, (RWKV_COLS, m_main + LANES))
    y_a = _rwkv_pallas(p_r.reshape(B, S, -1), l0_rwkv_mu, l0_rwkv_w0, l0_rwkv_w_up, l0_rwkv_a0,
                       l0_rwkv_a_up, l0_rwkv_g_up, l0_rwkv_k_k, l0_rwkv_k_a, l0_rwkv_r_k,
                       l0_rwkv_ln_g, l0_rwkv_ln_b)
    y_b = _mlstm_pallas(p_m.reshape(B, S, -1), l0_mlstm_conv_w, l0_mlstm_conv_b, l0_mlstm_ig_b,
                        l0_mlstm_fg_b, l0_mlstm_hn_g, l0_mlstm_hn_b)
    x2 = _out_proj_ln([y_a.reshape(n, -1), y_b.reshape(n, -1)], l0_w_out.astype(bf16), x2,
                      l0_ln1_g, l0_ln1_b)
    y = _peer_pallas(x2, l0_peer_wq, l0_peer_k1, l0_peer_k2, l0_peer_u, l0_peer_v)
    x2 = _resid_ln(x2, y, l0_ln2_g, l0_ln2_b)

    f_main = 4 * FOX_WIDTH
    w1 = jnp.concatenate([l1_w_in[:, :f_main], pad_lanes(l1_w_in[:, f_main:])], axis=1).astype(bf16)
    (p_f,) = _proj(x2, w1, (f_main + LANES,))
    p_f = p_f.reshape(B, S, -1)
    qa, ka, vt = _fox_prep(p_f, l1_fox_qn_g, l1_fox_kn_g, l1_fox_f_b)
    o = _fox_attn(qa, ka, vt, p_f)
    x2 = _out_proj_ln([o.reshape(n, -1)], l1_w_out.astype(bf16), x2, l1_ln1_g, l1_ln1_b)
    y = _peer_pallas(x2, l1_peer_wq, l1_peer_k1, l1_peer_k2, l1_peer_u, l1_peer_v)
    x2 = _resid_ln(x2, y, l1_ln2_g, l1_ln2_b)
    return x2.reshape(B, S, D)
```

```python
import functools

import jax
import jax.numpy as jnp
from jax import lax
from jax.experimental import pallas as pl
from jax.experimental.pallas import tpu as pltpu

D_MODEL = 1024
DEPTH = 2
DN_ALPHA = (2.0 * DEPTH) ** 0.25
LN_EPS = 1e-5

RWKV_WIDTH = D_MODEL // 2
RWKV_HEAD = 64
RWKV_HEADS = RWKV_WIDTH // RWKV_HEAD
RWKV_W_RANK = 64
RWKV_A_RANK = 64
RWKV_G_RANK = 128
RWKV_GN_EPS = 1e-5 * RWKV_HEAD
RWKV_COLS = 3 * RWKV_WIDTH + RWKV_W_RANK + RWKV_A_RANK + RWKV_G_RANK

MLSTM_WIDTH = D_MODEL // 2
MLSTM_HEAD = 128
MLSTM_HEADS = MLSTM_WIDTH // MLSTM_HEAD
MLSTM_CHUNK = 64

FOX_HEAD = 64
FOX_HEADS = D_MODEL // FOX_HEAD
FOX_WIDTH = FOX_HEADS * FOX_HEAD
FOX_QBLOCK = 128

PEER_HEADS = 8
PEER_NKEYS = 128
PEER_TOPK = 16
PEER_DKEY = 256
PEER_DHALF = PEER_DKEY // 2
PEER_TOKBLOCK = 128


def _split_cols(p, sizes):
    out, start = [], 0
    for s in sizes:
        out.append(p[..., start:start + s])
        start += s
    return out


def _resid_ln_kernel(x_ref, y_ref, g_ref, b_ref, o_ref):
    z = DN_ALPHA * x_ref[...] + y_ref[...]
    mu = jnp.mean(z, axis=-1, keepdims=True)
    zc = z - mu
    var = jnp.mean(zc * zc, axis=-1, keepdims=True)
    o_ref[...] = zc * lax.rsqrt(var + LN_EPS) * g_ref[...] + b_ref[...]


def _resid_ln(x2, y2, g, b, tm=512):
    n, d = x2.shape
    row = pl.BlockSpec((tm, d), lambda i: (i, 0))
    vec = pl.BlockSpec((1, d), lambda i: (0, 0))
    return pl.pallas_call(
        _resid_ln_kernel,
        grid=(n // tm,),
        in_specs=[row, row, vec, vec],
        out_specs=row,
        out_shape=jax.ShapeDtypeStruct((n, d), jnp.float32),
        name="resid_ln",
    )(x2, y2, g.reshape(1, d), b.reshape(1, d))


VMEM_LIMIT_BYTES = 56 * 1024 * 1024


def _proj_kernel(x_ref, w_ref, *o_refs):
    xb = x_ref[...].astype(jnp.bfloat16)
    start = 0
    for o_ref in o_refs:
        width = o_ref.shape[1]
        o_ref[...] = jnp.dot(xb, w_ref[:, start:start + width], preferred_element_type=jnp.float32)
        start += width


def _proj(x2, w_bf16, widths, tm=256):
    n, d = x2.shape
    assert sum(widths) == w_bf16.shape[1]
    return pl.pallas_call(
        _proj_kernel,
        grid=(n // tm,),
        in_specs=[pl.BlockSpec((tm, d), lambda i: (i, 0)),
                  pl.BlockSpec(w_bf16.shape, lambda i: (0, 0))],
        out_specs=[pl.BlockSpec((tm, w), lambda i: (i, 0)) for w in widths],
        out_shape=[jax.ShapeDtypeStruct((n, w), jnp.float32) for w in widths],
        compiler_params=pltpu.CompilerParams(vmem_limit_bytes=VMEM_LIMIT_BYTES),
        name="in_proj",
    )(x2, w_bf16)


def _out_proj_ln_kernel(*refs, n_parts):
    y_refs, (w_ref, x_ref, g_ref, b_ref, o_ref) = refs[:n_parts], refs[n_parts:]
    acc, start = None, 0
    for y_ref in y_refs:
        width = y_ref.shape[1]
        part = jnp.dot(y_ref[...].astype(jnp.bfloat16), w_ref[start:start + width, :],
                       preferred_element_type=jnp.float32)
        acc = part if acc is None else acc + part
        start += width
    z = DN_ALPHA * x_ref[...] + acc
    mu = jnp.mean(z, axis=-1, keepdims=True)
    zc = z - mu
    var = jnp.mean(zc * zc, axis=-1, keepdims=True)
    o_ref[...] = zc * lax.rsqrt(var + LN_EPS) * g_ref[...] + b_ref[...]


def _out_proj_ln(ys, w_bf16, x2, g, b, tm=256):
    n, d = x2.shape
    row = lambda width: pl.BlockSpec((tm, width), lambda i: (i, 0))
    vec = pl.BlockSpec((1, d), lambda i: (0, 0))
    return pl.pallas_call(
        functools.partial(_out_proj_ln_kernel, n_parts=len(ys)),
        grid=(n // tm,),
        in_specs=[row(y.shape[1]) for y in ys]
                 + [pl.BlockSpec(w_bf16.shape, lambda i: (0, 0)), row(d), vec, vec],
        out_specs=row(d),
        out_shape=jax.ShapeDtypeStruct((n, d), jnp.float32),
        name="out_proj_ln",
    )(*ys, w_bf16, x2, g.reshape(1, d), b.reshape(1, d))


def _head_norm(y, g, b, eps):
    mu = jnp.mean(y, -1, keepdims=True)
    var = jnp.mean(jnp.square(y - mu), -1, keepdims=True)
    return (y - mu) * lax.rsqrt(var + eps) * g + b


def _rms_norm(y, g):
    return y * lax.rsqrt(jnp.mean(y * y, -1, keepdims=True) + 1e-6) * g


def _token_shift(z):
    return jnp.pad(z, ((0, 0), (1, 0), (0, 0)))[:, :-1]


def _causal_conv(z, w, b):
    c = z.shape[-1]
    out = lax.conv_general_dilated(z, w[:, None, :], window_strides=(1,),
                                   padding=((w.shape[0] - 1, 0),),
                                   dimension_numbers=('NWC', 'WIO', 'NWC'),
                                   feature_group_count=c)
    return out + b


def _rwkv7_mix(p, mu, w0, w_up, a0, a_up, g_up, k_k, k_a, r_k, ln_g, ln_b):
    B, S, _ = p.shape
    H, N = RWKV_HEADS, RWKV_HEAD
    p = p + (_token_shift(p) - p) * mu
    r, k, v, xw, xa, xg = _split_cols(p, (RWKV_WIDTH, RWKV_WIDTH, RWKV_WIDTH,
                                          RWKV_W_RANK, RWKV_A_RANK, RWKV_G_RANK))
    log_w = -jnp.exp(-jax.nn.softplus(-(w0 + jnp.tanh(xw) @ w_up)) - 0.5)
    a = jax.nn.sigmoid(a0 + xa @ a_up)
    g = jax.nn.sigmoid(xg) @ g_up
    heads = lambda t: t.reshape(B, S, H, N)
    kk = heads(k * k_k)
    kk = kk / jnp.maximum(jnp.sqrt(jnp.sum(kk * kk, -1, keepdims=True)), 1e-12)
    k = k * (1.0 + (a - 1.0) * k_a)
    r_h, k_h, v_h, a_h = heads(r), heads(k), heads(v), heads(a)
    w_h = jnp.exp(heads(log_w))

    def step(state, inp):
        r_t, w_t, k_t, v_t, kk_t, a_t = inp
        s_kk = jnp.einsum('bhvk,bhk->bhv', state, kk_t)
        state = (state * w_t[:, :, None, :]
                 - s_kk[..., None] * (kk_t * a_t)[:, :, None, :]
                 + v_t[..., None] * k_t[:, :, None, :])
        return state, jnp.einsum('bhvk,bhk->bhv', state, r_t)

    seq_first = lambda t: jnp.moveaxis(t, 1, 0)
    state0 = jnp.zeros((B, H, N, N), jnp.float32)
    _, y = lax.scan(step, state0, (seq_first(r_h), seq_first(w_h), seq_first(k_h),
                                   seq_first(v_h), seq_first(kk), seq_first(a_h)))
    y = jnp.moveaxis(y, 0, 1)
    y = _head_norm(y, ln_g, ln_b, RWKV_GN_EPS)
    y = y + jnp.sum(r_h * k_h * r_k, -1, keepdims=True) * v_h
    return y.reshape(B, S, RWKV_WIDTH) * g


def _mlstm_chunkwise(q, k, v, ig, lf):
    B, S, H, D = q.shape
    L = MLSTM_CHUNK
    NC = S // L

    def to_chunks(t):
        t = t.reshape((B, NC, L, H) + t.shape[3:])
        return jnp.moveaxis(t, (1, 3), (0, 2))

    causal = jnp.tril(jnp.ones((L, L), dtype=bool))

    def body(carry, inp):
        C, n, m = carry
        qc, kc, vc, igc, lfc = inp
        b = jnp.cumsum(lfc, axis=-1)
        dmat = jnp.where(causal, b[..., :, None] - b[..., None, :] + igc[..., None, :], -jnp.inf)
        inter = b + m[..., None]
        m_t = jnp.maximum(inter, jnp.max(dmat, -1))
        weights = jnp.exp(dmat - m_t[..., None])
        sc = jnp.einsum('bhtd,bhsd->bhts', qc, kc) * weights
        carry_in = jnp.exp(inter - m_t)
        num = (jnp.einsum('bhts,bhsd->bhtd', sc, vc)
               + carry_in[..., None] * jnp.einsum('bhtk,bhkv->bhtv', qc, C))
        den = jnp.sum(sc, -1) + carry_in * jnp.einsum('bhtk,bhk->bht', qc, n)
        h = num / jnp.maximum(jnp.abs(den), jnp.exp(-m_t))[..., None]
        b_last = b[..., -1]
        gs = b_last[..., None] - b + igc
        m_new = jnp.maximum(b_last + m, jnp.max(gs, -1))
        ws = jnp.exp(gs - m_new[..., None])
        keep = jnp.exp(b_last + m - m_new)
        C = keep[..., None, None] * C + jnp.einsum('bhs,bhsk,bhsv->bhkv', ws, kc, vc)
        n = keep[..., None] * n + jnp.einsum('bhs,bhsk->bhk', ws, kc)
        return (C, n, m_new), h

    f32 = jnp.float32
    init = (jnp.zeros((B, H, D, D), f32), jnp.zeros((B, H, D), f32), jnp.zeros((B, H), f32))
    _, h = lax.scan(body, init, (to_chunks(q), to_chunks(k), to_chunks(v),
                                 to_chunks(ig), to_chunks(lf)))
    return jnp.moveaxis(h, (0, 2), (1, 3)).reshape(B, S, H, D)


def _mlstm_mix(p, conv_w, conv_b, ig_b, fg_b, hn_g, hn_b):
    B, S, _ = p.shape
    H, N = MLSTM_HEADS, MLSTM_HEAD
    q, k, v, o, ig, fg = _split_cols(p, (MLSTM_WIDTH, MLSTM_WIDTH, MLSTM_WIDTH, MLSTM_WIDTH, H, H))
    qk = jax.nn.silu(_causal_conv(jnp.concatenate([q, k], -1), conv_w, conv_b))
    q, k = qk[..., :MLSTM_WIDTH], qk[..., MLSTM_WIDTH:]
    heads = lambda t: t.reshape(B, S, H, N)
    q = heads(q) * (N ** -0.5)
    ig = ig + ig_b
    lf = jax.nn.log_sigmoid(fg + fg_b)
    h = _mlstm_chunkwise(q, heads(k), heads(v), ig, lf)
    h = _head_norm(h, hn_g, hn_b, LN_EPS).reshape(B, S, MLSTM_WIDTH)
    return jax.nn.sigmoid(o) * h


def _even_mixer(x, w_in, mu, w0, w_up, a0, a_up, g_up, k_k, k_a, r_k, rln_g, rln_b,
                conv_w, conv_b, ig_b, fg_b, hn_g, hn_b, w_out):
    p = x @ w_in
    y_a = _rwkv_pallas(p[..., :RWKV_COLS], mu, w0, w_up, a0, a_up, g_up, k_k, k_a, r_k, rln_g, rln_b)
    y_b = _mlstm_mix(p[..., RWKV_COLS:], conv_w, conv_b, ig_b, fg_b, hn_g, hn_b)
    return jnp.concatenate([y_a, y_b], -1) @ w_out


def _fox_attention(q, k, v, logf):
    B, S, H, D = q.shape
    NB = S // FOX_QBLOCK
    c = jnp.cumsum(logf, axis=1)
    c_key = jnp.transpose(c, (0, 2, 1))[:, :, None, :]
    q_blocks = jnp.moveaxis(q.reshape(B, NB, FOX_QBLOCK, H, D), 1, 0)
    c_blocks = jnp.moveaxis(c.reshape(B, NB, FOX_QBLOCK, H), 1, 0)
    key_pos = jnp.arange(S)
    scale = D ** -0.5

    def one_block(args):
        qb, cb, blk = args
        q_pos = blk * FOX_QBLOCK + jnp.arange(FOX_QBLOCK)
        logits = jnp.einsum('bqhd,bkhd->bhqk', qb, k) * scale
        logits = logits + jnp.transpose(cb, (0, 2, 1))[..., None] - c_key
        logits = jnp.where(key_pos[None, :] <= q_pos[:, None], logits, -jnp.inf)
        probs = jax.nn.softmax(logits, axis=-1)
        return jnp.einsum('bhqk,bkhd->bqhd', probs, v)

    out = lax.map(one_block, (q_blocks, c_blocks, jnp.arange(NB)))
    return jnp.moveaxis(out, 0, 1).reshape(B, S, H, D)


def _odd_mixer(x, w_in, qn_g, kn_g, f_b, w_out):
    B, S, _ = x.shape
    p = x @ w_in
    q, k, v, og, fl = _split_cols(p, (FOX_WIDTH, FOX_WIDTH, FOX_WIDTH, FOX_WIDTH, FOX_HEADS))
    heads = lambda t: t.reshape(B, S, FOX_HEADS, FOX_HEAD)
    q = _rms_norm(heads(q), qn_g)
    k = _rms_norm(heads(k), kn_g)
    logf = jax.nn.log_sigmoid(fl + f_b)
    o = _fox_attention(q, k, heads(v), logf).reshape(B, S, FOX_WIDTH)
    return (jax.nn.sigmoid(og) * o) @ w_out


def _peer_ffn(x, w_q, sub_k1, sub_k2, u_tab, v_tab):
    B, S, Dm = x.shape
    K = PEER_TOPK
    xt = x.reshape((B * S) // PEER_TOKBLOCK, PEER_TOKBLOCK, Dm)

    def block(xb):
        T = xb.shape[0]
        q = (xb @ w_q).reshape(T, PEER_HEADS, 2, PEER_DHALF)
        s1 = jnp.einsum('thd,hnd->thn', q[:, :, 0], sub_k1)
        s2 = jnp.einsum('thd,hnd->thn', q[:, :, 1], sub_k2)
        v1, i1 = lax.top_k(s1, K)
        v2, i2 = lax.top_k(s2, K)
        cand = (v1[..., :, None] + v2[..., None, :]).reshape(T, PEER_HEADS, K * K)
        sc, ci = lax.top_k(cand, K)
        e1 = jnp.take_along_axis(i1, ci // K, axis=-1)
        e2 = jnp.take_along_axis(i2, ci % K, axis=-1)
        eid = e1 * PEER_NKEYS + e2
        gate = jax.nn.softmax(sc, axis=-1)
        act = jax.nn.gelu(jnp.einsum('td,thkd->thk', xb, u_tab[eid]), approximate=False)
        return jnp.einsum('thk,thkd->td', gate * act, v_tab[eid])

    return lax.map(block, xt).reshape(B, S, Dm)


PEER_EXPERTS = PEER_NKEYS * PEER_NKEYS
PEER_TOK_TILE = 256
PEER_EXPERT_CHUNK = 1024
PEER_E1_GROUP = 4
PEER_E2_SUB = 32
PEER_A_PIECES = 4
PEER_C_PIECES = 1
LANES = 128
MXU_DIM = 256
INV_SQRT2 = 0.7071067811865476
_NT = (((1,), (1,)), ((), ()))


def _topk_desc(work, k):
    vals = []
    for _ in range(k):
        m = jnp.max(work, axis=0, keepdims=True)
        vals.append(m)
        work = jnp.where(work >= m, -jnp.inf, work)
    return vals


def _peer_select_kernel(x_ref, wqt_ref, k1_ref, k2_ref,
                        phi_ref, a1_ref, a2_ref, qt_ref):
    f32, bf16 = jnp.float32, jnp.bfloat16
    K = PEER_TOPK
    qt_ref[...] = lax.dot_general(wqt_ref[...], x_ref[...], _NT, preferred_element_type=f32)
    n_groups = x_ref.shape[0] // LANES

    def head(h, carry):
        base = pl.multiple_of(h * PEER_DKEY, PEER_DKEY)
        q1 = qt_ref[pl.ds(base, PEER_DHALF), :].astype(bf16)
        q2 = qt_ref[pl.ds(base + PEER_DHALF, PEER_DHALF), :].astype(bf16)
        s1 = jnp.dot(k1_ref[h], q1, preferred_element_type=f32)
        s2 = jnp.dot(k2_ref[h], q2, preferred_element_type=f32)
        for c in range(n_groups):
            sl = slice(c * LANES, (c + 1) * LANES)
            s1c, s2c = s1[:, sl], s2[:, sl]
            v1 = _topk_desc(s1c, K)
            v2 = _topk_desc(s2c, K)
            rows = [v1[a] + v2[b] for a in range(K) for b in range(K) if (a + 1) * (b + 1) <= K + 1]
            rows += [jnp.full_like(v1[0], -jnp.inf)] * (-len(rows) % 8)
            cand = jnp.concatenate(rows, axis=0)
            vc = _topk_desc(cand, K + 1)
            cmax = vc[0]
            z = jnp.sum(jnp.where(cand >= vc[K - 1], jnp.exp(cand - cmax), 0.0), axis=0, keepdims=True)
            cut = 0.5 * (vc[K - 1] + vc[K])
            s1m = jnp.where(s1c >= v1[K - 1], s1c, -jnp.inf)
            s2m = jnp.where(s2c >= v2[K - 1], s2c, -jnp.inf)
            inv_z = 1.0 / z
            phi_ref[h, :, sl] = jnp.exp(cut - s1m - v2[0]) * inv_z
            a1_ref[h, :, sl] = jnp.exp(s1m - v1[0])
            a2_ref[h, :, sl] = jnp.exp(s2m - v2[0]) * inv_z
        return carry

    lax.fori_loop(0, PEER_HEADS, head, 0)


def _peer_select(xb, wqt, k1, k2):
    n, d = xb.shape
    T = PEER_TOK_TILE
    H, NK = PEER_HEADS, PEER_NKEYS
    full = lambda shape: pl.BlockSpec(shape, lambda i: (0,) * len(shape))
    tok3 = lambda rows: pl.BlockSpec((H, rows, T), lambda i: (0, 0, i))
    f32 = jnp.float32
    return pl.pallas_call(
        _peer_select_kernel,
        grid=(n // T,),
        in_specs=[pl.BlockSpec((T, d), lambda i: (i, 0)), full(wqt.shape), full(k1.shape), full(k2.shape)],
        out_specs=[tok3(NK)] * 3,
        out_shape=[jax.ShapeDtypeStruct((H, NK, n), f32)] * 3,
        scratch_shapes=[pltpu.VMEM((H * PEER_DKEY, T), f32)],
        name="peer_select",
    )(xb, wqt, k1, k2)


def _peer_expert_kernel(xt_ref, u_ref, v_ref, phi_ref, a1_ref, a2_ref,
                        o_ref, acc_ref, hn_ref, hc_ref, wn_ref, wc_ref):
    f32, bf16 = jnp.float32, jnp.bfloat16
    s = pl.program_id(0)
    n_chunks = PEER_EXPERTS // PEER_EXPERT_CHUNK
    c_chunk = jnp.maximum(s - 2, 0) % n_chunks

    @pl.when(s == 0)
    def _():
        hn_ref[...] = jnp.zeros_like(hn_ref)
        wn_ref[...] = jnp.zeros_like(wn_ref)

    @pl.when(c_chunk == 0)
    def _():
        acc_ref[...] = jnp.zeros_like(acc_ref)

    T = xt_ref.shape[1]
    NK, G, SB = PEER_NKEYS, PEER_E1_GROUP, PEER_E2_SUB
    hc_ref[...] = hn_ref[...]
    wc_ref[...] = wn_ref[...]
    CE, d = PEER_EXPERT_CHUNK, v_ref.shape[1]
    PA, PC = PEER_A_PIECES, PEER_C_PIECES

    def stage_a(k):
        rows = CE // PA
        u_blk = pltpu.bitcast(u_ref[k * rows // 2:(k + 1) * rows // 2, :], bf16)
        hn_ref[k * rows:(k + 1) * rows, :] = jnp.dot(u_blk, xt_ref[...], preferred_element_type=f32)

    def stage_c(k):
        cols = slice(k * d // PC, (k + 1) * d // PC)
        acc_ref[:, cols] += lax.dot_general(wc_ref[...], v_ref[:, cols], _TN, preferred_element_type=f32)

    def stage_b(q, c, sb):
        ts = slice(c * LANES, (c + 1) * LANES)
        e2s = slice(sb * SB, (sb + 1) * SB)
        g = [None] * G
        for hd in range(PEER_HEADS):
            a2 = a2_ref[hd, e2s, ts]
            for e in range(G):
                r = q * G + e
                term = jnp.where(a2 > phi_ref[hd, r:r + 1, ts], a1_ref[hd, r:r + 1, ts] * a2, 0.0)
                g[e] = term if g[e] is None else g[e] + term
        for e in range(G):
            lo = (q * G + e) * NK + sb * SB
            hh = hc_ref[lo:lo + SB, ts]
            wn_ref[lo:lo + SB, ts] = (g[e] * (0.5 * hh * (1.0 + lax.erf(hh * INV_SQRT2)))).astype(bf16)

    b_blocks = [(q, c, sb) for q in range(CE // (G * NK)) for c in range(T // LANES)
                for sb in range(NK // SB)]
    sa, sc = len(b_blocks) // PA, len(b_blocks) // PC
    for t, blk in enumerate(b_blocks):
        stage_b(*blk)
        if t % sa == 0:
            stage_a(t // sa)
        if t % sc == sc // 2:
            stage_c(t // sc)

    @pl.when(c_chunk == n_chunks - 1)
    def _():
        o_ref[...] = acc_ref[...]


def _peer_experts(xtb, ub, vb, phi, a1, a2):
    d, n = xtb.shape
    T, CE = PEER_TOK_TILE, PEER_EXPERT_CHUNK
    H, NK = PEER_HEADS, PEER_NKEYS
    e1_per_chunk = CE // NK
    nc = PEER_EXPERTS // CE
    last = (n // T) * nc - 1
    at = lambda s, lag: jnp.clip(s - lag, 0, last)
    return pl.pallas_call(
        _peer_expert_kernel,
        grid=(last + 3,),
        in_specs=[
            pl.BlockSpec((d, T), lambda s: (0, at(s, 0) // nc)),
            pl.BlockSpec((CE // 2, d), lambda s: (at(s, 0) % nc, 0)),
            pl.BlockSpec((CE, d), lambda s: (at(s, 2) % nc, 0)),
            pl.BlockSpec((H, e1_per_chunk, T), lambda s: (0, at(s, 1) % nc, at(s, 1) // nc)),
            pl.BlockSpec((H, e1_per_chunk, T), lambda s: (0, at(s, 1) % nc, at(s, 1) // nc)),
            pl.BlockSpec((H, NK, T), lambda s: (0, 0, at(s, 1) // nc)),
        ],
        out_specs=pl.BlockSpec((T, d), lambda s: (at(s, 2) // nc, 0)),
        out_shape=jax.ShapeDtypeStruct((n, d), jnp.float32),
        scratch_shapes=[pltpu.VMEM((T, d), jnp.float32),
                        pltpu.VMEM((CE, T), jnp.float32), pltpu.VMEM((CE, T), jnp.float32),
                        pltpu.VMEM((CE, T), jnp.bfloat16), pltpu.VMEM((CE, T), jnp.bfloat16)],
        compiler_params=pltpu.CompilerParams(dimension_semantics=("arbitrary",)),
        name="peer_experts",
    )(xtb, ub, vb, phi, a1, a2)


def _pack_row_pairs(t_bf16):
    rows, cols = t_bf16.shape
    pairs = jnp.swapaxes(t_bf16.reshape(rows // 2, 2, cols), -1, -2)
    return lax.bitcast_convert_type(pairs, jnp.uint32)


def _peer_pallas(x2, w_q, sub_k1, sub_k2, u_tab, v_tab):
    bf16 = jnp.bfloat16
    xb = x2.astype(bf16)
    sel = _peer_select(xb, w_q.T.astype(bf16), sub_k1.astype(bf16), sub_k2.astype(bf16))
    return _peer_experts(xb.T, _pack_row_pairs(u_tab.astype(bf16)), v_tab.astype(bf16), *sel)


RWKV_CHUNK = 64
RWKV_PAIR = LANES // RWKV_HEAD
RWKV_PAIRS = RWKV_HEADS // RWKV_PAIR
_NN = (((1,), (0,)), ((), ()))
_TN = (((0,), (0,)), ((), ()))


def _split_bf16(a):
    hi = a.astype(jnp.bfloat16)
    lo = (a - hi.astype(jnp.float32)).astype(jnp.bfloat16)
    return hi, lo


def _mm2(a, b, dims):
    d = lambda u, v: lax.dot_general(u, v, dims, preferred_element_type=jnp.float32)
    (ah, al), (bh, bl) = _split_bf16(a), _split_bf16(b)
    return d(ah, bh) + d(ah, bl) + d(al, bh)


def _mm_exact_rhs(a, b_exact, dims, terms=2):
    d = lambda u: lax.dot_general(u, b_exact, dims, preferred_element_type=jnp.float32)
    out, rem = None, a
    for _ in range(terms):
        piece = rem.astype(jnp.bfloat16)
        rem = rem - piece.astype(jnp.float32)
        out = d(piece) if out is None else out + d(piece)
    return out


def _mm_exact_lhs(a_exact, b, dims, terms=3):
    d = lambda v: lax.dot_general(a_exact, v, dims, preferred_element_type=jnp.float32)
    out, rem = None, b
    for _ in range(terms):
        piece = rem.astype(jnp.bfloat16)
        rem = rem - piece.astype(jnp.float32)
        out = d(piece) if out is None else out + d(piece)
    return out


def _softplus(z):
    return jnp.maximum(z, 0.0) + jnp.log1p(jnp.exp(-jnp.abs(z)))


def _rwkv_kernel(p_ref, prev_ref, mu_ref, w0_ref, wup_ref, a0_ref, aup_ref, gup_ref,
                 kk_ref, ka_ref, rk_ref, lng_ref, lnb_ref, ones_ref,
                 o_ref, state_ref):
    f32, bf16 = jnp.float32, jnp.bfloat16
    L, W = RWKV_CHUNK, RWKV_WIDTH
    c_idx = pl.program_id(1)

    @pl.when(c_idx == 0)
    def _():
        state_ref[...] = jnp.zeros_like(state_ref)

    p = p_ref[0]
    row = lax.broadcasted_iota(jnp.int32, p.shape, 0)
    prev_row = jnp.where(c_idx == 0, 0.0, prev_ref[0, 7:8, :])
    shifted = jnp.where(row == 0, prev_row, pltpu.roll(p, 1, axis=0))
    p = p + (shifted - p) * mu_ref[...]
    r, k, v = p[:, 0:W], p[:, W:2 * W], p[:, 2 * W:3 * W]
    o = 3 * W
    xw = p[:, o:o + RWKV_W_RANK]
    xa = p[:, o + RWKV_W_RANK:o + RWKV_W_RANK + RWKV_A_RANK]
    xg = p[:, o + RWKV_W_RANK + RWKV_A_RANK:]
    dotd = lambda u, m: jnp.dot(u.astype(bf16), m, preferred_element_type=f32)
    lw = -jnp.exp(-_softplus(-(w0_ref[...] + dotd(jnp.tanh(xw), wup_ref[...]))) - 0.5)
    a = jax.nn.sigmoid(a0_ref[...] + dotd(xa, aup_ref[...]))
    g = dotd(jax.nn.sigmoid(xg), gup_ref[...])
    ones_bd = ones_ref[...]
    head_sum = lambda t: _mm_exact_rhs(t, ones_bd, _NN)
    kk = k * kk_ref[...]
    kk = kk / jnp.maximum(jnp.sqrt(head_sum(kk * kk)), 1e-12)
    k = k * (1.0 + (a - 1.0) * ka_ref[...])

    lane = lax.broadcasted_iota(jnp.int32, (L, LANES), 1)
    trow = lax.broadcasted_iota(jnp.int32, (L, LANES), 0)
    s_in = lane % RWKV_HEAD
    strict, incl = s_in < trow, s_in <= trow
    eye_pair = (s_in == trow).astype(f32)
    m0 = lane < RWKV_HEAD
    tril = (lax.broadcasted_iota(jnp.int32, (L, L), 1)
            <= lax.broadcasted_iota(jnp.int32, (L, L), 0)).astype(bf16)
    r2 = lax.broadcasted_iota(jnp.int32, (LANES, LANES), 0) // RWKV_HEAD
    c2 = lax.broadcasted_iota(jnp.int32, (LANES, LANES), 1) // RWKV_HEAD
    bd_mask = r2 == c2

    def bd(t):
        return jnp.concatenate([jnp.where(m0, t, 0.0), jnp.where(m0, 0.0, t)], axis=0)

    pairs = range(RWKV_PAIRS)
    cut = lambda t: [t[:, pr * LANES:(pr + 1) * LANES] for pr in pairs]
    rp, kp, vp, kkp, ap, lwp = cut(r), cut(k), cut(v), cut(kk), cut(a), cut(lw)
    c = [_mm_exact_lhs(tril, lwp[p], _NN) for p in pairs]
    at = [-kkp[p] * jnp.exp(c[p] - lwp[p]) for p in pairs]
    eni = [jnp.exp(-c[p]) for p in pairs]
    bt = [kkp[p] * ap[p] * eni[p] for p in pairs]
    kt = [kp[p] * eni[p] for p in pairs]
    rt = [rp[p] * jnp.exp(c[p]) for p in pairs]
    g_last = [jnp.exp(c[p][L - 1:L, :]) for p in pairs]
    gram = [_mm2(jnp.concatenate([at[p], rt[p]], axis=0),
                 jnp.concatenate([bd(bt[p]), bd(kt[p])], axis=0), _NT) for p in pairs]
    n_ab = [jnp.where(strict, gram[p][0:L, 0:LANES], 0.0) for p in pairs]
    a_ak = [jnp.where(strict, gram[p][0:L, LANES:], 0.0) for p in pairs]
    a_rb = [jnp.where(incl, gram[p][L:, 0:LANES], 0.0) for p in pairs]
    a_rk = [jnp.where(incl, gram[p][L:, LANES:], 0.0) for p in pairs]
    tinv, m = [eye_pair + n_ab[p] for p in pairs], n_ab
    for _ in range(5):
        m = [_mm2(m[p], bd(m[p]), _NN) for p in pairs]
        tinv = [tinv[p] + _mm2(m[p], bd(tinv[p]), _NN) for p in pairs]
    s0 = [state_ref[p] for p in pairs]
    pq = [_mm2(at[p], s0[p], _NT) + _mm2(a_ak[p], bd(vp[p]), _NN) for p in pairs]
    u = [_mm2(tinv[p], bd(pq[p]), _NN) for p in pairs]
    ys = [_mm2(rt[p], s0[p], _NT)
          + _mm2(jnp.concatenate([a_rb[p], a_rk[p]], axis=1),
                 jnp.concatenate([bd(u[p]), bd(vp[p])], axis=0), _NN) for p in pairs]
    for p in pairs:
        upd = _mm2(jnp.concatenate([u[p], vp[p]], axis=0),
                   jnp.concatenate([bt[p], kt[p]], axis=0), _TN)
        state_ref[p] = (s0[p] + jnp.where(bd_mask, upd, 0.0)) * g_last[p]
    y = jnp.concatenate(ys, axis=1)

    inv_n = 1.0 / RWKV_HEAD
    mean = head_sum(y) * inv_n
    yc = y - mean
    var = head_sum(yc * yc) * inv_n
    y = yc * lax.rsqrt(var + RWKV_GN_EPS) * lng_ref[...] + lnb_ref[...]
    y = y + head_sum(r * k * rk_ref[...]) * v
    o_ref[0] = y * g


def _rwkv_pallas(p_r, mu, w0, w_up, a0, a_up, g_up, k_k, k_a, r_k, ln_g, ln_b):
    B, S, C = p_r.shape
    L, W = RWKV_CHUNK, RWKV_WIDTH
    bf16 = jnp.bfloat16
    vecw = lambda t: t.reshape(1, W)
    ones_bd = jnp.kron(jnp.eye(RWKV_HEADS, dtype=bf16), jnp.ones((RWKV_HEAD, RWKV_HEAD), bf16))
    full = lambda a: pl.BlockSpec(a.shape, lambda b, c: (0,) * a.ndim)
    args = [mu.reshape(1, C), vecw(w0), w_up.astype(bf16), vecw(a0), a_up.astype(bf16),
            g_up.astype(bf16), vecw(k_k), vecw(k_a), vecw(r_k), vecw(ln_g), vecw(ln_b), ones_bd]
    return pl.pallas_call(
        _rwkv_kernel,
        grid=(B, S // L),
        in_specs=[pl.BlockSpec((1, L, C), lambda b, c: (b, c, 0)),
                  pl.BlockSpec((1, 8, C), lambda b, c: (b, jnp.maximum(c * (L // 8) - 1, 0), 0))]
                 + [full(a) for a in args],
        out_specs=pl.BlockSpec((1, L, W), lambda b, c: (b, c, 0)),
        out_shape=jax.ShapeDtypeStruct((B, S, W), jnp.float32),
        scratch_shapes=[pltpu.VMEM((RWKV_PAIRS, LANES, LANES), jnp.float32)],
        compiler_params=pltpu.CompilerParams(dimension_semantics=("arbitrary", "arbitrary")),
        name="rwkv7",
    )(p_r, p_r, *args)


MLSTM_CONV = 4
MLSTM_GATE_LANES = LANES


def _mlstm_kernel(p_ref, prev_ref, cw_ref, cb_ref, igb_ref, fgb_ref, hng_ref, hnb_ref,
                  o_ref, c_ref, n_ref, m_ref):
    f32, bf16 = jnp.float32, jnp.bfloat16
    L, W, H, D = MLSTM_CHUNK, MLSTM_WIDTH, MLSTM_HEADS, MLSTM_HEAD
    c_idx = pl.program_id(1)

    @pl.when(c_idx == 0)
    def _():
        c_ref[...] = jnp.zeros_like(c_ref)
        n_ref[...] = jnp.zeros_like(n_ref)
        m_ref[...] = jnp.zeros_like(m_ref)

    p = p_ref[0]
    z = p[:, 0:2 * W]
    prev = jnp.where(c_idx == 0, 0.0, prev_ref[0, :, 0:2 * W])
    ext = jnp.concatenate([prev, z], axis=0)
    conv = cb_ref[...] + cw_ref[MLSTM_CONV - 1:MLSTM_CONV, :] * z
    for j in range(MLSTM_CONV - 1):
        d = MLSTM_CONV - 1 - j
        conv = conv + cw_ref[j:j + 1, :] * ext[8 - d:8 - d + L, :]
    qk = conv * jax.nn.sigmoid(conv)
    q_all, k_all = qk[:, 0:W] * (D ** -0.5), qk[:, W:2 * W]
    v_all, o_all = p[:, 2 * W:3 * W], p[:, 3 * W:4 * W]
    gates = p[:, 4 * W:]
    igl = gates + igb_ref[...]
    lfl = -_softplus(-(gates + fgb_ref[...]))
    tril = (lax.broadcasted_iota(jnp.int32, (L, L), 1)
            <= lax.broadcasted_iota(jnp.int32, (L, L), 0))
    bcum = _mm_exact_lhs(tril.astype(bf16), lfl, _NN)
    e_all = igl - pltpu.roll(bcum, LANES - H, axis=1)
    lane = lax.broadcasted_iota(jnp.int32, (L, LANES), 1)
    outs = []
    for h in range(H):
        hs = slice(h * D, (h + 1) * D)
        q, k, v = q_all[:, hs], k_all[:, hs], v_all[:, hs]
        b_col = bcum[:, H + h:H + h + 1]
        ig_col = igl[:, h:h + 1]
        onehot = (lane == h).astype(bf16)
        e_row = _mm_exact_lhs(onehot, e_all, _NT)
        m_prev = m_ref[h, 0:1, 0:1]
        dmat = jnp.where(tril, b_col + e_row, -jnp.inf)
        inter = b_col + m_prev
        m_t = jnp.maximum(inter, jnp.max(dmat, axis=-1, keepdims=True))
        weights = jnp.exp(dmat - m_t)
        sc = lax.dot_general(q.astype(bf16), k.astype(bf16), _NT, preferred_element_type=f32) * weights
        carry_in = jnp.exp(inter - m_t)
        c_prev, n_prev = c_ref[h], n_ref[h, 0:1, :]
        num = (jnp.dot(sc.astype(bf16), v.astype(bf16), preferred_element_type=f32)
               + carry_in * jnp.dot(q.astype(bf16), c_prev.astype(bf16), preferred_element_type=f32))
        den = (jnp.sum(sc, axis=-1, keepdims=True)
               + carry_in * jnp.sum(q * n_prev, axis=-1, keepdims=True))
        hval = num / jnp.maximum(jnp.abs(den), jnp.exp(-m_t))
        b_last = b_col[L - 1:L, :]
        gs = b_last - b_col + ig_col
        m_new = jnp.maximum(b_last + m_prev, jnp.max(gs, axis=0, keepdims=True))
        ws = jnp.exp(gs - m_new)
        keep = jnp.exp(b_last + m_prev - m_new)
        wk = ws * k
        c_ref[h] = keep * c_prev + lax.dot_general(wk.astype(bf16), v.astype(bf16), _TN,
                                                   preferred_element_type=f32)
        n_ref[h] = jnp.broadcast_to(keep * n_prev + jnp.sum(wk, axis=0, keepdims=True), (8, D))
        m_ref[h] = jnp.broadcast_to(m_new, (8, LANES))
        mu = jnp.mean(hval, axis=-1, keepdims=True)
        hc = hval - mu
        var = jnp.mean(hc * hc, axis=-1, keepdims=True)
        hn = hc * lax.rsqrt(var + LN_EPS) * hng_ref[:, hs] + hnb_ref[:, hs]
        outs.append(jax.nn.sigmoid(o_all[:, hs]) * hn)
    o_ref[0] = jnp.concatenate(outs, axis=1)


def _mlstm_pallas(p_m, conv_w, conv_b, ig_b, fg_b, hn_g, hn_b):
    B, S, C = p_m.shape
    L, W, H, D = MLSTM_CHUNK, MLSTM_WIDTH, MLSTM_HEADS, MLSTM_HEAD
    pad = lambda t, off: jnp.zeros((1, LANES), jnp.float32).at[0, off:off + H].set(t)
    args = [conv_w, conv_b.reshape(1, 2 * W), pad(ig_b, 0), pad(fg_b, H),
            hn_g.reshape(1, W), hn_b.reshape(1, W)]
    full = lambda a: pl.BlockSpec(a.shape, lambda b, c: (0,) * a.ndim)
    return pl.pallas_call(
        _mlstm_kernel,
        grid=(B, S // L),
        in_specs=[pl.BlockSpec((1, L, C), lambda b, c: (b, c, 0)),
                  pl.BlockSpec((1, 8, C), lambda b, c: (b, jnp.maximum(c * (L // 8) - 1, 0), 0))]
                 + [full(a) for a in args],
        out_specs=pl.BlockSpec((1, L, W), lambda b, c: (b, c, 0)),
        out_shape=jax.ShapeDtypeStruct((B, S, W), jnp.float32),
        scratch_shapes=[pltpu.VMEM((H, D, D), jnp.float32),
                        pltpu.VMEM((H, 8, D), jnp.float32),
                        pltpu.VMEM((H, 8, LANES), jnp.float32)],
        compiler_params=pltpu.CompilerParams(dimension_semantics=("arbitrary", "arbitrary")),
        name="mlstm",
    )(p_m, p_m, *args)


FOX_TILE = 256
FOX_KEY_GROUP = 4
FOX_HEADS_PER_STEP = 8
FOX_AUG = LANES
FOX_GATE_LANES = LANES


def _fox_prep_kernel(p_ref, qg_ref, kg_ref, fb_ref, ones_ref, qa_ref, ka_ref, vt_ref, carry_ref):
    f32, bf16 = jnp.float32, jnp.bfloat16
    T, W, H, D = FOX_TILE, FOX_WIDTH, FOX_HEADS, FOX_HEAD
    i = pl.program_id(1)

    @pl.when(i == 0)
    def _():
        carry_ref[...] = jnp.zeros_like(carry_ref)

    p = p_ref[0]
    ones_bd = ones_ref[...]
    ms = lambda t: _mm_exact_rhs(t * t, ones_bd, _NN) * (1.0 / D)
    q = p[:, 0:W]
    q = q * lax.rsqrt(ms(q) + 1e-6) * qg_ref[...]
    k = p[:, W:2 * W]
    k = k * lax.rsqrt(ms(k) + 1e-6) * kg_ref[...] * (D ** -0.5)
    vt = p[:, 2 * W:3 * W].T.astype(bf16)
    logf = -_softplus(-(p[:, 4 * W:] + fb_ref[...]))
    tril = (lax.broadcasted_iota(jnp.int32, (T, T), 1)
            <= lax.broadcasted_iota(jnp.int32, (T, T), 0)).astype(bf16)
    c = _mm_exact_lhs(tril, logf, _NN) + carry_ref[0:1, :]
    carry_ref[...] = jnp.broadcast_to(c[T - 1:T, :], carry_ref.shape)
    c1 = c.astype(bf16).astype(f32)
    c2 = (c - c1).astype(bf16).astype(f32)
    c3 = (c - c1 - c2).astype(bf16).astype(f32)
    lane = lax.broadcasted_iota(jnp.int32, (T, D), 1)
    for h in range(H):
        hs = slice(h * D, (h + 1) * D)
        c1h, c2h, c3h = c1[:, h:h + 1], c2[:, h:h + 1], c3[:, h:h + 1]
        pieces = jnp.where(lane % 3 == 0, c1h, jnp.where(lane % 3 == 1, c2h, c3h))
        q_aug = jnp.where(lane < 3, 1.0, jnp.where(lane < 6, pieces, 0.0))
        k_aug = jnp.where(lane < 3, -pieces, jnp.where(lane < 6, 1.0, 0.0))
        qa_ref[0, h] = jnp.concatenate([q[:, hs], q_aug], axis=1).astype(bf16)
        ka_ref[0, h] = jnp.concatenate([k[:, hs], k_aug], axis=1).astype(bf16)
        vt_ref[0, h, 0] = vt[hs, :]


def _fox_prep(p_f, qn_g, kn_g, f_b):
    B, S, C = p_f.shape
    T, W, H, D = FOX_TILE, FOX_WIDTH, FOX_HEADS, FOX_HEAD
    bf16 = jnp.bfloat16
    ones_bd = jnp.kron(jnp.eye(H, dtype=bf16), jnp.ones((D, D), bf16))
    fb = jnp.zeros((1, FOX_GATE_LANES), jnp.float32).at[0, :H].set(f_b)
    args = [qn_g.reshape(1, W), kn_g.reshape(1, W), fb, ones_bd]
    full = lambda a: pl.BlockSpec(a.shape, lambda b, i: (0,) * a.ndim)
    return pl.pallas_call(
        _fox_prep_kernel,
        grid=(B, S // T),
        in_specs=[pl.BlockSpec((1, T, C), lambda b, i: (b, i, 0))] + [full(a) for a in args],
        out_specs=[pl.BlockSpec((1, H, T, FOX_AUG), lambda b, i: (b, 0, i, 0)),
                   pl.BlockSpec((1, H, T, FOX_AUG), lambda b, i: (b, 0, i, 0)),
                   pl.BlockSpec((1, H, 1, D, T), lambda b, i: (b, 0, i, 0, 0))],
        out_shape=[jax.ShapeDtypeStruct((B, H, S, FOX_AUG), bf16),
                   jax.ShapeDtypeStruct((B, H, S, FOX_AUG), bf16),
                   jax.ShapeDtypeStruct((B, H, S // T, D, T), bf16)],
        scratch_shapes=[pltpu.VMEM((8, FOX_GATE_LANES), jnp.float32)],
        compiler_params=pltpu.CompilerParams(dimension_semantics=("arbitrary", "arbitrary"),
                                             vmem_limit_bytes=VMEM_LIMIT_BYTES),
        name="fox_prep",
    )(p_f, *args)


def _fox_attn_kernel(qa_ref, ka_ref, vt_ref, og_ref, o_ref, m_ref, l_ref, acc_ref):
    f32, bf16 = jnp.float32, jnp.bfloat16
    T, D = FOX_TILE, FOX_HEAD
    i = pl.program_id(2)
    heads = range(FOX_HEADS_PER_STEP)
    n_blocks = ka_ref.shape[2] // T
    GK = FOX_KEY_GROUP
    rel = (lax.broadcasted_iota(jnp.int32, (GK * T, T), 0)
           - lax.broadcasted_iota(jnp.int32, (GK * T, T), 1))
    qas = [qa_ref[0, j] for j in heads]

    def key_group(g, first):
        visible = rel <= (i - g * GK) * T

        def scores(j):
            ka = ka_ref[0, j, g * GK * T:(g + 1) * GK * T, :]
            return lax.dot_general(ka, qas[j], _NT, preferred_element_type=f32)

        nxt = scores(0)
        for j in heads:
            s = nxt
            if j + 1 < len(heads):
                nxt = scores(j + 1)
            s = jnp.where(visible, s, -jnp.inf)
            m_g = jnp.max(s, axis=0, keepdims=True)
            m_new = m_g if first else jnp.maximum(m_ref[j], m_g)
            pr = jnp.exp(s - m_new)
            l_g = jnp.sum(pr, axis=0, keepdims=True)
            vt = jnp.concatenate([vt_ref[0, j, g * GK + b] for b in range(GK)], axis=1)
            pv = jnp.dot(vt, pr.astype(bf16), preferred_element_type=f32)
            if first:
                l_ref[j], acc_ref[j] = l_g, pv
            else:
                alpha = jnp.exp(m_ref[j] - m_new)
                l_ref[j] = l_ref[j] * alpha + l_g
                acc_ref[j] = acc_ref[j] * alpha + pv
            m_ref[j] = m_new

    key_group(0, True)
    for g in range(1, n_blocks // GK):
        pl.when(i >= g * GK)(functools.partial(key_group, g, False))
    outs = [(acc_ref[j] / l_ref[j]).T for j in heads]
    o_ref[0] = jax.nn.sigmoid(og_ref[0]) * jnp.concatenate(outs, axis=1)


def _fox_attn(qa, ka, vt, p_f):
    B, H, S, A = qa.shape
    T, D = FOX_TILE, FOX_HEAD
    hp = FOX_HEADS_PER_STEP
    width = hp * D
    og_block0 = 3 * FOX_WIDTH // width
    return pl.pallas_call(
        _fox_attn_kernel,
        grid=(B, H // hp, S // T),
        in_specs=[pl.BlockSpec((1, hp, T, A), lambda b, g, i: (b, g, i, 0)),
                  pl.BlockSpec((1, hp, S, A), lambda b, g, i: (b, g, 0, 0)),
                  pl.BlockSpec((1, hp, S // T, D, T), lambda b, g, i: (b, g, 0, 0, 0)),
                  pl.BlockSpec((1, T, width), lambda b, g, i: (b, i, og_block0 + g))],
        out_specs=pl.BlockSpec((1, T, width), lambda b, g, i: (b, i, g)),
        out_shape=jax.ShapeDtypeStruct((B, S, FOX_WIDTH), jnp.float32),
        scratch_shapes=[pltpu.VMEM((hp, 1, T), jnp.float32), pltpu.VMEM((hp, 1, T), jnp.float32),
                        pltpu.VMEM((hp, D, T), jnp.float32)],
        name="fox_attn",
    )(qa, ka, vt, p_f)


def kernel(x, l0_w_in, l0_rwkv_mu, l0_rwkv_w0, l0_rwkv_w_up, l0_rwkv_a0, l0_rwkv_a_up,
           l0_rwkv_g_up, l0_rwkv_k_k, l0_rwkv_k_a, l0_rwkv_r_k, l0_rwkv_ln_g, l0_rwkv_ln_b,
           l0_mlstm_conv_w, l0_mlstm_conv_b, l0_mlstm_ig_b, l0_mlstm_fg_b,
           l0_mlstm_hn_g, l0_mlstm_hn_b, l0_w_out, l0_ln1_g, l0_ln1_b,
           l0_peer_wq, l0_peer_k1, l0_peer_k2, l0_peer_u, l0_peer_v, l0_ln2_g, l0_ln2_b,
           l1_w_in, l1_fox_qn_g, l1_fox_kn_g, l1_fox_f_b, l1_w_out, l1_ln1_g, l1_ln1_b,
           l1_peer_wq, l1_peer_k1, l1_peer_k2, l1_peer_u, l1_peer_v, l1_ln2_g, l1_ln2_b):
    B, S, D = x.shape
    n = B * S
    bf16 = jnp.bfloat16
    x2 = x.reshape(n, D)

    def pad_lanes(w):
        return jnp.pad(w, ((0, 0), (0, LANES - w.shape[1])))

    m_main = 4 * MLSTM_WIDTH
    w0 = jnp.concatenate([l0_w_in[:, :RWKV_COLS + m_main],
                          pad_lanes(l0_w_in[:, RWKV_COLS + m_main:])], axis=1).astype(bf16)
    p_r, p_m = _proj(x2, w0, (RWKV_COLS, m_main + LANES))
    y_a = _rwkv_pallas(p_r.reshape(B, S, -1), l0_rwkv_mu, l0_rwkv_w0, l0_rwkv_w_up, l0_rwkv_a0,
                       l0_rwkv_a_up, l0_rwkv_g_up, l0_rwkv_k_k, l0_rwkv_k_a, l0_rwkv_r_k,
                       l0_rwkv_ln_g, l0_rwkv_ln_b)
    y_b = _mlstm_pallas(p_m.reshape(B, S, -1), l0_mlstm_conv_w, l0_mlstm_conv_b, l0_mlstm_ig_b,
                        l0_mlstm_fg_b, l0_mlstm_hn_g, l0_mlstm_hn_b)
    x2 = _out_proj_ln([y_a.reshape(n, -1), y_b.reshape(n, -1)], l0_w_out.astype(bf16), x2,
                      l0_ln1_g, l0_ln1_b)
    y = _peer_pallas(x2, l0_peer_wq, l0_peer_k1, l0_peer_k2, l0_peer_u, l0_peer_v)
    x2 = _resid_ln(x2, y, l0_ln2_g, l0_ln2_b)

    f_main = 4 * FOX_WIDTH
    w1 = jnp.concatenate([l1_w_in[:, :f_main], pad_lanes(l1_w_in[:, f_main:])], axis=1).astype(bf16)
    (p_f,) = _proj(x2, w1, (f_main + LANES,))
    p_f = p_f.reshape(B, S, -1)
    qa, ka, vt = _fox_prep(p_f, l1_fox_qn_g, l1_fox_kn_g, l1_fox_f_b)
    o = _fox_attn(qa, ka, vt, p_f)
    x2 = _out_proj_ln([o.reshape(n, -1)], l1_w_out.astype(bf16), x2, l1_ln1_g, l1_ln1_b)
    y = _peer_pallas(x2, l1_peer_wq, l1_peer_k1, l1_peer_k2, l1_peer_u, l1_peer_v)
    x2 = _resid_ln(x2, y, l1_ln2_g, l1_ln2_b)
    return x2.reshape(B, S, D)
```

```python
import functools

import jax
import jax.numpy as jnp
from jax import lax
from jax.experimental import pallas as pl
from jax.experimental.pallas import tpu as pltpu

D_MODEL = 1024
DEPTH = 2
DN_ALPHA = (2.0 * DEPTH) ** 0.25
LN_EPS = 1e-5

RWKV_WIDTH = D_MODEL // 2
RWKV_HEAD = 64
RWKV_HEADS = RWKV_WIDTH // RWKV_HEAD
RWKV_W_RANK = 64
RWKV_A_RANK = 64
RWKV_G_RANK = 128
RWKV_GN_EPS = 1e-5 * RWKV_HEAD
RWKV_COLS = 3 * RWKV_WIDTH + RWKV_W_RANK + RWKV_A_RANK + RWKV_G_RANK

MLSTM_WIDTH = D_MODEL // 2
MLSTM_HEAD = 128
MLSTM_HEADS = MLSTM_WIDTH // MLSTM_HEAD
MLSTM_CHUNK = 64

FOX_HEAD = 64
FOX_HEADS = D_MODEL // FOX_HEAD
FOX_WIDTH = FOX_HEADS * FOX_HEAD
FOX_QBLOCK = 128

PEER_HEADS = 8
PEER_NKEYS = 128
PEER_TOPK = 16
PEER_DKEY = 256
PEER_DHALF = PEER_DKEY // 2
PEER_TOKBLOCK = 128


def _split_cols(p, sizes):
    out, start = [], 0
    for s in sizes:
        out.append(p[..., start:start + s])
        start += s
    return out


def _resid_ln_kernel(x_ref, y_ref, g_ref, b_ref, o_ref):
    z = DN_ALPHA * x_ref[...] + y_ref[...]
    mu = jnp.mean(z, axis=-1, keepdims=True)
    zc = z - mu
    var = jnp.mean(zc * zc, axis=-1, keepdims=True)
    o_ref[...] = zc * lax.rsqrt(var + LN_EPS) * g_ref[...] + b_ref[...]


def _resid_ln(x2, y2, g, b, tm=512):
    n, d = x2.shape
    row = pl.BlockSpec((tm, d), lambda i: (i, 0))
    vec = pl.BlockSpec((1, d), lambda i: (0, 0))
    return pl.pallas_call(
        _resid_ln_kernel,
        grid=(n // tm,),
        in_specs=[row, row, vec, vec],
        out_specs=row,
        out_shape=jax.ShapeDtypeStruct((n, d), jnp.float32),
        name="resid_ln",
    )(x2, y2, g.reshape(1, d), b.reshape(1, d))


VMEM_LIMIT_BYTES = 56 * 1024 * 1024


def _proj_kernel(x_ref, w_ref, *o_refs):
    xb = x_ref[...].astype(jnp.bfloat16)
    start = 0
    for o_ref in o_refs:
        width = o_ref.shape[1]
        o_ref[...] = jnp.dot(xb, w_ref[:, start:start + width], preferred_element_type=jnp.float32)
        start += width


def _proj(x2, w_bf16, widths, tm=256):
    n, d = x2.shape
    assert sum(widths) == w_bf16.shape[1]
    return pl.pallas_call(
        _proj_kernel,
        grid=(n // tm,),
        in_specs=[pl.BlockSpec((tm, d), lambda i: (i, 0)),
                  pl.BlockSpec(w_bf16.shape, lambda i: (0, 0))],
        out_specs=[pl.BlockSpec((tm, w), lambda i: (i, 0)) for w in widths],
        out_shape=[jax.ShapeDtypeStruct((n, w), jnp.float32) for w in widths],
        compiler_params=pltpu.CompilerParams(vmem_limit_bytes=VMEM_LIMIT_BYTES),
        name="in_proj",
    )(x2, w_bf16)


def _out_proj_ln_kernel(*refs, n_parts):
    y_refs, (w_ref, x_ref, g_ref, b_ref, o_ref) = refs[:n_parts], refs[n_parts:]
    acc, start = None, 0
    for y_ref in y_refs:
        width = y_ref.shape[1]
        part = jnp.dot(y_ref[...].astype(jnp.bfloat16), w_ref[start:start + width, :],
                       preferred_element_type=jnp.float32)
        acc = part if acc is None else acc + part
        start += width
    z = DN_ALPHA * x_ref[...] + acc
    mu = jnp.mean(z, axis=-1, keepdims=True)
    zc = z - mu
    var = jnp.mean(zc * zc, axis=-1, keepdims=True)
    o_ref[...] = zc * lax.rsqrt(var + LN_EPS) * g_ref[...] + b_ref[...]


def _out_proj_ln(ys, w_bf16, x2, g, b, tm=256):
    n, d = x2.shape
    row = lambda width: pl.BlockSpec((tm, width), lambda i: (i, 0))
    vec = pl.BlockSpec((1, d), lambda i: (0, 0))
    return pl.pallas_call(
        functools.partial(_out_proj_ln_kernel, n_parts=len(ys)),
        grid=(n // tm,),
        in_specs=[row(y.shape[1]) for y in ys]
                 + [pl.BlockSpec(w_bf16.shape, lambda i: (0, 0)), row(d), vec, vec],
        out_specs=row(d),
        out_shape=jax.ShapeDtypeStruct((n, d), jnp.float32),
        name="out_proj_ln",
    )(*ys, w_bf16, x2, g.reshape(1, d), b.reshape(1, d))


def _head_norm(y, g, b, eps):
    mu = jnp.mean(y, -1, keepdims=True)
    var = jnp.mean(jnp.square(y - mu), -1, keepdims=True)
    return (y - mu) * lax.rsqrt(var + eps) * g + b


def _rms_norm(y, g):
    return y * lax.rsqrt(jnp.mean(y * y, -1, keepdims=True) + 1e-6) * g


def _token_shift(z):
    return jnp.pad(z, ((0, 0), (1, 0), (0, 0)))[:, :-1]


def _causal_conv(z, w, b):
    c = z.shape[-1]
    out = lax.conv_general_dilated(z, w[:, None, :], window_strides=(1,),
                                   padding=((w.shape[0] - 1, 0),),
                                   dimension_numbers=('NWC', 'WIO', 'NWC'),
                                   feature_group_count=c)
    return out + b


def _rwkv7_mix(p, mu, w0, w_up, a0, a_up, g_up, k_k, k_a, r_k, ln_g, ln_b):
    B, S, _ = p.shape
    H, N = RWKV_HEADS, RWKV_HEAD
    p = p + (_token_shift(p) - p) * mu
    r, k, v, xw, xa, xg = _split_cols(p, (RWKV_WIDTH, RWKV_WIDTH, RWKV_WIDTH,
                                          RWKV_W_RANK, RWKV_A_RANK, RWKV_G_RANK))
    log_w = -jnp.exp(-jax.nn.softplus(-(w0 + jnp.tanh(xw) @ w_up)) - 0.5)
    a = jax.nn.sigmoid(a0 + xa @ a_up)
    g = jax.nn.sigmoid(xg) @ g_up
    heads = lambda t: t.reshape(B, S, H, N)
    kk = heads(k * k_k)
    kk = kk / jnp.maximum(jnp.sqrt(jnp.sum(kk * kk, -1, keepdims=True)), 1e-12)
    k = k * (1.0 + (a - 1.0) * k_a)
    r_h, k_h, v_h, a_h = heads(r), heads(k), heads(v), heads(a)
    w_h = jnp.exp(heads(log_w))

    def step(state, inp):
        r_t, w_t, k_t, v_t, kk_t, a_t = inp
        s_kk = jnp.einsum('bhvk,bhk->bhv', state, kk_t)
        state = (state * w_t[:, :, None, :]
                 - s_kk[..., None] * (kk_t * a_t)[:, :, None, :]
                 + v_t[..., None] * k_t[:, :, None, :])
        return state, jnp.einsum('bhvk,bhk->bhv', state, r_t)

    seq_first = lambda t: jnp.moveaxis(t, 1, 0)
    state0 = jnp.zeros((B, H, N, N), jnp.float32)
    _, y = lax.scan(step, state0, (seq_first(r_h), seq_first(w_h), seq_first(k_h),
                                   seq_first(v_h), seq_first(kk), seq_first(a_h)))
    y = jnp.moveaxis(y, 0, 1)
    y = _head_norm(y, ln_g, ln_b, RWKV_GN_EPS)
    y = y + jnp.sum(r_h * k_h * r_k, -1, keepdims=True) * v_h
    return y.reshape(B, S, RWKV_WIDTH) * g


def _mlstm_chunkwise(q, k, v, ig, lf):
    B, S, H, D = q.shape
    L = MLSTM_CHUNK
    NC = S // L

    def to_chunks(t):
        t = t.reshape((B, NC, L, H) + t.shape[3:])
        return jnp.moveaxis(t, (1, 3), (0, 2))

    causal = jnp.tril(jnp.ones((L, L), dtype=bool))

    def body(carry, inp):
        C, n, m = carry
        qc, kc, vc, igc, lfc = inp
        b = jnp.cumsum(lfc, axis=-1)
        dmat = jnp.where(causal, b[..., :, None] - b[..., None, :] + igc[..., None, :], -jnp.inf)
        inter = b + m[..., None]
        m_t = jnp.maximum(inter, jnp.max(dmat, -1))
        weights = jnp.exp(dmat - m_t[..., None])
        sc = jnp.einsum('bhtd,bhsd->bhts', qc, kc) * weights
        carry_in = jnp.exp(inter - m_t)
        num = (jnp.einsum('bhts,bhsd->bhtd', sc, vc)
               + carry_in[..., None] * jnp.einsum('bhtk,bhkv->bhtv', qc, C))
        den = jnp.sum(sc, -1) + carry_in * jnp.einsum('bhtk,bhk->bht', qc, n)
        h = num / jnp.maximum(jnp.abs(den), jnp.exp(-m_t))[..., None]
        b_last = b[..., -1]
        gs = b_last[..., None] - b + igc
        m_new = jnp.maximum(b_last + m, jnp.max(gs, -1))
        ws = jnp.exp(gs - m_new[..., None])
        keep = jnp.exp(b_last + m - m_new)
        C = keep[..., None, None] * C + jnp.einsum('bhs,bhsk,bhsv->bhkv', ws, kc, vc)
        n = keep[..., None] * n + jnp.einsum('bhs,bhsk->bhk', ws, kc)
        return (C, n, m_new), h

    f32 = jnp.float32
    init = (jnp.zeros((B, H, D, D), f32), jnp.zeros((B, H, D), f32), jnp.zeros((B, H), f32))
    _, h = lax.scan(body, init, (to_chunks(q), to_chunks(k), to_chunks(v),
                                 to_chunks(ig), to_chunks(lf)))
    return jnp.moveaxis(h, (0, 2), (1, 3)).reshape(B, S, H, D)


def _mlstm_mix(p, conv_w, conv_b, ig_b, fg_b, hn_g, hn_b):
    B, S, _ = p.shape
    H, N = MLSTM_HEADS, MLSTM_HEAD
    q, k, v, o, ig, fg = _split_cols(p, (MLSTM_WIDTH, MLSTM_WIDTH, MLSTM_WIDTH, MLSTM_WIDTH, H, H))
    qk = jax.nn.silu(_causal_conv(jnp.concatenate([q, k], -1), conv_w, conv_b))
    q, k = qk[..., :MLSTM_WIDTH], qk[..., MLSTM_WIDTH:]
    heads = lambda t: t.reshape(B, S, H, N)
    q = heads(q) * (N ** -0.5)
    ig = ig + ig_b
    lf = jax.nn.log_sigmoid(fg + fg_b)
    h = _mlstm_chunkwise(q, heads(k), heads(v), ig, lf)
    h = _head_norm(h, hn_g, hn_b, LN_EPS).reshape(B, S, MLSTM_WIDTH)
    return jax.nn.sigmoid(o) * h


def _even_mixer(x, w_in, mu, w0, w_up, a0, a_up, g_up, k_k, k_a, r_k, rln_g, rln_b,
                conv_w, conv_b, ig_b, fg_b, hn_g, hn_b, w_out):
    p = x @ w_in
    y_a = _rwkv_pallas(p[..., :RWKV_COLS], mu, w0, w_up, a0, a_up, g_up, k_k, k_a, r_k, rln_g, rln_b)
    y_b = _mlstm_mix(p[..., RWKV_COLS:], conv_w, conv_b, ig_b, fg_b, hn_g, hn_b)
    return jnp.concatenate([y_a, y_b], -1) @ w_out


def _fox_attention(q, k, v, logf):
    B, S, H, D = q.shape
    NB = S // FOX_QBLOCK
    c = jnp.cumsum(logf, axis=1)
    c_key = jnp.transpose(c, (0, 2, 1))[:, :, None, :]
    q_blocks = jnp.moveaxis(q.reshape(B, NB, FOX_QBLOCK, H, D), 1, 0)
    c_blocks = jnp.moveaxis(c.reshape(B, NB, FOX_QBLOCK, H), 1, 0)
    key_pos = jnp.arange(S)
    scale = D ** -0.5

    def one_block(args):
        qb, cb, blk = args
        q_pos = blk * FOX_QBLOCK + jnp.arange(FOX_QBLOCK)
        logits = jnp.einsum('bqhd,bkhd->bhqk', qb, k) * scale
        logits = logits + jnp.transpose(cb, (0, 2, 1))[..., None] - c_key
        logits = jnp.where(key_pos[None, :] <= q_pos[:, None], logits, -jnp.inf)
        probs = jax.nn.softmax(logits, axis=-1)
        return jnp.einsum('bhqk,bkhd->bqhd', probs, v)

    out = lax.map(one_block, (q_blocks, c_blocks, jnp.arange(NB)))
    return jnp.moveaxis(out, 0, 1).reshape(B, S, H, D)


def _odd_mixer(x, w_in, qn_g, kn_g, f_b, w_out):
    B, S, _ = x.shape
    p = x @ w_in
    q, k, v, og, fl = _split_cols(p, (FOX_WIDTH, FOX_WIDTH, FOX_WIDTH, FOX_WIDTH, FOX_HEADS))
    heads = lambda t: t.reshape(B, S, FOX_HEADS, FOX_HEAD)
    q = _rms_norm(heads(q), qn_g)
    k = _rms_norm(heads(k), kn_g)
    logf = jax.nn.log_sigmoid(fl + f_b)
    o = _fox_attention(q, k, heads(v), logf).reshape(B, S, FOX_WIDTH)
    return (jax.nn.sigmoid(og) * o) @ w_out


def _peer_ffn(x, w_q, sub_k1, sub_k2, u_tab, v_tab):
    B, S, Dm = x.shape
    K = PEER_TOPK
    xt = x.reshape((B * S) // PEER_TOKBLOCK, PEER_TOKBLOCK, Dm)

    def block(xb):
        T = xb.shape[0]
        q = (xb @ w_q).reshape(T, PEER_HEADS, 2, PEER_DHALF)
        s1 = jnp.einsum('thd,hnd->thn', q[:, :, 0], sub_k1)
        s2 = jnp.einsum('thd,hnd->thn', q[:, :, 1], sub_k2)
        v1, i1 = lax.top_k(s1, K)
        v2, i2 = lax.top_k(s2, K)
        cand = (v1[..., :, None] + v2[..., None, :]).reshape(T, PEER_HEADS, K * K)
        sc, ci = lax.top_k(cand, K)
        e1 = jnp.take_along_axis(i1, ci // K, axis=-1)
        e2 = jnp.take_along_axis(i2, ci % K, axis=-1)
        eid = e1 * PEER_NKEYS + e2
        gate = jax.nn.softmax(sc, axis=-1)
        act = jax.nn.gelu(jnp.einsum('td,thkd->thk', xb, u_tab[eid]), approximate=False)
        return jnp.einsum('thk,thkd->td', gate * act, v_tab[eid])

    return lax.map(block, xt).reshape(B, S, Dm)


PEER_EXPERTS = PEER_NKEYS * PEER_NKEYS
PEER_TOK_TILE = 256
PEER_EXPERT_CHUNK = 1024
PEER_E1_GROUP = 4
PEER_E2_SUB = 32
PEER_A_PIECES = 4
PEER_C_PIECES = 1
LANES = 128
MXU_DIM = 256
INV_SQRT2 = 0.7071067811865476
_NT = (((1,), (1,)), ((), ()))


def _topk_desc(work, k):
    vals = []
    for _ in range(k):
        m = jnp.max(work, axis=0, keepdims=True)
        vals.append(m)
        work = jnp.where(work >= m, -jnp.inf, work)
    return vals


def _peer_select_kernel(x_ref, wqt_ref, k1_ref, k2_ref,
                        phi_ref, a1_ref, a2_ref, qt_ref):
    f32, bf16 = jnp.float32, jnp.bfloat16
    K = PEER_TOPK
    qt_ref[...] = lax.dot_general(wqt_ref[...], x_ref[...], _NT, preferred_element_type=f32)
    n_groups = x_ref.shape[0] // LANES

    def head(h, carry):
        base = pl.multiple_of(h * PEER_DKEY, PEER_DKEY)
        q1 = qt_ref[pl.ds(base, PEER_DHALF), :].astype(bf16)
        q2 = qt_ref[pl.ds(base + PEER_DHALF, PEER_DHALF), :].astype(bf16)
        s1 = jnp.dot(k1_ref[h], q1, preferred_element_type=f32)
        s2 = jnp.dot(k2_ref[h], q2, preferred_element_type=f32)
        for c in range(n_groups):
            sl = slice(c * LANES, (c + 1) * LANES)
            s1c, s2c = s1[:, sl], s2[:, sl]
            v1 = _topk_desc(s1c, K)
            v2 = _topk_desc(s2c, K)
            rows = [v1[a] + v2[b] for a in range(K) for b in range(K) if (a + 1) * (b + 1) <= K + 1]
            rows += [jnp.full_like(v1[0], -jnp.inf)] * (-len(rows) % 8)
            cand = jnp.concatenate(rows, axis=0)
            vc = _topk_desc(cand, K + 1)
            cmax = vc[0]
            z = jnp.sum(jnp.where(cand >= vc[K - 1], jnp.exp(cand - cmax), 0.0), axis=0, keepdims=True)
            cut = 0.5 * (vc[K - 1] + vc[K])
            s1m = jnp.where(s1c >= v1[K - 1], s1c, -jnp.inf)
            s2m = jnp.where(s2c >= v2[K - 1], s2c, -jnp.inf)
            inv_z = 1.0 / z
            phi_ref[h, :, sl] = jnp.exp(cut - s1m - v2[0]) * inv_z
            a1_ref[h, :, sl] = jnp.exp(s1m - v1[0])
            a2_ref[h, :, sl] = jnp.exp(s2m - v2[0]) * inv_z
        return carry

    lax.fori_loop(0, PEER_HEADS, head, 0)


def _peer_select(xb, wqt, k1, k2):
    n, d = xb.shape
    T = PEER_TOK_TILE
    H, NK = PEER_HEADS, PEER_NKEYS
    full = lambda shape: pl.BlockSpec(shape, lambda i: (0,) * len(shape))
    tok3 = lambda rows: pl.BlockSpec((H, rows, T), lambda i: (0, 0, i))
    f32 = jnp.float32
    return pl.pallas_call(
        _peer_select_kernel,
        grid=(n // T,),
        in_specs=[pl.BlockSpec((T, d), lambda i: (i, 0)), full(wqt.shape), full(k1.shape), full(k2.shape)],
        out_specs=[tok3(NK)] * 3,
        out_shape=[jax.ShapeDtypeStruct((H, NK, n), f32)] * 3,
        scratch_shapes=[pltpu.VMEM((H * PEER_DKEY, T), f32)],
        name="peer_select",
    )(xb, wqt, k1, k2)


def _peer_expert_kernel(xt_ref, u_ref, v_ref, phi_ref, a1_ref, a2_ref,
                        o_ref, acc_ref, h0_ref, h1_ref, w0_ref, w1_ref):
    f32, bf16 = jnp.float32, jnp.bfloat16
    s = pl.program_id(0)
    n_chunks = PEER_EXPERTS // PEER_EXPERT_CHUNK
    c_chunk = jnp.maximum(s - 2, 0) % n_chunks

    @pl.when(s == 0)
    def _():
        for ref in (h0_ref, h1_ref, w0_ref, w1_ref):
            ref[...] = jnp.zeros_like(ref)

    @pl.when(c_chunk == 0)
    def _():
        acc_ref[...] = jnp.zeros_like(acc_ref)

    T = xt_ref.shape[1]
    NK, G, SB = PEER_NKEYS, PEER_E1_GROUP, PEER_E2_SUB
    CE = PEER_EXPERT_CHUNK
    PA = PEER_A_PIECES

    def step(ha_ref, hb_ref, wb_ref, wc_ref):
        def stage_a(k):
            rows = CE // PA
            u_blk = pltpu.bitcast(u_ref[k * rows // 2:(k + 1) * rows // 2, :], bf16)
            ha_ref[k * rows:(k + 1) * rows, :] = jnp.dot(u_blk, xt_ref[...], preferred_element_type=f32)

        def stage_b(q, c, sb):
            ts = slice(c * LANES, (c + 1) * LANES)
            e2s = slice(sb * SB, (sb + 1) * SB)
            g = [None] * G
            for hd in range(PEER_HEADS):
                a2 = a2_ref[hd, e2s, ts]
                for e in range(G):
                    r = q * G + e
                    term = jnp.where(a2 > phi_ref[hd, r:r + 1, ts], a1_ref[hd, r:r + 1, ts] * a2, 0.0)
                    g[e] = term if g[e] is None else g[e] + term
            for e in range(G):
                lo = (q * G + e) * NK + sb * SB
                hh = hb_ref[lo:lo + SB, ts]
                wb_ref[lo:lo + SB, ts] = (g[e] * (0.5 * hh * (1.0 + lax.erf(hh * INV_SQRT2)))).astype(bf16)

        b_blocks = [(q, c, sb) for q in range(CE // (G * NK)) for c in range(T // LANES)
                    for sb in range(NK // SB)]
        sa = len(b_blocks) // PA
        for t, blk in enumerate(b_blocks):
            stage_b(*blk)
            if t % sa == 0:
                stage_a(t // sa)
            if t == len(b_blocks) // 2:
                acc_ref[...] += lax.dot_general(wc_ref[...], v_ref[...], _TN, preferred_element_type=f32)

    pl.when(s % 2 == 0)(functools.partial(step, h0_ref, h1_ref, w1_ref, w0_ref))
    pl.when(s % 2 == 1)(functools.partial(step, h1_ref, h0_ref, w0_ref, w1_ref))

    @pl.when(c_chunk == n_chunks - 1)
    def _():
        o_ref[...] = acc_ref[...]


def _peer_experts(xtb, ub, vb, phi, a1, a2):
    d, n = xtb.shape
    T, CE = PEER_TOK_TILE, PEER_EXPERT_CHUNK
    H, NK = PEER_HEADS, PEER_NKEYS
    e1_per_chunk = CE // NK
    nc = PEER_EXPERTS // CE
    last = (n // T) * nc - 1
    at = lambda s, lag: jnp.clip(s - lag, 0, last)
    return pl.pallas_call(
        _peer_expert_kernel,
        grid=(last + 3,),
        in_specs=[
            pl.BlockSpec((d, T), lambda s: (0, at(s, 0) // nc)),
            pl.BlockSpec((CE // 2, d), lambda s: (at(s, 0) % nc, 0)),
            pl.BlockSpec((CE, d), lambda s: (at(s, 2) % nc, 0)),
            pl.BlockSpec((H, e1_per_chunk, T), lambda s: (0, at(s, 1) % nc, at(s, 1) // nc)),
            pl.BlockSpec((H, e1_per_chunk, T), lambda s: (0, at(s, 1) % nc, at(s, 1) // nc)),
            pl.BlockSpec((H, NK, T), lambda s: (0, 0, at(s, 1) // nc)),
        ],
        out_specs=pl.BlockSpec((T, d), lambda s: (at(s, 2) // nc, 0)),
        out_shape=jax.ShapeDtypeStruct((n, d), jnp.float32),
        scratch_shapes=[pltpu.VMEM((T, d), jnp.float32),
                        pltpu.VMEM((CE, T), jnp.float32), pltpu.VMEM((CE, T), jnp.float32),
                        pltpu.VMEM((CE, T), jnp.bfloat16), pltpu.VMEM((CE, T), jnp.bfloat16)],
        compiler_params=pltpu.CompilerParams(dimension_semantics=("arbitrary",)),
        name="peer_experts",
    )(xtb, ub, vb, phi, a1, a2)


def _pack_row_pairs(t_bf16):
    rows, cols = t_bf16.shape
    pairs = jnp.swapaxes(t_bf16.reshape(rows // 2, 2, cols), -1, -2)
    return lax.bitcast_convert_type(pairs, jnp.uint32)


def _peer_pallas(x2, w_q, sub_k1, sub_k2, u_tab, v_tab):
    bf16 = jnp.bfloat16
    xb = x2.astype(bf16)
    sel = _peer_select(xb, w_q.T.astype(bf16), sub_k1.astype(bf16), sub_k2.astype(bf16))
    return _peer_experts(xb.T, _pack_row_pairs(u_tab.astype(bf16)), v_tab.astype(bf16), *sel)


RWKV_CHUNK = 64
RWKV_PAIR = LANES // RWKV_HEAD
RWKV_PAIRS = RWKV_HEADS // RWKV_PAIR
_NN = (((1,), (0,)), ((), ()))
_TN = (((0,), (0,)), ((), ()))


def _split_bf16(a):
    hi = a.astype(jnp.bfloat16)
    lo = (a - hi.astype(jnp.float32)).astype(jnp.bfloat16)
    return hi, lo


def _mm2(a, b, dims):
    d = lambda u, v: lax.dot_general(u, v, dims, preferred_element_type=jnp.float32)
    (ah, al), (bh, bl) = _split_bf16(a), _split_bf16(b)
    return d(ah, bh) + d(ah, bl) + d(al, bh)


def _mm_exact_rhs(a, b_exact, dims, terms=2):
    d = lambda u: lax.dot_general(u, b_exact, dims, preferred_element_type=jnp.float32)
    out, rem = None, a
    for _ in range(terms):
        piece = rem.astype(jnp.bfloat16)
        rem = rem - piece.astype(jnp.float32)
        out = d(piece) if out is None else out + d(piece)
    return out


def _mm_exact_lhs(a_exact, b, dims, terms=3):
    d = lambda v: lax.dot_general(a_exact, v, dims, preferred_element_type=jnp.float32)
    out, rem = None, b
    for _ in range(terms):
        piece = rem.astype(jnp.bfloat16)
        rem = rem - piece.astype(jnp.float32)
        out = d(piece) if out is None else out + d(piece)
    return out


def _softplus(z):
    return jnp.maximum(z, 0.0) + jnp.log1p(jnp.exp(-jnp.abs(z)))


def _rwkv_kernel(p_ref, prev_ref, mu_ref, w0_ref, wup_ref, a0_ref, aup_ref, gup_ref,
                 kk_ref, ka_ref, rk_ref, lng_ref, lnb_ref, ones_ref,
                 o_ref, state_ref):
    f32, bf16 = jnp.float32, jnp.bfloat16
    L, W = RWKV_CHUNK, RWKV_WIDTH
    c_idx = pl.program_id(1)

    @pl.when(c_idx == 0)
    def _():
        state_ref[...] = jnp.zeros_like(state_ref)

    p = p_ref[0]
    row = lax.broadcasted_iota(jnp.int32, p.shape, 0)
    prev_row = jnp.where(c_idx == 0, 0.0, prev_ref[0, 7:8, :])
    shifted = jnp.where(row == 0, prev_row, pltpu.roll(p, 1, axis=0))
    p = p + (shifted - p) * mu_ref[...]
    r, k, v = p[:, 0:W], p[:, W:2 * W], p[:, 2 * W:3 * W]
    o = 3 * W
    xw = p[:, o:o + RWKV_W_RANK]
    xa = p[:, o + RWKV_W_RANK:o + RWKV_W_RANK + RWKV_A_RANK]
    xg = p[:, o + RWKV_W_RANK + RWKV_A_RANK:]
    dotd = lambda u, m: jnp.dot(u.astype(bf16), m, preferred_element_type=f32)
    lw = -jnp.exp(-_softplus(-(w0_ref[...] + dotd(jnp.tanh(xw), wup_ref[...]))) - 0.5)
    a = jax.nn.sigmoid(a0_ref[...] + dotd(xa, aup_ref[...]))
    g = dotd(jax.nn.sigmoid(xg), gup_ref[...])
    ones_bd = ones_ref[...]
    head_sum = lambda t: _mm_exact_rhs(t, ones_bd, _NN)
    kk = k * kk_ref[...]
    kk = kk / jnp.maximum(jnp.sqrt(head_sum(kk * kk)), 1e-12)
    k = k * (1.0 + (a - 1.0) * ka_ref[...])

    lane = lax.broadcasted_iota(jnp.int32, (L, LANES), 1)
    trow = lax.broadcasted_iota(jnp.int32, (L, LANES), 0)
    s_in = lane % RWKV_HEAD
    strict, incl = s_in < trow, s_in <= trow
    eye_pair = (s_in == trow).astype(f32)
    m0 = lane < RWKV_HEAD
    tril = (lax.broadcasted_iota(jnp.int32, (L, L), 1)
            <= lax.broadcasted_iota(jnp.int32, (L, L), 0)).astype(bf16)
    r2 = lax.broadcasted_iota(jnp.int32, (LANES, LANES), 0) // RWKV_HEAD
    c2 = lax.broadcasted_iota(jnp.int32, (LANES, LANES), 1) // RWKV_HEAD
    bd_mask = r2 == c2

    def bd(t):
        return jnp.concatenate([jnp.where(m0, t, 0.0), jnp.where(m0, 0.0, t)], axis=0)

    pairs = range(RWKV_PAIRS)
    cut = lambda t: [t[:, pr * LANES:(pr + 1) * LANES] for pr in pairs]
    rp, kp, vp, kkp, ap, lwp = cut(r), cut(k), cut(v), cut(kk), cut(a), cut(lw)
    c = [_mm_exact_lhs(tril, lwp[p], _NN) for p in pairs]
    at = [-kkp[p] * jnp.exp(c[p] - lwp[p]) for p in pairs]
    eni = [jnp.exp(-c[p]) for p in pairs]
    bt = [kkp[p] * ap[p] * eni[p] for p in pairs]
    kt = [kp[p] * eni[p] for p in pairs]
    rt = [rp[p] * jnp.exp(c[p]) for p in pairs]
    g_last = [jnp.exp(c[p][L - 1:L, :]) for p in pairs]
    gram = [_mm2(jnp.concatenate([at[p], rt[p]], axis=0),
                 jnp.concatenate([bd(bt[p]), bd(kt[p])], axis=0), _NT) for p in pairs]
    n_ab = [jnp.where(strict, gram[p][0:L, 0:LANES], 0.0) for p in pairs]
    a_ak = [jnp.where(strict, gram[p][0:L, LANES:], 0.0) for p in pairs]
    a_rb = [jnp.where(incl, gram[p][L:, 0:LANES], 0.0) for p in pairs]
    a_rk = [jnp.where(incl, gram[p][L:, LANES:], 0.0) for p in pairs]
    tinv, m = [eye_pair + n_ab[p] for p in pairs], n_ab
    for _ in range(5):
        m = [_mm2(m[p], bd(m[p]), _NN) for p in pairs]
        tinv = [tinv[p] + _mm2(m[p], bd(tinv[p]), _NN) for p in pairs]
    s0 = [state_ref[p] for p in pairs]
    pq = [_mm2(at[p], s0[p], _NT) + _mm2(a_ak[p], bd(vp[p]), _NN) for p in pairs]
    u = [_mm2(tinv[p], bd(pq[p]), _NN) for p in pairs]
    ys = [_mm2(rt[p], s0[p], _NT)
          + _mm2(jnp.concatenate([a_rb[p], a_rk[p]], axis=1),
                 jnp.concatenate([bd(u[p]), bd(vp[p])], axis=0), _NN) for p in pairs]
    for p in pairs:
        upd = _mm2(jnp.concatenate([u[p], vp[p]], axis=0),
                   jnp.concatenate([bt[p], kt[p]], axis=0), _TN)
        state_ref[p] = (s0[p] + jnp.where(bd_mask, upd, 0.0)) * g_last[p]
    y = jnp.concatenate(ys, axis=1)

    inv_n = 1.0 / RWKV_HEAD
    mean = head_sum(y) * inv_n
    yc = y - mean
    var = head_sum(yc * yc) * inv_n
    y = yc * lax.rsqrt(var + RWKV_GN_EPS) * lng_ref[...] + lnb_ref[...]
    y = y + head_sum(r * k * rk_ref[...]) * v
    o_ref[0] = y * g


def _rwkv_pallas(p_r, mu, w0, w_up, a0, a_up, g_up, k_k, k_a, r_k, ln_g, ln_b):
    B, S, C = p_r.shape
    L, W = RWKV_CHUNK, RWKV_WIDTH
    bf16 = jnp.bfloat16
    vecw = lambda t: t.reshape(1, W)
    ones_bd = jnp.kron(jnp.eye(RWKV_HEADS, dtype=bf16), jnp.ones((RWKV_HEAD, RWKV_HEAD), bf16))
    full = lambda a: pl.BlockSpec(a.shape, lambda b, c: (0,) * a.ndim)
    args = [mu.reshape(1, C), vecw(w0), w_up.astype(bf16), vecw(a0), a_up.astype(bf16),
            g_up.astype(bf16), vecw(k_k), vecw(k_a), vecw(r_k), vecw(ln_g), vecw(ln_b), ones_bd]
    return pl.pallas_call(
        _rwkv_kernel,
        grid=(B, S // L),
        in_specs=[pl.BlockSpec((1, L, C), lambda b, c: (b, c, 0)),
                  pl.BlockSpec((1, 8, C), lambda b, c: (b, jnp.maximum(c * (L // 8) - 1, 0), 0))]
                 + [full(a) for a in args],
        out_specs=pl.BlockSpec((1, L, W), lambda b, c: (b, c, 0)),
        out_shape=jax.ShapeDtypeStruct((B, S, W), jnp.float32),
        scratch_shapes=[pltpu.VMEM((RWKV_PAIRS, LANES, LANES), jnp.float32)],
        compiler_params=pltpu.CompilerParams(dimension_semantics=("arbitrary", "arbitrary")),
        name="rwkv7",
    )(p_r, p_r, *args)


MLSTM_CONV = 4
MLSTM_GATE_LANES = LANES


def _mlstm_kernel(p_ref, prev_ref, cw_ref, cb_ref, igb_ref, fgb_ref, hng_ref, hnb_ref,
                  o_ref, c_ref, n_ref, m_ref):
    f32, bf16 = jnp.float32, jnp.bfloat16
    L, W, H, D = MLSTM_CHUNK, MLSTM_WIDTH, MLSTM_HEADS, MLSTM_HEAD
    c_idx = pl.program_id(1)

    @pl.when(c_idx == 0)
    def _():
        c_ref[...] = jnp.zeros_like(c_ref)
        n_ref[...] = jnp.zeros_like(n_ref)
        m_ref[...] = jnp.zeros_like(m_ref)

    p = p_ref[0]
    z = p[:, 0:2 * W]
    prev = jnp.where(c_idx == 0, 0.0, prev_ref[0, :, 0:2 * W])
    ext = jnp.concatenate([prev, z], axis=0)
    conv = cb_ref[...] + cw_ref[MLSTM_CONV - 1:MLSTM_CONV, :] * z
    for j in range(MLSTM_CONV - 1):
        d = MLSTM_CONV - 1 - j
        conv = conv + cw_ref[j:j + 1, :] * ext[8 - d:8 - d + L, :]
    qk = conv * jax.nn.sigmoid(conv)
    q_all, k_all = qk[:, 0:W] * (D ** -0.5), qk[:, W:2 * W]
    v_all, o_all = p[:, 2 * W:3 * W], p[:, 3 * W:4 * W]
    gates = p[:, 4 * W:]
    igl = gates + igb_ref[...]
    lfl = -_softplus(-(gates + fgb_ref[...]))
    tril = (lax.broadcasted_iota(jnp.int32, (L, L), 1)
            <= lax.broadcasted_iota(jnp.int32, (L, L), 0))
    bcum = _mm_exact_lhs(tril.astype(bf16), lfl, _NN)
    e_all = igl - pltpu.roll(bcum, LANES - H, axis=1)
    lane = lax.broadcasted_iota(jnp.int32, (L, LANES), 1)
    outs = []
    for h in range(H):
        hs = slice(h * D, (h + 1) * D)
        q, k, v = q_all[:, hs], k_all[:, hs], v_all[:, hs]
        b_col = bcum[:, H + h:H + h + 1]
        ig_col = igl[:, h:h + 1]
        onehot = (lane == h).astype(bf16)
        e_row = _mm_exact_lhs(onehot, e_all, _NT)
        m_prev = m_ref[h, 0:1, 0:1]
        dmat = jnp.where(tril, b_col + e_row, -jnp.inf)
        inter = b_col + m_prev
        m_t = jnp.maximum(inter, jnp.max(dmat, axis=-1, keepdims=True))
        weights = jnp.exp(dmat - m_t)
        sc = lax.dot_general(q.astype(bf16), k.astype(bf16), _NT, preferred_element_type=f32) * weights
        carry_in = jnp.exp(inter - m_t)
        c_prev, n_prev = c_ref[h], n_ref[h, 0:1, :]
        num = (jnp.dot(sc.astype(bf16), v.astype(bf16), preferred_element_type=f32)
               + carry_in * jnp.dot(q.astype(bf16), c_prev.astype(bf16), preferred_element_type=f32))
        den = (jnp.sum(sc, axis=-1, keepdims=True)
               + carry_in * jnp.sum(q * n_prev, axis=-1, keepdims=True))
        hval = num / jnp.maximum(jnp.abs(den), jnp.exp(-m_t))
        b_last = b_col[L - 1:L, :]
        gs = b_last - b_col + ig_col
        m_new = jnp.maximum(b_last + m_prev, jnp.max(gs, axis=0, keepdims=True))
        ws = jnp.exp(gs - m_new)
        keep = jnp.exp(b_last + m_prev - m_new)
        wk = ws * k
        c_ref[h] = keep * c_prev + lax.dot_general(wk.astype(bf16), v.astype(bf16), _TN,
                                                   preferred_element_type=f32)
        n_ref[h] = jnp.broadcast_to(keep * n_prev + jnp.sum(wk, axis=0, keepdims=True), (8, D))
        m_ref[h] = jnp.broadcast_to(m_new, (8, LANES))
        mu = jnp.mean(hval, axis=-1, keepdims=True)
        hc = hval - mu
        var = jnp.mean(hc * hc, axis=-1, keepdims=True)
        hn = hc * lax.rsqrt(var + LN_EPS) * hng_ref[:, hs] + hnb_ref[:, hs]
        outs.append(jax.nn.sigmoid(o_all[:, hs]) * hn)
    o_ref[0] = jnp.concatenate(outs, axis=1)


def _mlstm_pallas(p_m, conv_w, conv_b, ig_b, fg_b, hn_g, hn_b):
    B, S, C = p_m.shape
    L, W, H, D = MLSTM_CHUNK, MLSTM_WIDTH, MLSTM_HEADS, MLSTM_HEAD
    pad = lambda t, off: jnp.zeros((1, LANES), jnp.float32).at[0, off:off + H].set(t)
    args = [conv_w, conv_b.reshape(1, 2 * W), pad(ig_b, 0), pad(fg_b, H),
            hn_g.reshape(1, W), hn_b.reshape(1, W)]
    full = lambda a: pl.BlockSpec(a.shape, lambda b, c: (0,) * a.ndim)
    return pl.pallas_call(
        _mlstm_kernel,
        grid=(B, S // L),
        in_specs=[pl.BlockSpec((1, L, C), lambda b, c: (b, c, 0)),
                  pl.BlockSpec((1, 8, C), lambda b, c: (b, jnp.maximum(c * (L // 8) - 1, 0), 0))]
                 + [full(a) for a in args],
        out_specs=pl.BlockSpec((1, L, W), lambda b, c: (b, c, 0)),
        out_shape=jax.ShapeDtypeStruct((B, S, W), jnp.float32),
        scratch_shapes=[pltpu.VMEM((H, D, D), jnp.float32),
                        pltpu.VMEM((H, 8, D), jnp.float32),
                        pltpu.VMEM((H, 8, LANES), jnp.float32)],
        compiler_params=pltpu.CompilerParams(dimension_semantics=("arbitrary", "arbitrary")),
        name="mlstm",
    )(p_m, p_m, *args)


FOX_TILE = 256
FOX_KEY_GROUP = 4
FOX_HEADS_PER_STEP = 8
FOX_AUG = LANES
FOX_GATE_LANES = LANES


def _fox_prep_kernel(p_ref, qg_ref, kg_ref, fb_ref, ones_ref, qa_ref, ka_ref, vt_ref, carry_ref):
    f32, bf16 = jnp.float32, jnp.bfloat16
    T, W, H, D = FOX_TILE, FOX_WIDTH, FOX_HEADS, FOX_HEAD
    i = pl.program_id(1)

    @pl.when(i == 0)
    def _():
        carry_ref[...] = jnp.zeros_like(carry_ref)

    p = p_ref[0]
    ones_bd = ones_ref[...]
    ms = lambda t: _mm_exact_rhs(t * t, ones_bd, _NN) * (1.0 / D)
    q = p[:, 0:W]
    q = q * lax.rsqrt(ms(q) + 1e-6) * qg_ref[...]
    k = p[:, W:2 * W]
    k = k * lax.rsqrt(ms(k) + 1e-6) * kg_ref[...] * (D ** -0.5)
    vt = p[:, 2 * W:3 * W].T.astype(bf16)
    logf = -_softplus(-(p[:, 4 * W:] + fb_ref[...]))
    tril = (lax.broadcasted_iota(jnp.int32, (T, T), 1)
            <= lax.broadcasted_iota(jnp.int32, (T, T), 0)).astype(bf16)
    c = _mm_exact_lhs(tril, logf, _NN) + carry_ref[0:1, :]
    carry_ref[...] = jnp.broadcast_to(c[T - 1:T, :], carry_ref.shape)
    c1 = c.astype(bf16).astype(f32)
    c2 = (c - c1).astype(bf16).astype(f32)
    c3 = (c - c1 - c2).astype(bf16).astype(f32)
    lane = lax.broadcasted_iota(jnp.int32, (T, D), 1)
    for h in range(H):
        hs = slice(h * D, (h + 1) * D)
        c1h, c2h, c3h = c1[:, h:h + 1], c2[:, h:h + 1], c3[:, h:h + 1]
        pieces = jnp.where(lane % 3 == 0, c1h, jnp.where(lane % 3 == 1, c2h, c3h))
        q_aug = jnp.where(lane < 3, 1.0, jnp.where(lane < 6, pieces, 0.0))
        k_aug = jnp.where(lane < 3, -pieces, jnp.where(lane < 6, 1.0, 0.0))
        qa_ref[0, h] = jnp.concatenate([q[:, hs], q_aug], axis=1).astype(bf16)
        ka_ref[0, h] = jnp.concatenate([k[:, hs], k_aug], axis=1).astype(bf16)
        vt_ref[0, h, 0] = vt[hs, :]


def _fox_prep(p_f, qn_g, kn_g, f_b):
    B, S, C = p_f.shape
    T, W, H, D = FOX_TILE, FOX_WIDTH, FOX_HEADS, FOX_HEAD
    bf16 = jnp.bfloat16
    ones_bd = jnp.kron(jnp.eye(H, dtype=bf16), jnp.ones((D, D), bf16))
    fb = jnp.zeros((1, FOX_GATE_LANES), jnp.float32).at[0, :H].set(f_b)
    args = [qn_g.reshape(1, W), kn_g.reshape(1, W), fb, ones_bd]
    full = lambda a: pl.BlockSpec(a.shape, lambda b, i: (0,) * a.ndim)
    return pl.pallas_call(
        _fox_prep_kernel,
        grid=(B, S // T),
        in_specs=[pl.BlockSpec((1, T, C), lambda b, i: (b, i, 0))] + [full(a) for a in args],
        out_specs=[pl.BlockSpec((1, H, T, FOX_AUG), lambda b, i: (b, 0, i, 0)),
                   pl.BlockSpec((1, H, T, FOX_AUG), lambda b, i: (b, 0, i, 0)),
                   pl.BlockSpec((1, H, 1, D, T), lambda b, i: (b, 0, i, 0, 0))],
        out_shape=[jax.ShapeDtypeStruct((B, H, S, FOX_AUG), bf16),
                   jax.ShapeDtypeStruct((B, H, S, FOX_AUG), bf16),
                   jax.ShapeDtypeStruct((B, H, S // T, D, T), bf16)],
        scratch_shapes=[pltpu.VMEM((8, FOX_GATE_LANES), jnp.float32)],
        compiler_params=pltpu.CompilerParams(dimension_semantics=("arbitrary", "arbitrary"),
                                             vmem_limit_bytes=VMEM_LIMIT_BYTES),
        name="fox_prep",
    )(p_f, *args)


def _fox_attn_kernel(qa_ref, ka_ref, vt_ref, og_ref, o_ref, m_ref, l_ref, acc_ref):
    f32, bf16 = jnp.float32, jnp.bfloat16
    T, D = FOX_TILE, FOX_HEAD
    i = pl.program_id(2)
    heads = range(FOX_HEADS_PER_STEP)
    n_blocks = ka_ref.shape[2] // T
    GK = FOX_KEY_GROUP
    rel = (lax.broadcasted_iota(jnp.int32, (GK * T, T), 0)
           - lax.broadcasted_iota(jnp.int32, (GK * T, T), 1))
    qas = [qa_ref[0, j] for j in heads]

    def key_group(g, first):
        visible = rel <= (i - g * GK) * T

        def scores(j):
            ka = ka_ref[0, j, g * GK * T:(g + 1) * GK * T, :]
            return lax.dot_general(ka, qas[j], _NT, preferred_element_type=f32)

        nxt = scores(0)
        for j in heads:
            s = nxt
            if j + 1 < len(heads):
                nxt = scores(j + 1)
            s = jnp.where(visible, s, -jnp.inf)
            m_g = jnp.max(s, axis=0, keepdims=True)
            m_new = m_g if first else jnp.maximum(m_ref[j], m_g)
            pr = jnp.exp(s - m_new)
            l_g = jnp.sum(pr, axis=0, keepdims=True)
            vt = jnp.concatenate([vt_ref[0, j, g * GK + b] for b in range(GK)], axis=1)
            pv = jnp.dot(vt, pr.astype(bf16), preferred_element_type=f32)
            if first:
                l_ref[j], acc_ref[j] = l_g, pv
            else:
                alpha = jnp.exp(m_ref[j] - m_new)
                l_ref[j] = l_ref[j] * alpha + l_g
                acc_ref[j] = acc_ref[j] * alpha + pv
            m_ref[j] = m_new

    key_group(0, True)
    for g in range(1, n_blocks // GK):
        pl.when(i >= g * GK)(functools.partial(key_group, g, False))
    outs = [(acc_ref[j] / l_ref[j]).T for j in heads]
    o_ref[0] = jax.nn.sigmoid(og_ref[0]) * jnp.concatenate(outs, axis=1)


def _fox_attn(qa, ka, vt, p_f):
    B, H, S, A = qa.shape
    T, D = FOX_TILE, FOX_HEAD
    hp = FOX_HEADS_PER_STEP
    width = hp * D
    og_block0 = 3 * FOX_WIDTH // width
    return pl.pallas_call(
        _fox_attn_kernel,
        grid=(B, H // hp, S // T),
        in_specs=[pl.BlockSpec((1, hp, T, A), lambda b, g, i: (b, g, i, 0)),
                  pl.BlockSpec((1, hp, S, A), lambda b, g, i: (b, g, 0, 0)),
                  pl.BlockSpec((1, hp, S // T, D, T), lambda b, g, i: (b, g, 0, 0, 0)),
                  pl.BlockSpec((1, T, width), lambda b, g, i: (b, i, og_block0 + g))],
        out_specs=pl.BlockSpec((1, T, width), lambda b, g, i: (b, i, g)),
        out_shape=jax.ShapeDtypeStruct((B, S, FOX_WIDTH), jnp.float32),
        scratch_shapes=[pltpu.VMEM((hp, 1, T), jnp.float32), pltpu.VMEM((hp, 1, T), jnp.float32),
                        pltpu.VMEM((hp, D, T), jnp.float32)],
        name="fox_attn",
    )(qa, ka, vt, p_f)


def kernel(x, l0_w_in, l0_rwkv_mu, l0_rwkv_w0, l0_rwkv_w_up, l0_rwkv_a0, l0_rwkv_a_up,
           l0_rwkv_g_up, l0_rwkv_k_k, l0_rwkv_k_a, l0_rwkv_r_k, l0_rwkv_ln_g, l0_rwkv_ln_b,
           l0_mlstm_conv_w, l0_mlstm_conv_b, l0_mlstm_ig_b, l0_mlstm_fg_b,
           l0_mlstm_hn_g, l0_mlstm_hn_b, l0_w_out, l0_ln1_g, l0_ln1_b,
           l0_peer_wq, l0_peer_k1, l0_peer_k2, l0_peer_u, l0_peer_v, l0_ln2_g, l0_ln2_b,
           l1_w_in, l1_fox_qn_g, l1_fox_kn_g, l1_fox_f_b, l1_w_out, l1_ln1_g, l1_ln1_b,
           l1_peer_wq, l1_peer_k1, l1_peer_k2, l1_peer_u, l1_peer_v, l1_ln2_g, l1_ln2_b):
    B, S, D = x.shape
    n = B * S
    bf16 = jnp.bfloat16
    x2 = x.reshape(n, D)

    def pad_lanes(w):
        return jnp.pad(w, ((0, 0), (0, LANES - w.shape[1])))

    m_main = 4 * MLSTM_WIDTH
    w0 = jnp.concatenate([l0_w_in[:, :RWKV_COLS + m_main],
                          pad_lanes(l0_w_in[:, RWKV_COLS + m_main:])], axis=1).astype(bf16)
    p_r, p_m = _proj(x2, w0, (RWKV_COLS, m_main + LANES))
    y_a = _rwkv_pallas(p_r.reshape(B, S, -1), l0_rwkv_mu, l0_rwkv_w0, l0_rwkv_w_up, l0_rwkv_a0,
                       l0_rwkv_a_up, l0_rwkv_g_up, l0_rwkv_k_k, l0_rwkv_k_a, l0_rwkv_r_k,
                       l0_rwkv_ln_g, l0_rwkv_ln_b)
    y_b = _mlstm_pallas(p_m.reshape(B, S, -1), l0_mlstm_conv_w, l0_mlstm_conv_b, l0_mlstm_ig_b,
                        l0_mlstm_fg_b, l0_mlstm_hn_g, l0_mlstm_hn_b)
    x2 = _out_proj_ln([y_a.reshape(n, -1), y_b.reshape(n, -1)], l0_w_out.astype(bf16), x2,
                      l0_ln1_g, l0_ln1_b)
    y = _peer_pallas(x2, l0_peer_wq, l0_peer_k1, l0_peer_k2, l0_peer_u, l0_peer_v)
    x2 = _resid_ln(x2, y, l0_ln2_g, l0_ln2_b)

    f_main = 4 * FOX_WIDTH
    w1 = jnp.concatenate([l1_w_in[:, :f_main], pad_lanes(l1_w_in[:, f_main:])], axis=1).astype(bf16)
    (p_f,) = _proj(x2, w1, (f_main + LANES,))
    p_f = p_f.reshape(B, S, -1)
    qa, ka, vt = _fox_prep(p_f, l1_fox_qn_g, l1_fox_kn_g, l1_fox_f_b)
    o = _fox_attn(qa, ka, vt, p_f)
    x2 = _out_proj_ln([o.reshape(n, -1)], l1_w_out.astype(bf16), x2, l1_ln1_g, l1_ln1_b)
    y = _peer_pallas(x2, l1_peer_wq, l1_peer_k1, l1_peer_k2, l1_peer_u, l1_peer_v)
    x2 = _resid_ln(x2, y, l1_ln2_g, l1_ln2_b)
    return x2.reshape(B, S, D)
```

```python
import functools

import jax
import jax.numpy as jnp
from jax import lax
from jax.experimental import pallas as pl
from jax.experimental.pallas import tpu as pltpu

D_MODEL = 1024
DEPTH = 2
DN_ALPHA = (2.0 * DEPTH) ** 0.25
LN_EPS = 1e-5

RWKV_WIDTH = D_MODEL // 2
RWKV_HEAD = 64
RWKV_HEADS = RWKV_WIDTH // RWKV_HEAD
RWKV_W_RANK = 64
RWKV_A_RANK = 64
RWKV_G_RANK = 128
RWKV_GN_EPS = 1e-5 * RWKV_HEAD
RWKV_COLS = 3 * RWKV_WIDTH + RWKV_W_RANK + RWKV_A_RANK + RWKV_G_RANK

MLSTM_WIDTH = D_MODEL // 2
MLSTM_HEAD = 128
MLSTM_HEADS = MLSTM_WIDTH // MLSTM_HEAD
MLSTM_CHUNK = 64

FOX_HEAD = 64
FOX_HEADS = D_MODEL // FOX_HEAD
FOX_WIDTH = FOX_HEADS * FOX_HEAD
FOX_QBLOCK = 128

PEER_HEADS = 8
PEER_NKEYS = 128
PEER_TOPK = 16
PEER_DKEY = 256
PEER_DHALF = PEER_DKEY // 2
PEER_TOKBLOCK = 128


def _split_cols(p, sizes):
    out, start = [], 0
    for s in sizes:
        out.append(p[..., start:start + s])
        start += s
    return out


def _resid_ln_kernel(x_ref, y_ref, g_ref, b_ref, o_ref):
    z = DN_ALPHA * x_ref[...] + y_ref[...]
    mu = jnp.mean(z, axis=-1, keepdims=True)
    zc = z - mu
    var = jnp.mean(zc * zc, axis=-1, keepdims=True)
    o_ref[...] = zc * lax.rsqrt(var + LN_EPS) * g_ref[...] + b_ref[...]


def _resid_ln(x2, y2, g, b, tm=512):
    n, d = x2.shape
    row = pl.BlockSpec((tm, d), lambda i: (i, 0))
    vec = pl.BlockSpec((1, d), lambda i: (0, 0))
    return pl.pallas_call(
        _resid_ln_kernel,
        grid=(n // tm,),
        in_specs=[row, row, vec, vec],
        out_specs=row,
        out_shape=jax.ShapeDtypeStruct((n, d), jnp.float32),
        name="resid_ln",
    )(x2, y2, g.reshape(1, d), b.reshape(1, d))


VMEM_LIMIT_BYTES = 56 * 1024 * 1024


def _proj_kernel(x_ref, w_ref, *o_refs):
    xb = x_ref[...].astype(jnp.bfloat16)
    start = 0
    for o_ref in o_refs:
        width = o_ref.shape[1]
        o_ref[...] = jnp.dot(xb, w_ref[:, start:start + width], preferred_element_type=jnp.float32)
        start += width


def _proj(x2, w_bf16, widths, tm=256):
    n, d = x2.shape
    assert sum(widths) == w_bf16.shape[1]
    return pl.pallas_call(
        _proj_kernel,
        grid=(n // tm,),
        in_specs=[pl.BlockSpec((tm, d), lambda i: (i, 0)),
                  pl.BlockSpec(w_bf16.shape, lambda i: (0, 0))],
        out_specs=[pl.BlockSpec((tm, w), lambda i: (i, 0)) for w in widths],
        out_shape=[jax.ShapeDtypeStruct((n, w), jnp.float32) for w in widths],
        compiler_params=pltpu.CompilerParams(vmem_limit_bytes=VMEM_LIMIT_BYTES),
        name="in_proj",
    )(x2, w_bf16)


def _out_proj_ln_kernel(*refs, n_parts):
    y_refs, (w_ref, x_ref, g_ref, b_ref, o_ref) = refs[:n_parts], refs[n_parts:]
    acc, start = None, 0
    for y_ref in y_refs:
        width = y_ref.shape[1]
        part = jnp.dot(y_ref[...].astype(jnp.bfloat16), w_ref[start:start + width, :],
                       preferred_element_type=jnp.float32)
        acc = part if acc is None else acc + part
        start += width
    z = DN_ALPHA * x_ref[...] + acc
    mu = jnp.mean(z, axis=-1, keepdims=True)
    zc = z - mu
    var = jnp.mean(zc * zc, axis=-1, keepdims=True)
    o_ref[...] = zc * lax.rsqrt(var + LN_EPS) * g_ref[...] + b_ref[...]


def _out_proj_ln(ys, w_bf16, x2, g, b, tm=256):
    n, d = x2.shape
    row = lambda width: pl.BlockSpec((tm, width), lambda i: (i, 0))
    vec = pl.BlockSpec((1, d), lambda i: (0, 0))
    return pl.pallas_call(
        functools.partial(_out_proj_ln_kernel, n_parts=len(ys)),
        grid=(n // tm,),
        in_specs=[row(y.shape[1]) for y in ys]
                 + [pl.BlockSpec(w_bf16.shape, lambda i: (0, 0)), row(d), vec, vec],
        out_specs=row(d),
        out_shape=jax.ShapeDtypeStruct((n, d), jnp.float32),
        name="out_proj_ln",
    )(*ys, w_bf16, x2, g.reshape(1, d), b.reshape(1, d))


def _head_norm(y, g, b, eps):
    mu = jnp.mean(y, -1, keepdims=True)
    var = jnp.mean(jnp.square(y - mu), -1, keepdims=True)
    return (y - mu) * lax.rsqrt(var + eps) * g + b


def _rms_norm(y, g):
    return y * lax.rsqrt(jnp.mean(y * y, -1, keepdims=True) + 1e-6) * g


def _token_shift(z):
    return jnp.pad(z, ((0, 0), (1, 0), (0, 0)))[:, :-1]


def _causal_conv(z, w, b):
    c = z.shape[-1]
    out = lax.conv_general_dilated(z, w[:, None, :], window_strides=(1,),
                                   padding=((w.shape[0] - 1, 0),),
                                   dimension_numbers=('NWC', 'WIO', 'NWC'),
                                   feature_group_count=c)
    return out + b


def _rwkv7_mix(p, mu, w0, w_up, a0, a_up, g_up, k_k, k_a, r_k, ln_g, ln_b):
    B, S, _ = p.shape
    H, N = RWKV_HEADS, RWKV_HEAD
    p = p + (_token_shift(p) - p) * mu
    r, k, v, xw, xa, xg = _split_cols(p, (RWKV_WIDTH, RWKV_WIDTH, RWKV_WIDTH,
                                          RWKV_W_RANK, RWKV_A_RANK, RWKV_G_RANK))
    log_w = -jnp.exp(-jax.nn.softplus(-(w0 + jnp.tanh(xw) @ w_up)) - 0.5)
    a = jax.nn.sigmoid(a0 + xa @ a_up)
    g = jax.nn.sigmoid(xg) @ g_up
    heads = lambda t: t.reshape(B, S, H, N)
    kk = heads(k * k_k)
    kk = kk / jnp.maximum(jnp.sqrt(jnp.sum(kk * kk, -1, keepdims=True)), 1e-12)
    k = k * (1.0 + (a - 1.0) * k_a)
    r_h, k_h, v_h, a_h = heads(r), heads(k), heads(v), heads(a)
    w_h = jnp.exp(heads(log_w))

    def step(state, inp):
        r_t, w_t, k_t, v_t, kk_t, a_t = inp
        s_kk = jnp.einsum('bhvk,bhk->bhv', state, kk_t)
        state = (state * w_t[:, :, None, :]
                 - s_kk[..., None] * (kk_t * a_t)[:, :, None, :]
                 + v_t[..., None] * k_t[:, :, None, :])
        return state, jnp.einsum('bhvk,bhk->bhv', state, r_t)

    seq_first = lambda t: jnp.moveaxis(t, 1, 0)
    state0 = jnp.zeros((B, H, N, N), jnp.float32)
    _, y = lax.scan(step, state0, (seq_first(r_h), seq_first(w_h), seq_first(k_h),
                                   seq_first(v_h), seq_first(kk), seq_first(a_h)))
    y = jnp.moveaxis(y, 0, 1)
    y = _head_norm(y, ln_g, ln_b, RWKV_GN_EPS)
    y = y + jnp.sum(r_h * k_h * r_k, -1, keepdims=True) * v_h
    return y.reshape(B, S, RWKV_WIDTH) * g


def _mlstm_chunkwise(q, k, v, ig, lf):
    B, S, H, D = q.shape
    L = MLSTM_CHUNK
    NC = S // L

    def to_chunks(t):
        t = t.reshape((B, NC, L, H) + t.shape[3:])
        return jnp.moveaxis(t, (1, 3), (0, 2))

    causal = jnp.tril(jnp.ones((L, L), dtype=bool))

    def body(carry, inp):
        C, n, m = carry
        qc, kc, vc, igc, lfc = inp
        b = jnp.cumsum(lfc, axis=-1)
        dmat = jnp.where(causal, b[..., :, None] - b[..., None, :] + igc[..., None, :], -jnp.inf)
        inter = b + m[..., None]
        m_t = jnp.maximum(inter, jnp.max(dmat, -1))
        weights = jnp.exp(dmat - m_t[..., None])
        sc = jnp.einsum('bhtd,bhsd->bhts', qc, kc) * weights
        carry_in = jnp.exp(inter - m_t)
        num = (jnp.einsum('bhts,bhsd->bhtd', sc, vc)
               + carry_in[..., None] * jnp.einsum('bhtk,bhkv->bhtv', qc, C))
        den = jnp.sum(sc, -1) + carry_in * jnp.einsum('bhtk,bhk->bht', qc, n)
        h = num / jnp.maximum(jnp.abs(den), jnp.exp(-m_t))[..., None]
        b_last = b[..., -1]
        gs = b_last[..., None] - b + igc
        m_new = jnp.maximum(b_last + m, jnp.max(gs, -1))
        ws = jnp.exp(gs - m_new[..., None])
        keep = jnp.exp(b_last + m - m_new)
        C = keep[..., None, None] * C + jnp.einsum('bhs,bhsk,bhsv->bhkv', ws, kc, vc)
        n = keep[..., None] * n + jnp.einsum('bhs,bhsk->bhk', ws, kc)
        return (C, n, m_new), h

    f32 = jnp.float32
    init = (jnp.zeros((B, H, D, D), f32), jnp.zeros((B, H, D), f32), jnp.zeros((B, H), f32))
    _, h = lax.scan(body, init, (to_chunks(q), to_chunks(k), to_chunks(v),
                                 to_chunks(ig), to_chunks(lf)))
    return jnp.moveaxis(h, (0, 2), (1, 3)).reshape(B, S, H, D)


def _mlstm_mix(p, conv_w, conv_b, ig_b, fg_b, hn_g, hn_b):
    B, S, _ = p.shape
    H, N = MLSTM_HEADS, MLSTM_HEAD
    q, k, v, o, ig, fg = _split_cols(p, (MLSTM_WIDTH, MLSTM_WIDTH, MLSTM_WIDTH, MLSTM_WIDTH, H, H))
    qk = jax.nn.silu(_causal_conv(jnp.concatenate([q, k], -1), conv_w, conv_b))
    q, k = qk[..., :MLSTM_WIDTH], qk[..., MLSTM_WIDTH:]
    heads = lambda t: t.reshape(B, S, H, N)
    q = heads(q) * (N ** -0.5)
    ig = ig + ig_b
    lf = jax.nn.log_sigmoid(fg + fg_b)
    h = _mlstm_chunkwise(q, heads(k), heads(v), ig, lf)
    h = _head_norm(h, hn_g, hn_b, LN_EPS).reshape(B, S, MLSTM_WIDTH)
    return jax.nn.sigmoid(o) * h


def _even_mixer(x, w_in, mu, w0, w_up, a0, a_up, g_up, k_k, k_a, r_k, rln_g, rln_b,
                conv_w, conv_b, ig_b, fg_b, hn_g, hn_b, w_out):
    p = x @ w_in
    y_a = _rwkv_pallas(p[..., :RWKV_COLS], mu, w0, w_up, a0, a_up, g_up, k_k, k_a, r_k, rln_g, rln_b)
    y_b = _mlstm_mix(p[..., RWKV_COLS:], conv_w, conv_b, ig_b, fg_b, hn_g, hn_b)
    return jnp.concatenate([y_a, y_b], -1) @ w_out


def _fox_attention(q, k, v, logf):
    B, S, H, D = q.shape
    NB = S // FOX_QBLOCK
    c = jnp.cumsum(logf, axis=1)
    c_key = jnp.transpose(c, (0, 2, 1))[:, :, None, :]
    q_blocks = jnp.moveaxis(q.reshape(B, NB, FOX_QBLOCK, H, D), 1, 0)
    c_blocks = jnp.moveaxis(c.reshape(B, NB, FOX_QBLOCK, H), 1, 0)
    key_pos = jnp.arange(S)
    scale = D ** -0.5

    def one_block(args):
        qb, cb, blk = args
        q_pos = blk * FOX_QBLOCK + jnp.arange(FOX_QBLOCK)
        logits = jnp.einsum('bqhd,bkhd->bhqk', qb, k) * scale
        logits = logits + jnp.transpose(cb, (0, 2, 1))[..., None] - c_key
        logits = jnp.where(key_pos[None, :] <= q_pos[:, None], logits, -jnp.inf)
        probs = jax.nn.softmax(logits, axis=-1)
        return jnp.einsum('bhqk,bkhd->bqhd', probs, v)

    out = lax.map(one_block, (q_blocks, c_blocks, jnp.arange(NB)))
    return jnp.moveaxis(out, 0, 1).reshape(B, S, H, D)


def _odd_mixer(x, w_in, qn_g, kn_g, f_b, w_out):
    B, S, _ = x.shape
    p = x @ w_in
    q, k, v, og, fl = _split_cols(p, (FOX_WIDTH, FOX_WIDTH, FOX_WIDTH, FOX_WIDTH, FOX_HEADS))
    heads = lambda t: t.reshape(B, S, FOX_HEADS, FOX_HEAD)
    q = _rms_norm(heads(q), qn_g)
    k = _rms_norm(heads(k), kn_g)
    logf = jax.nn.log_sigmoid(fl + f_b)
    o = _fox_attention(q, k, heads(v), logf).reshape(B, S, FOX_WIDTH)
    return (jax.nn.sigmoid(og) * o) @ w_out


def _peer_ffn(x, w_q, sub_k1, sub_k2, u_tab, v_tab):
    B, S, Dm = x.shape
    K = PEER_TOPK
    xt = x.reshape((B * S) // PEER_TOKBLOCK, PEER_TOKBLOCK, Dm)

    def block(xb):
        T = xb.shape[0]
        q = (xb @ w_q).reshape(T, PEER_HEADS, 2, PEER_DHALF)
        s1 = jnp.einsum('thd,hnd->thn', q[:, :, 0], sub_k1)
        s2 = jnp.einsum('thd,hnd->thn', q[:, :, 1], sub_k2)
        v1, i1 = lax.top_k(s1, K)
        v2, i2 = lax.top_k(s2, K)
        cand = (v1[..., :, None] + v2[..., None, :]).reshape(T, PEER_HEADS, K * K)
        sc, ci = lax.top_k(cand, K)
        e1 = jnp.take_along_axis(i1, ci // K, axis=-1)
        e2 = jnp.take_along_axis(i2, ci % K, axis=-1)
        eid = e1 * PEER_NKEYS + e2
        gate = jax.nn.softmax(sc, axis=-1)
        act = jax.nn.gelu(jnp.einsum('td,thkd->thk', xb, u_tab[eid]), approximate=False)
        return jnp.einsum('thk,thkd->td', gate * act, v_tab[eid])

    return lax.map(block, xt).reshape(B, S, Dm)


PEER_EXPERTS = PEER_NKEYS * PEER_NKEYS
PEER_SELECT_TILE = 256
PEER_TOK_TILE = 512
PEER_EXPERT_CHUNK = 1024
PEER_E1_GROUP = 4
PEER_E2_SUB = 32
PEER_A_PIECES = 4
PEER_C_PIECES = 1
LANES = 128
MXU_DIM = 256
INV_SQRT2 = 0.7071067811865476
_NT = (((1,), (1,)), ((), ()))


def _topk_desc(work, k):
    vals = []
    for _ in range(k):
        m = jnp.max(work, axis=0, keepdims=True)
        vals.append(m)
        work = jnp.where(work >= m, -jnp.inf, work)
    return vals


def _peer_select_kernel(x_ref, wqt_ref, k1_ref, k2_ref,
                        phi_ref, a1_ref, a2_ref, qt_ref):
    f32, bf16 = jnp.float32, jnp.bfloat16
    K = PEER_TOPK
    qt_ref[...] = lax.dot_general(wqt_ref[...], x_ref[...], _NT, preferred_element_type=f32)
    n_groups = x_ref.shape[0] // LANES

    def head(h, carry):
        base = pl.multiple_of(h * PEER_DKEY, PEER_DKEY)
        q1 = qt_ref[pl.ds(base, PEER_DHALF), :].astype(bf16)
        q2 = qt_ref[pl.ds(base + PEER_DHALF, PEER_DHALF), :].astype(bf16)
        s1 = jnp.dot(k1_ref[h], q1, preferred_element_type=f32)
        s2 = jnp.dot(k2_ref[h], q2, preferred_element_type=f32)
        for c in range(n_groups):
            sl = slice(c * LANES, (c + 1) * LANES)
            s1c, s2c = s1[:, sl], s2[:, sl]
            v1 = _topk_desc(s1c, K)
            v2 = _topk_desc(s2c, K)
            rows = [v1[a] + v2[b] for a in range(K) for b in range(K) if (a + 1) * (b + 1) <= K + 1]
            rows += [jnp.full_like(v1[0], -jnp.inf)] * (-len(rows) % 8)
            cand = jnp.concatenate(rows, axis=0)
            vc = _topk_desc(cand, K + 1)
            cmax = vc[0]
            z = jnp.sum(jnp.where(cand >= vc[K - 1], jnp.exp(cand - cmax), 0.0), axis=0, keepdims=True)
            cut = 0.5 * (vc[K - 1] + vc[K])
            s1m = jnp.where(s1c >= v1[K - 1], s1c, -jnp.inf)
            s2m = jnp.where(s2c >= v2[K - 1], s2c, -jnp.inf)
            inv_z = 1.0 / z
            phi_ref[h, :, sl] = jnp.exp(cut - s1m - v2[0]) * inv_z
            a1_ref[h, :, sl] = jnp.exp(s1m - v1[0])
            a2_ref[h, :, sl] = jnp.exp(s2m - v2[0]) * inv_z
        return carry

    lax.fori_loop(0, PEER_HEADS, head, 0)


def _peer_select(xb, wqt, k1, k2):
    n, d = xb.shape
    T = PEER_SELECT_TILE
    H, NK = PEER_HEADS, PEER_NKEYS
    full = lambda shape: pl.BlockSpec(shape, lambda i: (0,) * len(shape))
    tok3 = lambda rows: pl.BlockSpec((H, rows, T), lambda i: (0, 0, i))
    f32 = jnp.float32
    return pl.pallas_call(
        _peer_select_kernel,
        grid=(n // T,),
        in_specs=[pl.BlockSpec((T, d), lambda i: (i, 0)), full(wqt.shape), full(k1.shape), full(k2.shape)],
        out_specs=[tok3(NK)] * 3,
        out_shape=[jax.ShapeDtypeStruct((H, NK, n), f32)] * 3,
        scratch_shapes=[pltpu.VMEM((H * PEER_DKEY, T), f32)],
        name="peer_select",
    )(xb, wqt, k1, k2)


def _peer_expert_kernel(xt_ref, u_ref, v_ref, phi_ref, a1_ref, a2_ref,
                        o_ref, acc_ref, h0_ref, h1_ref, w0_ref, w1_ref):
    f32, bf16 = jnp.float32, jnp.bfloat16
    s = pl.program_id(0)
    n_chunks = PEER_EXPERTS // PEER_EXPERT_CHUNK
    c_chunk = jnp.maximum(s - 2, 0) % n_chunks

    @pl.when(s == 0)
    def _():
        for ref in (h0_ref, h1_ref, w0_ref, w1_ref):
            ref[...] = jnp.zeros_like(ref)

    @pl.when(c_chunk == 0)
    def _():
        acc_ref[...] = jnp.zeros_like(acc_ref)

    T = xt_ref.shape[1]
    NK, G, SB = PEER_NKEYS, PEER_E1_GROUP, PEER_E2_SUB
    CE = PEER_EXPERT_CHUNK
    PA = PEER_A_PIECES

    def step(ha_ref, hb_ref, wb_ref, wc_ref):
        def stage_a(k):
            rows = CE // PA
            u_blk = pltpu.bitcast(u_ref[k * rows // 2:(k + 1) * rows // 2, :], bf16)
            ha_ref[k * rows:(k + 1) * rows, :] = jnp.dot(u_blk, xt_ref[...], preferred_element_type=f32)

        def stage_b(q, c, sb):
            ts = slice(c * LANES, (c + 1) * LANES)
            e2s = slice(sb * SB, (sb + 1) * SB)
            g = [None] * G
            for hd in range(PEER_HEADS):
                a2 = a2_ref[hd, e2s, ts]
                for e in range(G):
                    r = q * G + e
                    term = jnp.where(a2 > phi_ref[hd, r:r + 1, ts], a1_ref[hd, r:r + 1, ts] * a2, 0.0)
                    g[e] = term if g[e] is None else g[e] + term
            for e in range(G):
                lo = (q * G + e) * NK + sb * SB
                hh = hb_ref[lo:lo + SB, ts]
                wb_ref[lo:lo + SB, ts] = (g[e] * (0.5 * hh * (1.0 + lax.erf(hh * INV_SQRT2)))).astype(bf16)

        b_blocks = [(q, c, sb) for q in range(CE // (G * NK)) for c in range(T // LANES)
                    for sb in range(NK // SB)]
        sa = len(b_blocks) // PA
        for t, blk in enumerate(b_blocks):
            stage_b(*blk)
            if t % sa == 0:
                stage_a(t // sa)
            if t == len(b_blocks) // 2:
                acc_ref[...] += lax.dot_general(wc_ref[...], v_ref[...], _TN, preferred_element_type=f32)

    pl.when(s % 2 == 0)(functools.partial(step, h0_ref, h1_ref, w1_ref, w0_ref))
    pl.when(s % 2 == 1)(functools.partial(step, h1_ref, h0_ref, w0_ref, w1_ref))

    @pl.when(c_chunk == n_chunks - 1)
    def _():
        o_ref[...] = acc_ref[...]


def _peer_experts(xtb, ub, vb, phi, a1, a2):
    d, n = xtb.shape
    T, CE = PEER_TOK_TILE, PEER_EXPERT_CHUNK
    H, NK = PEER_HEADS, PEER_NKEYS
    e1_per_chunk = CE // NK
    nc = PEER_EXPERTS // CE
    last = (n // T) * nc - 1
    at = lambda s, lag: jnp.clip(s - lag, 0, last)
    return pl.pallas_call(
        _peer_expert_kernel,
        grid=(last + 3,),
        in_specs=[
            pl.BlockSpec((d, T), lambda s: (0, at(s, 0) // nc)),
            pl.BlockSpec((CE // 2, d), lambda s: (at(s, 0) % nc, 0)),
            pl.BlockSpec((CE, d), lambda s: (at(s, 2) % nc, 0)),
            pl.BlockSpec((H, e1_per_chunk, T), lambda s: (0, at(s, 1) % nc, at(s, 1) // nc)),
            pl.BlockSpec((H, e1_per_chunk, T), lambda s: (0, at(s, 1) % nc, at(s, 1) // nc)),
            pl.BlockSpec((H, NK, T), lambda s: (0, 0, at(s, 1) // nc)),
        ],
        out_specs=pl.BlockSpec((T, d), lambda s: (at(s, 2) // nc, 0)),
        out_shape=jax.ShapeDtypeStruct((n, d), jnp.float32),
        scratch_shapes=[pltpu.VMEM((T, d), jnp.float32),
                        pltpu.VMEM((CE, T), jnp.float32), pltpu.VMEM((CE, T), jnp.float32),
                        pltpu.VMEM((CE, T), jnp.bfloat16), pltpu.VMEM((CE, T), jnp.bfloat16)],
        compiler_params=pltpu.CompilerParams(dimension_semantics=("arbitrary",),
                                             vmem_limit_bytes=VMEM_LIMIT_BYTES),
        name="peer_experts",
    )(xtb, ub, vb, phi, a1, a2)


def _pack_row_pairs(t_bf16):
    rows, cols = t_bf16.shape
    pairs = jnp.swapaxes(t_bf16.reshape(rows // 2, 2, cols), -1, -2)
    return lax.bitcast_convert_type(pairs, jnp.uint32)


def _peer_pallas(x2, w_q, sub_k1, sub_k2, u_tab, v_tab):
    bf16 = jnp.bfloat16
    xb = x2.astype(bf16)
    sel = _peer_select(xb, w_q.T.astype(bf16), sub_k1.astype(bf16), sub_k2.astype(bf16))
    return _peer_experts(xb.T, _pack_row_pairs(u_tab.astype(bf16)), v_tab.astype(bf16), *sel)


RWKV_CHUNK = 64
RWKV_PAIR = LANES // RWKV_HEAD
RWKV_PAIRS = RWKV_HEADS // RWKV_PAIR
_NN = (((1,), (0,)), ((), ()))
_TN = (((0,), (0,)), ((), ()))


def _split_bf16(a):
    hi = a.astype(jnp.bfloat16)
    lo = (a - hi.astype(jnp.float32)).astype(jnp.bfloat16)
    return hi, lo


def _mm2(a, b, dims):
    d = lambda u, v: lax.dot_general(u, v, dims, preferred_element_type=jnp.float32)
    (ah, al), (bh, bl) = _split_bf16(a), _split_bf16(b)
    return d(ah, bh) + d(ah, bl) + d(al, bh)


def _mm_exact_rhs(a, b_exact, dims, terms=2):
    d = lambda u: lax.dot_general(u, b_exact, dims, preferred_element_type=jnp.float32)
    out, rem = None, a
    for _ in range(terms):
        piece = rem.astype(jnp.bfloat16)
        rem = rem - piece.astype(jnp.float32)
        out = d(piece) if out is None else out + d(piece)
    return out


def _mm_exact_lhs(a_exact, b, dims, terms=3):
    d = lambda v: lax.dot_general(a_exact, v, dims, preferred_element_type=jnp.float32)
    out, rem = None, b
    for _ in range(terms):
        piece = rem.astype(jnp.bfloat16)
        rem = rem - piece.astype(jnp.float32)
        out = d(piece) if out is None else out + d(piece)
    return out


def _softplus(z):
    return jnp.maximum(z, 0.0) + jnp.log1p(jnp.exp(-jnp.abs(z)))


def _rwkv_kernel(p_ref, prev_ref, mu_ref, w0_ref, wup_ref, a0_ref, aup_ref, gup_ref,
                 kk_ref, ka_ref, rk_ref, lng_ref, lnb_ref, ones_ref,
                 o_ref, state_ref):
    f32, bf16 = jnp.float32, jnp.bfloat16
    L, W = RWKV_CHUNK, RWKV_WIDTH
    c_idx = pl.program_id(1)

    @pl.when(c_idx == 0)
    def _():
        state_ref[...] = jnp.zeros_like(state_ref)

    p = p_ref[0]
    row = lax.broadcasted_iota(jnp.int32, p.shape, 0)
    prev_row = jnp.where(c_idx == 0, 0.0, prev_ref[0, 7:8, :])
    shifted = jnp.where(row == 0, prev_row, pltpu.roll(p, 1, axis=0))
    p = p + (shifted - p) * mu_ref[...]
    r, k, v = p[:, 0:W], p[:, W:2 * W], p[:, 2 * W:3 * W]
    o = 3 * W
    xw = p[:, o:o + RWKV_W_RANK]
    xa = p[:, o + RWKV_W_RANK:o + RWKV_W_RANK + RWKV_A_RANK]
    xg = p[:, o + RWKV_W_RANK + RWKV_A_RANK:]
    dotd = lambda u, m: jnp.dot(u.astype(bf16), m, preferred_element_type=f32)
    lw = -jnp.exp(-_softplus(-(w0_ref[...] + dotd(jnp.tanh(xw), wup_ref[...]))) - 0.5)
    a = jax.nn.sigmoid(a0_ref[...] + dotd(xa, aup_ref[...]))
    g = dotd(jax.nn.sigmoid(xg), gup_ref[...])
    ones_bd = ones_ref[...]
    head_sum = lambda t: _mm_exact_rhs(t, ones_bd, _NN)
    kk = k * kk_ref[...]
    kk = kk / jnp.maximum(jnp.sqrt(head_sum(kk * kk)), 1e-12)
    k = k * (1.0 + (a - 1.0) * ka_ref[...])

    lane = lax.broadcasted_iota(jnp.int32, (L, LANES), 1)
    trow = lax.broadcasted_iota(jnp.int32, (L, LANES), 0)
    s_in = lane % RWKV_HEAD
    strict, incl = s_in < trow, s_in <= trow
    eye_pair = (s_in == trow).astype(f32)
    m0 = lane < RWKV_HEAD
    tril = (lax.broadcasted_iota(jnp.int32, (L, L), 1)
            <= lax.broadcasted_iota(jnp.int32, (L, L), 0)).astype(bf16)
    r2 = lax.broadcasted_iota(jnp.int32, (LANES, LANES), 0) // RWKV_HEAD
    c2 = lax.broadcasted_iota(jnp.int32, (LANES, LANES), 1) // RWKV_HEAD
    bd_mask = r2 == c2

    def bd(t):
        return jnp.concatenate([jnp.where(m0, t, 0.0), jnp.where(m0, 0.0, t)], axis=0)

    pairs = range(RWKV_PAIRS)
    cut = lambda t: [t[:, pr * LANES:(pr + 1) * LANES] for pr in pairs]
    rp, kp, vp, kkp, ap, lwp = cut(r), cut(k), cut(v), cut(kk), cut(a), cut(lw)
    c = [_mm_exact_lhs(tril, lwp[p], _NN) for p in pairs]
    at = [-kkp[p] * jnp.exp(c[p] - lwp[p]) for p in pairs]
    eni = [jnp.exp(-c[p]) for p in pairs]
    bt = [kkp[p] * ap[p] * eni[p] for p in pairs]
    kt = [kp[p] * eni[p] for p in pairs]
    rt = [rp[p] * jnp.exp(c[p]) for p in pairs]
    g_last = [jnp.exp(c[p][L - 1:L, :]) for p in pairs]
    gram = [_mm2(jnp.concatenate([at[p], rt[p]], axis=0),
                 jnp.concatenate([bd(bt[p]), bd(kt[p])], axis=0), _NT) for p in pairs]
    n_ab = [jnp.where(strict, gram[p][0:L, 0:LANES], 0.0) for p in pairs]
    a_ak = [jnp.where(strict, gram[p][0:L, LANES:], 0.0) for p in pairs]
    a_rb = [jnp.where(incl, gram[p][L:, 0:LANES], 0.0) for p in pairs]
    a_rk = [jnp.where(incl, gram[p][L:, LANES:], 0.0) for p in pairs]
    tinv, m = [eye_pair + n_ab[p] for p in pairs], n_ab
    for _ in range(5):
        m = [_mm2(m[p], bd(m[p]), _NN) for p in pairs]
        tinv = [tinv[p] + _mm2(m[p], bd(tinv[p]), _NN) for p in pairs]
    s0 = [state_ref[p] for p in pairs]
    pq = [_mm2(at[p], s0[p], _NT) + _mm2(a_ak[p], bd(vp[p]), _NN) for p in pairs]
    u = [_mm2(tinv[p], bd(pq[p]), _NN) for p in pairs]
    ys = [_mm2(rt[p], s0[p], _NT)
          + _mm2(jnp.concatenate([a_rb[p], a_rk[p]], axis=1),
                 jnp.concatenate([bd(u[p]), bd(vp[p])], axis=0), _NN) for p in pairs]
    for p in pairs:
        upd = _mm2(jnp.concatenate([u[p], vp[p]], axis=0),
                   jnp.concatenate([bt[p], kt[p]], axis=0), _TN)
        state_ref[p] = (s0[p] + jnp.where(bd_mask, upd, 0.0)) * g_last[p]
    y = jnp.concatenate(ys, axis=1)

    inv_n = 1.0 / RWKV_HEAD
    mean = head_sum(y) * inv_n
    yc = y - mean
    var = head_sum(yc * yc) * inv_n
    y = yc * lax.rsqrt(var + RWKV_GN_EPS) * lng_ref[...] + lnb_ref[...]
    y = y + head_sum(r * k * rk_ref[...]) * v
    o_ref[0] = y * g


def _rwkv_pallas(p_r, mu, w0, w_up, a0, a_up, g_up, k_k, k_a, r_k, ln_g, ln_b):
    B, S, C = p_r.shape
    L, W = RWKV_CHUNK, RWKV_WIDTH
    bf16 = jnp.bfloat16
    vecw = lambda t: t.reshape(1, W)
    ones_bd = jnp.kron(jnp.eye(RWKV_HEADS, dtype=bf16), jnp.ones((RWKV_HEAD, RWKV_HEAD), bf16))
    full = lambda a: pl.BlockSpec(a.shape, lambda b, c: (0,) * a.ndim)
    args = [mu.reshape(1, C), vecw(w0), w_up.astype(bf16), vecw(a0), a_up.astype(bf16),
            g_up.astype(bf16), vecw(k_k), vecw(k_a), vecw(r_k), vecw(ln_g), vecw(ln_b), ones_bd]
    return pl.pallas_call(
        _rwkv_kernel,
        grid=(B, S // L),
        in_specs=[pl.BlockSpec((1, L, C), lambda b, c: (b, c, 0)),
                  pl.BlockSpec((1, 8, C), lambda b, c: (b, jnp.maximum(c * (L // 8) - 1, 0), 0))]
                 + [full(a) for a in args],
        out_specs=pl.BlockSpec((1, L, W), lambda b, c: (b, c, 0)),
        out_shape=jax.ShapeDtypeStruct((B, S, W), jnp.float32),
        scratch_shapes=[pltpu.VMEM((RWKV_PAIRS, LANES, LANES), jnp.float32)],
        compiler_params=pltpu.CompilerParams(dimension_semantics=("arbitrary", "arbitrary")),
        name="rwkv7",
    )(p_r, p_r, *args)


MLSTM_CONV = 4
MLSTM_GATE_LANES = LANES


def _mlstm_kernel(p_ref, prev_ref, cw_ref, cb_ref, igb_ref, fgb_ref, hng_ref, hnb_ref,
                  o_ref, c_ref, n_ref, m_ref):
    f32, bf16 = jnp.float32, jnp.bfloat16
    L, W, H, D = MLSTM_CHUNK, MLSTM_WIDTH, MLSTM_HEADS, MLSTM_HEAD
    c_idx = pl.program_id(1)

    @pl.when(c_idx == 0)
    def _():
        c_ref[...] = jnp.zeros_like(c_ref)
        n_ref[...] = jnp.zeros_like(n_ref)
        m_ref[...] = jnp.zeros_like(m_ref)

    p = p_ref[0]
    z = p[:, 0:2 * W]
    prev = jnp.where(c_idx == 0, 0.0, prev_ref[0, :, 0:2 * W])
    ext = jnp.concatenate([prev, z], axis=0)
    conv = cb_ref[...] + cw_ref[MLSTM_CONV - 1:MLSTM_CONV, :] * z
    for j in range(MLSTM_CONV - 1):
        d = MLSTM_CONV - 1 - j
        conv = conv + cw_ref[j:j + 1, :] * ext[8 - d:8 - d + L, :]
    qk = conv * jax.nn.sigmoid(conv)
    q_all, k_all = qk[:, 0:W] * (D ** -0.5), qk[:, W:2 * W]
    v_all, o_all = p[:, 2 * W:3 * W], p[:, 3 * W:4 * W]
    gates = p[:, 4 * W:]
    igl = gates + igb_ref[...]
    lfl = -_softplus(-(gates + fgb_ref[...]))
    tril = (lax.broadcasted_iota(jnp.int32, (L, L), 1)
            <= lax.broadcasted_iota(jnp.int32, (L, L), 0))
    bcum = _mm_exact_lhs(tril.astype(bf16), lfl, _NN)
    e_all = igl - pltpu.roll(bcum, LANES - H, axis=1)
    lane = lax.broadcasted_iota(jnp.int32, (L, LANES), 1)
    outs = []
    for h in range(H):
        hs = slice(h * D, (h + 1) * D)
        q, k, v = q_all[:, hs], k_all[:, hs], v_all[:, hs]
        b_col = bcum[:, H + h:H + h + 1]
        ig_col = igl[:, h:h + 1]
        onehot = (lane == h).astype(bf16)
        e_row = _mm_exact_lhs(onehot, e_all, _NT)
        m_prev = m_ref[h, 0:1, 0:1]
        dmat = jnp.where(tril, b_col + e_row, -jnp.inf)
        inter = b_col + m_prev
        m_t = jnp.maximum(inter, jnp.max(dmat, axis=-1, keepdims=True))
        weights = jnp.exp(dmat - m_t)
        sc = lax.dot_general(q.astype(bf16), k.astype(bf16), _NT, preferred_element_type=f32) * weights
        carry_in = jnp.exp(inter - m_t)
        c_prev, n_prev = c_ref[h], n_ref[h, 0:1, :]
        num = (jnp.dot(sc.astype(bf16), v.astype(bf16), preferred_element_type=f32)
               + carry_in * jnp.dot(q.astype(bf16), c_prev.astype(bf16), preferred_element_type=f32))
        den = (jnp.sum(sc, axis=-1, keepdims=True)
               + carry_in * jnp.sum(q * n_prev, axis=-1, keepdims=True))
        hval = num / jnp.maximum(jnp.abs(den), jnp.exp(-m_t))
        b_last = b_col[L - 1:L, :]
        gs = b_last - b_col + ig_col
        m_new = jnp.maximum(b_last + m_prev, jnp.max(gs, axis=0, keepdims=True))
        ws = jnp.exp(gs - m_new)
        keep = jnp.exp(b_last + m_prev - m_new)
        wk = ws * k
        c_ref[h] = keep * c_prev + lax.dot_general(wk.astype(bf16), v.astype(bf16), _TN,
                                                   preferred_element_type=f32)
        n_ref[h] = jnp.broadcast_to(keep * n_prev + jnp.sum(wk, axis=0, keepdims=True), (8, D))
        m_ref[h] = jnp.broadcast_to(m_new, (8, LANES))
        mu = jnp.mean(hval, axis=-1, keepdims=True)
        hc = hval - mu
        var = jnp.mean(hc * hc, axis=-1, keepdims=True)
        hn = hc * lax.rsqrt(var + LN_EPS) * hng_ref[:, hs] + hnb_ref[:, hs]
        outs.append(jax.nn.sigmoid(o_all[:, hs]) * hn)
    o_ref[0] = jnp.concatenate(outs, axis=1)


def _mlstm_pallas(p_m, conv_w, conv_b, ig_b, fg_b, hn_g, hn_b):
    B, S, C = p_m.shape
    L, W, H, D = MLSTM_CHUNK, MLSTM_WIDTH, MLSTM_HEADS, MLSTM_HEAD
    pad = lambda t, off: jnp.zeros((1, LANES), jnp.float32).at[0, off:off + H].set(t)
    args = [conv_w, conv_b.reshape(1, 2 * W), pad(ig_b, 0), pad(fg_b, H),
            hn_g.reshape(1, W), hn_b.reshape(1, W)]
    full = lambda a: pl.BlockSpec(a.shape, lambda b, c: (0,) * a.ndim)
    return pl.pallas_call(
        _mlstm_kernel,
        grid=(B, S // L),
        in_specs=[pl.BlockSpec((1, L, C), lambda b, c: (b, c, 0)),
                  pl.BlockSpec((1, 8, C), lambda b, c: (b, jnp.maximum(c * (L // 8) - 1, 0), 0))]
                 + [full(a) for a in args],
        out_specs=pl.BlockSpec((1, L, W), lambda b, c: (b, c, 0)),
        out_shape=jax.ShapeDtypeStruct((B, S, W), jnp.float32),
        scratch_shapes=[pltpu.VMEM((H, D, D), jnp.float32),
                        pltpu.VMEM((H, 8, D), jnp.float32),
                        pltpu.VMEM((H, 8, LANES), jnp.float32)],
        compiler_params=pltpu.CompilerParams(dimension_semantics=("arbitrary", "arbitrary")),
        name="mlstm",
    )(p_m, p_m, *args)


FOX_TILE = 256
FOX_KEY_GROUP = 4
FOX_HEADS_PER_STEP = 8
FOX_AUG = LANES
FOX_GATE_LANES = LANES


def _fox_prep_kernel(p_ref, qg_ref, kg_ref, fb_ref, ones_ref, qa_ref, ka_ref, vt_ref, carry_ref):
    f32, bf16 = jnp.float32, jnp.bfloat16
    T, W, H, D = FOX_TILE, FOX_WIDTH, FOX_HEADS, FOX_HEAD
    i = pl.program_id(1)

    @pl.when(i == 0)
    def _():
        carry_ref[...] = jnp.zeros_like(carry_ref)

    p = p_ref[0]
    ones_bd = ones_ref[...]
    ms = lambda t: _mm_exact_rhs(t * t, ones_bd, _NN) * (1.0 / D)
    q = p[:, 0:W]
    q = q * lax.rsqrt(ms(q) + 1e-6) * qg_ref[...]
    k = p[:, W:2 * W]
    k = k * lax.rsqrt(ms(k) + 1e-6) * kg_ref[...] * (D ** -0.5)
    vt = p[:, 2 * W:3 * W].T.astype(bf16)
    logf = -_softplus(-(p[:, 4 * W:] + fb_ref[...]))
    tril = (lax.broadcasted_iota(jnp.int32, (T, T), 1)
            <= lax.broadcasted_iota(jnp.int32, (T, T), 0)).astype(bf16)
    c = _mm_exact_lhs(tril, logf, _NN) + carry_ref[0:1, :]
    carry_ref[...] = jnp.broadcast_to(c[T - 1:T, :], carry_ref.shape)
    c1 = c.astype(bf16).astype(f32)
    c2 = (c - c1).astype(bf16).astype(f32)
    c3 = (c - c1 - c2).astype(bf16).astype(f32)
    lane = lax.broadcasted_iota(jnp.int32, (T, D), 1)
    for h in range(H):
        hs = slice(h * D, (h + 1) * D)
        c1h, c2h, c3h = c1[:, h:h + 1], c2[:, h:h + 1], c3[:, h:h + 1]
        pieces = jnp.where(lane % 3 == 0, c1h, jnp.where(lane % 3 == 1, c2h, c3h))
        q_aug = jnp.where(lane < 3, 1.0, jnp.where(lane < 6, pieces, 0.0))
        k_aug = jnp.where(lane < 3, -pieces, jnp.where(lane < 6, 1.0, 0.0))
        qa_ref[0, h] = jnp.concatenate([q[:, hs], q_aug], axis=1).astype(bf16)
        ka_ref[0, h] = jnp.concatenate([k[:, hs], k_aug], axis=1).astype(bf16)
        vt_ref[0, h, 0] = vt[hs, :]


def _fox_prep(p_f, qn_g, kn_g, f_b):
    B, S, C = p_f.shape
    T, W, H, D = FOX_TILE, FOX_WIDTH, FOX_HEADS, FOX_HEAD
    bf16 = jnp.bfloat16
    ones_bd = jnp.kron(jnp.eye(H, dtype=bf16), jnp.ones((D, D), bf16))
    fb = jnp.zeros((1, FOX_GATE_LANES), jnp.float32).at[0, :H].set(f_b)
    args = [qn_g.reshape(1, W), kn_g.reshape(1, W), fb, ones_bd]
    full = lambda a: pl.BlockSpec(a.shape, lambda b, i: (0,) * a.ndim)
    return pl.pallas_call(
        _fox_prep_kernel,
        grid=(B, S // T),
        in_specs=[pl.BlockSpec((1, T, C), lambda b, i: (b, i, 0))] + [full(a) for a in args],
        out_specs=[pl.BlockSpec((1, H, T, FOX_AUG), lambda b, i: (b, 0, i, 0)),
                   pl.BlockSpec((1, H, T, FOX_AUG), lambda b, i: (b, 0, i, 0)),
                   pl.BlockSpec((1, H, 1, D, T), lambda b, i: (b, 0, i, 0, 0))],
        out_shape=[jax.ShapeDtypeStruct((B, H, S, FOX_AUG), bf16),
                   jax.ShapeDtypeStruct((B, H, S, FOX_AUG), bf16),
                   jax.ShapeDtypeStruct((B, H, S // T, D, T), bf16)],
        scratch_shapes=[pltpu.VMEM((8, FOX_GATE_LANES), jnp.float32)],
        compiler_params=pltpu.CompilerParams(dimension_semantics=("arbitrary", "arbitrary"),
                                             vmem_limit_bytes=VMEM_LIMIT_BYTES),
        name="fox_prep",
    )(p_f, *args)


def _fox_attn_kernel(qa_ref, ka_ref, vt_ref, og_ref, o_ref, m_ref, l_ref, acc_ref):
    f32, bf16 = jnp.float32, jnp.bfloat16
    T, D = FOX_TILE, FOX_HEAD
    i = pl.program_id(2)
    heads = range(FOX_HEADS_PER_STEP)
    n_blocks = ka_ref.shape[2] // T
    GK = FOX_KEY_GROUP
    rel = (lax.broadcasted_iota(jnp.int32, (GK * T, T), 0)
           - lax.broadcasted_iota(jnp.int32, (GK * T, T), 1))
    qas = [qa_ref[0, j] for j in heads]

    def key_group(g, first):
        visible = rel <= (i - g * GK) * T

        def scores(j):
            ka = ka_ref[0, j, g * GK * T:(g + 1) * GK * T, :]
            return lax.dot_general(ka, qas[j], _NT, preferred_element_type=f32)

        nxt = scores(0)
        for j in heads:
            s = nxt
            if j + 1 < len(heads):
                nxt = scores(j + 1)
            s = jnp.where(visible, s, -jnp.inf)
            m_g = jnp.max(s, axis=0, keepdims=True)
            m_new = m_g if first else jnp.maximum(m_ref[j], m_g)
            pr = jnp.exp(s - m_new)
            l_g = jnp.sum(pr, axis=0, keepdims=True)
            vt = jnp.concatenate([vt_ref[0, j, g * GK + b] for b in range(GK)], axis=1)
            pv = jnp.dot(vt, pr.astype(bf16), preferred_element_type=f32)
            if first:
                l_ref[j], acc_ref[j] = l_g, pv
            else:
                alpha = jnp.exp(m_ref[j] - m_new)
                l_ref[j] = l_ref[j] * alpha + l_g
                acc_ref[j] = acc_ref[j] * alpha + pv
            m_ref[j] = m_new

    key_group(0, True)
    for g in range(1, n_blocks // GK):
        pl.when(i >= g * GK)(functools.partial(key_group, g, False))
    outs = [(acc_ref[j] / l_ref[j]).T for j in heads]
    o_ref[0] = jax.nn.sigmoid(og_ref[0]) * jnp.concatenate(outs, axis=1)


def _fox_attn(qa, ka, vt, p_f):
    B, H, S, A = qa.shape
    T, D = FOX_TILE, FOX_HEAD
    hp = FOX_HEADS_PER_STEP
    width = hp * D
    og_block0 = 3 * FOX_WIDTH // width
    return pl.pallas_call(
        _fox_attn_kernel,
        grid=(B, H // hp, S // T),
        in_specs=[pl.BlockSpec((1, hp, T, A), lambda b, g, i: (b, g, i, 0)),
                  pl.BlockSpec((1, hp, S, A), lambda b, g, i: (b, g, 0, 0)),
                  pl.BlockSpec((1, hp, S // T, D, T), lambda b, g, i: (b, g, 0, 0, 0)),
                  pl.BlockSpec((1, T, width), lambda b, g, i: (b, i, og_block0 + g))],
        out_specs=pl.BlockSpec((1, T, width), lambda b, g, i: (b, i, g)),
        out_shape=jax.ShapeDtypeStruct((B, S, FOX_WIDTH), jnp.float32),
        scratch_shapes=[pltpu.VMEM((hp, 1, T), jnp.float32), pltpu.VMEM((hp, 1, T), jnp.float32),
                        pltpu.VMEM((hp, D, T), jnp.float32)],
        name="fox_attn",
    )(qa, ka, vt, p_f)


def kernel(x, l0_w_in, l0_rwkv_mu, l0_rwkv_w0, l0_rwkv_w_up, l0_rwkv_a0, l0_rwkv_a_up,
           l0_rwkv_g_up, l0_rwkv_k_k, l0_rwkv_k_a, l0_rwkv_r_k, l0_rwkv_ln_g, l0_rwkv_ln_b,
           l0_mlstm_conv_w, l0_mlstm_conv_b, l0_mlstm_ig_b, l0_mlstm_fg_b,
           l0_mlstm_hn_g, l0_mlstm_hn_b, l0_w_out, l0_ln1_g, l0_ln1_b,
           l0_peer_wq, l0_peer_k1, l0_peer_k2, l0_peer_u, l0_peer_v, l0_ln2_g, l0_ln2_b,
           l1_w_in, l1_fox_qn_g, l1_fox_kn_g, l1_fox_f_b, l1_w_out, l1_ln1_g, l1_ln1_b,
           l1_peer_wq, l1_peer_k1, l1_peer_k2, l1_peer_u, l1_peer_v, l1_ln2_g, l1_ln2_b):
    B, S, D = x.shape
    n = B * S
    bf16 = jnp.bfloat16
    x2 = x.reshape(n, D)

    def pad_lanes(w):
        return jnp.pad(w, ((0, 0), (0, LANES - w.shape[1])))

    m_main = 4 * MLSTM_WIDTH
    w0 = jnp.concatenate([l0_w_in[:, :RWKV_COLS + m_main],
                          pad_lanes(l0_w_in[:, RWKV_COLS + m_main:])], axis=1).astype(bf16)
    p_r, p_m = _proj(x2, w0, (RWKV_COLS, m_main + LANES))
    y_a = _rwkv_pallas(p_r.reshape(B, S, -1), l0_rwkv_mu, l0_rwkv_w0, l0_rwkv_w_up, l0_rwkv_a0,
                       l0_rwkv_a_up, l0_rwkv_g_up, l0_rwkv_k_k, l0_rwkv_k_a, l0_rwkv_r_k,
                       l0_rwkv_ln_g, l0_rwkv_ln_b)
    y_b = _mlstm_pallas(p_m.reshape(B, S, -1), l0_mlstm_conv_w, l0_mlstm_conv_b, l0_mlstm_ig_b,
                        l0_mlstm_fg_b, l0_mlstm_hn_g, l0_mlstm_hn_b)
    x2 = _out_proj_ln([y_a.reshape(n, -1), y_b.reshape(n, -1)], l0_w_out.astype(bf16), x2,
                      l0_ln1_g, l0_ln1_b)
    y = _peer_pallas(x2, l0_peer_wq, l0_peer_k1, l0_peer_k2, l0_peer_u, l0_peer_v)
    x2 = _resid_ln(x2, y, l0_ln2_g, l0_ln2_b)

    f_main = 4 * FOX_WIDTH
    w1 = jnp.concatenate([l1_w_in[:, :f_main], pad_lanes(l1_w_in[:, f_main:])], axis=1).astype(bf16)
    (p_f,) = _proj(x2, w1, (f_main + LANES,))
    p_f = p_f.reshape(B, S, -1)
    qa, ka, vt = _fox_prep(p_f, l1_fox_qn_g, l1_fox_kn_g, l1_fox_f_b)
    o = _fox_attn(qa, ka, vt, p_f)
    x2 = _out_proj_ln([o.reshape(n, -1)], l1_w_out.astype(bf16), x2, l1_ln1_g, l1_ln1_b)
    y = _peer_pallas(x2, l1_peer_wq, l1_peer_k1, l1_peer_k2, l1_peer_u, l1_peer_v)
    x2 = _resid_ln(x2, y, l1_ln2_g, l1_ln2_b)
    return x2.reshape(B, S, D)
```

```python
import functools

import jax
import jax.numpy as jnp
from jax import lax
from jax.experimental import pallas as pl
from jax.experimental.pallas import tpu as pltpu

D_MODEL = 1024
DEPTH = 2
DN_ALPHA = (2.0 * DEPTH) ** 0.25
LN_EPS = 1e-5

RWKV_WIDTH = D_MODEL // 2
RWKV_HEAD = 64
RWKV_HEADS = RWKV_WIDTH // RWKV_HEAD
RWKV_W_RANK = 64
RWKV_A_RANK = 64
RWKV_G_RANK = 128
RWKV_GN_EPS = 1e-5 * RWKV_HEAD
RWKV_COLS = 3 * RWKV_WIDTH + RWKV_W_RANK + RWKV_A_RANK + RWKV_G_RANK

MLSTM_WIDTH = D_MODEL // 2
MLSTM_HEAD = 128
MLSTM_HEADS = MLSTM_WIDTH // MLSTM_HEAD
MLSTM_CHUNK = 64

FOX_HEAD = 64
FOX_HEADS = D_MODEL // FOX_HEAD
FOX_WIDTH = FOX_HEADS * FOX_HEAD
FOX_QBLOCK = 128

PEER_HEADS = 8
PEER_NKEYS = 128
PEER_TOPK = 16
PEER_DKEY = 256
PEER_DHALF = PEER_DKEY // 2
PEER_TOKBLOCK = 128


def _split_cols(p, sizes):
    out, start = [], 0
    for s in sizes:
        out.append(p[..., start:start + s])
        start += s
    return out


def _resid_ln_kernel(x_ref, y_ref, g_ref, b_ref, o_ref):
    z = DN_ALPHA * x_ref[...] + y_ref[...]
    mu = jnp.mean(z, axis=-1, keepdims=True)
    zc = z - mu
    var = jnp.mean(zc * zc, axis=-1, keepdims=True)
    o_ref[...] = zc * lax.rsqrt(var + LN_EPS) * g_ref[...] + b_ref[...]


def _resid_ln(x2, y2, g, b, tm=512):
    n, d = x2.shape
    row = pl.BlockSpec((tm, d), lambda i: (i, 0))
    vec = pl.BlockSpec((1, d), lambda i: (0, 0))
    return pl.pallas_call(
        _resid_ln_kernel,
        grid=(n // tm,),
        in_specs=[row, row, vec, vec],
        out_specs=row,
        out_shape=jax.ShapeDtypeStruct((n, d), jnp.float32),
        name="resid_ln",
    )(x2, y2, g.reshape(1, d), b.reshape(1, d))


VMEM_LIMIT_BYTES = 56 * 1024 * 1024


def _proj_kernel(x_ref, w_ref, *o_refs):
    xb = x_ref[...].astype(jnp.bfloat16)
    start = 0
    for o_ref in o_refs:
        width = o_ref.shape[1]
        o_ref[...] = jnp.dot(xb, w_ref[:, start:start + width], preferred_element_type=jnp.float32)
        start += width


def _proj(x2, w_bf16, widths, tm=256):
    n, d = x2.shape
    assert sum(widths) == w_bf16.shape[1]
    return pl.pallas_call(
        _proj_kernel,
        grid=(n // tm,),
        in_specs=[pl.BlockSpec((tm, d), lambda i: (i, 0)),
                  pl.BlockSpec(w_bf16.shape, lambda i: (0, 0))],
        out_specs=[pl.BlockSpec((tm, w), lambda i: (i, 0)) for w in widths],
        out_shape=[jax.ShapeDtypeStruct((n, w), jnp.float32) for w in widths],
        compiler_params=pltpu.CompilerParams(vmem_limit_bytes=VMEM_LIMIT_BYTES),
        name="in_proj",
    )(x2, w_bf16)


def _out_proj_ln_kernel(*refs, n_parts):
    y_refs, (w_ref, x_ref, g_ref, b_ref, o_ref, ot_ref) = refs[:n_parts], refs[n_parts:]
    acc, start = None, 0
    for y_ref in y_refs:
        width = y_ref.shape[1]
        part = jnp.dot(y_ref[...].astype(jnp.bfloat16), w_ref[start:start + width, :],
                       preferred_element_type=jnp.float32)
        acc = part if acc is None else acc + part
        start += width
    z = DN_ALPHA * x_ref[...] + acc
    mu = jnp.mean(z, axis=-1, keepdims=True)
    zc = z - mu
    var = jnp.mean(zc * zc, axis=-1, keepdims=True)
    out = zc * lax.rsqrt(var + LN_EPS) * g_ref[...] + b_ref[...]
    o_ref[...] = out
    ot_ref[...] = out.T.astype(jnp.bfloat16)


def _out_proj_ln(ys, w_bf16, x2, g, b, tm=256):
    n, d = x2.shape
    row = lambda width: pl.BlockSpec((tm, width), lambda i: (i, 0))
    vec = pl.BlockSpec((1, d), lambda i: (0, 0))
    return pl.pallas_call(
        functools.partial(_out_proj_ln_kernel, n_parts=len(ys)),
        grid=(n // tm,),
        in_specs=[row(y.shape[1]) for y in ys]
                 + [pl.BlockSpec(w_bf16.shape, lambda i: (0, 0)), row(d), vec, vec],
        out_specs=[row(d), pl.BlockSpec((d, tm), lambda i: (0, i))],
        out_shape=[jax.ShapeDtypeStruct((n, d), jnp.float32), jax.ShapeDtypeStruct((d, n), jnp.bfloat16)],
        name="out_proj_ln",
    )(*ys, w_bf16, x2, g.reshape(1, d), b.reshape(1, d))


def _head_norm(y, g, b, eps):
    mu = jnp.mean(y, -1, keepdims=True)
    var = jnp.mean(jnp.square(y - mu), -1, keepdims=True)
    return (y - mu) * lax.rsqrt(var + eps) * g + b


def _rms_norm(y, g):
    return y * lax.rsqrt(jnp.mean(y * y, -1, keepdims=True) + 1e-6) * g


def _token_shift(z):
    return jnp.pad(z, ((0, 0), (1, 0), (0, 0)))[:, :-1]


def _causal_conv(z, w, b):
    c = z.shape[-1]
    out = lax.conv_general_dilated(z, w[:, None, :], window_strides=(1,),
                                   padding=((w.shape[0] - 1, 0),),
                                   dimension_numbers=('NWC', 'WIO', 'NWC'),
                                   feature_group_count=c)
    return out + b


def _rwkv7_mix(p, mu, w0, w_up, a0, a_up, g_up, k_k, k_a, r_k, ln_g, ln_b):
    B, S, _ = p.shape
    H, N = RWKV_HEADS, RWKV_HEAD
    p = p + (_token_shift(p) - p) * mu
    r, k, v, xw, xa, xg = _split_cols(p, (RWKV_WIDTH, RWKV_WIDTH, RWKV_WIDTH,
                                          RWKV_W_RANK, RWKV_A_RANK, RWKV_G_RANK))
    log_w = -jnp.exp(-jax.nn.softplus(-(w0 + jnp.tanh(xw) @ w_up)) - 0.5)
    a = jax.nn.sigmoid(a0 + xa @ a_up)
    g = jax.nn.sigmoid(xg) @ g_up
    heads = lambda t: t.reshape(B, S, H, N)
    kk = heads(k * k_k)
    kk = kk / jnp.maximum(jnp.sqrt(jnp.sum(kk * kk, -1, keepdims=True)), 1e-12)
    k = k * (1.0 + (a - 1.0) * k_a)
    r_h, k_h, v_h, a_h = heads(r), heads(k), heads(v), heads(a)
    w_h = jnp.exp(heads(log_w))

    def step(state, inp):
        r_t, w_t, k_t, v_t, kk_t, a_t = inp
        s_kk = jnp.einsum('bhvk,bhk->bhv', state, kk_t)
        state = (state * w_t[:, :, None, :]
                 - s_kk[..., None] * (kk_t * a_t)[:, :, None, :]
                 + v_t[..., None] * k_t[:, :, None, :])
        return state, jnp.einsum('bhvk,bhk->bhv', state, r_t)

    seq_first = lambda t: jnp.moveaxis(t, 1, 0)
    state0 = jnp.zeros((B, H, N, N), jnp.float32)
    _, y = lax.scan(step, state0, (seq_first(r_h), seq_first(w_h), seq_first(k_h),
                                   seq_first(v_h), seq_first(kk), seq_first(a_h)))
    y = jnp.moveaxis(y, 0, 1)
    y = _head_norm(y, ln_g, ln_b, RWKV_GN_EPS)
    y = y + jnp.sum(r_h * k_h * r_k, -1, keepdims=True) * v_h
    return y.reshape(B, S, RWKV_WIDTH) * g


def _mlstm_chunkwise(q, k, v, ig, lf):
    B, S, H, D = q.shape
    L = MLSTM_CHUNK
    NC = S // L

    def to_chunks(t):
        t = t.reshape((B, NC, L, H) + t.shape[3:])
        return jnp.moveaxis(t, (1, 3), (0, 2))

    causal = jnp.tril(jnp.ones((L, L), dtype=bool))

    def body(carry, inp):
        C, n, m = carry
        qc, kc, vc, igc, lfc = inp
        b = jnp.cumsum(lfc, axis=-1)
        dmat = jnp.where(causal, b[..., :, None] - b[..., None, :] + igc[..., None, :], -jnp.inf)
        inter = b + m[..., None]
        m_t = jnp.maximum(inter, jnp.max(dmat, -1))
        weights = jnp.exp(dmat - m_t[..., None])
        sc = jnp.einsum('bhtd,bhsd->bhts', qc, kc) * weights
        carry_in = jnp.exp(inter - m_t)
        num = (jnp.einsum('bhts,bhsd->bhtd', sc, vc)
               + carry_in[..., None] * jnp.einsum('bhtk,bhkv->bhtv', qc, C))
        den = jnp.sum(sc, -1) + carry_in * jnp.einsum('bhtk,bhk->bht', qc, n)
        h = num / jnp.maximum(jnp.abs(den), jnp.exp(-m_t))[..., None]
        b_last = b[..., -1]
        gs = b_last[..., None] - b + igc
        m_new = jnp.maximum(b_last + m, jnp.max(gs, -1))
        ws = jnp.exp(gs - m_new[..., None])
        keep = jnp.exp(b_last + m - m_new)
        C = keep[..., None, None] * C + jnp.einsum('bhs,bhsk,bhsv->bhkv', ws, kc, vc)
        n = keep[..., None] * n + jnp.einsum('bhs,bhsk->bhk', ws, kc)
        return (C, n, m_new), h

    f32 = jnp.float32
    init = (jnp.zeros((B, H, D, D), f32), jnp.zeros((B, H, D), f32), jnp.zeros((B, H), f32))
    _, h = lax.scan(body, init, (to_chunks(q), to_chunks(k), to_chunks(v),
                                 to_chunks(ig), to_chunks(lf)))
    return jnp.moveaxis(h, (0, 2), (1, 3)).reshape(B, S, H, D)


def _mlstm_mix(p, conv_w, conv_b, ig_b, fg_b, hn_g, hn_b):
    B, S, _ = p.shape
    H, N = MLSTM_HEADS, MLSTM_HEAD
    q, k, v, o, ig, fg = _split_cols(p, (MLSTM_WIDTH, MLSTM_WIDTH, MLSTM_WIDTH, MLSTM_WIDTH, H, H))
    qk = jax.nn.silu(_causal_conv(jnp.concatenate([q, k], -1), conv_w, conv_b))
    q, k = qk[..., :MLSTM_WIDTH], qk[..., MLSTM_WIDTH:]
    heads = lambda t: t.reshape(B, S, H, N)
    q = heads(q) * (N ** -0.5)
    ig = ig + ig_b
    lf = jax.nn.log_sigmoid(fg + fg_b)
    h = _mlstm_chunkwise(q, heads(k), heads(v), ig, lf)
    h = _head_norm(h, hn_g, hn_b, LN_EPS).reshape(B, S, MLSTM_WIDTH)
    return jax.nn.sigmoid(o) * h


def _even_mixer(x, w_in, mu, w0, w_up, a0, a_up, g_up, k_k, k_a, r_k, rln_g, rln_b,
                conv_w, conv_b, ig_b, fg_b, hn_g, hn_b, w_out):
    p = x @ w_in
    y_a = _rwkv_pallas(p[..., :RWKV_COLS], mu, w0, w_up, a0, a_up, g_up, k_k, k_a, r_k, rln_g, rln_b)
    y_b = _mlstm_mix(p[..., RWKV_COLS:], conv_w, conv_b, ig_b, fg_b, hn_g, hn_b)
    return jnp.concatenate([y_a, y_b], -1) @ w_out


def _fox_attention(q, k, v, logf):
    B, S, H, D = q.shape
    NB = S // FOX_QBLOCK
    c = jnp.cumsum(logf, axis=1)
    c_key = jnp.transpose(c, (0, 2, 1))[:, :, None, :]
    q_blocks = jnp.moveaxis(q.reshape(B, NB, FOX_QBLOCK, H, D), 1, 0)
    c_blocks = jnp.moveaxis(c.reshape(B, NB, FOX_QBLOCK, H), 1, 0)
    key_pos = jnp.arange(S)
    scale = D ** -0.5

    def one_block(args):
        qb, cb, blk = args
        q_pos = blk * FOX_QBLOCK + jnp.arange(FOX_QBLOCK)
        logits = jnp.einsum('bqhd,bkhd->bhqk', qb, k) * scale
        logits = logits + jnp.transpose(cb, (0, 2, 1))[..., None] - c_key
        logits = jnp.where(key_pos[None, :] <= q_pos[:, None], logits, -jnp.inf)
        probs = jax.nn.softmax(logits, axis=-1)
        return jnp.einsum('bhqk,bkhd->bqhd', probs, v)

    out = lax.map(one_block, (q_blocks, c_blocks, jnp.arange(NB)))
    return jnp.moveaxis(out, 0, 1).reshape(B, S, H, D)


def _odd_mixer(x, w_in, qn_g, kn_g, f_b, w_out):
    B, S, _ = x.shape
    p = x @ w_in
    q, k, v, og, fl = _split_cols(p, (FOX_WIDTH, FOX_WIDTH, FOX_WIDTH, FOX_WIDTH, FOX_HEADS))
    heads = lambda t: t.reshape(B, S, FOX_HEADS, FOX_HEAD)
    q = _rms_norm(heads(q), qn_g)
    k = _rms_norm(heads(k), kn_g)
    logf = jax.nn.log_sigmoid(fl + f_b)
    o = _fox_attention(q, k, heads(v), logf).reshape(B, S, FOX_WIDTH)
    return (jax.nn.sigmoid(og) * o) @ w_out


def _peer_ffn(x, w_q, sub_k1, sub_k2, u_tab, v_tab):
    B, S, Dm = x.shape
    K = PEER_TOPK
    xt = x.reshape((B * S) // PEER_TOKBLOCK, PEER_TOKBLOCK, Dm)

    def block(xb):
        T = xb.shape[0]
        q = (xb @ w_q).reshape(T, PEER_HEADS, 2, PEER_DHALF)
        s1 = jnp.einsum('thd,hnd->thn', q[:, :, 0], sub_k1)
        s2 = jnp.einsum('thd,hnd->thn', q[:, :, 1], sub_k2)
        v1, i1 = lax.top_k(s1, K)
        v2, i2 = lax.top_k(s2, K)
        cand = (v1[..., :, None] + v2[..., None, :]).reshape(T, PEER_HEADS, K * K)
        sc, ci = lax.top_k(cand, K)
        e1 = jnp.take_along_axis(i1, ci // K, axis=-1)
        e2 = jnp.take_along_axis(i2, ci % K, axis=-1)
        eid = e1 * PEER_NKEYS + e2
        gate = jax.nn.softmax(sc, axis=-1)
        act = jax.nn.gelu(jnp.einsum('td,thkd->thk', xb, u_tab[eid]), approximate=False)
        return jnp.einsum('thk,thkd->td', gate * act, v_tab[eid])

    return lax.map(block, xt).reshape(B, S, Dm)


PEER_EXPERTS = PEER_NKEYS * PEER_NKEYS
PEER_SELECT_TILE = 256
PEER_TOK_TILE = 512
PEER_EXPERT_CHUNK = 1024
PEER_E1_GROUP = 4
PEER_E2_SUB = 32
PEER_A_PIECES = 4
PEER_C_PIECES = 1
LANES = 128
MXU_DIM = 256
INV_SQRT2 = 0.7071067811865476
_NT = (((1,), (1,)), ((), ()))


def _topk_desc(work, k):
    vals = []
    for _ in range(k):
        m = jnp.max(work, axis=0, keepdims=True)
        vals.append(m)
        work = jnp.where(work >= m, -jnp.inf, work)
    return vals


def _oddeven_merge(lo, hi, r):
    step = r * 2
    if step < hi - lo:
        yield from _oddeven_merge(lo, hi, step)
        yield from _oddeven_merge(lo + r, hi, step)
        yield from [(i, i + r) for i in range(lo + r, hi - r, step)]
    else:
        yield (lo, lo + r)


def _oddeven_sort(lo, hi):
    if hi - lo >= 1:
        mid = lo + (hi - lo) // 2
        yield from _oddeven_sort(lo, mid)
        yield from _oddeven_sort(mid + 1, hi)
        yield from _oddeven_merge(lo, hi, 1)


SUBLANES = 8


def _sorted_top(s, k):
    v = [s[i * SUBLANES:(i + 1) * SUBLANES, :] for i in range(k)]

    def exchange(i, j):
        v[i], v[j] = jnp.maximum(v[i], v[j]), jnp.minimum(v[i], v[j])

    for i, j in _oddeven_sort(0, k - 1):
        exchange(i, j)
    shift = SUBLANES // 2
    while shift >= 1:
        v = [jnp.maximum(v[i], pltpu.roll(v[k - 1 - i], shift, axis=0)) for i in range(k)]
        d = k // 2
        while d >= 1:
            for i in range(k):
                if i & d == 0:
                    exchange(i, i + d)
            d //= 2
        shift //= 2
    return [t[0:1, :] for t in v]


def _peer_select_kernel(x_ref, wqt_ref, k1_ref, k2_ref,
                        phi_ref, a1_ref, a2_ref, qt_ref):
    f32, bf16 = jnp.float32, jnp.bfloat16
    K = PEER_TOPK
    qt_ref[...] = lax.dot_general(wqt_ref[...], x_ref[...].astype(bf16), _NT, preferred_element_type=f32)
    n_groups = x_ref.shape[0] // LANES

    def head(h, carry):
        base = pl.multiple_of(h * PEER_DKEY, PEER_DKEY)
        q1 = qt_ref[pl.ds(base, PEER_DHALF), :].astype(bf16)
        q2 = qt_ref[pl.ds(base + PEER_DHALF, PEER_DHALF), :].astype(bf16)
        s1 = jnp.dot(k1_ref[h], q1, preferred_element_type=f32)
        s2 = jnp.dot(k2_ref[h], q2, preferred_element_type=f32)
        for c in range(n_groups):
            sl = slice(c * LANES, (c + 1) * LANES)
            s1c, s2c = s1[:, sl], s2[:, sl]
            v1 = _sorted_top(s1c, K)
            v2 = _sorted_top(s2c, K)
            rows = [v1[a] + v2[b] for a in range(K) for b in range(K) if (a + 1) * (b + 1) <= K + 1]
            rows += [jnp.full_like(v1[0], -jnp.inf)] * (-len(rows) % 8)
            cand = jnp.concatenate(rows, axis=0)
            vc = _topk_desc(cand, K + 1)
            cmax = vc[0]
            z = jnp.sum(jnp.where(cand >= vc[K - 1], jnp.exp(cand - cmax), 0.0), axis=0, keepdims=True)
            cut = 0.5 * (vc[K - 1] + vc[K])
            s1m = jnp.where(s1c >= v1[K - 1], s1c, -jnp.inf)
            s2m = jnp.where(s2c >= v2[K - 1], s2c, -jnp.inf)
            inv_z = 1.0 / z
            phi_ref[h, :, sl] = jnp.exp(cut - s1m - v2[0]) * inv_z
            a1_ref[h, :, sl] = jnp.exp(s1m - v1[0])
            a2_ref[h, :, sl] = jnp.exp(s2m - v2[0]) * inv_z
        return carry

    lax.fori_loop(0, PEER_HEADS, head, 0)


def _peer_select(xb, wqt, k1, k2):
    n, d = xb.shape
    T = PEER_SELECT_TILE
    H, NK = PEER_HEADS, PEER_NKEYS
    full = lambda shape: pl.BlockSpec(shape, lambda i: (0,) * len(shape))
    tok3 = lambda rows: pl.BlockSpec((H, rows, T), lambda i: (0, 0, i))
    f32 = jnp.float32
    return pl.pallas_call(
        _peer_select_kernel,
        grid=(n // T,),
        in_specs=[pl.BlockSpec((T, d), lambda i: (i, 0)), full(wqt.shape), full(k1.shape), full(k2.shape)],
        out_specs=[tok3(NK)] * 3,
        out_shape=[jax.ShapeDtypeStruct((H, NK, n), f32)] * 3,
        scratch_shapes=[pltpu.VMEM((H * PEER_DKEY, T), f32)],
        name="peer_select",
    )(xb, wqt, k1, k2)


def _peer_expert_kernel(xt_ref, u_ref, v_ref, phi_ref, a1_ref, a2_ref,
                        o_ref, acc_ref, h0_ref, h1_ref, w0_ref, w1_ref):
    f32, bf16 = jnp.float32, jnp.bfloat16
    s = pl.program_id(0)
    n_chunks = PEER_EXPERTS // PEER_EXPERT_CHUNK
    c_chunk = jnp.maximum(s - 2, 0) % n_chunks

    @pl.when(s == 0)
    def _():
        for ref in (h0_ref, h1_ref, w0_ref, w1_ref):
            ref[...] = jnp.zeros_like(ref)

    @pl.when(c_chunk == 0)
    def _():
        acc_ref[...] = jnp.zeros_like(acc_ref)

    T = xt_ref.shape[1]
    NK, G, SB = PEER_NKEYS, PEER_E1_GROUP, PEER_E2_SUB
    CE = PEER_EXPERT_CHUNK
    PA = PEER_A_PIECES

    def step(ha_ref, hb_ref, wb_ref, wc_ref):
        def stage_a(k):
            rows = CE // PA
            u_blk = pltpu.bitcast(u_ref[k * rows // 2:(k + 1) * rows // 2, :], bf16)
            ha_ref[k * rows:(k + 1) * rows, :] = jnp.dot(u_blk, xt_ref[...], preferred_element_type=f32)

        def stage_b(q, c, sb):
            ts = slice(c * LANES, (c + 1) * LANES)
            e2s = slice(sb * SB, (sb + 1) * SB)
            g = [None] * G
            for hd in range(PEER_HEADS):
                a2 = a2_ref[hd, e2s, ts]
                for e in range(G):
                    r = q * G + e
                    term = jnp.where(a2 > phi_ref[hd, r:r + 1, ts], a1_ref[hd, r:r + 1, ts] * a2, 0.0)
                    g[e] = term if g[e] is None else g[e] + term
            for e in range(G):
                lo = (q * G + e) * NK + sb * SB
                hh = hb_ref[lo:lo + SB, ts]
                wb_ref[lo:lo + SB, ts] = (g[e] * (0.5 * hh * (1.0 + lax.erf(hh * INV_SQRT2)))).astype(bf16)

        b_blocks = [(q, c, sb) for q in range(CE // (G * NK)) for c in range(T // LANES)
                    for sb in range(NK // SB)]
        sa = len(b_blocks) // PA
        for t, blk in enumerate(b_blocks):
            stage_b(*blk)
            if t % sa == 0:
                stage_a(t // sa)
            if t == len(b_blocks) // 2:
                acc_ref[...] += lax.dot_general(wc_ref[...], v_ref[...], _TN, preferred_element_type=f32)

    pl.when(s % 2 == 0)(functools.partial(step, h0_ref, h1_ref, w1_ref, w0_ref))
    pl.when(s % 2 == 1)(functools.partial(step, h1_ref, h0_ref, w0_ref, w1_ref))

    @pl.when(c_chunk == n_chunks - 1)
    def _():
        o_ref[...] = acc_ref[...]


def _peer_experts(xtb, ub, vb, phi, a1, a2):
    d, n = xtb.shape
    T, CE = PEER_TOK_TILE, PEER_EXPERT_CHUNK
    H, NK = PEER_HEADS, PEER_NKEYS
    e1_per_chunk = CE // NK
    nc = PEER_EXPERTS // CE
    last = (n // T) * nc - 1
    at = lambda s, lag: jnp.clip(s - lag, 0, last)
    return pl.pallas_call(
        _peer_expert_kernel,
        grid=(last + 3,),
        in_specs=[
            pl.BlockSpec((d, T), lambda s: (0, at(s, 0) // nc)),
            pl.BlockSpec((CE // 2, d), lambda s: (at(s, 0) % nc, 0)),
            pl.BlockSpec((CE, d), lambda s: (at(s, 2) % nc, 0)),
            pl.BlockSpec((H, e1_per_chunk, T), lambda s: (0, at(s, 1) % nc, at(s, 1) // nc)),
            pl.BlockSpec((H, e1_per_chunk, T), lambda s: (0, at(s, 1) % nc, at(s, 1) // nc)),
            pl.BlockSpec((H, NK, T), lambda s: (0, 0, at(s, 1) // nc)),
        ],
        out_specs=pl.BlockSpec((T, d), lambda s: (at(s, 2) // nc, 0)),
        out_shape=jax.ShapeDtypeStruct((n, d), jnp.float32),
        scratch_shapes=[pltpu.VMEM((T, d), jnp.float32),
                        pltpu.VMEM((CE, T), jnp.float32), pltpu.VMEM((CE, T), jnp.float32),
                        pltpu.VMEM((CE, T), jnp.bfloat16), pltpu.VMEM((CE, T), jnp.bfloat16)],
        compiler_params=pltpu.CompilerParams(dimension_semantics=("arbitrary",),
                                             vmem_limit_bytes=VMEM_LIMIT_BYTES),
        name="peer_experts",
    )(xtb, ub, vb, phi, a1, a2)


def _pack_row_pairs(t):
    half = lambda part: lax.bitcast_convert_type(part.astype(jnp.bfloat16), jnp.uint16).astype(jnp.uint32)
    return half(t[0::2]) | (half(t[1::2]) << 16)


def _peer_pallas(x2, xt_bf16, w_q, sub_k1, sub_k2, u_tab, v_tab):
    bf16 = jnp.bfloat16
    sel = _peer_select(x2, w_q.T.astype(bf16), sub_k1.astype(bf16), sub_k2.astype(bf16))
    return _peer_experts(xt_bf16, _pack_row_pairs(u_tab), v_tab.astype(bf16), *sel)


RWKV_CHUNK = 64
RWKV_PAIR = LANES // RWKV_HEAD
RWKV_PAIRS = RWKV_HEADS // RWKV_PAIR
_NN = (((1,), (0,)), ((), ()))
_TN = (((0,), (0,)), ((), ()))


def _split_bf16(a):
    hi = a.astype(jnp.bfloat16)
    lo = (a - hi.astype(jnp.float32)).astype(jnp.bfloat16)
    return hi, lo


def _mm2(a, b, dims):
    d = lambda u, v: lax.dot_general(u, v, dims, preferred_element_type=jnp.float32)
    (ah, al), (bh, bl) = _split_bf16(a), _split_bf16(b)
    return d(ah, bh) + d(ah, bl) + d(al, bh)


def _mm_exact_rhs(a, b_exact, dims, terms=2):
    d = lambda u: lax.dot_general(u, b_exact, dims, preferred_element_type=jnp.float32)
    out, rem = None, a
    for _ in range(terms):
        piece = rem.astype(jnp.bfloat16)
        rem = rem - piece.astype(jnp.float32)
        out = d(piece) if out is None else out + d(piece)
    return out


def _mm_exact_lhs(a_exact, b, dims, terms=3):
    d = lambda v: lax.dot_general(a_exact, v, dims, preferred_element_type=jnp.float32)
    out, rem = None, b
    for _ in range(terms):
        piece = rem.astype(jnp.bfloat16)
        rem = rem - piece.astype(jnp.float32)
        out = d(piece) if out is None else out + d(piece)
    return out


def _softplus(z):
    return jnp.maximum(z, 0.0) + jnp.log1p(jnp.exp(-jnp.abs(z)))


def _rwkv_kernel(p_ref, prev_ref, mu_ref, w0_ref, wup_ref, a0_ref, aup_ref, gup_ref,
                 kk_ref, ka_ref, rk_ref, lng_ref, lnb_ref, ones_ref,
                 o_ref, state_ref):
    f32, bf16 = jnp.float32, jnp.bfloat16
    L, W = RWKV_CHUNK, RWKV_WIDTH
    c_idx = pl.program_id(1)

    @pl.when(c_idx == 0)
    def _():
        state_ref[...] = jnp.zeros_like(state_ref)

    p = p_ref[0]
    row = lax.broadcasted_iota(jnp.int32, p.shape, 0)
    prev_row = jnp.where(c_idx == 0, 0.0, prev_ref[0, 7:8, :])
    shifted = jnp.where(row == 0, prev_row, pltpu.roll(p, 1, axis=0))
    p = p + (shifted - p) * mu_ref[...]
    r, k, v = p[:, 0:W], p[:, W:2 * W], p[:, 2 * W:3 * W]
    o = 3 * W
    xw = p[:, o:o + RWKV_W_RANK]
    xa = p[:, o + RWKV_W_RANK:o + RWKV_W_RANK + RWKV_A_RANK]
    xg = p[:, o + RWKV_W_RANK + RWKV_A_RANK:]
    dotd = lambda u, m: jnp.dot(u.astype(bf16), m, preferred_element_type=f32)
    lw = -jnp.exp(-_softplus(-(w0_ref[...] + dotd(jnp.tanh(xw), wup_ref[...]))) - 0.5)
    a = jax.nn.sigmoid(a0_ref[...] + dotd(xa, aup_ref[...]))
    g = dotd(jax.nn.sigmoid(xg), gup_ref[...])
    ones_bd = ones_ref[...]
    head_sum = lambda t: _mm_exact_rhs(t, ones_bd, _NN)
    kk = k * kk_ref[...]
    kk = kk / jnp.maximum(jnp.sqrt(head_sum(kk * kk)), 1e-12)
    k = k * (1.0 + (a - 1.0) * ka_ref[...])

    lane = lax.broadcasted_iota(jnp.int32, (L, LANES), 1)
    trow = lax.broadcasted_iota(jnp.int32, (L, LANES), 0)
    s_in = lane % RWKV_HEAD
    strict, incl = s_in < trow, s_in <= trow
    eye_pair = (s_in == trow).astype(f32)
    m0 = lane < RWKV_HEAD
    tril = (lax.broadcasted_iota(jnp.int32, (L, L), 1)
            <= lax.broadcasted_iota(jnp.int32, (L, L), 0)).astype(bf16)
    r2 = lax.broadcasted_iota(jnp.int32, (LANES, LANES), 0) // RWKV_HEAD
    c2 = lax.broadcasted_iota(jnp.int32, (LANES, LANES), 1) // RWKV_HEAD
    bd_mask = r2 == c2

    def bd(t):
        return jnp.concatenate([jnp.where(m0, t, 0.0), jnp.where(m0, 0.0, t)], axis=0)

    pairs = range(RWKV_PAIRS)
    cut = lambda t: [t[:, pr * LANES:(pr + 1) * LANES] for pr in pairs]
    rp, kp, vp, kkp, ap, lwp = cut(r), cut(k), cut(v), cut(kk), cut(a), cut(lw)
    c = [_mm_exact_lhs(tril, lwp[p], _NN) for p in pairs]
    at = [-kkp[p] * jnp.exp(c[p] - lwp[p]) for p in pairs]
    eni = [jnp.exp(-c[p]) for p in pairs]
    bt = [kkp[p] * ap[p] * eni[p] for p in pairs]
    kt = [kp[p] * eni[p] for p in pairs]
    rt = [rp[p] * jnp.exp(c[p]) for p in pairs]
    g_last = [jnp.exp(c[p][L - 1:L, :]) for p in pairs]
    gram = [_mm2(jnp.concatenate([at[p], rt[p]], axis=0),
                 jnp.concatenate([bd(bt[p]), bd(kt[p])], axis=0), _NT) for p in pairs]
    n_ab = [jnp.where(strict, gram[p][0:L, 0:LANES], 0.0) for p in pairs]
    a_ak = [jnp.where(strict, gram[p][0:L, LANES:], 0.0) for p in pairs]
    a_rb = [jnp.where(incl, gram[p][L:, 0:LANES], 0.0) for p in pairs]
    a_rk = [jnp.where(incl, gram[p][L:, LANES:], 0.0) for p in pairs]
    tinv, m = [eye_pair + n_ab[p] for p in pairs], n_ab
    for _ in range(5):
        m = [_mm2(m[p], bd(m[p]), _NN) for p in pairs]
        tinv = [tinv[p] + _mm2(m[p], bd(tinv[p]), _NN) for p in pairs]
    s0 = [state_ref[p] for p in pairs]
    pq = [_mm2(at[p], s0[p], _NT) + _mm2(a_ak[p], bd(vp[p]), _NN) for p in pairs]
    u = [_mm2(tinv[p], bd(pq[p]), _NN) for p in pairs]
    ys = [_mm2(rt[p], s0[p], _NT)
          + _mm2(jnp.concatenate([a_rb[p], a_rk[p]], axis=1),
                 jnp.concatenate([bd(u[p]), bd(vp[p])], axis=0), _NN) for p in pairs]
    for p in pairs:
        upd = _mm2(jnp.concatenate([u[p], vp[p]], axis=0),
                   jnp.concatenate([bt[p], kt[p]], axis=0), _TN)
        state_ref[p] = (s0[p] + jnp.where(bd_mask, upd, 0.0)) * g_last[p]
    y = jnp.concatenate(ys, axis=1)

    inv_n = 1.0 / RWKV_HEAD
    mean = head_sum(y) * inv_n
    yc = y - mean
    var = head_sum(yc * yc) * inv_n
    y = yc * lax.rsqrt(var + RWKV_GN_EPS) * lng_ref[...] + lnb_ref[...]
    y = y + head_sum(r * k * rk_ref[...]) * v
    o_ref[0] = y * g


def _rwkv_pallas(p_r, mu, w0, w_up, a0, a_up, g_up, k_k, k_a, r_k, ln_g, ln_b):
    B, S, C = p_r.shape
    L, W = RWKV_CHUNK, RWKV_WIDTH
    bf16 = jnp.bfloat16
    vecw = lambda t: t.reshape(1, W)
    ones_bd = jnp.kron(jnp.eye(RWKV_HEADS, dtype=bf16), jnp.ones((RWKV_HEAD, RWKV_HEAD), bf16))
    full = lambda a: pl.BlockSpec(a.shape, lambda b, c: (0,) * a.ndim)
    args = [mu.reshape(1, C), vecw(w0), w_up.astype(bf16), vecw(a0), a_up.astype(bf16),
            g_up.astype(bf16), vecw(k_k), vecw(k_a), vecw(r_k), vecw(ln_g), vecw(ln_b), ones_bd]
    return pl.pallas_call(
        _rwkv_kernel,
        grid=(B, S // L),
        in_specs=[pl.BlockSpec((1, L, C), lambda b, c: (b, c, 0)),
                  pl.BlockSpec((1, 8, C), lambda b, c: (b, jnp.maximum(c * (L // 8) - 1, 0), 0))]
                 + [full(a) for a in args],
        out_specs=pl.BlockSpec((1, L, W), lambda b, c: (b, c, 0)),
        out_shape=jax.ShapeDtypeStruct((B, S, W), jnp.float32),
        scratch_shapes=[pltpu.VMEM((RWKV_PAIRS, LANES, LANES), jnp.float32)],
        compiler_params=pltpu.CompilerParams(dimension_semantics=("arbitrary", "arbitrary")),
        name="rwkv7",
    )(p_r, p_r, *args)


MLSTM_CONV = 4
MLSTM_GATE_LANES = LANES


def _mlstm_kernel(p_ref, prev_ref, cw_ref, cb_ref, igb_ref, fgb_ref, hng_ref, hnb_ref,
                  o_ref, c_ref, n_ref, m_ref):
    f32, bf16 = jnp.float32, jnp.bfloat16
    L, W, H, D = MLSTM_CHUNK, MLSTM_WIDTH, MLSTM_HEADS, MLSTM_HEAD
    c_idx = pl.program_id(1)

    @pl.when(c_idx == 0)
    def _():
        c_ref[...] = jnp.zeros_like(c_ref)
        n_ref[...] = jnp.zeros_like(n_ref)
        m_ref[...] = jnp.zeros_like(m_ref)

    p = p_ref[0]
    z = p[:, 0:2 * W]
    prev = jnp.where(c_idx == 0, 0.0, prev_ref[0, :, 0:2 * W])
    ext = jnp.concatenate([prev, z], axis=0)
    conv = cb_ref[...] + cw_ref[MLSTM_CONV - 1:MLSTM_CONV, :] * z
    for j in range(MLSTM_CONV - 1):
        d = MLSTM_CONV - 1 - j
        conv = conv + cw_ref[j:j + 1, :] * ext[8 - d:8 - d + L, :]
    qk = conv * jax.nn.sigmoid(conv)
    q_all, k_all = qk[:, 0:W] * (D ** -0.5), qk[:, W:2 * W]
    v_all, o_all = p[:, 2 * W:3 * W], p[:, 3 * W:4 * W]
    gates = p[:, 4 * W:]
    igl = gates + igb_ref[...]
    lfl = -_softplus(-(gates + fgb_ref[...]))
    tril = (lax.broadcasted_iota(jnp.int32, (L, L), 1)
            <= lax.broadcasted_iota(jnp.int32, (L, L), 0))
    bcum = _mm_exact_lhs(tril.astype(bf16), lfl, _NN)
    e_all = igl - pltpu.roll(bcum, LANES - H, axis=1)
    lane = lax.broadcasted_iota(jnp.int32, (L, LANES), 1)
    hd = range(H)
    cols = lambda t: [t[:, h * D:(h + 1) * D] for h in hd]
    q, k, v, o = cols(q_all), cols(k_all), cols(v_all), cols(o_all)
    qb, kb, vb = ([t.astype(bf16) for t in ts] for ts in (q, k, v))
    b_col = [bcum[:, H + h:H + h + 1] for h in hd]
    ig_col = [igl[:, h:h + 1] for h in hd]
    e_row = [_mm_exact_lhs((lane == h).astype(bf16), e_all, _NT) for h in hd]
    qk = [lax.dot_general(qb[h], kb[h], _NT, preferred_element_type=f32) for h in hd]
    m_prev = [m_ref[h, 0:1, 0:1] for h in hd]
    c_prev = [c_ref[h] for h in hd]
    n_prev = [n_ref[h, 0:1, :] for h in hd]
    qc = [jnp.dot(qb[h], c_prev[h].astype(bf16), preferred_element_type=f32) for h in hd]
    dmat = [jnp.where(tril, b_col[h] + e_row[h], -jnp.inf) for h in hd]
    inter = [b_col[h] + m_prev[h] for h in hd]
    m_t = [jnp.maximum(inter[h], jnp.max(dmat[h], axis=-1, keepdims=True)) for h in hd]
    sc = [qk[h] * jnp.exp(dmat[h] - m_t[h]) for h in hd]
    carry_in = [jnp.exp(inter[h] - m_t[h]) for h in hd]
    sv = [jnp.dot(sc[h].astype(bf16), vb[h], preferred_element_type=f32) for h in hd]
    den = [jnp.sum(sc[h], axis=-1, keepdims=True)
           + carry_in[h] * jnp.sum(q[h] * n_prev[h], axis=-1, keepdims=True) for h in hd]
    hval = [(sv[h] + carry_in[h] * qc[h]) / jnp.maximum(jnp.abs(den[h]), jnp.exp(-m_t[h])) for h in hd]
    b_last = [b_col[h][L - 1:L, :] for h in hd]
    gs = [b_last[h] - b_col[h] + ig_col[h] for h in hd]
    m_new = [jnp.maximum(b_last[h] + m_prev[h], jnp.max(gs[h], axis=0, keepdims=True)) for h in hd]
    keep = [jnp.exp(b_last[h] + m_prev[h] - m_new[h]) for h in hd]
    wk = [jnp.exp(gs[h] - m_new[h]) * k[h] for h in hd]
    kv = [lax.dot_general(wk[h].astype(bf16), vb[h], _TN, preferred_element_type=f32) for h in hd]
    outs = []
    for h in hd:
        c_ref[h] = keep[h] * c_prev[h] + kv[h]
        n_ref[h] = jnp.broadcast_to(keep[h] * n_prev[h] + jnp.sum(wk[h], axis=0, keepdims=True), (8, D))
        m_ref[h] = jnp.broadcast_to(m_new[h], (8, LANES))
        mu = jnp.mean(hval[h], axis=-1, keepdims=True)
        hc = hval[h] - mu
        var = jnp.mean(hc * hc, axis=-1, keepdims=True)
        hs = slice(h * D, (h + 1) * D)
        hn = hc * lax.rsqrt(var + LN_EPS) * hng_ref[:, hs] + hnb_ref[:, hs]
        outs.append(jax.nn.sigmoid(o[h]) * hn)
    o_ref[0] = jnp.concatenate(outs, axis=1)


def _mlstm_pallas(p_m, conv_w, conv_b, ig_b, fg_b, hn_g, hn_b):
    B, S, C = p_m.shape
    L, W, H, D = MLSTM_CHUNK, MLSTM_WIDTH, MLSTM_HEADS, MLSTM_HEAD
    pad = lambda t, off: jnp.zeros((1, LANES), jnp.float32).at[0, off:off + H].set(t)
    args = [conv_w, conv_b.reshape(1, 2 * W), pad(ig_b, 0), pad(fg_b, H),
            hn_g.reshape(1, W), hn_b.reshape(1, W)]
    full = lambda a: pl.BlockSpec(a.shape, lambda b, c: (0,) * a.ndim)
    return pl.pallas_call(
        _mlstm_kernel,
        grid=(B, S // L),
        in_specs=[pl.BlockSpec((1, L, C), lambda b, c: (b, c, 0)),
                  pl.BlockSpec((1, 8, C), lambda b, c: (b, jnp.maximum(c * (L // 8) - 1, 0), 0))]
                 + [full(a) for a in args],
        out_specs=pl.BlockSpec((1, L, W), lambda b, c: (b, c, 0)),
        out_shape=jax.ShapeDtypeStruct((B, S, W), jnp.float32),
        scratch_shapes=[pltpu.VMEM((H, D, D), jnp.float32),
                        pltpu.VMEM((H, 8, D), jnp.float32),
                        pltpu.VMEM((H, 8, LANES), jnp.float32)],
        compiler_params=pltpu.CompilerParams(dimension_semantics=("arbitrary", "arbitrary")),
        name="mlstm",
    )(p_m, p_m, *args)


FOX_TILE = 256
FOX_KEY_GROUP = 4
FOX_HEADS_PER_STEP = 8
FOX_AUG = LANES
FOX_GATE_LANES = LANES


def _fox_prep_kernel(p_ref, qg_ref, kg_ref, fb_ref, ones_ref, qa_ref, ka_ref, vt_ref, carry_ref):
    f32, bf16 = jnp.float32, jnp.bfloat16
    T, W, H, D = FOX_TILE, FOX_WIDTH, FOX_HEADS, FOX_HEAD
    i = pl.program_id(1)

    @pl.when(i == 0)
    def _():
        carry_ref[...] = jnp.zeros_like(carry_ref)

    p = p_ref[0]
    ones_bd = ones_ref[...]
    ms = lambda t: _mm_exact_rhs(t * t, ones_bd, _NN) * (1.0 / D)
    q = p[:, 0:W]
    q = q * lax.rsqrt(ms(q) + 1e-6) * qg_ref[...]
    k = p[:, W:2 * W]
    k = k * lax.rsqrt(ms(k) + 1e-6) * kg_ref[...] * (D ** -0.5)
    vt = p[:, 2 * W:3 * W].T.astype(bf16)
    logf = -_softplus(-(p[:, 4 * W:] + fb_ref[...]))
    tril = (lax.broadcasted_iota(jnp.int32, (T, T), 1)
            <= lax.broadcasted_iota(jnp.int32, (T, T), 0)).astype(bf16)
    c = _mm_exact_lhs(tril, logf, _NN) + carry_ref[0:1, :]
    carry_ref[...] = jnp.broadcast_to(c[T - 1:T, :], carry_ref.shape)
    c1 = c.astype(bf16).astype(f32)
    c2 = (c - c1).astype(bf16).astype(f32)
    c3 = (c - c1 - c2).astype(bf16).astype(f32)
    lane = lax.broadcasted_iota(jnp.int32, (T, D), 1)
    for h in range(H):
        hs = slice(h * D, (h + 1) * D)
        c1h, c2h, c3h = c1[:, h:h + 1], c2[:, h:h + 1], c3[:, h:h + 1]
        pieces = jnp.where(lane % 3 == 0, c1h, jnp.where(lane % 3 == 1, c2h, c3h))
        q_aug = jnp.where(lane < 3, 1.0, jnp.where(lane < 6, pieces, 0.0))
        k_aug = jnp.where(lane < 3, -pieces, jnp.where(lane < 6, 1.0, 0.0))
        qa_ref[0, h] = jnp.concatenate([q[:, hs], q_aug], axis=1).astype(bf16)
        ka_ref[0, h] = jnp.concatenate([k[:, hs], k_aug], axis=1).astype(bf16)
        vt_ref[0, h, 0] = vt[hs, :]


def _fox_prep(p_f, qn_g, kn_g, f_b):
    B, S, C = p_f.shape
    T, W, H, D = FOX_TILE, FOX_WIDTH, FOX_HEADS, FOX_HEAD
    bf16 = jnp.bfloat16
    ones_bd = jnp.kron(jnp.eye(H, dtype=bf16), jnp.ones((D, D), bf16))
    fb = jnp.zeros((1, FOX_GATE_LANES), jnp.float32).at[0, :H].set(f_b)
    args = [qn_g.reshape(1, W), kn_g.reshape(1, W), fb, ones_bd]
    full = lambda a: pl.BlockSpec(a.shape, lambda b, i: (0,) * a.ndim)
    return pl.pallas_call(
        _fox_prep_kernel,
        grid=(B, S // T),
        in_specs=[pl.BlockSpec((1, T, C), lambda b, i: (b, i, 0))] + [full(a) for a in args],
        out_specs=[pl.BlockSpec((1, H, T, FOX_AUG), lambda b, i: (b, 0, i, 0)),
                   pl.BlockSpec((1, H, T, FOX_AUG), lambda b, i: (b, 0, i, 0)),
                   pl.BlockSpec((1, H, 1, D, T), lambda b, i: (b, 0, i, 0, 0))],
        out_shape=[jax.ShapeDtypeStruct((B, H, S, FOX_AUG), bf16),
                   jax.ShapeDtypeStruct((B, H, S, FOX_AUG), bf16),
                   jax.ShapeDtypeStruct((B, H, S // T, D, T), bf16)],
        scratch_shapes=[pltpu.VMEM((8, FOX_GATE_LANES), jnp.float32)],
        compiler_params=pltpu.CompilerParams(dimension_semantics=("arbitrary", "arbitrary"),
                                             vmem_limit_bytes=VMEM_LIMIT_BYTES),
        name="fox_prep",
    )(p_f, *args)


def _fox_attn_kernel(qa_ref, ka_ref, vt_ref, og_ref, o_ref, m_ref, l_ref, acc_ref):
    f32, bf16 = jnp.float32, jnp.bfloat16
    T, D = FOX_TILE, FOX_HEAD
    i = pl.program_id(2)
    heads = range(FOX_HEADS_PER_STEP)
    n_blocks = ka_ref.shape[2] // T
    GK = FOX_KEY_GROUP
    rel = (lax.broadcasted_iota(jnp.int32, (GK * T, T), 0)
           - lax.broadcasted_iota(jnp.int32, (GK * T, T), 1))
    qas = [qa_ref[0, j] for j in heads]

    def key_group(g, first):
        visible = rel <= (i - g * GK) * T

        def scores(j):
            ka = ka_ref[0, j, g * GK * T:(g + 1) * GK * T, :]
            return lax.dot_general(ka, qas[j], _NT, preferred_element_type=f32)

        nxt = scores(0)
        for j in heads:
            s = nxt
            if j + 1 < len(heads):
                nxt = scores(j + 1)
            s = jnp.where(visible, s, -jnp.inf)
            m_g = jnp.max(s, axis=0, keepdims=True)
            m_new = m_g if first else jnp.maximum(m_ref[j], m_g)
            pr = jnp.exp(s - m_new)
            l_g = jnp.sum(pr, axis=0, keepdims=True)
            vt = jnp.concatenate([vt_ref[0, j, g * GK + b] for b in range(GK)], axis=1)
            pv = jnp.dot(vt, pr.astype(bf16), preferred_element_type=f32)
            if first:
                l_ref[j], acc_ref[j] = l_g, pv
            else:
                alpha = jnp.exp(m_ref[j] - m_new)
                l_ref[j] = l_ref[j] * alpha + l_g
                acc_ref[j] = acc_ref[j] * alpha + pv
            m_ref[j] = m_new

    key_group(0, True)
    for g in range(1, n_blocks // GK):
        pl.when(i >= g * GK)(functools.partial(key_group, g, False))
    outs = [(acc_ref[j] / l_ref[j]).T for j in heads]
    o_ref[0] = jax.nn.sigmoid(og_ref[0]) * jnp.concatenate(outs, axis=1)


def _fox_attn(qa, ka, vt, p_f):
    B, H, S, A = qa.shape
    T, D = FOX_TILE, FOX_HEAD
    hp = FOX_HEADS_PER_STEP
    width = hp * D
    og_block0 = 3 * FOX_WIDTH // width
    return pl.pallas_call(
        _fox_attn_kernel,
        grid=(B, H // hp, S // T),
        in_specs=[pl.BlockSpec((1, hp, T, A), lambda b, g, i: (b, g, i, 0)),
                  pl.BlockSpec((1, hp, S, A), lambda b, g, i: (b, g, 0, 0)),
                  pl.BlockSpec((1, hp, S // T, D, T), lambda b, g, i: (b, g, 0, 0, 0)),
                  pl.BlockSpec((1, T, width), lambda b, g, i: (b, i, og_block0 + g))],
        out_specs=pl.BlockSpec((1, T, width), lambda b, g, i: (b, i, g)),
        out_shape=jax.ShapeDtypeStruct((B, S, FOX_WIDTH), jnp.float32),
        scratch_shapes=[pltpu.VMEM((hp, 1, T), jnp.float32), pltpu.VMEM((hp, 1, T), jnp.float32),
                        pltpu.VMEM((hp, D, T), jnp.float32)],
        name="fox_attn",
    )(qa, ka, vt, p_f)


def kernel(x, l0_w_in, l0_rwkv_mu, l0_rwkv_w0, l0_rwkv_w_up, l0_rwkv_a0, l0_rwkv_a_up,
           l0_rwkv_g_up, l0_rwkv_k_k, l0_rwkv_k_a, l0_rwkv_r_k, l0_rwkv_ln_g, l0_rwkv_ln_b,
           l0_mlstm_conv_w, l0_mlstm_conv_b, l0_mlstm_ig_b, l0_mlstm_fg_b,
           l0_mlstm_hn_g, l0_mlstm_hn_b, l0_w_out, l0_ln1_g, l0_ln1_b,
           l0_peer_wq, l0_peer_k1, l0_peer_k2, l0_peer_u, l0_peer_v, l0_ln2_g, l0_ln2_b,
           l1_w_in, l1_fox_qn_g, l1_fox_kn_g, l1_fox_f_b, l1_w_out, l1_ln1_g, l1_ln1_b,
           l1_peer_wq, l1_peer_k1, l1_peer_k2, l1_peer_u, l1_peer_v, l1_ln2_g, l1_ln2_b):
    B, S, D = x.shape
    n = B * S
    bf16 = jnp.bfloat16
    x2 = x.reshape(n, D)

    def pad_lanes(w):
        return jnp.pad(w, ((0, 0), (0, LANES - w.shape[1])))

    m_main = 4 * MLSTM_WIDTH
    w0 = jnp.concatenate([l0_w_in[:, :RWKV_COLS + m_main],
                          pad_lanes(l0_w_in[:, RWKV_COLS + m_main:])], axis=1).astype(bf16)
    p_r, p_m = _proj(x2, w0, (RWKV_COLS, m_main + LANES))
    y_a = _rwkv_pallas(p_r.reshape(B, S, -1), l0_rwkv_mu, l0_rwkv_w0, l0_rwkv_w_up, l0_rwkv_a0,
                       l0_rwkv_a_up, l0_rwkv_g_up, l0_rwkv_k_k, l0_rwkv_k_a, l0_rwkv_r_k,
                       l0_rwkv_ln_g, l0_rwkv_ln_b)
    y_b = _mlstm_pallas(p_m.reshape(B, S, -1), l0_mlstm_conv_w, l0_mlstm_conv_b, l0_mlstm_ig_b,
                        l0_mlstm_fg_b, l0_mlstm_hn_g, l0_mlstm_hn_b)
    x2, xt = _out_proj_ln([y_a.reshape(n, -1), y_b.reshape(n, -1)], l0_w_out.astype(bf16), x2,
                          l0_ln1_g, l0_ln1_b)
    y = _peer_pallas(x2, xt, l0_peer_wq, l0_peer_k1, l0_peer_k2, l0_peer_u, l0_peer_v)
    x2 = _resid_ln(x2, y, l0_ln2_g, l0_ln2_b)

    f_main = 4 * FOX_WIDTH
    w1 = jnp.concatenate([l1_w_in[:, :f_main], pad_lanes(l1_w_in[:, f_main:])], axis=1).astype(bf16)
    (p_f,) = _proj(x2, w1, (f_main + LANES,))
    p_f = p_f.reshape(B, S, -1)
    qa, ka, vt = _fox_prep(p_f, l1_fox_qn_g, l1_fox_kn_g, l1_fox_f_b)
    o = _fox_attn(qa, ka, vt, p_f)
    x2, xt = _out_proj_ln([o.reshape(n, -1)], l1_w_out.astype(bf16), x2, l1_ln1_g, l1_ln1_b)
    y = _peer_pallas(x2, xt, l1_peer_wq, l1_peer_k1, l1_peer_k2, l1_peer_u, l1_peer_v)
    x2 = _resid_ln(x2, y, l1_ln2_g, l1_ln2_b)
    return x2.reshape(B, S, D)
```

```python
import functools

import jax
import jax.numpy as jnp
from jax import lax
from jax.experimental import pallas as pl
from jax.experimental.pallas import tpu as pltpu

D_MODEL = 1024
DEPTH = 2
DN_ALPHA = (2.0 * DEPTH) ** 0.25
LN_EPS = 1e-5

RWKV_WIDTH = D_MODEL // 2
RWKV_HEAD = 64
RWKV_HEADS = RWKV_WIDTH // RWKV_HEAD
RWKV_W_RANK = 64
RWKV_A_RANK = 64
RWKV_G_RANK = 128
RWKV_GN_EPS = 1e-5 * RWKV_HEAD
RWKV_COLS = 3 * RWKV_WIDTH + RWKV_W_RANK + RWKV_A_RANK + RWKV_G_RANK

MLSTM_WIDTH = D_MODEL // 2
MLSTM_HEAD = 128
MLSTM_HEADS = MLSTM_WIDTH // MLSTM_HEAD
MLSTM_CHUNK = 64

FOX_HEAD = 64
FOX_HEADS = D_MODEL // FOX_HEAD
FOX_WIDTH = FOX_HEADS * FOX_HEAD
FOX_QBLOCK = 128

PEER_HEADS = 8
PEER_NKEYS = 128
PEER_TOPK = 16
PEER_DKEY = 256
PEER_DHALF = PEER_DKEY // 2
PEER_TOKBLOCK = 128


def _split_cols(p, sizes):
    out, start = [], 0
    for s in sizes:
        out.append(p[..., start:start + s])
        start += s
    return out


def _resid_ln_kernel(x_ref, y_ref, g_ref, b_ref, o_ref):
    z = DN_ALPHA * x_ref[...] + y_ref[...]
    mu = jnp.mean(z, axis=-1, keepdims=True)
    zc = z - mu
    var = jnp.mean(zc * zc, axis=-1, keepdims=True)
    o_ref[...] = zc * lax.rsqrt(var + LN_EPS) * g_ref[...] + b_ref[...]


def _resid_ln(x2, y2, g, b, tm=512):
    n, d = x2.shape
    row = pl.BlockSpec((tm, d), lambda i: (i, 0))
    vec = pl.BlockSpec((1, d), lambda i: (0, 0))
    return pl.pallas_call(
        _resid_ln_kernel,
        grid=(n // tm,),
        in_specs=[row, row, vec, vec],
        out_specs=row,
        out_shape=jax.ShapeDtypeStruct((n, d), jnp.float32),
        name="resid_ln",
    )(x2, y2, g.reshape(1, d), b.reshape(1, d))


VMEM_LIMIT_BYTES = 56 * 1024 * 1024


def _proj_kernel(x_ref, w_ref, *o_refs):
    xb = x_ref[...].astype(jnp.bfloat16)
    start = 0
    for o_ref in o_refs:
        width = o_ref.shape[1]
        o_ref[...] = jnp.dot(xb, w_ref[:, start:start + width], preferred_element_type=jnp.float32)
        start += width


def _proj(x2, w_bf16, widths, tm=256):
    n, d = x2.shape
    assert sum(widths) == w_bf16.shape[1]
    return pl.pallas_call(
        _proj_kernel,
        grid=(n // tm,),
        in_specs=[pl.BlockSpec((tm, d), lambda i: (i, 0)),
                  pl.BlockSpec(w_bf16.shape, lambda i: (0, 0))],
        out_specs=[pl.BlockSpec((tm, w), lambda i: (i, 0)) for w in widths],
        out_shape=[jax.ShapeDtypeStruct((n, w), jnp.float32) for w in widths],
        compiler_params=pltpu.CompilerParams(vmem_limit_bytes=VMEM_LIMIT_BYTES),
        name="in_proj",
    )(x2, w_bf16)


def _out_proj_ln_kernel(*refs, n_parts):
    y_refs, (w_ref, x_ref, g_ref, b_ref, o_ref, ot_ref) = refs[:n_parts], refs[n_parts:]
    acc, start = None, 0
    for y_ref in y_refs:
        width = y_ref.shape[1]
        part = jnp.dot(y_ref[...].astype(jnp.bfloat16), w_ref[start:start + width, :],
                       preferred_element_type=jnp.float32)
        acc = part if acc is None else acc + part
        start += width
    z = DN_ALPHA * x_ref[...] + acc
    mu = jnp.mean(z, axis=-1, keepdims=True)
    zc = z - mu
    var = jnp.mean(zc * zc, axis=-1, keepdims=True)
    out = zc * lax.rsqrt(var + LN_EPS) * g_ref[...] + b_ref[...]
    o_ref[...] = out
    ot_ref[...] = out.T.astype(jnp.bfloat16)


def _out_proj_ln(ys, w_bf16, x2, g, b, tm=256):
    n, d = x2.shape
    row = lambda width: pl.BlockSpec((tm, width), lambda i: (i, 0))
    vec = pl.BlockSpec((1, d), lambda i: (0, 0))
    return pl.pallas_call(
        functools.partial(_out_proj_ln_kernel, n_parts=len(ys)),
        grid=(n // tm,),
        in_specs=[row(y.shape[1]) for y in ys]
                 + [pl.BlockSpec(w_bf16.shape, lambda i: (0, 0)), row(d), vec, vec],
        out_specs=[row(d), pl.BlockSpec((d, tm), lambda i: (0, i))],
        out_shape=[jax.ShapeDtypeStruct((n, d), jnp.float32), jax.ShapeDtypeStruct((d, n), jnp.bfloat16)],
        name="out_proj_ln",
    )(*ys, w_bf16, x2, g.reshape(1, d), b.reshape(1, d))


def _head_norm(y, g, b, eps):
    mu = jnp.mean(y, -1, keepdims=True)
    var = jnp.mean(jnp.square(y - mu), -1, keepdims=True)
    return (y - mu) * lax.rsqrt(var + eps) * g + b


def _rms_norm(y, g):
    return y * lax.rsqrt(jnp.mean(y * y, -1, keepdims=True) + 1e-6) * g


def _token_shift(z):
    return jnp.pad(z, ((0, 0), (1, 0), (0, 0)))[:, :-1]


def _causal_conv(z, w, b):
    c = z.shape[-1]
    out = lax.conv_general_dilated(z, w[:, None, :], window_strides=(1,),
                                   padding=((w.shape[0] - 1, 0),),
                                   dimension_numbers=('NWC', 'WIO', 'NWC'),
                                   feature_group_count=c)
    return out + b


def _rwkv7_mix(p, mu, w0, w_up, a0, a_up, g_up, k_k, k_a, r_k, ln_g, ln_b):
    B, S, _ = p.shape
    H, N = RWKV_HEADS, RWKV_HEAD
    p = p + (_token_shift(p) - p) * mu
    r, k, v, xw, xa, xg = _split_cols(p, (RWKV_WIDTH, RWKV_WIDTH, RWKV_WIDTH,
                                          RWKV_W_RANK, RWKV_A_RANK, RWKV_G_RANK))
    log_w = -jnp.exp(-jax.nn.softplus(-(w0 + jnp.tanh(xw) @ w_up)) - 0.5)
    a = jax.nn.sigmoid(a0 + xa @ a_up)
    g = jax.nn.sigmoid(xg) @ g_up
    heads = lambda t: t.reshape(B, S, H, N)
    kk = heads(k * k_k)
    kk = kk / jnp.maximum(jnp.sqrt(jnp.sum(kk * kk, -1, keepdims=True)), 1e-12)
    k = k * (1.0 + (a - 1.0) * k_a)
    r_h, k_h, v_h, a_h = heads(r), heads(k), heads(v), heads(a)
    w_h = jnp.exp(heads(log_w))

    def step(state, inp):
        r_t, w_t, k_t, v_t, kk_t, a_t = inp
        s_kk = jnp.einsum('bhvk,bhk->bhv', state, kk_t)
        state = (state * w_t[:, :, None, :]
                 - s_kk[..., None] * (kk_t * a_t)[:, :, None, :]
                 + v_t[..., None] * k_t[:, :, None, :])
        return state, jnp.einsum('bhvk,bhk->bhv', state, r_t)

    seq_first = lambda t: jnp.moveaxis(t, 1, 0)
    state0 = jnp.zeros((B, H, N, N), jnp.float32)
    _, y = lax.scan(step, state0, (seq_first(r_h), seq_first(w_h), seq_first(k_h),
                                   seq_first(v_h), seq_first(kk), seq_first(a_h)))
    y = jnp.moveaxis(y, 0, 1)
    y = _head_norm(y, ln_g, ln_b, RWKV_GN_EPS)
    y = y + jnp.sum(r_h * k_h * r_k, -1, keepdims=True) * v_h
    return y.reshape(B, S, RWKV_WIDTH) * g


def _mlstm_chunkwise(q, k, v, ig, lf):
    B, S, H, D = q.shape
    L = MLSTM_CHUNK
    NC = S // L

    def to_chunks(t):
        t = t.reshape((B, NC, L, H) + t.shape[3:])
        return jnp.moveaxis(t, (1, 3), (0, 2))

    causal = jnp.tril(jnp.ones((L, L), dtype=bool))

    def body(carry, inp):
        C, n, m = carry
        qc, kc, vc, igc, lfc = inp
        b = jnp.cumsum(lfc, axis=-1)
        dmat = jnp.where(causal, b[..., :, None] - b[..., None, :] + igc[..., None, :], -jnp.inf)
        inter = b + m[..., None]
        m_t = jnp.maximum(inter, jnp.max(dmat, -1))
        weights = jnp.exp(dmat - m_t[..., None])
        sc = jnp.einsum('bhtd,bhsd->bhts', qc, kc) * weights
        carry_in = jnp.exp(inter - m_t)
        num = (jnp.einsum('bhts,bhsd->bhtd', sc, vc)
               + carry_in[..., None] * jnp.einsum('bhtk,bhkv->bhtv', qc, C))
        den = jnp.sum(sc, -1) + carry_in * jnp.einsum('bhtk,bhk->bht', qc, n)
        h = num / jnp.maximum(jnp.abs(den), jnp.exp(-m_t))[..., None]
        b_last = b[..., -1]
        gs = b_last[..., None] - b + igc
        m_new = jnp.maximum(b_last + m, jnp.max(gs, -1))
        ws = jnp.exp(gs - m_new[..., None])
        keep = jnp.exp(b_last + m - m_new)
        C = keep[..., None, None] * C + jnp.einsum('bhs,bhsk,bhsv->bhkv', ws, kc, vc)
        n = keep[..., None] * n + jnp.einsum('bhs,bhsk->bhk', ws, kc)
        return (C, n, m_new), h

    f32 = jnp.float32
    init = (jnp.zeros((B, H, D, D), f32), jnp.zeros((B, H, D), f32), jnp.zeros((B, H), f32))
    _, h = lax.scan(body, init, (to_chunks(q), to_chunks(k), to_chunks(v),
                                 to_chunks(ig), to_chunks(lf)))
    return jnp.moveaxis(h, (0, 2), (1, 3)).reshape(B, S, H, D)


def _mlstm_mix(p, conv_w, conv_b, ig_b, fg_b, hn_g, hn_b):
    B, S, _ = p.shape
    H, N = MLSTM_HEADS, MLSTM_HEAD
    q, k, v, o, ig, fg = _split_cols(p, (MLSTM_WIDTH, MLSTM_WIDTH, MLSTM_WIDTH, MLSTM_WIDTH, H, H))
    qk = jax.nn.silu(_causal_conv(jnp.concatenate([q, k], -1), conv_w, conv_b))
    q, k = qk[..., :MLSTM_WIDTH], qk[..., MLSTM_WIDTH:]
    heads = lambda t: t.reshape(B, S, H, N)
    q = heads(q) * (N ** -0.5)
    ig = ig + ig_b
    lf = jax.nn.log_sigmoid(fg + fg_b)
    h = _mlstm_chunkwise(q, heads(k), heads(v), ig, lf)
    h = _head_norm(h, hn_g, hn_b, LN_EPS).reshape(B, S, MLSTM_WIDTH)
    return jax.nn.sigmoid(o) * h


def _even_mixer(x, w_in, mu, w0, w_up, a0, a_up, g_up, k_k, k_a, r_k, rln_g, rln_b,
                conv_w, conv_b, ig_b, fg_b, hn_g, hn_b, w_out):
    p = x @ w_in
    y_a = _rwkv_pallas(p[..., :RWKV_COLS], mu, w0, w_up, a0, a_up, g_up, k_k, k_a, r_k, rln_g, rln_b)
    y_b = _mlstm_mix(p[..., RWKV_COLS:], conv_w, conv_b, ig_b, fg_b, hn_g, hn_b)
    return jnp.concatenate([y_a, y_b], -1) @ w_out


def _fox_attention(q, k, v, logf):
    B, S, H, D = q.shape
    NB = S // FOX_QBLOCK
    c = jnp.cumsum(logf, axis=1)
    c_key = jnp.transpose(c, (0, 2, 1))[:, :, None, :]
    q_blocks = jnp.moveaxis(q.reshape(B, NB, FOX_QBLOCK, H, D), 1, 0)
    c_blocks = jnp.moveaxis(c.reshape(B, NB, FOX_QBLOCK, H), 1, 0)
    key_pos = jnp.arange(S)
    scale = D ** -0.5

    def one_block(args):
        qb, cb, blk = args
        q_pos = blk * FOX_QBLOCK + jnp.arange(FOX_QBLOCK)
        logits = jnp.einsum('bqhd,bkhd->bhqk', qb, k) * scale
        logits = logits + jnp.transpose(cb, (0, 2, 1))[..., None] - c_key
        logits = jnp.where(key_pos[None, :] <= q_pos[:, None], logits, -jnp.inf)
        probs = jax.nn.softmax(logits, axis=-1)
        return jnp.einsum('bhqk,bkhd->bqhd', probs, v)

    out = lax.map(one_block, (q_blocks, c_blocks, jnp.arange(NB)))
    return jnp.moveaxis(out, 0, 1).reshape(B, S, H, D)


def _odd_mixer(x, w_in, qn_g, kn_g, f_b, w_out):
    B, S, _ = x.shape
    p = x @ w_in
    q, k, v, og, fl = _split_cols(p, (FOX_WIDTH, FOX_WIDTH, FOX_WIDTH, FOX_WIDTH, FOX_HEADS))
    heads = lambda t: t.reshape(B, S, FOX_HEADS, FOX_HEAD)
    q = _rms_norm(heads(q), qn_g)
    k = _rms_norm(heads(k), kn_g)
    logf = jax.nn.log_sigmoid(fl + f_b)
    o = _fox_attention(q, k, heads(v), logf).reshape(B, S, FOX_WIDTH)
    return (jax.nn.sigmoid(og) * o) @ w_out


def _peer_ffn(x, w_q, sub_k1, sub_k2, u_tab, v_tab):
    B, S, Dm = x.shape
    K = PEER_TOPK
    xt = x.reshape((B * S) // PEER_TOKBLOCK, PEER_TOKBLOCK, Dm)

    def block(xb):
        T = xb.shape[0]
        q = (xb @ w_q).reshape(T, PEER_HEADS, 2, PEER_DHALF)
        s1 = jnp.einsum('thd,hnd->thn', q[:, :, 0], sub_k1)
        s2 = jnp.einsum('thd,hnd->thn', q[:, :, 1], sub_k2)
        v1, i1 = lax.top_k(s1, K)
        v2, i2 = lax.top_k(s2, K)
        cand = (v1[..., :, None] + v2[..., None, :]).reshape(T, PEER_HEADS, K * K)
        sc, ci = lax.top_k(cand, K)
        e1 = jnp.take_along_axis(i1, ci // K, axis=-1)
        e2 = jnp.take_along_axis(i2, ci % K, axis=-1)
        eid = e1 * PEER_NKEYS + e2
        gate = jax.nn.softmax(sc, axis=-1)
        act = jax.nn.gelu(jnp.einsum('td,thkd->thk', xb, u_tab[eid]), approximate=False)
        return jnp.einsum('thk,thkd->td', gate * act, v_tab[eid])

    return lax.map(block, xt).reshape(B, S, Dm)


PEER_EXPERTS = PEER_NKEYS * PEER_NKEYS
PEER_SELECT_TILE = 256
PEER_TOK_TILE = 512
PEER_EXPERT_CHUNK = 1024
PEER_E1_GROUP = 4
PEER_E2_SUB = 32
PEER_A_PIECES = 4
PEER_C_PIECES = 1
LANES = 128
MXU_DIM = 256
INV_SQRT2 = 0.7071067811865476
_NT = (((1,), (1,)), ((), ()))


def _topk_desc(work, k):
    vals = []
    for _ in range(k):
        m = jnp.max(work, axis=0, keepdims=True)
        vals.append(m)
        work = jnp.where(work >= m, -jnp.inf, work)
    return vals


def _oddeven_merge(lo, hi, r):
    step = r * 2
    if step < hi - lo:
        yield from _oddeven_merge(lo, hi, step)
        yield from _oddeven_merge(lo + r, hi, step)
        yield from [(i, i + r) for i in range(lo + r, hi - r, step)]
    else:
        yield (lo, lo + r)


def _oddeven_sort(lo, hi):
    if hi - lo >= 1:
        mid = lo + (hi - lo) // 2
        yield from _oddeven_sort(lo, mid)
        yield from _oddeven_sort(mid + 1, hi)
        yield from _oddeven_merge(lo, hi, 1)


SUBLANES = 8


def _sorted_top(s, k):
    v = [s[i * SUBLANES:(i + 1) * SUBLANES, :] for i in range(k)]

    def exchange(i, j):
        v[i], v[j] = jnp.maximum(v[i], v[j]), jnp.minimum(v[i], v[j])

    for i, j in _oddeven_sort(0, k - 1):
        exchange(i, j)
    shift = SUBLANES // 2
    while shift >= 1:
        v = [jnp.maximum(v[i], pltpu.roll(v[k - 1 - i], shift, axis=0)) for i in range(k)]
        d = k // 2
        while d >= 1:
            for i in range(k):
                if i & d == 0:
                    exchange(i, i + d)
            d //= 2
        shift //= 2
    return [t[0:1, :] for t in v]


def _peer_select_kernel(x_ref, wqt_ref, k1_ref, k2_ref,
                        phi_ref, a1_ref, a2_ref, qt_ref):
    f32, bf16 = jnp.float32, jnp.bfloat16
    K = PEER_TOPK
    qt_ref[...] = lax.dot_general(wqt_ref[...], x_ref[...].astype(bf16), _NT, preferred_element_type=f32)
    n_groups = x_ref.shape[0] // LANES

    def head(h, carry):
        base = pl.multiple_of(h * PEER_DKEY, PEER_DKEY)
        q1 = qt_ref[pl.ds(base, PEER_DHALF), :].astype(bf16)
        q2 = qt_ref[pl.ds(base + PEER_DHALF, PEER_DHALF), :].astype(bf16)
        s1 = jnp.dot(k1_ref[h], q1, preferred_element_type=f32)
        s2 = jnp.dot(k2_ref[h], q2, preferred_element_type=f32)
        for c in range(n_groups):
            sl = slice(c * LANES, (c + 1) * LANES)
            s1c, s2c = s1[:, sl], s2[:, sl]
            v1 = _sorted_top(s1c, K)
            v2 = _sorted_top(s2c, K)
            rows = [v1[a] + v2[b] for a in range(K) for b in range(K) if (a + 1) * (b + 1) <= K + 1]
            rows += [jnp.full_like(v1[0], -jnp.inf)] * (-len(rows) % 8)
            cand = jnp.concatenate(rows, axis=0)
            vc = _topk_desc(cand, K + 1)
            cmax = vc[0]
            z = jnp.sum(jnp.where(cand >= vc[K - 1], jnp.exp(cand - cmax), 0.0), axis=0, keepdims=True)
            cut = 0.5 * (vc[K - 1] + vc[K])
            s1m = jnp.where(s1c >= v1[K - 1], s1c, -jnp.inf)
            s2m = jnp.where(s2c >= v2[K - 1], s2c, -jnp.inf)
            inv_z = 1.0 / z
            phi_ref[h, :, sl] = jnp.exp(cut - s1m - v2[0]) * inv_z
            a1_ref[h, :, sl] = jnp.exp(s1m - v1[0])
            a2_ref[h, :, sl] = jnp.exp(s2m - v2[0]) * inv_z
        return carry

    lax.fori_loop(0, PEER_HEADS, head, 0)


def _peer_select(xb, wqt, k1, k2):
    n, d = xb.shape
    T = PEER_SELECT_TILE
    H, NK = PEER_HEADS, PEER_NKEYS
    full = lambda shape: pl.BlockSpec(shape, lambda i: (0,) * len(shape))
    tok3 = lambda rows: pl.BlockSpec((H, rows, T), lambda i: (0, 0, i))
    f32 = jnp.float32
    return pl.pallas_call(
        _peer_select_kernel,
        grid=(n // T,),
        in_specs=[pl.BlockSpec((T, d), lambda i: (i, 0)), full(wqt.shape), full(k1.shape), full(k2.shape)],
        out_specs=[tok3(NK)] * 3,
        out_shape=[jax.ShapeDtypeStruct((H, NK, n), f32)] * 3,
        scratch_shapes=[pltpu.VMEM((H * PEER_DKEY, T), f32)],
        name="peer_select",
    )(xb, wqt, k1, k2)


def _peer_expert_kernel(xt_ref, u_ref, v_ref, phi_ref, a1_ref, a2_ref,
                        o_ref, acc_ref, h0_ref, h1_ref, w0_ref, w1_ref):
    f32, bf16 = jnp.float32, jnp.bfloat16
    s = pl.program_id(0)
    n_chunks = PEER_EXPERTS // PEER_EXPERT_CHUNK
    c_chunk = jnp.maximum(s - 2, 0) % n_chunks

    @pl.when(s == 0)
    def _():
        for ref in (h0_ref, h1_ref, w0_ref, w1_ref):
            ref[...] = jnp.zeros_like(ref)

    @pl.when(c_chunk == 0)
    def _():
        acc_ref[...] = jnp.zeros_like(acc_ref)

    T = xt_ref.shape[1]
    NK, G, SB = PEER_NKEYS, PEER_E1_GROUP, PEER_E2_SUB
    CE = PEER_EXPERT_CHUNK
    PA = PEER_A_PIECES

    def step(ha_ref, hb_ref, wb_ref, wc_ref):
        def stage_a(k):
            rows = CE // PA
            u_blk = pltpu.bitcast(u_ref[k * rows // 2:(k + 1) * rows // 2, :], bf16)
            ha_ref[k * rows:(k + 1) * rows, :] = jnp.dot(u_blk, xt_ref[...], preferred_element_type=f32)

        def stage_b(q, c, sb):
            ts = slice(c * LANES, (c + 1) * LANES)
            e2s = slice(sb * SB, (sb + 1) * SB)
            g = [None] * G
            for hd in range(PEER_HEADS):
                a2 = a2_ref[hd, e2s, ts]
                for e in range(G):
                    r = q * G + e
                    term = jnp.where(a2 > phi_ref[hd, r:r + 1, ts], a1_ref[hd, r:r + 1, ts] * a2, 0.0)
                    g[e] = term if g[e] is None else g[e] + term
            for e in range(G):
                lo = (q * G + e) * NK + sb * SB
                hh = hb_ref[lo:lo + SB, ts]
                wb_ref[lo:lo + SB, ts] = (g[e] * (0.5 * hh * (1.0 + lax.erf(hh * INV_SQRT2)))).astype(bf16)

        b_blocks = [(q, c, sb) for q in range(CE // (G * NK)) for c in range(T // LANES)
                    for sb in range(NK // SB)]
        sa = len(b_blocks) // PA
        for t, blk in enumerate(b_blocks):
            stage_b(*blk)
            if t % sa == 0:
                stage_a(t // sa)
            if t == len(b_blocks) // 2:
                acc_ref[...] += lax.dot_general(wc_ref[...], v_ref[...], _TN, preferred_element_type=f32)

    pl.when(s % 2 == 0)(functools.partial(step, h0_ref, h1_ref, w1_ref, w0_ref))
    pl.when(s % 2 == 1)(functools.partial(step, h1_ref, h0_ref, w0_ref, w1_ref))

    @pl.when(c_chunk == n_chunks - 1)
    def _():
        o_ref[...] = acc_ref[...]


def _peer_experts(xtb, ub, vb, phi, a1, a2):
    d, n = xtb.shape
    T, CE = PEER_TOK_TILE, PEER_EXPERT_CHUNK
    H, NK = PEER_HEADS, PEER_NKEYS
    e1_per_chunk = CE // NK
    nc = PEER_EXPERTS // CE
    last = (n // T) * nc - 1
    at = lambda s, lag: jnp.clip(s - lag, 0, last)
    return pl.pallas_call(
        _peer_expert_kernel,
        grid=(last + 3,),
        in_specs=[
            pl.BlockSpec((d, T), lambda s: (0, at(s, 0) // nc)),
            pl.BlockSpec((CE // 2, d), lambda s: (at(s, 0) % nc, 0)),
            pl.BlockSpec((CE, d), lambda s: (at(s, 2) % nc, 0)),
            pl.BlockSpec((H, e1_per_chunk, T), lambda s: (0, at(s, 1) % nc, at(s, 1) // nc)),
            pl.BlockSpec((H, e1_per_chunk, T), lambda s: (0, at(s, 1) % nc, at(s, 1) // nc)),
            pl.BlockSpec((H, NK, T), lambda s: (0, 0, at(s, 1) // nc)),
        ],
        out_specs=pl.BlockSpec((T, d), lambda s: (at(s, 2) // nc, 0)),
        out_shape=jax.ShapeDtypeStruct((n, d), jnp.float32),
        scratch_shapes=[pltpu.VMEM((T, d), jnp.float32),
                        pltpu.VMEM((CE, T), jnp.float32), pltpu.VMEM((CE, T), jnp.float32),
                        pltpu.VMEM((CE, T), jnp.bfloat16), pltpu.VMEM((CE, T), jnp.bfloat16)],
        compiler_params=pltpu.CompilerParams(dimension_semantics=("arbitrary",),
                                             vmem_limit_bytes=VMEM_LIMIT_BYTES),
        name="peer_experts",
    )(xtb, ub, vb, phi, a1, a2)


def _pack_row_pairs(t):
    rows, cols = t.shape
    half = lambda part: lax.bitcast_convert_type(part.astype(jnp.bfloat16), jnp.uint16).astype(jnp.uint32)
    pairs = t.reshape(rows // 2, 2 * cols)
    return half(pairs[:, :cols]) | (half(pairs[:, cols:]) << 16)


def _peer_pallas(x2, xt_bf16, w_q, sub_k1, sub_k2, u_tab, v_tab):
    bf16 = jnp.bfloat16
    sel = _peer_select(x2, w_q.T.astype(bf16), sub_k1.astype(bf16), sub_k2.astype(bf16))
    return _peer_experts(xt_bf16, _pack_row_pairs(u_tab), v_tab.astype(bf16), *sel)


RWKV_CHUNK = 64
RWKV_PAIR = LANES // RWKV_HEAD
RWKV_PAIRS = RWKV_HEADS // RWKV_PAIR
_NN = (((1,), (0,)), ((), ()))
_TN = (((0,), (0,)), ((), ()))


def _split_bf16(a):
    hi = a.astype(jnp.bfloat16)
    lo = (a - hi.astype(jnp.float32)).astype(jnp.bfloat16)
    return hi, lo


def _mm2(a, b, dims):
    d = lambda u, v: lax.dot_general(u, v, dims, preferred_element_type=jnp.float32)
    (ah, al), (bh, bl) = _split_bf16(a), _split_bf16(b)
    return d(ah, bh) + d(ah, bl) + d(al, bh)


def _mm_exact_rhs(a, b_exact, dims, terms=2):
    d = lambda u: lax.dot_general(u, b_exact, dims, preferred_element_type=jnp.float32)
    out, rem = None, a
    for _ in range(terms):
        piece = rem.astype(jnp.bfloat16)
        rem = rem - piece.astype(jnp.float32)
        out = d(piece) if out is None else out + d(piece)
    return out


def _mm_exact_lhs(a_exact, b, dims, terms=3):
    d = lambda v: lax.dot_general(a_exact, v, dims, preferred_element_type=jnp.float32)
    out, rem = None, b
    for _ in range(terms):
        piece = rem.astype(jnp.bfloat16)
        rem = rem - piece.astype(jnp.float32)
        out = d(piece) if out is None else out + d(piece)
    return out


def _softplus(z):
    return jnp.maximum(z, 0.0) + jnp.log1p(jnp.exp(-jnp.abs(z)))


def _rwkv_kernel(p_ref, prev_ref, mu_ref, w0_ref, wup_ref, a0_ref, aup_ref, gup_ref,
                 kk_ref, ka_ref, rk_ref, lng_ref, lnb_ref, ones_ref,
                 o_ref, state_ref):
    f32, bf16 = jnp.float32, jnp.bfloat16
    L, W = RWKV_CHUNK, RWKV_WIDTH
    c_idx = pl.program_id(1)

    @pl.when(c_idx == 0)
    def _():
        state_ref[...] = jnp.zeros_like(state_ref)

    p = p_ref[0]
    row = lax.broadcasted_iota(jnp.int32, p.shape, 0)
    prev_row = jnp.where(c_idx == 0, 0.0, prev_ref[0, 7:8, :])
    shifted = jnp.where(row == 0, prev_row, pltpu.roll(p, 1, axis=0))
    p = p + (shifted - p) * mu_ref[...]
    r, k, v = p[:, 0:W], p[:, W:2 * W], p[:, 2 * W:3 * W]
    o = 3 * W
    xw = p[:, o:o + RWKV_W_RANK]
    xa = p[:, o + RWKV_W_RANK:o + RWKV_W_RANK + RWKV_A_RANK]
    xg = p[:, o + RWKV_W_RANK + RWKV_A_RANK:]
    dotd = lambda u, m: jnp.dot(u.astype(bf16), m, preferred_element_type=f32)
    lw = -jnp.exp(-_softplus(-(w0_ref[...] + dotd(jnp.tanh(xw), wup_ref[...]))) - 0.5)
    a = jax.nn.sigmoid(a0_ref[...] + dotd(xa, aup_ref[...]))
    g = dotd(jax.nn.sigmoid(xg), gup_ref[...])
    ones_bd = ones_ref[...]
    head_sum = lambda t: _mm_exact_rhs(t, ones_bd, _NN)
    kk = k * kk_ref[...]
    kk = kk / jnp.maximum(jnp.sqrt(head_sum(kk * kk)), 1e-12)
    k = k * (1.0 + (a - 1.0) * ka_ref[...])

    lane = lax.broadcasted_iota(jnp.int32, (L, LANES), 1)
    trow = lax.broadcasted_iota(jnp.int32, (L, LANES), 0)
    s_in = lane % RWKV_HEAD
    strict, incl = s_in < trow, s_in <= trow
    eye_pair = (s_in == trow).astype(f32)
    m0 = lane < RWKV_HEAD
    tril = (lax.broadcasted_iota(jnp.int32, (L, L), 1)
            <= lax.broadcasted_iota(jnp.int32, (L, L), 0)).astype(bf16)
    r2 = lax.broadcasted_iota(jnp.int32, (LANES, LANES), 0) // RWKV_HEAD
    c2 = lax.broadcasted_iota(jnp.int32, (LANES, LANES), 1) // RWKV_HEAD
    bd_mask = r2 == c2

    def bd(t):
        return jnp.concatenate([jnp.where(m0, t, 0.0), jnp.where(m0, 0.0, t)], axis=0)

    pairs = range(RWKV_PAIRS)
    cut = lambda t: [t[:, pr * LANES:(pr + 1) * LANES] for pr in pairs]
    rp, kp, vp, kkp, ap, lwp = cut(r), cut(k), cut(v), cut(kk), cut(a), cut(lw)
    c = [_mm_exact_lhs(tril, lwp[p], _NN) for p in pairs]
    at = [-kkp[p] * jnp.exp(c[p] - lwp[p]) for p in pairs]
    eni = [jnp.exp(-c[p]) for p in pairs]
    bt = [kkp[p] * ap[p] * eni[p] for p in pairs]
    kt = [kp[p] * eni[p] for p in pairs]
    rt = [rp[p] * jnp.exp(c[p]) for p in pairs]
    g_last = [jnp.exp(c[p][L - 1:L, :]) for p in pairs]
    gram = [_mm2(jnp.concatenate([at[p], rt[p]], axis=0),
                 jnp.concatenate([bd(bt[p]), bd(kt[p])], axis=0), _NT) for p in pairs]
    n_ab = [jnp.where(strict, gram[p][0:L, 0:LANES], 0.0) for p in pairs]
    a_ak = [jnp.where(strict, gram[p][0:L, LANES:], 0.0) for p in pairs]
    a_rb = [jnp.where(incl, gram[p][L:, 0:LANES], 0.0) for p in pairs]
    a_rk = [jnp.where(incl, gram[p][L:, LANES:], 0.0) for p in pairs]
    tinv, m = [eye_pair + n_ab[p] for p in pairs], n_ab
    for _ in range(5):
        m = [_mm2(m[p], bd(m[p]), _NN) for p in pairs]
        tinv = [tinv[p] + _mm2(m[p], bd(tinv[p]), _NN) for p in pairs]
    s0 = [state_ref[p] for p in pairs]
    pq = [_mm2(at[p], s0[p], _NT) + _mm2(a_ak[p], bd(vp[p]), _NN) for p in pairs]
    u = [_mm2(tinv[p], bd(pq[p]), _NN) for p in pairs]
    ys = [_mm2(rt[p], s0[p], _NT)
          + _mm2(jnp.concatenate([a_rb[p], a_rk[p]], axis=1),
                 jnp.concatenate([bd(u[p]), bd(vp[p])], axis=0), _NN) for p in pairs]
    for p in pairs:
        upd = _mm2(jnp.concatenate([u[p], vp[p]], axis=0),
                   jnp.concatenate([bt[p], kt[p]], axis=0), _TN)
        state_ref[p] = (s0[p] + jnp.where(bd_mask, upd, 0.0)) * g_last[p]
    y = jnp.concatenate(ys, axis=1)

    inv_n = 1.0 / RWKV_HEAD
    mean = head_sum(y) * inv_n
    yc = y - mean
    var = head_sum(yc * yc) * inv_n
    y = yc * lax.rsqrt(var + RWKV_GN_EPS) * lng_ref[...] + lnb_ref[...]
    y = y + head_sum(r * k * rk_ref[...]) * v
    o_ref[0] = y * g


def _rwkv_pallas(p_r, mu, w0, w_up, a0, a_up, g_up, k_k, k_a, r_k, ln_g, ln_b):
    B, S, C = p_r.shape
    L, W = RWKV_CHUNK, RWKV_WIDTH
    bf16 = jnp.bfloat16
    vecw = lambda t: t.reshape(1, W)
    ones_bd = jnp.kron(jnp.eye(RWKV_HEADS, dtype=bf16), jnp.ones((RWKV_HEAD, RWKV_HEAD), bf16))
    full = lambda a: pl.BlockSpec(a.shape, lambda b, c: (0,) * a.ndim)
    args = [mu.reshape(1, C), vecw(w0), w_up.astype(bf16), vecw(a0), a_up.astype(bf16),
            g_up.astype(bf16), vecw(k_k), vecw(k_a), vecw(r_k), vecw(ln_g), vecw(ln_b), ones_bd]
    return pl.pallas_call(
        _rwkv_kernel,
        grid=(B, S // L),
        in_specs=[pl.BlockSpec((1, L, C), lambda b, c: (b, c, 0)),
                  pl.BlockSpec((1, 8, C), lambda b, c: (b, jnp.maximum(c * (L // 8) - 1, 0), 0))]
                 + [full(a) for a in args],
        out_specs=pl.BlockSpec((1, L, W), lambda b, c: (b, c, 0)),
        out_shape=jax.ShapeDtypeStruct((B, S, W), jnp.float32),
        scratch_shapes=[pltpu.VMEM((RWKV_PAIRS, LANES, LANES), jnp.float32)],
        compiler_params=pltpu.CompilerParams(dimension_semantics=("arbitrary", "arbitrary")),
        name="rwkv7",
    )(p_r, p_r, *args)


MLSTM_CONV = 4
MLSTM_GATE_LANES = LANES


def _mlstm_kernel(p_ref, prev_ref, cw_ref, cb_ref, igb_ref, fgb_ref, hng_ref, hnb_ref,
                  o_ref, c_ref, n_ref, m_ref):
    f32, bf16 = jnp.float32, jnp.bfloat16
    L, W, H, D = MLSTM_CHUNK, MLSTM_WIDTH, MLSTM_HEADS, MLSTM_HEAD
    c_idx = pl.program_id(1)

    @pl.when(c_idx == 0)
    def _():
        c_ref[...] = jnp.zeros_like(c_ref)
        n_ref[...] = jnp.zeros_like(n_ref)
        m_ref[...] = jnp.zeros_like(m_ref)

    p = p_ref[0]
    z = p[:, 0:2 * W]
    prev = jnp.where(c_idx == 0, 0.0, prev_ref[0, :, 0:2 * W])
    ext = jnp.concatenate([prev, z], axis=0)
    conv = cb_ref[...] + cw_ref[MLSTM_CONV - 1:MLSTM_CONV, :] * z
    for j in range(MLSTM_CONV - 1):
        d = MLSTM_CONV - 1 - j
        conv = conv + cw_ref[j:j + 1, :] * ext[8 - d:8 - d + L, :]
    qk = conv * jax.nn.sigmoid(conv)
    q_all, k_all = qk[:, 0:W] * (D ** -0.5), qk[:, W:2 * W]
    v_all, o_all = p[:, 2 * W:3 * W], p[:, 3 * W:4 * W]
    gates = p[:, 4 * W:]
    igl = gates + igb_ref[...]
    lfl = -_softplus(-(gates + fgb_ref[...]))
    tril = (lax.broadcasted_iota(jnp.int32, (L, L), 1)
            <= lax.broadcasted_iota(jnp.int32, (L, L), 0))
    bcum = _mm_exact_lhs(tril.astype(bf16), lfl, _NN)
    e_all = igl - pltpu.roll(bcum, LANES - H, axis=1)
    lane = lax.broadcasted_iota(jnp.int32, (L, LANES), 1)
    hd = range(H)
    cols = lambda t: [t[:, h * D:(h + 1) * D] for h in hd]
    q, k, v, o = cols(q_all), cols(k_all), cols(v_all), cols(o_all)
    qb, kb, vb = ([t.astype(bf16) for t in ts] for ts in (q, k, v))
    b_col = [bcum[:, H + h:H + h + 1] for h in hd]
    ig_col = [igl[:, h:h + 1] for h in hd]
    e_row = [_mm_exact_lhs((lane == h).astype(bf16), e_all, _NT) for h in hd]
    qk = [lax.dot_general(qb[h], kb[h], _NT, preferred_element_type=f32) for h in hd]
    m_prev = [m_ref[h, 0:1, 0:1] for h in hd]
    c_prev = [c_ref[h] for h in hd]
    n_prev = [n_ref[h, 0:1, :] for h in hd]
    qc = [jnp.dot(qb[h], c_prev[h].astype(bf16), preferred_element_type=f32) for h in hd]
    dmat = [jnp.where(tril, b_col[h] + e_row[h], -jnp.inf) for h in hd]
    inter = [b_col[h] + m_prev[h] for h in hd]
    m_t = [jnp.maximum(inter[h], jnp.max(dmat[h], axis=-1, keepdims=True)) for h in hd]
    sc = [qk[h] * jnp.exp(dmat[h] - m_t[h]) for h in hd]
    carry_in = [jnp.exp(inter[h] - m_t[h]) for h in hd]
    sv = [jnp.dot(sc[h].astype(bf16), vb[h], preferred_element_type=f32) for h in hd]
    den = [jnp.sum(sc[h], axis=-1, keepdims=True)
           + carry_in[h] * jnp.sum(q[h] * n_prev[h], axis=-1, keepdims=True) for h in hd]
    hval = [(sv[h] + carry_in[h] * qc[h]) / jnp.maximum(jnp.abs(den[h]), jnp.exp(-m_t[h])) for h in hd]
    b_last = [b_col[h][L - 1:L, :] for h in hd]
    gs = [b_last[h] - b_col[h] + ig_col[h] for h in hd]
    m_new = [jnp.maximum(b_last[h] + m_prev[h], jnp.max(gs[h], axis=0, keepdims=True)) for h in hd]
    keep = [jnp.exp(b_last[h] + m_prev[h] - m_new[h]) for h in hd]
    wk = [jnp.exp(gs[h] - m_new[h]) * k[h] for h in hd]
    kv = [lax.dot_general(wk[h].astype(bf16), vb[h], _TN, preferred_element_type=f32) for h in hd]
    outs = []
    for h in hd:
        c_ref[h] = keep[h] * c_prev[h] + kv[h]
        n_ref[h] = jnp.broadcast_to(keep[h] * n_prev[h] + jnp.sum(wk[h], axis=0, keepdims=True), (8, D))
        m_ref[h] = jnp.broadcast_to(m_new[h], (8, LANES))
        mu = jnp.mean(hval[h], axis=-1, keepdims=True)
        hc = hval[h] - mu
        var = jnp.mean(hc * hc, axis=-1, keepdims=True)
        hs = slice(h * D, (h + 1) * D)
        hn = hc * lax.rsqrt(var + LN_EPS) * hng_ref[:, hs] + hnb_ref[:, hs]
        outs.append(jax.nn.sigmoid(o[h]) * hn)
    o_ref[0] = jnp.concatenate(outs, axis=1)


def _mlstm_pallas(p_m, conv_w, conv_b, ig_b, fg_b, hn_g, hn_b):
    B, S, C = p_m.shape
    L, W, H, D = MLSTM_CHUNK, MLSTM_WIDTH, MLSTM_HEADS, MLSTM_HEAD
    pad = lambda t, off: jnp.zeros((1, LANES), jnp.float32).at[0, off:off + H].set(t)
    args = [conv_w, conv_b.reshape(1, 2 * W), pad(ig_b, 0), pad(fg_b, H),
            hn_g.reshape(1, W), hn_b.reshape(1, W)]
    full = lambda a: pl.BlockSpec(a.shape, lambda b, c: (0,) * a.ndim)
    return pl.pallas_call(
        _mlstm_kernel,
        grid=(B, S // L),
        in_specs=[pl.BlockSpec((1, L, C), lambda b, c: (b, c, 0)),
                  pl.BlockSpec((1, 8, C), lambda b, c: (b, jnp.maximum(c * (L // 8) - 1, 0), 0))]
                 + [full(a) for a in args],
        out_specs=pl.BlockSpec((1, L, W), lambda b, c: (b, c, 0)),
        out_shape=jax.ShapeDtypeStruct((B, S, W), jnp.float32),
        scratch_shapes=[pltpu.VMEM((H, D, D), jnp.float32),
                        pltpu.VMEM((H, 8, D), jnp.float32),
                        pltpu.VMEM((H, 8, LANES), jnp.float32)],
        compiler_params=pltpu.CompilerParams(dimension_semantics=("arbitrary", "arbitrary")),
        name="mlstm",
    )(p_m, p_m, *args)


FOX_TILE = 256
FOX_KEY_GROUP = 4
FOX_HEADS_PER_STEP = 8
FOX_AUG = LANES
FOX_GATE_LANES = LANES


def _fox_prep_kernel(p_ref, qg_ref, kg_ref, fb_ref, ones_ref, qa_ref, ka_ref, vt_ref, carry_ref):
    f32, bf16 = jnp.float32, jnp.bfloat16
    T, W, H, D = FOX_TILE, FOX_WIDTH, FOX_HEADS, FOX_HEAD
    i = pl.program_id(1)

    @pl.when(i == 0)
    def _():
        carry_ref[...] = jnp.zeros_like(carry_ref)

    p = p_ref[0]
    ones_bd = ones_ref[...]
    ms = lambda t: _mm_exact_rhs(t * t, ones_bd, _NN) * (1.0 / D)
    q = p[:, 0:W]
    q = q * lax.rsqrt(ms(q) + 1e-6) * qg_ref[...]
    k = p[:, W:2 * W]
    k = k * lax.rsqrt(ms(k) + 1e-6) * kg_ref[...] * (D ** -0.5)
    vt = p[:, 2 * W:3 * W].T.astype(bf16)
    logf = -_softplus(-(p[:, 4 * W:] + fb_ref[...]))
    tril = (lax.broadcasted_iota(jnp.int32, (T, T), 1)
            <= lax.broadcasted_iota(jnp.int32, (T, T), 0)).astype(bf16)
    c = _mm_exact_lhs(tril, logf, _NN) + carry_ref[0:1, :]
    carry_ref[...] = jnp.broadcast_to(c[T - 1:T, :], carry_ref.shape)
    c1 = c.astype(bf16).astype(f32)
    c2 = (c - c1).astype(bf16).astype(f32)
    c3 = (c - c1 - c2).astype(bf16).astype(f32)
    lane = lax.broadcasted_iota(jnp.int32, (T, D), 1)
    for h in range(H):
        hs = slice(h * D, (h + 1) * D)
        c1h, c2h, c3h = c1[:, h:h + 1], c2[:, h:h + 1], c3[:, h:h + 1]
        pieces = jnp.where(lane % 3 == 0, c1h, jnp.where(lane % 3 == 1, c2h, c3h))
        q_aug = jnp.where(lane < 3, 1.0, jnp.where(lane < 6, pieces, 0.0))
        k_aug = jnp.where(lane < 3, -pieces, jnp.where(lane < 6, 1.0, 0.0))
        qa_ref[0, h] = jnp.concatenate([q[:, hs], q_aug], axis=1).astype(bf16)
        ka_ref[0, h] = jnp.concatenate([k[:, hs], k_aug], axis=1).astype(bf16)
        vt_ref[0, h, 0] = vt[hs, :]


def _fox_prep(p_f, qn_g, kn_g, f_b):
    B, S, C = p_f.shape
    T, W, H, D = FOX_TILE, FOX_WIDTH, FOX_HEADS, FOX_HEAD
    bf16 = jnp.bfloat16
    ones_bd = jnp.kron(jnp.eye(H, dtype=bf16), jnp.ones((D, D), bf16))
    fb = jnp.zeros((1, FOX_GATE_LANES), jnp.float32).at[0, :H].set(f_b)
    args = [qn_g.reshape(1, W), kn_g.reshape(1, W), fb, ones_bd]
    full = lambda a: pl.BlockSpec(a.shape, lambda b, i: (0,) * a.ndim)
    return pl.pallas_call(
        _fox_prep_kernel,
        grid=(B, S // T),
        in_specs=[pl.BlockSpec((1, T, C), lambda b, i: (b, i, 0))] + [full(a) for a in args],
        out_specs=[pl.BlockSpec((1, H, T, FOX_AUG), lambda b, i: (b, 0, i, 0)),
                   pl.BlockSpec((1, H, T, FOX_AUG), lambda b, i: (b, 0, i, 0)),
                   pl.BlockSpec((1, H, 1, D, T), lambda b, i: (b, 0, i, 0, 0))],
        out_shape=[jax.ShapeDtypeStruct((B, H, S, FOX_AUG), bf16),
                   jax.ShapeDtypeStruct((B, H, S, FOX_AUG), bf16),
                   jax.ShapeDtypeStruct((B, H, S // T, D, T), bf16)],
        scratch_shapes=[pltpu.VMEM((8, FOX_GATE_LANES), jnp.float32)],
        compiler_params=pltpu.CompilerParams(dimension_semantics=("arbitrary", "arbitrary"),
                                             vmem_limit_bytes=VMEM_LIMIT_BYTES),
        name="fox_prep",
    )(p_f, *args)


def _fox_attn_kernel(qa_ref, ka_ref, vt_ref, og_ref, o_ref, m_ref, l_ref, acc_ref):
    f32, bf16 = jnp.float32, jnp.bfloat16
    T, D = FOX_TILE, FOX_HEAD
    i = pl.program_id(2)
    heads = range(FOX_HEADS_PER_STEP)
    n_blocks = ka_ref.shape[2] // T
    GK = FOX_KEY_GROUP
    rel = (lax.broadcasted_iota(jnp.int32, (GK * T, T), 0)
           - lax.broadcasted_iota(jnp.int32, (GK * T, T), 1))
    qas = [qa_ref[0, j] for j in heads]

    def key_group(g, first):
        visible = rel <= (i - g * GK) * T

        def scores(j):
            ka = ka_ref[0, j, g * GK * T:(g + 1) * GK * T, :]
            return lax.dot_general(ka, qas[j], _NT, preferred_element_type=f32)

        nxt = scores(0)
        for j in heads:
            s = nxt
            if j + 1 < len(heads):
                nxt = scores(j + 1)
            s = jnp.where(visible, s, -jnp.inf)
            m_g = jnp.max(s, axis=0, keepdims=True)
            m_new = m_g if first else jnp.maximum(m_ref[j], m_g)
            pr = jnp.exp(s - m_new)
            l_g = jnp.sum(pr, axis=0, keepdims=True)
            vt = jnp.concatenate([vt_ref[0, j, g * GK + b] for b in range(GK)], axis=1)
            pv = jnp.dot(vt, pr.astype(bf16), preferred_element_type=f32)
            if first:
                l_ref[j], acc_ref[j] = l_g, pv
            else:
                alpha = jnp.exp(m_ref[j] - m_new)
                l_ref[j] = l_ref[j] * alpha + l_g
                acc_ref[j] = acc_ref[j] * alpha + pv
            m_ref[j] = m_new

    key_group(0, True)
    for g in range(1, n_blocks // GK):
        pl.when(i >= g * GK)(functools.partial(key_group, g, False))
    outs = [(acc_ref[j] / l_ref[j]).T for j in heads]
    o_ref[0] = jax.nn.sigmoid(og_ref[0]) * jnp.concatenate(outs, axis=1)


def _fox_attn(qa, ka, vt, p_f):
    B, H, S, A = qa.shape
    T, D = FOX_TILE, FOX_HEAD
    hp = FOX_HEADS_PER_STEP
    width = hp * D
    og_block0 = 3 * FOX_WIDTH // width
    return pl.pallas_call(
        _fox_attn_kernel,
        grid=(B, H // hp, S // T),
        in_specs=[pl.BlockSpec((1, hp, T, A), lambda b, g, i: (b, g, i, 0)),
                  pl.BlockSpec((1, hp, S, A), lambda b, g, i: (b, g, 0, 0)),
                  pl.BlockSpec((1, hp, S // T, D, T), lambda b, g, i: (b, g, 0, 0, 0)),
                  pl.BlockSpec((1, T, width), lambda b, g, i: (b, i, og_block0 + g))],
        out_specs=pl.BlockSpec((1, T, width), lambda b, g, i: (b, i, g)),
        out_shape=jax.ShapeDtypeStruct((B, S, FOX_WIDTH), jnp.float32),
        scratch_shapes=[pltpu.VMEM((hp, 1, T), jnp.float32), pltpu.VMEM((hp, 1, T), jnp.float32),
                        pltpu.VMEM((hp, D, T), jnp.float32)],
        name="fox_attn",
    )(qa, ka, vt, p_f)


def kernel(x, l0_w_in, l0_rwkv_mu, l0_rwkv_w0, l0_rwkv_w_up, l0_rwkv_a0, l0_rwkv_a_up,
           l0_rwkv_g_up, l0_rwkv_k_k, l0_rwkv_k_a, l0_rwkv_r_k, l0_rwkv_ln_g, l0_rwkv_ln_b,
           l0_mlstm_conv_w, l0_mlstm_conv_b, l0_mlstm_ig_b, l0_mlstm_fg_b,
           l0_mlstm_hn_g, l0_mlstm_hn_b, l0_w_out, l0_ln1_g, l0_ln1_b,
           l0_peer_wq, l0_peer_k1, l0_peer_k2, l0_peer_u, l0_peer_v, l0_ln2_g, l0_ln2_b,
           l1_w_in, l1_fox_qn_g, l1_fox_kn_g, l1_fox_f_b, l1_w_out, l1_ln1_g, l1_ln1_b,
           l1_peer_wq, l1_peer_k1, l1_peer_k2, l1_peer_u, l1_peer_v, l1_ln2_g, l1_ln2_b):
    B, S, D = x.shape
    n = B * S
    bf16 = jnp.bfloat16
    x2 = x.reshape(n, D)

    def pad_lanes(w):
        return jnp.pad(w, ((0, 0), (0, LANES - w.shape[1])))

    m_main = 4 * MLSTM_WIDTH
    w0 = jnp.concatenate([l0_w_in[:, :RWKV_COLS + m_main],
                          pad_lanes(l0_w_in[:, RWKV_COLS + m_main:])], axis=1).astype(bf16)
    p_r, p_m = _proj(x2, w0, (RWKV_COLS, m_main + LANES))
    y_a = _rwkv_pallas(p_r.reshape(B, S, -1), l0_rwkv_mu, l0_rwkv_w0, l0_rwkv_w_up, l0_rwkv_a0,
                       l0_rwkv_a_up, l0_rwkv_g_up, l0_rwkv_k_k, l0_rwkv_k_a, l0_rwkv_r_k,
                       l0_rwkv_ln_g, l0_rwkv_ln_b)
    y_b = _mlstm_pallas(p_m.reshape(B, S, -1), l0_mlstm_conv_w, l0_mlstm_conv_b, l0_mlstm_ig_b,
                        l0_mlstm_fg_b, l0_mlstm_hn_g, l0_mlstm_hn_b)
    x2, xt = _out_proj_ln([y_a.reshape(n, -1), y_b.reshape(n, -1)], l0_w_out.astype(bf16), x2,
                          l0_ln1_g, l0_ln1_b)
    y = _peer_pallas(x2, xt, l0_peer_wq, l0_peer_k1, l0_peer_k2, l0_peer_u, l0_peer_v)
    x2 = _resid_ln(x2, y, l0_ln2_g, l0_ln2_b)

    f_main = 4 * FOX_WIDTH
    w1 = jnp.concatenate([l1_w_in[:, :f_main], pad_lanes(l1_w_in[:, f_main:])], axis=1).astype(bf16)
    (p_f,) = _proj(x2, w1, (f_main + LANES,))
    p_f = p_f.reshape(B, S, -1)
    qa, ka, vt = _fox_prep(p_f, l1_fox_qn_g, l1_fox_kn_g, l1_fox_f_b)
    o = _fox_attn(qa, ka, vt, p_f)
    x2, xt = _out_proj_ln([o.reshape(n, -1)], l1_w_out.astype(bf16), x2, l1_ln1_g, l1_ln1_b)
    y = _peer_pallas(x2, xt, l1_peer_wq, l1_peer_k1, l1_peer_k2, l1_peer_u, l1_peer_v)
    x2 = _resid_ln(x2, y, l1_ln2_g, l1_ln2_b)
    return x2.reshape(B, S, D)
```

```python
import functools

import jax
import jax.numpy as jnp
from jax import lax
from jax.experimental import pallas as pl
from jax.experimental.pallas import tpu as pltpu

D_MODEL = 1024
DEPTH = 2
DN_ALPHA = (2.0 * DEPTH) ** 0.25
LN_EPS = 1e-5

RWKV_WIDTH = D_MODEL // 2
RWKV_HEAD = 64
RWKV_HEADS = RWKV_WIDTH // RWKV_HEAD
RWKV_W_RANK = 64
RWKV_A_RANK = 64
RWKV_G_RANK = 128
RWKV_GN_EPS = 1e-5 * RWKV_HEAD
RWKV_COLS = 3 * RWKV_WIDTH + RWKV_W_RANK + RWKV_A_RANK + RWKV_G_RANK

MLSTM_WIDTH = D_MODEL // 2
MLSTM_HEAD = 128
MLSTM_HEADS = MLSTM_WIDTH // MLSTM_HEAD
MLSTM_CHUNK = 64

FOX_HEAD = 64
FOX_HEADS = D_MODEL // FOX_HEAD
FOX_WIDTH = FOX_HEADS * FOX_HEAD
FOX_QBLOCK = 128

PEER_HEADS = 8
PEER_NKEYS = 128
PEER_TOPK = 16
PEER_DKEY = 256
PEER_DHALF = PEER_DKEY // 2
PEER_TOKBLOCK = 128


def _split_cols(p, sizes):
    out, start = [], 0
    for s in sizes:
        out.append(p[..., start:start + s])
        start += s
    return out


def _resid_ln_kernel(x_ref, y_ref, g_ref, b_ref, o_ref):
    z = DN_ALPHA * x_ref[...] + y_ref[...]
    mu = jnp.mean(z, axis=-1, keepdims=True)
    zc = z - mu
    var = jnp.mean(zc * zc, axis=-1, keepdims=True)
    o_ref[...] = zc * lax.rsqrt(var + LN_EPS) * g_ref[...] + b_ref[...]


def _resid_ln(x2, y2, g, b, tm=512):
    n, d = x2.shape
    row = pl.BlockSpec((tm, d), lambda i: (i, 0))
    vec = pl.BlockSpec((1, d), lambda i: (0, 0))
    return pl.pallas_call(
        _resid_ln_kernel,
        grid=(n // tm,),
        in_specs=[row, row, vec, vec],
        out_specs=row,
        out_shape=jax.ShapeDtypeStruct((n, d), jnp.float32),
        name="resid_ln",
    )(x2, y2, g.reshape(1, d), b.reshape(1, d))


VMEM_LIMIT_BYTES = 56 * 1024 * 1024


def _proj_kernel(x_ref, w_ref, *o_refs):
    xb = x_ref[...].astype(jnp.bfloat16)
    start = 0
    for o_ref in o_refs:
        width = o_ref.shape[1]
        o_ref[...] = jnp.dot(xb, w_ref[:, start:start + width], preferred_element_type=jnp.float32)
        start += width


def _proj(x2, w_bf16, widths, tm=256):
    n, d = x2.shape
    assert sum(widths) == w_bf16.shape[1]
    return pl.pallas_call(
        _proj_kernel,
        grid=(n // tm,),
        in_specs=[pl.BlockSpec((tm, d), lambda i: (i, 0)),
                  pl.BlockSpec(w_bf16.shape, lambda i: (0, 0))],
        out_specs=[pl.BlockSpec((tm, w), lambda i: (i, 0)) for w in widths],
        out_shape=[jax.ShapeDtypeStruct((n, w), jnp.float32) for w in widths],
        compiler_params=pltpu.CompilerParams(vmem_limit_bytes=VMEM_LIMIT_BYTES),
        name="in_proj",
    )(x2, w_bf16)


def _out_proj_ln_kernel(*refs, n_parts):
    y_refs, (w_ref, x_ref, g_ref, b_ref, o_ref, ot_ref) = refs[:n_parts], refs[n_parts:]
    acc, start = None, 0
    for y_ref in y_refs:
        width = y_ref.shape[1]
        part = jnp.dot(y_ref[...].astype(jnp.bfloat16), w_ref[start:start + width, :],
                       preferred_element_type=jnp.float32)
        acc = part if acc is None else acc + part
        start += width
    z = DN_ALPHA * x_ref[...] + acc
    mu = jnp.mean(z, axis=-1, keepdims=True)
    zc = z - mu
    var = jnp.mean(zc * zc, axis=-1, keepdims=True)
    out = zc * lax.rsqrt(var + LN_EPS) * g_ref[...] + b_ref[...]
    o_ref[...] = out
    ot_ref[...] = out.T.astype(jnp.bfloat16)


def _out_proj_ln(ys, w_bf16, x2, g, b, tm=256):
    n, d = x2.shape
    row = lambda width: pl.BlockSpec((tm, width), lambda i: (i, 0))
    vec = pl.BlockSpec((1, d), lambda i: (0, 0))
    return pl.pallas_call(
        functools.partial(_out_proj_ln_kernel, n_parts=len(ys)),
        grid=(n // tm,),
        in_specs=[row(y.shape[1]) for y in ys]
                 + [pl.BlockSpec(w_bf16.shape, lambda i: (0, 0)), row(d), vec, vec],
        out_specs=[row(d), pl.BlockSpec((d, tm), lambda i: (0, i))],
        out_shape=[jax.ShapeDtypeStruct((n, d), jnp.float32), jax.ShapeDtypeStruct((d, n), jnp.bfloat16)],
        name="out_proj_ln",
    )(*ys, w_bf16, x2, g.reshape(1, d), b.reshape(1, d))


def _head_norm(y, g, b, eps):
    mu = jnp.mean(y, -1, keepdims=True)
    var = jnp.mean(jnp.square(y - mu), -1, keepdims=True)
    return (y - mu) * lax.rsqrt(var + eps) * g + b


def _rms_norm(y, g):
    return y * lax.rsqrt(jnp.mean(y * y, -1, keepdims=True) + 1e-6) * g


def _token_shift(z):
    return jnp.pad(z, ((0, 0), (1, 0), (0, 0)))[:, :-1]


def _causal_conv(z, w, b):
    c = z.shape[-1]
    out = lax.conv_general_dilated(z, w[:, None, :], window_strides=(1,),
                                   padding=((w.shape[0] - 1, 0),),
                                   dimension_numbers=('NWC', 'WIO', 'NWC'),
                                   feature_group_count=c)
    return out + b


def _rwkv7_mix(p, mu, w0, w_up, a0, a_up, g_up, k_k, k_a, r_k, ln_g, ln_b):
    B, S, _ = p.shape
    H, N = RWKV_HEADS, RWKV_HEAD
    p = p + (_token_shift(p) - p) * mu
    r, k, v, xw, xa, xg = _split_cols(p, (RWKV_WIDTH, RWKV_WIDTH, RWKV_WIDTH,
                                          RWKV_W_RANK, RWKV_A_RANK, RWKV_G_RANK))
    log_w = -jnp.exp(-jax.nn.softplus(-(w0 + jnp.tanh(xw) @ w_up)) - 0.5)
    a = jax.nn.sigmoid(a0 + xa @ a_up)
    g = jax.nn.sigmoid(xg) @ g_up
    heads = lambda t: t.reshape(B, S, H, N)
    kk = heads(k * k_k)
    kk = kk / jnp.maximum(jnp.sqrt(jnp.sum(kk * kk, -1, keepdims=True)), 1e-12)
    k = k * (1.0 + (a - 1.0) * k_a)
    r_h, k_h, v_h, a_h = heads(r), heads(k), heads(v), heads(a)
    w_h = jnp.exp(heads(log_w))

    def step(state, inp):
        r_t, w_t, k_t, v_t, kk_t, a_t = inp
        s_kk = jnp.einsum('bhvk,bhk->bhv', state, kk_t)
        state = (state * w_t[:, :, None, :]
                 - s_kk[..., None] * (kk_t * a_t)[:, :, None, :]
                 + v_t[..., None] * k_t[:, :, None, :])
        return state, jnp.einsum('bhvk,bhk->bhv', state, r_t)

    seq_first = lambda t: jnp.moveaxis(t, 1, 0)
    state0 = jnp.zeros((B, H, N, N), jnp.float32)
    _, y = lax.scan(step, state0, (seq_first(r_h), seq_first(w_h), seq_first(k_h),
                                   seq_first(v_h), seq_first(kk), seq_first(a_h)))
    y = jnp.moveaxis(y, 0, 1)
    y = _head_norm(y, ln_g, ln_b, RWKV_GN_EPS)
    y = y + jnp.sum(r_h * k_h * r_k, -1, keepdims=True) * v_h
    return y.reshape(B, S, RWKV_WIDTH) * g


def _mlstm_chunkwise(q, k, v, ig, lf):
    B, S, H, D = q.shape
    L = MLSTM_CHUNK
    NC = S // L

    def to_chunks(t):
        t = t.reshape((B, NC, L, H) + t.shape[3:])
        return jnp.moveaxis(t, (1, 3), (0, 2))

    causal = jnp.tril(jnp.ones((L, L), dtype=bool))

    def body(carry, inp):
        C, n, m = carry
        qc, kc, vc, igc, lfc = inp
        b = jnp.cumsum(lfc, axis=-1)
        dmat = jnp.where(causal, b[..., :, None] - b[..., None, :] + igc[..., None, :], -jnp.inf)
        inter = b + m[..., None]
        m_t = jnp.maximum(inter, jnp.max(dmat, -1))
        weights = jnp.exp(dmat - m_t[..., None])
        sc = jnp.einsum('bhtd,bhsd->bhts', qc, kc) * weights
        carry_in = jnp.exp(inter - m_t)
        num = (jnp.einsum('bhts,bhsd->bhtd', sc, vc)
               + carry_in[..., None] * jnp.einsum('bhtk,bhkv->bhtv', qc, C))
        den = jnp.sum(sc, -1) + carry_in * jnp.einsum('bhtk,bhk->bht', qc, n)
        h = num / jnp.maximum(jnp.abs(den), jnp.exp(-m_t))[..., None]
        b_last = b[..., -1]
        gs = b_last[..., None] - b + igc
        m_new = jnp.maximum(b_last + m, jnp.max(gs, -1))
        ws = jnp.exp(gs - m_new[..., None])
        keep = jnp.exp(b_last + m - m_new)
        C = keep[..., None, None] * C + jnp.einsum('bhs,bhsk,bhsv->bhkv', ws, kc, vc)
        n = keep[..., None] * n + jnp.einsum('bhs,bhsk->bhk', ws, kc)
        return (C, n, m_new), h

    f32 = jnp.float32
    init = (jnp.zeros((B, H, D, D), f32), jnp.zeros((B, H, D), f32), jnp.zeros((B, H), f32))
    _, h = lax.scan(body, init, (to_chunks(q), to_chunks(k), to_chunks(v),
                                 to_chunks(ig), to_chunks(lf)))
    return jnp.moveaxis(h, (0, 2), (1, 3)).reshape(B, S, H, D)


def _mlstm_mix(p, conv_w, conv_b, ig_b, fg_b, hn_g, hn_b):
    B, S, _ = p.shape
    H, N = MLSTM_HEADS, MLSTM_HEAD
    q, k, v, o, ig, fg = _split_cols(p, (MLSTM_WIDTH, MLSTM_WIDTH, MLSTM_WIDTH, MLSTM_WIDTH, H, H))
    qk = jax.nn.silu(_causal_conv(jnp.concatenate([q, k], -1), conv_w, conv_b))
    q, k = qk[..., :MLSTM_WIDTH], qk[..., MLSTM_WIDTH:]
    heads = lambda t: t.reshape(B, S, H, N)
    q = heads(q) * (N ** -0.5)
    ig = ig + ig_b
    lf = jax.nn.log_sigmoid(fg + fg_b)
    h = _mlstm_chunkwise(q, heads(k), heads(v), ig, lf)
    h = _head_norm(h, hn_g, hn_b, LN_EPS).reshape(B, S, MLSTM_WIDTH)
    return jax.nn.sigmoid(o) * h


def _even_mixer(x, w_in, mu, w0, w_up, a0, a_up, g_up, k_k, k_a, r_k, rln_g, rln_b,
                conv_w, conv_b, ig_b, fg_b, hn_g, hn_b, w_out):
    p = x @ w_in
    y_a = _rwkv_pallas(p[..., :RWKV_COLS], mu, w0, w_up, a0, a_up, g_up, k_k, k_a, r_k, rln_g, rln_b)
    y_b = _mlstm_mix(p[..., RWKV_COLS:], conv_w, conv_b, ig_b, fg_b, hn_g, hn_b)
    return jnp.concatenate([y_a, y_b], -1) @ w_out


def _fox_attention(q, k, v, logf):
    B, S, H, D = q.shape
    NB = S // FOX_QBLOCK
    c = jnp.cumsum(logf, axis=1)
    c_key = jnp.transpose(c, (0, 2, 1))[:, :, None, :]
    q_blocks = jnp.moveaxis(q.reshape(B, NB, FOX_QBLOCK, H, D), 1, 0)
    c_blocks = jnp.moveaxis(c.reshape(B, NB, FOX_QBLOCK, H), 1, 0)
    key_pos = jnp.arange(S)
    scale = D ** -0.5

    def one_block(args):
        qb, cb, blk = args
        q_pos = blk * FOX_QBLOCK + jnp.arange(FOX_QBLOCK)
        logits = jnp.einsum('bqhd,bkhd->bhqk', qb, k) * scale
        logits = logits + jnp.transpose(cb, (0, 2, 1))[..., None] - c_key
        logits = jnp.where(key_pos[None, :] <= q_pos[:, None], logits, -jnp.inf)
        probs = jax.nn.softmax(logits, axis=-1)
        return jnp.einsum('bhqk,bkhd->bqhd', probs, v)

    out = lax.map(one_block, (q_blocks, c_blocks, jnp.arange(NB)))
    return jnp.moveaxis(out, 0, 1).reshape(B, S, H, D)


def _odd_mixer(x, w_in, qn_g, kn_g, f_b, w_out):
    B, S, _ = x.shape
    p = x @ w_in
    q, k, v, og, fl = _split_cols(p, (FOX_WIDTH, FOX_WIDTH, FOX_WIDTH, FOX_WIDTH, FOX_HEADS))
    heads = lambda t: t.reshape(B, S, FOX_HEADS, FOX_HEAD)
    q = _rms_norm(heads(q), qn_g)
    k = _rms_norm(heads(k), kn_g)
    logf = jax.nn.log_sigmoid(fl + f_b)
    o = _fox_attention(q, k, heads(v), logf).reshape(B, S, FOX_WIDTH)
    return (jax.nn.sigmoid(og) * o) @ w_out


def _peer_ffn(x, w_q, sub_k1, sub_k2, u_tab, v_tab):
    B, S, Dm = x.shape
    K = PEER_TOPK
    xt = x.reshape((B * S) // PEER_TOKBLOCK, PEER_TOKBLOCK, Dm)

    def block(xb):
        T = xb.shape[0]
        q = (xb @ w_q).reshape(T, PEER_HEADS, 2, PEER_DHALF)
        s1 = jnp.einsum('thd,hnd->thn', q[:, :, 0], sub_k1)
        s2 = jnp.einsum('thd,hnd->thn', q[:, :, 1], sub_k2)
        v1, i1 = lax.top_k(s1, K)
        v2, i2 = lax.top_k(s2, K)
        cand = (v1[..., :, None] + v2[..., None, :]).reshape(T, PEER_HEADS, K * K)
        sc, ci = lax.top_k(cand, K)
        e1 = jnp.take_along_axis(i1, ci // K, axis=-1)
        e2 = jnp.take_along_axis(i2, ci % K, axis=-1)
        eid = e1 * PEER_NKEYS + e2
        gate = jax.nn.softmax(sc, axis=-1)
        act = jax.nn.gelu(jnp.einsum('td,thkd->thk', xb, u_tab[eid]), approximate=False)
        return jnp.einsum('thk,thkd->td', gate * act, v_tab[eid])

    return lax.map(block, xt).reshape(B, S, Dm)


PEER_EXPERTS = PEER_NKEYS * PEER_NKEYS
PEER_SELECT_TILE = 256
PEER_TOK_TILE = 512
PEER_EXPERT_CHUNK = 1024
PEER_E1_GROUP = 4
PEER_E2_SUB = 32
PEER_A_PIECES = 4
PEER_C_PIECES = 4
LANES = 128
MXU_DIM = 256
INV_SQRT2 = 0.7071067811865476
_NT = (((1,), (1,)), ((), ()))


def _topk_desc(work, k):
    vals = []
    for _ in range(k):
        m = jnp.max(work, axis=0, keepdims=True)
        vals.append(m)
        work = jnp.where(work >= m, -jnp.inf, work)
    return vals


def _oddeven_merge(lo, hi, r):
    step = r * 2
    if step < hi - lo:
        yield from _oddeven_merge(lo, hi, step)
        yield from _oddeven_merge(lo + r, hi, step)
        yield from [(i, i + r) for i in range(lo + r, hi - r, step)]
    else:
        yield (lo, lo + r)


def _oddeven_sort(lo, hi):
    if hi - lo >= 1:
        mid = lo + (hi - lo) // 2
        yield from _oddeven_sort(lo, mid)
        yield from _oddeven_sort(mid + 1, hi)
        yield from _oddeven_merge(lo, hi, 1)


SUBLANES = 8


def _sorted_top(s, k):
    v = [s[i * SUBLANES:(i + 1) * SUBLANES, :] for i in range(k)]

    def exchange(i, j):
        v[i], v[j] = jnp.maximum(v[i], v[j]), jnp.minimum(v[i], v[j])

    for i, j in _oddeven_sort(0, k - 1):
        exchange(i, j)
    shift = SUBLANES // 2
    while shift >= 1:
        v = [jnp.maximum(v[i], pltpu.roll(v[k - 1 - i], shift, axis=0)) for i in range(k)]
        d = k // 2
        while d >= 1:
            for i in range(k):
                if i & d == 0:
                    exchange(i, i + d)
            d //= 2
        shift //= 2
    return [t[0:1, :] for t in v]


def _peer_select_kernel(x_ref, wqt_ref, k1_ref, k2_ref,
                        phi_ref, a1_ref, a2_ref, qt_ref):
    f32, bf16 = jnp.float32, jnp.bfloat16
    K = PEER_TOPK
    qt_ref[...] = lax.dot_general(wqt_ref[...], x_ref[...].astype(bf16), _NT, preferred_element_type=f32)
    n_groups = x_ref.shape[0] // LANES

    def head(h, carry):
        base = pl.multiple_of(h * PEER_DKEY, PEER_DKEY)
        q1 = qt_ref[pl.ds(base, PEER_DHALF), :].astype(bf16)
        q2 = qt_ref[pl.ds(base + PEER_DHALF, PEER_DHALF), :].astype(bf16)
        s1 = jnp.dot(k1_ref[h], q1, preferred_element_type=f32)
        s2 = jnp.dot(k2_ref[h], q2, preferred_element_type=f32)
        for c in range(n_groups):
            sl = slice(c * LANES, (c + 1) * LANES)
            s1c, s2c = s1[:, sl], s2[:, sl]
            v1 = _sorted_top(s1c, K)
            v2 = _sorted_top(s2c, K)
            rows = [v1[a] + v2[b] for a in range(K) for b in range(K) if (a + 1) * (b + 1) <= K + 1]
            rows += [jnp.full_like(v1[0], -jnp.inf)] * (-len(rows) % 8)
            cand = jnp.concatenate(rows, axis=0)
            vc = _topk_desc(cand, K + 1)
            cmax = vc[0]
            z = jnp.sum(jnp.where(cand >= vc[K - 1], jnp.exp(cand - cmax), 0.0), axis=0, keepdims=True)
            cut = 0.5 * (vc[K - 1] + vc[K])
            s1m = jnp.where(s1c >= v1[K - 1], s1c, -jnp.inf)
            s2m = jnp.where(s2c >= v2[K - 1], s2c, -jnp.inf)
            inv_z = 1.0 / z
            phi_ref[h, :, sl] = jnp.exp(cut - s1m - v2[0]) * inv_z
            a1_ref[h, :, sl] = jnp.exp(s1m - v1[0])
            a2_ref[h, :, sl] = jnp.exp(s2m - v2[0]) * inv_z
        return carry

    lax.fori_loop(0, PEER_HEADS, head, 0)


def _peer_select(xb, wqt, k1, k2):
    n, d = xb.shape
    T = PEER_SELECT_TILE
    H, NK = PEER_HEADS, PEER_NKEYS
    full = lambda shape: pl.BlockSpec(shape, lambda i: (0,) * len(shape))
    tok3 = lambda rows: pl.BlockSpec((H, rows, T), lambda i: (0, 0, i))
    f32 = jnp.float32
    return pl.pallas_call(
        _peer_select_kernel,
        grid=(n // T,),
        in_specs=[pl.BlockSpec((T, d), lambda i: (i, 0)), full(wqt.shape), full(k1.shape), full(k2.shape)],
        out_specs=[tok3(NK)] * 3,
        out_shape=[jax.ShapeDtypeStruct((H, NK, n), f32)] * 3,
        scratch_shapes=[pltpu.VMEM((H * PEER_DKEY, T), f32)],
        name="peer_select",
    )(xb, wqt, k1, k2)


def _peer_expert_kernel(xt_ref, u_ref, v_ref, phi_ref, a1_ref, a2_ref,
                        o_ref, acc_ref, h0_ref, h1_ref, w0_ref, w1_ref):
    f32, bf16 = jnp.float32, jnp.bfloat16
    s = pl.program_id(0)
    n_chunks = PEER_EXPERTS // PEER_EXPERT_CHUNK
    c_chunk = jnp.maximum(s - 2, 0) % n_chunks

    @pl.when(s == 0)
    def _():
        for ref in (h0_ref, h1_ref, w0_ref, w1_ref):
            ref[...] = jnp.zeros_like(ref)

    @pl.when(c_chunk == 0)
    def _():
        acc_ref[...] = jnp.zeros_like(acc_ref)

    T = xt_ref.shape[1]
    NK, G, SB = PEER_NKEYS, PEER_E1_GROUP, PEER_E2_SUB
    CE = PEER_EXPERT_CHUNK
    PA, PC = PEER_A_PIECES, PEER_C_PIECES

    def step(ha_ref, hb_ref, wb_ref, wc_ref):
        def stage_a(k):
            rows = CE // PA
            u_blk = pltpu.bitcast(u_ref[k * rows // 2:(k + 1) * rows // 2, :], bf16)
            ha_ref[k * rows:(k + 1) * rows, :] = jnp.dot(u_blk, xt_ref[...], preferred_element_type=f32)

        def stage_b(q, c, sb):
            ts = slice(c * LANES, (c + 1) * LANES)
            e2s = slice(sb * SB, (sb + 1) * SB)
            g = [None] * G
            for hd in range(PEER_HEADS):
                a2 = a2_ref[hd, e2s, ts]
                for e in range(G):
                    r = q * G + e
                    term = jnp.where(a2 > phi_ref[hd, r:r + 1, ts], a1_ref[hd, r:r + 1, ts] * a2, 0.0)
                    g[e] = term if g[e] is None else g[e] + term
            for e in range(G):
                lo = (q * G + e) * NK + sb * SB
                hh = hb_ref[lo:lo + SB, ts]
                wb_ref[lo:lo + SB, ts] = (g[e] * (0.5 * hh * (1.0 + lax.erf(hh * INV_SQRT2)))).astype(bf16)

        b_blocks = [(q, c, sb) for q in range(CE // (G * NK)) for c in range(T // LANES)
                    for sb in range(NK // SB)]
        def stage_c(k):
            rows = slice(k * CE // PC, (k + 1) * CE // PC)
            acc_ref[...] += lax.dot_general(wc_ref[rows, :], v_ref[rows, :], _TN,
                                            preferred_element_type=f32)

        nb = len(b_blocks)
        mxu_pieces = [functools.partial(stage_a, k) for k in range(PA)]
        mxu_pieces[1:1] = [functools.partial(stage_c, k) for k in range(PC)]
        every = nb // len(mxu_pieces)
        for t, blk in enumerate(b_blocks):
            stage_b(*blk)
            if t % every == 0 and t // every < len(mxu_pieces):
                mxu_pieces[t // every]()

    pl.when(s % 2 == 0)(functools.partial(step, h0_ref, h1_ref, w1_ref, w0_ref))
    pl.when(s % 2 == 1)(functools.partial(step, h1_ref, h0_ref, w0_ref, w1_ref))

    @pl.when(c_chunk == n_chunks - 1)
    def _():
        o_ref[...] = acc_ref[...]


def _peer_experts(xtb, ub, vb, phi, a1, a2):
    d, n = xtb.shape
    T, CE = PEER_TOK_TILE, PEER_EXPERT_CHUNK
    H, NK = PEER_HEADS, PEER_NKEYS
    e1_per_chunk = CE // NK
    nc = PEER_EXPERTS // CE
    last = (n // T) * nc - 1
    at = lambda s, lag: jnp.clip(s - lag, 0, last)
    return pl.pallas_call(
        _peer_expert_kernel,
        grid=(last + 3,),
        in_specs=[
            pl.BlockSpec((d, T), lambda s: (0, at(s, 0) // nc)),
            pl.BlockSpec((CE // 2, d), lambda s: (at(s, 0) % nc, 0)),
            pl.BlockSpec((CE, d), lambda s: (at(s, 2) % nc, 0)),
            pl.BlockSpec((H, e1_per_chunk, T), lambda s: (0, at(s, 1) % nc, at(s, 1) // nc)),
            pl.BlockSpec((H, e1_per_chunk, T), lambda s: (0, at(s, 1) % nc, at(s, 1) // nc)),
            pl.BlockSpec((H, NK, T), lambda s: (0, 0, at(s, 1) // nc)),
        ],
        out_specs=pl.BlockSpec((T, d), lambda s: (at(s, 2) // nc, 0)),
        out_shape=jax.ShapeDtypeStruct((n, d), jnp.float32),
        scratch_shapes=[pltpu.VMEM((T, d), jnp.float32),
                        pltpu.VMEM((CE, T), jnp.float32), pltpu.VMEM((CE, T), jnp.float32),
                        pltpu.VMEM((CE, T), jnp.bfloat16), pltpu.VMEM((CE, T), jnp.bfloat16)],
        compiler_params=pltpu.CompilerParams(dimension_semantics=("arbitrary",),
                                             vmem_limit_bytes=VMEM_LIMIT_BYTES),
        name="peer_experts",
    )(xtb, ub, vb, phi, a1, a2)


def _pack_row_pairs(t):
    rows, cols = t.shape
    half = lambda part: lax.bitcast_convert_type(part.astype(jnp.bfloat16), jnp.uint16).astype(jnp.uint32)
    pairs = t.reshape(rows // 2, 2 * cols)
    return half(pairs[:, :cols]) | (half(pairs[:, cols:]) << 16)


def _peer_pallas(x2, xt_bf16, w_q, sub_k1, sub_k2, u_tab, v_tab):
    bf16 = jnp.bfloat16
    sel = _peer_select(x2, w_q.T.astype(bf16), sub_k1.astype(bf16), sub_k2.astype(bf16))
    return _peer_experts(xt_bf16, _pack_row_pairs(u_tab), v_tab.astype(bf16), *sel)


RWKV_CHUNK = 64
RWKV_BATCH_PER_STEP = 2
RWKV_PAIR = LANES // RWKV_HEAD
RWKV_PAIRS = RWKV_HEADS // RWKV_PAIR
_NN = (((1,), (0,)), ((), ()))
_TN = (((0,), (0,)), ((), ()))


def _split_bf16(a):
    hi = a.astype(jnp.bfloat16)
    lo = (a - hi.astype(jnp.float32)).astype(jnp.bfloat16)
    return hi, lo


def _mm2(a, b, dims):
    d = lambda u, v: lax.dot_general(u, v, dims, preferred_element_type=jnp.float32)
    (ah, al), (bh, bl) = _split_bf16(a), _split_bf16(b)
    return d(ah, bh) + d(ah, bl) + d(al, bh)


def _mm_exact_rhs(a, b_exact, dims, terms=2):
    d = lambda u: lax.dot_general(u, b_exact, dims, preferred_element_type=jnp.float32)
    out, rem = None, a
    for _ in range(terms):
        piece = rem.astype(jnp.bfloat16)
        rem = rem - piece.astype(jnp.float32)
        out = d(piece) if out is None else out + d(piece)
    return out


def _mm_exact_lhs(a_exact, b, dims, terms=3):
    d = lambda v: lax.dot_general(a_exact, v, dims, preferred_element_type=jnp.float32)
    out, rem = None, b
    for _ in range(terms):
        piece = rem.astype(jnp.bfloat16)
        rem = rem - piece.astype(jnp.float32)
        out = d(piece) if out is None else out + d(piece)
    return out


def _softplus(z):
    return jnp.maximum(z, 0.0) + jnp.log1p(jnp.exp(-jnp.abs(z)))


def _rwkv_kernel(p_ref, prev_ref, mu_ref, w0_ref, wup_ref, a0_ref, aup_ref, gup_ref,
                 kk_ref, ka_ref, rk_ref, lng_ref, lnb_ref, ones_ref,
                 o_ref, state_ref):
    f32, bf16 = jnp.float32, jnp.bfloat16
    L, W = RWKV_CHUNK, RWKV_WIDTH
    c_idx = pl.program_id(1)

    @pl.when(c_idx == 0)
    def _():
        state_ref[...] = jnp.zeros_like(state_ref)

    NB = p_ref.shape[0]
    parts = []
    for bi in range(NB):
        pb = p_ref[bi]
        row = lax.broadcasted_iota(jnp.int32, pb.shape, 0)
        prev_row = jnp.where(c_idx == 0, 0.0, prev_ref[bi, 7:8, :])
        shifted = jnp.where(row == 0, prev_row, pltpu.roll(pb, 1, axis=0))
        parts.append(pb + (shifted - pb) * mu_ref[...])
    p = jnp.concatenate(parts, axis=0)
    r, k, v = p[:, 0:W], p[:, W:2 * W], p[:, 2 * W:3 * W]
    o = 3 * W
    xw = p[:, o:o + RWKV_W_RANK]
    xa = p[:, o + RWKV_W_RANK:o + RWKV_W_RANK + RWKV_A_RANK]
    xg = p[:, o + RWKV_W_RANK + RWKV_A_RANK:]
    dotd = lambda u, m: jnp.dot(u.astype(bf16), m, preferred_element_type=f32)
    lw = -jnp.exp(-_softplus(-(w0_ref[...] + dotd(jnp.tanh(xw), wup_ref[...]))) - 0.5)
    a = jax.nn.sigmoid(a0_ref[...] + dotd(xa, aup_ref[...]))
    g = dotd(jax.nn.sigmoid(xg), gup_ref[...])
    ones_bd = ones_ref[...]
    head_sum = lambda t: _mm_exact_rhs(t, ones_bd, _NN)
    kk = k * kk_ref[...]
    kk = kk / jnp.maximum(jnp.sqrt(head_sum(kk * kk)), 1e-12)
    k = k * (1.0 + (a - 1.0) * ka_ref[...])

    lane = lax.broadcasted_iota(jnp.int32, (L, LANES), 1)
    trow = lax.broadcasted_iota(jnp.int32, (L, LANES), 0)
    s_in = lane % RWKV_HEAD
    strict, incl = s_in < trow, s_in <= trow
    eye_pair = (s_in == trow).astype(f32)
    m0 = lane < RWKV_HEAD
    tril = (lax.broadcasted_iota(jnp.int32, (L, L), 1)
            <= lax.broadcasted_iota(jnp.int32, (L, L), 0)).astype(bf16)
    r2 = lax.broadcasted_iota(jnp.int32, (LANES, LANES), 0) // RWKV_HEAD
    c2 = lax.broadcasted_iota(jnp.int32, (LANES, LANES), 1) // RWKV_HEAD
    bd_mask = r2 == c2

    def bd(t):
        return jnp.concatenate([jnp.where(m0, t, 0.0), jnp.where(m0, 0.0, t)], axis=0)

    pairs = range(NB * RWKV_PAIRS)
    cut = lambda t: [t[bi * L:(bi + 1) * L, pr * LANES:(pr + 1) * LANES]
                     for bi in range(NB) for pr in range(RWKV_PAIRS)]
    rp, kp, vp, kkp, ap, lwp = cut(r), cut(k), cut(v), cut(kk), cut(a), cut(lw)
    c = [_mm_exact_lhs(tril, lwp[p], _NN) for p in pairs]
    at = [-kkp[p] * jnp.exp(c[p] - lwp[p]) for p in pairs]
    eni = [jnp.exp(-c[p]) for p in pairs]
    bt = [kkp[p] * ap[p] * eni[p] for p in pairs]
    kt = [kp[p] * eni[p] for p in pairs]
    rt = [rp[p] * jnp.exp(c[p]) for p in pairs]
    g_last = [jnp.exp(c[p][L - 1:L, :]) for p in pairs]
    gram = [_mm2(jnp.concatenate([at[p], rt[p]], axis=0),
                 jnp.concatenate([bd(bt[p]), bd(kt[p])], axis=0), _NT) for p in pairs]
    n_ab = [jnp.where(strict, gram[p][0:L, 0:LANES], 0.0) for p in pairs]
    a_ak = [jnp.where(strict, gram[p][0:L, LANES:], 0.0) for p in pairs]
    a_rb = [jnp.where(incl, gram[p][L:, 0:LANES], 0.0) for p in pairs]
    a_rk = [jnp.where(incl, gram[p][L:, LANES:], 0.0) for p in pairs]
    tinv, m = [eye_pair + n_ab[p] for p in pairs], n_ab
    for _ in range(5):
        m = [_mm2(m[p], bd(m[p]), _NN) for p in pairs]
        tinv = [tinv[p] + _mm2(m[p], bd(tinv[p]), _NN) for p in pairs]
    s0 = [state_ref[p] for p in pairs]
    pq = [_mm2(at[p], s0[p], _NT) + _mm2(a_ak[p], bd(vp[p]), _NN) for p in pairs]
    u = [_mm2(tinv[p], bd(pq[p]), _NN) for p in pairs]
    ys = [_mm2(rt[p], s0[p], _NT)
          + _mm2(jnp.concatenate([a_rb[p], a_rk[p]], axis=1),
                 jnp.concatenate([bd(u[p]), bd(vp[p])], axis=0), _NN) for p in pairs]
    for p in pairs:
        upd = _mm2(jnp.concatenate([u[p], vp[p]], axis=0),
                   jnp.concatenate([bt[p], kt[p]], axis=0), _TN)
        state_ref[p] = (s0[p] + jnp.where(bd_mask, upd, 0.0)) * g_last[p]
    y = jnp.concatenate([jnp.concatenate(ys[bi * RWKV_PAIRS:(bi + 1) * RWKV_PAIRS], axis=1)
                         for bi in range(NB)], axis=0)

    inv_n = 1.0 / RWKV_HEAD
    mean = head_sum(y) * inv_n
    yc = y - mean
    var = head_sum(yc * yc) * inv_n
    y = yc * lax.rsqrt(var + RWKV_GN_EPS) * lng_ref[...] + lnb_ref[...]
    y = y + head_sum(r * k * rk_ref[...]) * v
    out = y * g
    for bi in range(NB):
        o_ref[bi] = out[bi * L:(bi + 1) * L]


def _rwkv_pallas(p_r, mu, w0, w_up, a0, a_up, g_up, k_k, k_a, r_k, ln_g, ln_b):
    B, S, C = p_r.shape
    L, W, NB = RWKV_CHUNK, RWKV_WIDTH, RWKV_BATCH_PER_STEP
    bf16 = jnp.bfloat16
    vecw = lambda t: t.reshape(1, W)
    ones_bd = jnp.kron(jnp.eye(RWKV_HEADS, dtype=bf16), jnp.ones((RWKV_HEAD, RWKV_HEAD), bf16))
    full = lambda a: pl.BlockSpec(a.shape, lambda b, c: (0,) * a.ndim)
    args = [mu.reshape(1, C), vecw(w0), w_up.astype(bf16), vecw(a0), a_up.astype(bf16),
            g_up.astype(bf16), vecw(k_k), vecw(k_a), vecw(r_k), vecw(ln_g), vecw(ln_b), ones_bd]
    return pl.pallas_call(
        _rwkv_kernel,
        grid=(B // NB, S // L),
        in_specs=[pl.BlockSpec((NB, L, C), lambda b, c: (b, c, 0)),
                  pl.BlockSpec((NB, 8, C), lambda b, c: (b, jnp.maximum(c * (L // 8) - 1, 0), 0))]
                 + [full(a) for a in args],
        out_specs=pl.BlockSpec((NB, L, W), lambda b, c: (b, c, 0)),
        out_shape=jax.ShapeDtypeStruct((B, S, W), jnp.float32),
        scratch_shapes=[pltpu.VMEM((NB * RWKV_PAIRS, LANES, LANES), jnp.float32)],
        compiler_params=pltpu.CompilerParams(dimension_semantics=("arbitrary", "arbitrary")),
        name="rwkv7",
    )(p_r, p_r, *args)


MLSTM_CONV = 4
MLSTM_GATE_LANES = LANES


def _mlstm_kernel(p_ref, prev_ref, cw_ref, cb_ref, igb_ref, fgb_ref, hng_ref, hnb_ref,
                  o_ref, c_ref, n_ref, m_ref):
    f32, bf16 = jnp.float32, jnp.bfloat16
    L, W, H, D = MLSTM_CHUNK, MLSTM_WIDTH, MLSTM_HEADS, MLSTM_HEAD
    c_idx = pl.program_id(1)

    @pl.when(c_idx == 0)
    def _():
        c_ref[...] = jnp.zeros_like(c_ref)
        n_ref[...] = jnp.zeros_like(n_ref)
        m_ref[...] = jnp.zeros_like(m_ref)

    p = p_ref[0]
    z = p[:, 0:2 * W]
    prev = jnp.where(c_idx == 0, 0.0, prev_ref[0, :, 0:2 * W])
    ext = jnp.concatenate([prev, z], axis=0)
    conv = cb_ref[...] + cw_ref[MLSTM_CONV - 1:MLSTM_CONV, :] * z
    for j in range(MLSTM_CONV - 1):
        d = MLSTM_CONV - 1 - j
        conv = conv + cw_ref[j:j + 1, :] * ext[8 - d:8 - d + L, :]
    qk = conv * jax.nn.sigmoid(conv)
    q_all, k_all = qk[:, 0:W] * (D ** -0.5), qk[:, W:2 * W]
    v_all, o_all = p[:, 2 * W:3 * W], p[:, 3 * W:4 * W]
    gates = p[:, 4 * W:]
    igl = gates + igb_ref[...]
    lfl = -_softplus(-(gates + fgb_ref[...]))
    tril = (lax.broadcasted_iota(jnp.int32, (L, L), 1)
            <= lax.broadcasted_iota(jnp.int32, (L, L), 0))
    bcum = _mm_exact_lhs(tril.astype(bf16), lfl, _NN)
    e_all = igl - pltpu.roll(bcum, LANES - H, axis=1)
    lane = lax.broadcasted_iota(jnp.int32, (L, LANES), 1)
    hd = range(H)
    cols = lambda t: [t[:, h * D:(h + 1) * D] for h in hd]
    q, k, v, o = cols(q_all), cols(k_all), cols(v_all), cols(o_all)
    qb, kb, vb = ([t.astype(bf16) for t in ts] for ts in (q, k, v))
    b_col = [bcum[:, H + h:H + h + 1] for h in hd]
    ig_col = [igl[:, h:h + 1] for h in hd]
    e_row = [_mm_exact_lhs((lane == h).astype(bf16), e_all, _NT) for h in hd]
    qk = [lax.dot_general(qb[h], kb[h], _NT, preferred_element_type=f32) for h in hd]
    m_prev = [m_ref[h, 0:1, 0:1] for h in hd]
    c_prev = [c_ref[h] for h in hd]
    n_prev = [n_ref[h, 0:1, :] for h in hd]
    qc = [jnp.dot(qb[h], c_prev[h].astype(bf16), preferred_element_type=f32) for h in hd]
    dmat = [jnp.where(tril, b_col[h] + e_row[h], -jnp.inf) for h in hd]
    inter = [b_col[h] + m_prev[h] for h in hd]
    m_t = [jnp.maximum(inter[h], jnp.max(dmat[h], axis=-1, keepdims=True)) for h in hd]
    sc = [qk[h] * jnp.exp(dmat[h] - m_t[h]) for h in hd]
    carry_in = [jnp.exp(inter[h] - m_t[h]) for h in hd]
    sv = [jnp.dot(sc[h].astype(bf16), vb[h], preferred_element_type=f32) for h in hd]
    den = [jnp.sum(sc[h], axis=-1, keepdims=True)
           + carry_in[h] * jnp.sum(q[h] * n_prev[h], axis=-1, keepdims=True) for h in hd]
    hval = [(sv[h] + carry_in[h] * qc[h]) / jnp.maximum(jnp.abs(den[h]), jnp.exp(-m_t[h])) for h in hd]
    b_last = [b_col[h][L - 1:L, :] for h in hd]
    gs = [b_last[h] - b_col[h] + ig_col[h] for h in hd]
    m_new = [jnp.maximum(b_last[h] + m_prev[h], jnp.max(gs[h], axis=0, keepdims=True)) for h in hd]
    keep = [jnp.exp(b_last[h] + m_prev[h] - m_new[h]) for h in hd]
    wk = [jnp.exp(gs[h] - m_new[h]) * k[h] for h in hd]
    kv = [lax.dot_general(wk[h].astype(bf16), vb[h], _TN, preferred_element_type=f32) for h in hd]
    outs = []
    for h in hd:
        c_ref[h] = keep[h] * c_prev[h] + kv[h]
        n_ref[h] = jnp.broadcast_to(keep[h] * n_prev[h] + jnp.sum(wk[h], axis=0, keepdims=True), (8, D))
        m_ref[h] = jnp.broadcast_to(m_new[h], (8, LANES))
        mu = jnp.mean(hval[h], axis=-1, keepdims=True)
        hc = hval[h] - mu
        var = jnp.mean(hc * hc, axis=-1, keepdims=True)
        hs = slice(h * D, (h + 1) * D)
        hn = hc * lax.rsqrt(var + LN_EPS) * hng_ref[:, hs] + hnb_ref[:, hs]
        outs.append(jax.nn.sigmoid(o[h]) * hn)
    o_ref[0] = jnp.concatenate(outs, axis=1)


def _mlstm_pallas(p_m, conv_w, conv_b, ig_b, fg_b, hn_g, hn_b):
    B, S, C = p_m.shape
    L, W, H, D = MLSTM_CHUNK, MLSTM_WIDTH, MLSTM_HEADS, MLSTM_HEAD
    pad = lambda t, off: jnp.zeros((1, LANES), jnp.float32).at[0, off:off + H].set(t)
    args = [conv_w, conv_b.reshape(1, 2 * W), pad(ig_b, 0), pad(fg_b, H),
            hn_g.reshape(1, W), hn_b.reshape(1, W)]
    full = lambda a: pl.BlockSpec(a.shape, lambda b, c: (0,) * a.ndim)
    return pl.pallas_call(
        _mlstm_kernel,
        grid=(B, S // L),
        in_specs=[pl.BlockSpec((1, L, C), lambda b, c: (b, c, 0)),
                  pl.BlockSpec((1, 8, C), lambda b, c: (b, jnp.maximum(c * (L // 8) - 1, 0), 0))]
                 + [full(a) for a in args],
        out_specs=pl.BlockSpec((1, L, W), lambda b, c: (b, c, 0)),
        out_shape=jax.ShapeDtypeStruct((B, S, W), jnp.float32),
        scratch_shapes=[pltpu.VMEM((H, D, D), jnp.float32),
                        pltpu.VMEM((H, 8, D), jnp.float32),
                        pltpu.VMEM((H, 8, LANES), jnp.float32)],
        compiler_params=pltpu.CompilerParams(dimension_semantics=("arbitrary", "arbitrary")),
        name="mlstm",
    )(p_m, p_m, *args)


FOX_TILE = 256
FOX_KEY_GROUP = 4
FOX_HEADS_PER_STEP = 8
FOX_AUG = LANES
FOX_GATE_LANES = LANES


def _fox_prep_kernel(p_ref, qg_ref, kg_ref, fb_ref, ones_ref, qa_ref, ka_ref, vt_ref, carry_ref):
    f32, bf16 = jnp.float32, jnp.bfloat16
    T, W, H, D = FOX_TILE, FOX_WIDTH, FOX_HEADS, FOX_HEAD
    i = pl.program_id(1)

    @pl.when(i == 0)
    def _():
        carry_ref[...] = jnp.zeros_like(carry_ref)

    p = p_ref[0]
    ones_bd = ones_ref[...]
    ms = lambda t: _mm_exact_rhs(t * t, ones_bd, _NN) * (1.0 / D)
    q = p[:, 0:W]
    q = q * lax.rsqrt(ms(q) + 1e-6) * qg_ref[...]
    k = p[:, W:2 * W]
    k = k * lax.rsqrt(ms(k) + 1e-6) * kg_ref[...] * (D ** -0.5)
    vt = p[:, 2 * W:3 * W].T.astype(bf16)
    logf = -_softplus(-(p[:, 4 * W:] + fb_ref[...]))
    tril = (lax.broadcasted_iota(jnp.int32, (T, T), 1)
            <= lax.broadcasted_iota(jnp.int32, (T, T), 0)).astype(bf16)
    c = _mm_exact_lhs(tril, logf, _NN) + carry_ref[0:1, :]
    carry_ref[...] = jnp.broadcast_to(c[T - 1:T, :], carry_ref.shape)
    c1 = c.astype(bf16).astype(f32)
    c2 = (c - c1).astype(bf16).astype(f32)
    c3 = (c - c1 - c2).astype(bf16).astype(f32)
    lane = lax.broadcasted_iota(jnp.int32, (T, D), 1)
    for h in range(H):
        hs = slice(h * D, (h + 1) * D)
        c1h, c2h, c3h = c1[:, h:h + 1], c2[:, h:h + 1], c3[:, h:h + 1]
        pieces = jnp.where(lane % 3 == 0, c1h, jnp.where(lane % 3 == 1, c2h, c3h))
        q_aug = jnp.where(lane < 3, 1.0, jnp.where(lane < 6, pieces, 0.0))
        k_aug = jnp.where(lane < 3, -pieces, jnp.where(lane < 6, 1.0, 0.0))
        qa_ref[0, h] = jnp.concatenate([q[:, hs], q_aug], axis=1).astype(bf16)
        ka_ref[0, h] = jnp.concatenate([k[:, hs], k_aug], axis=1).astype(bf16)
        vt_ref[0, h, 0] = vt[hs, :]


def _fox_prep(p_f, qn_g, kn_g, f_b):
    B, S, C = p_f.shape
    T, W, H, D = FOX_TILE, FOX_WIDTH, FOX_HEADS, FOX_HEAD
    bf16 = jnp.bfloat16
    ones_bd = jnp.kron(jnp.eye(H, dtype=bf16), jnp.ones((D, D), bf16))
    fb = jnp.zeros((1, FOX_GATE_LANES), jnp.float32).at[0, :H].set(f_b)
    args = [qn_g.reshape(1, W), kn_g.reshape(1, W), fb, ones_bd]
    full = lambda a: pl.BlockSpec(a.shape, lambda b, i: (0,) * a.ndim)
    return pl.pallas_call(
        _fox_prep_kernel,
        grid=(B, S // T),
        in_specs=[pl.BlockSpec((1, T, C), lambda b, i: (b, i, 0))] + [full(a) for a in args],
        out_specs=[pl.BlockSpec((1, H, T, FOX_AUG), lambda b, i: (b, 0, i, 0)),
                   pl.BlockSpec((1, H, T, FOX_AUG), lambda b, i: (b, 0, i, 0)),
                   pl.BlockSpec((1, H, 1, D, T), lambda b, i: (b, 0, i, 0, 0))],
        out_shape=[jax.ShapeDtypeStruct((B, H, S, FOX_AUG), bf16),
                   jax.ShapeDtypeStruct((B, H, S, FOX_AUG), bf16),
                   jax.ShapeDtypeStruct((B, H, S // T, D, T), bf16)],
        scratch_shapes=[pltpu.VMEM((8, FOX_GATE_LANES), jnp.float32)],
        compiler_params=pltpu.CompilerParams(dimension_semantics=("arbitrary", "arbitrary"),
                                             vmem_limit_bytes=VMEM_LIMIT_BYTES),
        name="fox_prep",
    )(p_f, *args)


def _fox_attn_kernel(qa_ref, ka_ref, vt_ref, og_ref, o_ref, m_ref, l_ref, acc_ref):
    f32, bf16 = jnp.float32, jnp.bfloat16
    T, D = FOX_TILE, FOX_HEAD
    i = pl.program_id(2)
    heads = range(FOX_HEADS_PER_STEP)
    n_blocks = ka_ref.shape[2] // T
    GK = FOX_KEY_GROUP
    rel = (lax.broadcasted_iota(jnp.int32, (GK * T, T), 0)
           - lax.broadcasted_iota(jnp.int32, (GK * T, T), 1))
    qas = [qa_ref[0, j] for j in heads]

    def key_group(g, first):
        visible = rel <= (i - g * GK) * T

        def scores(j):
            ka = ka_ref[0, j, g * GK * T:(g + 1) * GK * T, :]
            return lax.dot_general(ka, qas[j], _NT, preferred_element_type=f32)

        nxt = scores(0)
        for j in heads:
            s = nxt
            if j + 1 < len(heads):
                nxt = scores(j + 1)
            s = jnp.where(visible, s, -jnp.inf)
            m_g = jnp.max(s, axis=0, keepdims=True)
            m_new = m_g if first else jnp.maximum(m_ref[j], m_g)
            pr = jnp.exp(s - m_new)
            l_g = jnp.sum(pr, axis=0, keepdims=True)
            vt = jnp.concatenate([vt_ref[0, j, g * GK + b] for b in range(GK)], axis=1)
            pv = jnp.dot(vt, pr.astype(bf16), preferred_element_type=f32)
            if first:
                l_ref[j], acc_ref[j] = l_g, pv
            else:
                alpha = jnp.exp(m_ref[j] - m_new)
                l_ref[j] = l_ref[j] * alpha + l_g
                acc_ref[j] = acc_ref[j] * alpha + pv
            m_ref[j] = m_new

    key_group(0, True)
    for g in range(1, n_blocks // GK):
        pl.when(i >= g * GK)(functools.partial(key_group, g, False))
    outs = [(acc_ref[j] / l_ref[j]).T for j in heads]
    o_ref[0] = jax.nn.sigmoid(og_ref[0]) * jnp.concatenate(outs, axis=1)


def _fox_attn(qa, ka, vt, p_f):
    B, H, S, A = qa.shape
    T, D = FOX_TILE, FOX_HEAD
    hp = FOX_HEADS_PER_STEP
    width = hp * D
    og_block0 = 3 * FOX_WIDTH // width
    return pl.pallas_call(
        _fox_attn_kernel,
        grid=(B, H // hp, S // T),
        in_specs=[pl.BlockSpec((1, hp, T, A), lambda b, g, i: (b, g, i, 0)),
                  pl.BlockSpec((1, hp, S, A), lambda b, g, i: (b, g, 0, 0)),
                  pl.BlockSpec((1, hp, S // T, D, T), lambda b, g, i: (b, g, 0, 0, 0)),
                  pl.BlockSpec((1, T, width), lambda b, g, i: (b, i, og_block0 + g))],
        out_specs=pl.BlockSpec((1, T, width), lambda b, g, i: (b, i, g)),
        out_shape=jax.ShapeDtypeStruct((B, S, FOX_WIDTH), jnp.float32),
        scratch_shapes=[pltpu.VMEM((hp, 1, T), jnp.float32), pltpu.VMEM((hp, 1, T), jnp.float32),
                        pltpu.VMEM((hp, D, T), jnp.float32)],
        name="fox_attn",
    )(qa, ka, vt, p_f)


def kernel(x, l0_w_in, l0_rwkv_mu, l0_rwkv_w0, l0_rwkv_w_up, l0_rwkv_a0, l0_rwkv_a_up,
           l0_rwkv_g_up, l0_rwkv_k_k, l0_rwkv_k_a, l0_rwkv_r_k, l0_rwkv_ln_g, l0_rwkv_ln_b,
           l0_mlstm_conv_w, l0_mlstm_conv_b, l0_mlstm_ig_b, l0_mlstm_fg_b,
           l0_mlstm_hn_g, l0_mlstm_hn_b, l0_w_out, l0_ln1_g, l0_ln1_b,
           l0_peer_wq, l0_peer_k1, l0_peer_k2, l0_peer_u, l0_peer_v, l0_ln2_g, l0_ln2_b,
           l1_w_in, l1_fox_qn_g, l1_fox_kn_g, l1_fox_f_b, l1_w_out, l1_ln1_g, l1_ln1_b,
           l1_peer_wq, l1_peer_k1, l1_peer_k2, l1_peer_u, l1_peer_v, l1_ln2_g, l1_ln2_b):
    B, S, D = x.shape
    n = B * S
    bf16 = jnp.bfloat16
    x2 = x.reshape(n, D)

    def pad_lanes(w):
        return jnp.pad(w, ((0, 0), (0, LANES - w.shape[1])))

    m_main = 4 * MLSTM_WIDTH
    w0 = jnp.concatenate([l0_w_in[:, :RWKV_COLS + m_main],
                          pad_lanes(l0_w_in[:, RWKV_COLS + m_main:])], axis=1).astype(bf16)
    p_r, p_m = _proj(x2, w0, (RWKV_COLS, m_main + LANES))
    y_a = _rwkv_pallas(p_r.reshape(B, S, -1), l0_rwkv_mu, l0_rwkv_w0, l0_rwkv_w_up, l0_rwkv_a0,
                       l0_rwkv_a_up, l0_rwkv_g_up, l0_rwkv_k_k, l0_rwkv_k_a, l0_rwkv_r_k,
                       l0_rwkv_ln_g, l0_rwkv_ln_b)
    y_b = _mlstm_pallas(p_m.reshape(B, S, -1), l0_mlstm_conv_w, l0_mlstm_conv_b, l0_mlstm_ig_b,
                        l0_mlstm_fg_b, l0_mlstm_hn_g, l0_mlstm_hn_b)
    x2, xt = _out_proj_ln([y_a.reshape(n, -1), y_b.reshape(n, -1)], l0_w_out.astype(bf16), x2,
                          l0_ln1_g, l0_ln1_b)
    y = _peer_pallas(x2, xt, l0_peer_wq, l0_peer_k1, l0_peer_k2, l0_peer_u, l0_peer_v)
    x2 = _resid_ln(x2, y, l0_ln2_g, l0_ln2_b)

    f_main = 4 * FOX_WIDTH
    w1 = jnp.concatenate([l1_w_in[:, :f_main], pad_lanes(l1_w_in[:, f_main:])], axis=1).astype(bf16)
    (p_f,) = _proj(x2, w1, (f_main + LANES,))
    p_f = p_f.reshape(B, S, -1)
    qa, ka, vt = _fox_prep(p_f, l1_fox_qn_g, l1_fox_kn_g, l1_fox_f_b)
    o = _fox_attn(qa, ka, vt, p_f)
    x2, xt = _out_proj_ln([o.reshape(n, -1)], l1_w_out.astype(bf16), x2, l1_ln1_g, l1_ln1_b)
    y = _peer_pallas(x2, xt, l1_peer_wq, l1_peer_k1, l1_peer_k2, l1_peer_u, l1_peer_v)
    x2 = _resid_ln(x2, y, l1_ln2_g, l1_ln2_b)
    return x2.reshape(B, S, D)
```

```python
import functools

import jax
import jax.numpy as jnp
from jax import lax
from jax.experimental import pallas as pl
from jax.experimental.pallas import tpu as pltpu

D_MODEL = 1024
DEPTH = 2
DN_ALPHA = (2.0 * DEPTH) ** 0.25
LN_EPS = 1e-5

RWKV_WIDTH = D_MODEL // 2
RWKV_HEAD = 64
RWKV_HEADS = RWKV_WIDTH // RWKV_HEAD
RWKV_W_RANK = 64
RWKV_A_RANK = 64
RWKV_G_RANK = 128
RWKV_GN_EPS = 1e-5 * RWKV_HEAD
RWKV_COLS = 3 * RWKV_WIDTH + RWKV_W_RANK + RWKV_A_RANK + RWKV_G_RANK

MLSTM_WIDTH = D_MODEL // 2
MLSTM_HEAD = 128
MLSTM_HEADS = MLSTM_WIDTH // MLSTM_HEAD
MLSTM_CHUNK = 64

FOX_HEAD = 64
FOX_HEADS = D_MODEL // FOX_HEAD
FOX_WIDTH = FOX_HEADS * FOX_HEAD
FOX_QBLOCK = 128

PEER_HEADS = 8
PEER_NKEYS = 128
PEER_TOPK = 16
PEER_DKEY = 256
PEER_DHALF = PEER_DKEY // 2
PEER_TOKBLOCK = 128


def _split_cols(p, sizes):
    out, start = [], 0
    for s in sizes:
        out.append(p[..., start:start + s])
        start += s
    return out


def _resid_ln_kernel(x_ref, y_ref, g_ref, b_ref, o_ref):
    z = DN_ALPHA * x_ref[...] + y_ref[...]
    mu = jnp.mean(z, axis=-1, keepdims=True)
    zc = z - mu
    var = jnp.mean(zc * zc, axis=-1, keepdims=True)
    o_ref[...] = zc * lax.rsqrt(var + LN_EPS) * g_ref[...] + b_ref[...]


def _resid_ln(x2, y2, g, b, tm=512):
    n, d = x2.shape
    row = pl.BlockSpec((tm, d), lambda i: (i, 0))
    vec = pl.BlockSpec((1, d), lambda i: (0, 0))
    return pl.pallas_call(
        _resid_ln_kernel,
        grid=(n // tm,),
        in_specs=[row, row, vec, vec],
        out_specs=row,
        out_shape=jax.ShapeDtypeStruct((n, d), jnp.float32),
        name="resid_ln",
    )(x2, y2, g.reshape(1, d), b.reshape(1, d))


VMEM_LIMIT_BYTES = 56 * 1024 * 1024


def _proj_kernel(x_ref, w_ref, *o_refs):
    xb = x_ref[...].astype(jnp.bfloat16)
    start = 0
    for o_ref in o_refs:
        width = o_ref.shape[1]
        o_ref[...] = jnp.dot(xb, w_ref[:, start:start + width], preferred_element_type=jnp.float32)
        start += width


def _proj(x2, w_bf16, widths, tm=256):
    n, d = x2.shape
    assert sum(widths) == w_bf16.shape[1]
    return pl.pallas_call(
        _proj_kernel,
        grid=(n // tm,),
        in_specs=[pl.BlockSpec((tm, d), lambda i: (i, 0)),
                  pl.BlockSpec(w_bf16.shape, lambda i: (0, 0))],
        out_specs=[pl.BlockSpec((tm, w), lambda i: (i, 0)) for w in widths],
        out_shape=[jax.ShapeDtypeStruct((n, w), jnp.float32) for w in widths],
        compiler_params=pltpu.CompilerParams(vmem_limit_bytes=VMEM_LIMIT_BYTES),
        name="in_proj",
    )(x2, w_bf16)


def _out_proj_ln_kernel(*refs, n_parts):
    y_refs, (w_ref, x_ref, g_ref, b_ref, o_ref, ot_ref) = refs[:n_parts], refs[n_parts:]
    acc, start = None, 0
    for y_ref in y_refs:
        width = y_ref.shape[1]
        part = jnp.dot(y_ref[...].astype(jnp.bfloat16), w_ref[start:start + width, :],
                       preferred_element_type=jnp.float32)
        acc = part if acc is None else acc + part
        start += width
    z = DN_ALPHA * x_ref[...] + acc
    mu = jnp.mean(z, axis=-1, keepdims=True)
    zc = z - mu
    var = jnp.mean(zc * zc, axis=-1, keepdims=True)
    out = zc * lax.rsqrt(var + LN_EPS) * g_ref[...] + b_ref[...]
    o_ref[...] = out
    ot_ref[...] = out.T.astype(jnp.bfloat16)


def _out_proj_ln(ys, w_bf16, x2, g, b, tm=256):
    n, d = x2.shape
    row = lambda width: pl.BlockSpec((tm, width), lambda i: (i, 0))
    vec = pl.BlockSpec((1, d), lambda i: (0, 0))
    return pl.pallas_call(
        functools.partial(_out_proj_ln_kernel, n_parts=len(ys)),
        grid=(n // tm,),
        in_specs=[row(y.shape[1]) for y in ys]
                 + [pl.BlockSpec(w_bf16.shape, lambda i: (0, 0)), row(d), vec, vec],
        out_specs=[row(d), pl.BlockSpec((d, tm), lambda i: (0, i))],
        out_shape=[jax.ShapeDtypeStruct((n, d), jnp.float32), jax.ShapeDtypeStruct((d, n), jnp.bfloat16)],
        name="out_proj_ln",
    )(*ys, w_bf16, x2, g.reshape(1, d), b.reshape(1, d))


def _head_norm(y, g, b, eps):
    mu = jnp.mean(y, -1, keepdims=True)
    var = jnp.mean(jnp.square(y - mu), -1, keepdims=True)
    return (y - mu) * lax.rsqrt(var + eps) * g + b


def _rms_norm(y, g):
    return y * lax.rsqrt(jnp.mean(y * y, -1, keepdims=True) + 1e-6) * g


def _token_shift(z):
    return jnp.pad(z, ((0, 0), (1, 0), (0, 0)))[:, :-1]


def _causal_conv(z, w, b):
    c = z.shape[-1]
    out = lax.conv_general_dilated(z, w[:, None, :], window_strides=(1,),
                                   padding=((w.shape[0] - 1, 0),),
                                   dimension_numbers=('NWC', 'WIO', 'NWC'),
                                   feature_group_count=c)
    return out + b


def _rwkv7_mix(p, mu, w0, w_up, a0, a_up, g_up, k_k, k_a, r_k, ln_g, ln_b):
    B, S, _ = p.shape
    H, N = RWKV_HEADS, RWKV_HEAD
    p = p + (_token_shift(p) - p) * mu
    r, k, v, xw, xa, xg = _split_cols(p, (RWKV_WIDTH, RWKV_WIDTH, RWKV_WIDTH,
                                          RWKV_W_RANK, RWKV_A_RANK, RWKV_G_RANK))
    log_w = -jnp.exp(-jax.nn.softplus(-(w0 + jnp.tanh(xw) @ w_up)) - 0.5)
    a = jax.nn.sigmoid(a0 + xa @ a_up)
    g = jax.nn.sigmoid(xg) @ g_up
    heads = lambda t: t.reshape(B, S, H, N)
    kk = heads(k * k_k)
    kk = kk / jnp.maximum(jnp.sqrt(jnp.sum(kk * kk, -1, keepdims=True)), 1e-12)
    k = k * (1.0 + (a - 1.0) * k_a)
    r_h, k_h, v_h, a_h = heads(r), heads(k), heads(v), heads(a)
    w_h = jnp.exp(heads(log_w))

    def step(state, inp):
        r_t, w_t, k_t, v_t, kk_t, a_t = inp
        s_kk = jnp.einsum('bhvk,bhk->bhv', state, kk_t)
        state = (state * w_t[:, :, None, :]
                 - s_kk[..., None] * (kk_t * a_t)[:, :, None, :]
                 + v_t[..., None] * k_t[:, :, None, :])
        return state, jnp.einsum('bhvk,bhk->bhv', state, r_t)

    seq_first = lambda t: jnp.moveaxis(t, 1, 0)
    state0 = jnp.zeros((B, H, N, N), jnp.float32)
    _, y = lax.scan(step, state0, (seq_first(r_h), seq_first(w_h), seq_first(k_h),
                                   seq_first(v_h), seq_first(kk), seq_first(a_h)))
    y = jnp.moveaxis(y, 0, 1)
    y = _head_norm(y, ln_g, ln_b, RWKV_GN_EPS)
    y = y + jnp.sum(r_h * k_h * r_k, -1, keepdims=True) * v_h
    return y.reshape(B, S, RWKV_WIDTH) * g


def _mlstm_chunkwise(q, k, v, ig, lf):
    B, S, H, D = q.shape
    L = MLSTM_CHUNK
    NC = S // L

    def to_chunks(t):
        t = t.reshape((B, NC, L, H) + t.shape[3:])
        return jnp.moveaxis(t, (1, 3), (0, 2))

    causal = jnp.tril(jnp.ones((L, L), dtype=bool))

    def body(carry, inp):
        C, n, m = carry
        qc, kc, vc, igc, lfc = inp
        b = jnp.cumsum(lfc, axis=-1)
        dmat = jnp.where(causal, b[..., :, None] - b[..., None, :] + igc[..., None, :], -jnp.inf)
        inter = b + m[..., None]
        m_t = jnp.maximum(inter, jnp.max(dmat, -1))
        weights = jnp.exp(dmat - m_t[..., None])
        sc = jnp.einsum('bhtd,bhsd->bhts', qc, kc) * weights
        carry_in = jnp.exp(inter - m_t)
        num = (jnp.einsum('bhts,bhsd->bhtd', sc, vc)
               + carry_in[..., None] * jnp.einsum('bhtk,bhkv->bhtv', qc, C))
        den = jnp.sum(sc, -1) + carry_in * jnp.einsum('bhtk,bhk->bht', qc, n)
        h = num / jnp.maximum(jnp.abs(den), jnp.exp(-m_t))[..., None]
        b_last = b[..., -1]
        gs = b_last[..., None] - b + igc
        m_new = jnp.maximum(b_last + m, jnp.max(gs, -1))
        ws = jnp.exp(gs - m_new[..., None])
        keep = jnp.exp(b_last + m - m_new)
        C = keep[..., None, None] * C + jnp.einsum('bhs,bhsk,bhsv->bhkv', ws, kc, vc)
        n = keep[..., None] * n + jnp.einsum('bhs,bhsk->bhk', ws, kc)
        return (C, n, m_new), h

    f32 = jnp.float32
    init = (jnp.zeros((B, H, D, D), f32), jnp.zeros((B, H, D), f32), jnp.zeros((B, H), f32))
    _, h = lax.scan(body, init, (to_chunks(q), to_chunks(k), to_chunks(v),
                                 to_chunks(ig), to_chunks(lf)))
    return jnp.moveaxis(h, (0, 2), (1, 3)).reshape(B, S, H, D)


def _mlstm_mix(p, conv_w, conv_b, ig_b, fg_b, hn_g, hn_b):
    B, S, _ = p.shape
    H, N = MLSTM_HEADS, MLSTM_HEAD
    q, k, v, o, ig, fg = _split_cols(p, (MLSTM_WIDTH, MLSTM_WIDTH, MLSTM_WIDTH, MLSTM_WIDTH, H, H))
    qk = jax.nn.silu(_causal_conv(jnp.concatenate([q, k], -1), conv_w, conv_b))
    q, k = qk[..., :MLSTM_WIDTH], qk[..., MLSTM_WIDTH:]
    heads = lambda t: t.reshape(B, S, H, N)
    q = heads(q) * (N ** -0.5)
    ig = ig + ig_b
    lf = jax.nn.log_sigmoid(fg + fg_b)
    h = _mlstm_chunkwise(q, heads(k), heads(v), ig, lf)
    h = _head_norm(h, hn_g, hn_b, LN_EPS).reshape(B, S, MLSTM_WIDTH)
    return jax.nn.sigmoid(o) * h


def _even_mixer(x, w_in, mu, w0, w_up, a0, a_up, g_up, k_k, k_a, r_k, rln_g, rln_b,
                conv_w, conv_b, ig_b, fg_b, hn_g, hn_b, w_out):
    p = x @ w_in
    y_a = _rwkv_pallas(p[..., :RWKV_COLS], mu, w0, w_up, a0, a_up, g_up, k_k, k_a, r_k, rln_g, rln_b)
    y_b = _mlstm_mix(p[..., RWKV_COLS:], conv_w, conv_b, ig_b, fg_b, hn_g, hn_b)
    return jnp.concatenate([y_a, y_b], -1) @ w_out


def _fox_attention(q, k, v, logf):
    B, S, H, D = q.shape
    NB = S // FOX_QBLOCK
    c = jnp.cumsum(logf, axis=1)
    c_key = jnp.transpose(c, (0, 2, 1))[:, :, None, :]
    q_blocks = jnp.moveaxis(q.reshape(B, NB, FOX_QBLOCK, H, D), 1, 0)
    c_blocks = jnp.moveaxis(c.reshape(B, NB, FOX_QBLOCK, H), 1, 0)
    key_pos = jnp.arange(S)
    scale = D ** -0.5

    def one_block(args):
        qb, cb, blk = args
        q_pos = blk * FOX_QBLOCK + jnp.arange(FOX_QBLOCK)
        logits = jnp.einsum('bqhd,bkhd->bhqk', qb, k) * scale
        logits = logits + jnp.transpose(cb, (0, 2, 1))[..., None] - c_key
        logits = jnp.where(key_pos[None, :] <= q_pos[:, None], logits, -jnp.inf)
        probs = jax.nn.softmax(logits, axis=-1)
        return jnp.einsum('bhqk,bkhd->bqhd', probs, v)

    out = lax.map(one_block, (q_blocks, c_blocks, jnp.arange(NB)))
    return jnp.moveaxis(out, 0, 1).reshape(B, S, H, D)


def _odd_mixer(x, w_in, qn_g, kn_g, f_b, w_out):
    B, S, _ = x.shape
    p = x @ w_in
    q, k, v, og, fl = _split_cols(p, (FOX_WIDTH, FOX_WIDTH, FOX_WIDTH, FOX_WIDTH, FOX_HEADS))
    heads = lambda t: t.reshape(B, S, FOX_HEADS, FOX_HEAD)
    q = _rms_norm(heads(q), qn_g)
    k = _rms_norm(heads(k), kn_g)
    logf = jax.nn.log_sigmoid(fl + f_b)
    o = _fox_attention(q, k, heads(v), logf).reshape(B, S, FOX_WIDTH)
    return (jax.nn.sigmoid(og) * o) @ w_out


def _peer_ffn(x, w_q, sub_k1, sub_k2, u_tab, v_tab):
    B, S, Dm = x.shape
    K = PEER_TOPK
    xt = x.reshape((B * S) // PEER_TOKBLOCK, PEER_TOKBLOCK, Dm)

    def block(xb):
        T = xb.shape[0]
        q = (xb @ w_q).reshape(T, PEER_HEADS, 2, PEER_DHALF)
        s1 = jnp.einsum('thd,hnd->thn', q[:, :, 0], sub_k1)
        s2 = jnp.einsum('thd,hnd->thn', q[:, :, 1], sub_k2)
        v1, i1 = lax.top_k(s1, K)
        v2, i2 = lax.top_k(s2, K)
        cand = (v1[..., :, None] + v2[..., None, :]).reshape(T, PEER_HEADS, K * K)
        sc, ci = lax.top_k(cand, K)
        e1 = jnp.take_along_axis(i1, ci // K, axis=-1)
        e2 = jnp.take_along_axis(i2, ci % K, axis=-1)
        eid = e1 * PEER_NKEYS + e2
        gate = jax.nn.softmax(sc, axis=-1)
        act = jax.nn.gelu(jnp.einsum('td,thkd->thk', xb, u_tab[eid]), approximate=False)
        return jnp.einsum('thk,thkd->td', gate * act, v_tab[eid])

    return lax.map(block, xt).reshape(B, S, Dm)


PEER_EXPERTS = PEER_NKEYS * PEER_NKEYS
PEER_SELECT_TILE = 256
PEER_TOK_TILE = 512
PEER_EXPERT_CHUNK = 1024
PEER_E1_GROUP = 4
PEER_E2_SUB = 32
PEER_A_PIECES = 8
PEER_C_PIECES = 4
LANES = 128
MXU_DIM = 256
INV_SQRT2 = 0.7071067811865476
_NT = (((1,), (1,)), ((), ()))


def _topk_desc(work, k):
    vals = []
    for _ in range(k):
        m = jnp.max(work, axis=0, keepdims=True)
        vals.append(m)
        work = jnp.where(work >= m, -jnp.inf, work)
    return vals


def _oddeven_merge(lo, hi, r):
    step = r * 2
    if step < hi - lo:
        yield from _oddeven_merge(lo, hi, step)
        yield from _oddeven_merge(lo + r, hi, step)
        yield from [(i, i + r) for i in range(lo + r, hi - r, step)]
    else:
        yield (lo, lo + r)


def _oddeven_sort(lo, hi):
    if hi - lo >= 1:
        mid = lo + (hi - lo) // 2
        yield from _oddeven_sort(lo, mid)
        yield from _oddeven_sort(mid + 1, hi)
        yield from _oddeven_merge(lo, hi, 1)


SUBLANES = 8


def _sorted_top(s, k):
    v = [s[i * SUBLANES:(i + 1) * SUBLANES, :] for i in range(k)]

    def exchange(i, j):
        v[i], v[j] = jnp.maximum(v[i], v[j]), jnp.minimum(v[i], v[j])

    for i, j in _oddeven_sort(0, k - 1):
        exchange(i, j)
    shift = SUBLANES // 2
    while shift >= 1:
        v = [jnp.maximum(v[i], pltpu.roll(v[k - 1 - i], shift, axis=0)) for i in range(k)]
        d = k // 2
        while d >= 1:
            for i in range(k):
                if i & d == 0:
                    exchange(i, i + d)
            d //= 2
        shift //= 2
    return [t[0:1, :] for t in v]


def _peer_select_kernel(x_ref, wqt_ref, k1_ref, k2_ref,
                        phi_ref, a1_ref, a2_ref, qt_ref):
    f32, bf16 = jnp.float32, jnp.bfloat16
    K = PEER_TOPK
    qt_ref[...] = lax.dot_general(wqt_ref[...], x_ref[...].astype(bf16), _NT, preferred_element_type=f32)
    n_groups = x_ref.shape[0] // LANES

    def head_pair(hp, carry):
        hs = [2 * hp, 2 * hp + 1]
        sc = []
        for h in hs:
            base = pl.multiple_of(h * PEER_DKEY, PEER_DKEY)
            q1 = qt_ref[pl.ds(base, PEER_DHALF), :].astype(bf16)
            q2 = qt_ref[pl.ds(base + PEER_DHALF, PEER_DHALF), :].astype(bf16)
            sc.append((jnp.dot(k1_ref[h], q1, preferred_element_type=f32),
                       jnp.dot(k2_ref[h], q2, preferred_element_type=f32)))
        items = [(j, c) for j in range(len(hs)) for c in range(n_groups)]
        lanes = lambda c: slice(c * LANES, (c + 1) * LANES)
        s1c = [sc[j][0][:, lanes(c)] for j, c in items]
        s2c = [sc[j][1][:, lanes(c)] for j, c in items]
        it = range(len(items))
        v1 = [_sorted_top(s1c[i], K) for i in it]
        v2 = [_sorted_top(s2c[i], K) for i in it]
        pairs = [(a, b) for a in range(K) for b in range(K) if (a + 1) * (b + 1) <= K + 1]
        pad = [jnp.full_like(v1[0][0], -jnp.inf)] * (-len(pairs) % SUBLANES)
        cand = [jnp.concatenate([v1[i][a] + v2[i][b] for a, b in pairs] + pad, axis=0) for i in it]
        vc = [_topk_desc(cand[i], K + 1) for i in it]
        inv_z = [1.0 / jnp.sum(jnp.where(cand[i] >= vc[i][K - 1], jnp.exp(cand[i] - vc[i][0]), 0.0),
                               axis=0, keepdims=True) for i in it]
        for i, (j, c) in enumerate(items):
            h, sl = hs[j], lanes(c)
            cut = 0.5 * (vc[i][K - 1] + vc[i][K])
            s1m = jnp.where(s1c[i] >= v1[i][K - 1], s1c[i], -jnp.inf)
            s2m = jnp.where(s2c[i] >= v2[i][K - 1], s2c[i], -jnp.inf)
            phi_ref[h, :, sl] = jnp.exp(cut - s1m - v2[i][0]) * inv_z[i]
            a1_ref[h, :, sl] = jnp.exp(s1m - v1[i][0])
            a2_ref[h, :, sl] = jnp.exp(s2m - v2[i][0]) * inv_z[i]
        return carry

    lax.fori_loop(0, PEER_HEADS // 2, head_pair, 0)


def _peer_select(xb, wqt, k1, k2):
    n, d = xb.shape
    T = PEER_SELECT_TILE
    H, NK = PEER_HEADS, PEER_NKEYS
    full = lambda shape: pl.BlockSpec(shape, lambda i: (0,) * len(shape))
    tok3 = lambda rows: pl.BlockSpec((H, rows, T), lambda i: (0, 0, i))
    f32 = jnp.float32
    return pl.pallas_call(
        _peer_select_kernel,
        grid=(n // T,),
        in_specs=[pl.BlockSpec((T, d), lambda i: (i, 0)), full(wqt.shape), full(k1.shape), full(k2.shape)],
        out_specs=[tok3(NK)] * 3,
        out_shape=[jax.ShapeDtypeStruct((H, NK, n), f32)] * 3,
        scratch_shapes=[pltpu.VMEM((H * PEER_DKEY, T), f32)],
        name="peer_select",
    )(xb, wqt, k1, k2)


def _peer_expert_kernel(xt_ref, u_ref, v_ref, phi_ref, a1_ref, a2_ref,
                        o_ref, acc_ref, h0_ref, h1_ref, w0_ref, w1_ref):
    f32, bf16 = jnp.float32, jnp.bfloat16
    s = pl.program_id(0)
    n_chunks = PEER_EXPERTS // PEER_EXPERT_CHUNK
    c_chunk = jnp.maximum(s - 2, 0) % n_chunks

    @pl.when(s == 0)
    def _():
        for ref in (h0_ref, h1_ref, w0_ref, w1_ref):
            ref[...] = jnp.zeros_like(ref)

    @pl.when(c_chunk == 0)
    def _():
        acc_ref[...] = jnp.zeros_like(acc_ref)

    T = xt_ref.shape[1]
    NK, G, SB = PEER_NKEYS, PEER_E1_GROUP, PEER_E2_SUB
    CE = PEER_EXPERT_CHUNK
    PA, PC = PEER_A_PIECES, PEER_C_PIECES

    def step(ha_ref, hb_ref, wb_ref, wc_ref):
        def stage_a(k):
            rows = CE // PA
            u_blk = pltpu.bitcast(u_ref[k * rows // 2:(k + 1) * rows // 2, :], bf16)
            ha_ref[k * rows:(k + 1) * rows, :] = jnp.dot(u_blk, xt_ref[...], preferred_element_type=f32)

        def stage_b(q, c, sb):
            ts = slice(c * LANES, (c + 1) * LANES)
            e2s = slice(sb * SB, (sb + 1) * SB)
            g = [None] * G
            for hd in range(PEER_HEADS):
                a2 = a2_ref[hd, e2s, ts]
                for e in range(G):
                    r = q * G + e
                    term = jnp.where(a2 > phi_ref[hd, r:r + 1, ts], a1_ref[hd, r:r + 1, ts] * a2, 0.0)
                    g[e] = term if g[e] is None else g[e] + term
            for e in range(G):
                lo = (q * G + e) * NK + sb * SB
                hh = hb_ref[lo:lo + SB, ts]
                wb_ref[lo:lo + SB, ts] = (g[e] * (0.5 * hh * (1.0 + lax.erf(hh * INV_SQRT2)))).astype(bf16)

        b_blocks = [(q, c, sb) for q in range(CE // (G * NK)) for c in range(T // LANES)
                    for sb in range(NK // SB)]
        def stage_c(k):
            rows = slice(k * CE // PC, (k + 1) * CE // PC)
            acc_ref[...] += lax.dot_general(wc_ref[rows, :], v_ref[rows, :], _TN,
                                            preferred_element_type=f32)

        nb = len(b_blocks)
        mxu_pieces = [functools.partial(stage_a, k) for k in range(PA)]
        mxu_pieces[1:1] = [functools.partial(stage_c, k) for k in range(PC)]
        every = nb // len(mxu_pieces)
        for t, blk in enumerate(b_blocks):
            stage_b(*blk)
            if t % every == 0 and t // every < len(mxu_pieces):
                mxu_pieces[t // every]()

    pl.when(s % 2 == 0)(functools.partial(step, h0_ref, h1_ref, w1_ref, w0_ref))
    pl.when(s % 2 == 1)(functools.partial(step, h1_ref, h0_ref, w0_ref, w1_ref))

    @pl.when(c_chunk == n_chunks - 1)
    def _():
        o_ref[...] = acc_ref[...]


def _peer_experts(xtb, ub, vb, phi, a1, a2):
    d, n = xtb.shape
    T, CE = PEER_TOK_TILE, PEER_EXPERT_CHUNK
    H, NK = PEER_HEADS, PEER_NKEYS
    e1_per_chunk = CE // NK
    nc = PEER_EXPERTS // CE
    last = (n // T) * nc - 1
    at = lambda s, lag: jnp.clip(s - lag, 0, last)
    return pl.pallas_call(
        _peer_expert_kernel,
        grid=(last + 3,),
        in_specs=[
            pl.BlockSpec((d, T), lambda s: (0, at(s, 0) // nc)),
            pl.BlockSpec((CE // 2, d), lambda s: (at(s, 0) % nc, 0)),
            pl.BlockSpec((CE, d), lambda s: (at(s, 2) % nc, 0)),
            pl.BlockSpec((H, e1_per_chunk, T), lambda s: (0, at(s, 1) % nc, at(s, 1) // nc)),
            pl.BlockSpec((H, e1_per_chunk, T), lambda s: (0, at(s, 1) % nc, at(s, 1) // nc)),
            pl.BlockSpec((H, NK, T), lambda s: (0, 0, at(s, 1) // nc)),
        ],
        out_specs=pl.BlockSpec((T, d), lambda s: (at(s, 2) // nc, 0)),
        out_shape=jax.ShapeDtypeStruct((n, d), jnp.float32),
        scratch_shapes=[pltpu.VMEM((T, d), jnp.float32),
                        pltpu.VMEM((CE, T), jnp.float32), pltpu.VMEM((CE, T), jnp.float32),
                        pltpu.VMEM((CE, T), jnp.bfloat16), pltpu.VMEM((CE, T), jnp.bfloat16)],
        compiler_params=pltpu.CompilerParams(dimension_semantics=("arbitrary",),
                                             vmem_limit_bytes=VMEM_LIMIT_BYTES),
        name="peer_experts",
    )(xtb, ub, vb, phi, a1, a2)


def _pack_row_pairs(t):
    rows, cols = t.shape
    half = lambda part: lax.bitcast_convert_type(part.astype(jnp.bfloat16), jnp.uint16).astype(jnp.uint32)
    pairs = t.reshape(rows // 2, 2 * cols)
    return half(pairs[:, :cols]) | (half(pairs[:, cols:]) << 16)


def _peer_pallas(x2, xt_bf16, w_q, sub_k1, sub_k2, u_tab, v_tab):
    bf16 = jnp.bfloat16
    sel = _peer_select(x2, w_q.T.astype(bf16), sub_k1.astype(bf16), sub_k2.astype(bf16))
    return _peer_experts(xt_bf16, _pack_row_pairs(u_tab), v_tab.astype(bf16), *sel)


RWKV_CHUNK = 64
RWKV_BATCH_PER_STEP = 2
RWKV_PAIR = LANES // RWKV_HEAD
RWKV_PAIRS = RWKV_HEADS // RWKV_PAIR
_NN = (((1,), (0,)), ((), ()))
_TN = (((0,), (0,)), ((), ()))


def _split_bf16(a):
    hi = a.astype(jnp.bfloat16)
    lo = (a - hi.astype(jnp.float32)).astype(jnp.bfloat16)
    return hi, lo


def _mm2(a, b, dims):
    d = lambda u, v: lax.dot_general(u, v, dims, preferred_element_type=jnp.float32)
    (ah, al), (bh, bl) = _split_bf16(a), _split_bf16(b)
    return d(ah, bh) + d(ah, bl) + d(al, bh)


def _mm_exact_rhs(a, b_exact, dims, terms=2):
    d = lambda u: lax.dot_general(u, b_exact, dims, preferred_element_type=jnp.float32)
    out, rem = None, a
    for _ in range(terms):
        piece = rem.astype(jnp.bfloat16)
        rem = rem - piece.astype(jnp.float32)
        out = d(piece) if out is None else out + d(piece)
    return out


def _mm_exact_lhs(a_exact, b, dims, terms=3):
    d = lambda v: lax.dot_general(a_exact, v, dims, preferred_element_type=jnp.float32)
    out, rem = None, b
    for _ in range(terms):
        piece = rem.astype(jnp.bfloat16)
        rem = rem - piece.astype(jnp.float32)
        out = d(piece) if out is None else out + d(piece)
    return out


def _softplus(z):
    return jnp.maximum(z, 0.0) + jnp.log1p(jnp.exp(-jnp.abs(z)))


def _rwkv_kernel(p_ref, prev_ref, mu_ref, w0_ref, wup_ref, a0_ref, aup_ref, gup_ref,
                 kk_ref, ka_ref, rk_ref, lng_ref, lnb_ref, ones_ref,
                 o_ref, state_ref):
    f32, bf16 = jnp.float32, jnp.bfloat16
    L, W = RWKV_CHUNK, RWKV_WIDTH
    c_idx = pl.program_id(1)

    @pl.when(c_idx == 0)
    def _():
        state_ref[...] = jnp.zeros_like(state_ref)

    NB = p_ref.shape[0]
    parts = []
    for bi in range(NB):
        pb = p_ref[bi]
        row = lax.broadcasted_iota(jnp.int32, pb.shape, 0)
        prev_row = jnp.where(c_idx == 0, 0.0, prev_ref[bi, 7:8, :])
        shifted = jnp.where(row == 0, prev_row, pltpu.roll(pb, 1, axis=0))
        parts.append(pb + (shifted - pb) * mu_ref[...])
    p = jnp.concatenate(parts, axis=0)
    r, k, v = p[:, 0:W], p[:, W:2 * W], p[:, 2 * W:3 * W]
    o = 3 * W
    xw = p[:, o:o + RWKV_W_RANK]
    xa = p[:, o + RWKV_W_RANK:o + RWKV_W_RANK + RWKV_A_RANK]
    xg = p[:, o + RWKV_W_RANK + RWKV_A_RANK:]
    dotd = lambda u, m: jnp.dot(u.astype(bf16), m, preferred_element_type=f32)
    lw = -jnp.exp(-_softplus(-(w0_ref[...] + dotd(jnp.tanh(xw), wup_ref[...]))) - 0.5)
    a = jax.nn.sigmoid(a0_ref[...] + dotd(xa, aup_ref[...]))
    g = dotd(jax.nn.sigmoid(xg), gup_ref[...])
    ones_bd = ones_ref[...]
    head_sum = lambda t: _mm_exact_rhs(t, ones_bd, _NN)
    kk = k * kk_ref[...]
    kk = kk / jnp.maximum(jnp.sqrt(head_sum(kk * kk)), 1e-12)
    k = k * (1.0 + (a - 1.0) * ka_ref[...])

    lane = lax.broadcasted_iota(jnp.int32, (L, LANES), 1)
    trow = lax.broadcasted_iota(jnp.int32, (L, LANES), 0)
    s_in = lane % RWKV_HEAD
    strict, incl = s_in < trow, s_in <= trow
    eye_pair = (s_in == trow).astype(f32)
    m0 = lane < RWKV_HEAD
    tril = (lax.broadcasted_iota(jnp.int32, (L, L), 1)
            <= lax.broadcasted_iota(jnp.int32, (L, L), 0)).astype(bf16)
    r2 = lax.broadcasted_iota(jnp.int32, (LANES, LANES), 0) // RWKV_HEAD
    c2 = lax.broadcasted_iota(jnp.int32, (LANES, LANES), 1) // RWKV_HEAD
    bd_mask = r2 == c2

    def bd(t):
        return jnp.concatenate([jnp.where(m0, t, 0.0), jnp.where(m0, 0.0, t)], axis=0)

    pairs = range(NB * RWKV_PAIRS)
    cut = lambda t: [t[bi * L:(bi + 1) * L, pr * LANES:(pr + 1) * LANES]
                     for bi in range(NB) for pr in range(RWKV_PAIRS)]
    rp, kp, vp, kkp, ap, lwp = cut(r), cut(k), cut(v), cut(kk), cut(a), cut(lw)
    c = [_mm_exact_lhs(tril, lwp[p], _NN) for p in pairs]
    at = [-kkp[p] * jnp.exp(c[p] - lwp[p]) for p in pairs]
    eni = [jnp.exp(-c[p]) for p in pairs]
    bt = [kkp[p] * ap[p] * eni[p] for p in pairs]
    kt = [kp[p] * eni[p] for p in pairs]
    rt = [rp[p] * jnp.exp(c[p]) for p in pairs]
    g_last = [jnp.exp(c[p][L - 1:L, :]) for p in pairs]
    gram = [_mm2(jnp.concatenate([at[p], rt[p]], axis=0),
                 jnp.concatenate([bd(bt[p]), bd(kt[p])], axis=0), _NT) for p in pairs]
    n_ab = [jnp.where(strict, gram[p][0:L, 0:LANES], 0.0) for p in pairs]
    a_ak = [jnp.where(strict, gram[p][0:L, LANES:], 0.0) for p in pairs]
    a_rb = [jnp.where(incl, gram[p][L:, 0:LANES], 0.0) for p in pairs]
    a_rk = [jnp.where(incl, gram[p][L:, LANES:], 0.0) for p in pairs]
    tinv, m = [eye_pair + n_ab[p] for p in pairs], n_ab
    for _ in range(5):
        m = [_mm2(m[p], bd(m[p]), _NN) for p in pairs]
        tinv = [tinv[p] + _mm2(m[p], bd(tinv[p]), _NN) for p in pairs]
    s0 = [state_ref[p] for p in pairs]
    pq = [_mm2(at[p], s0[p], _NT) + _mm2(a_ak[p], bd(vp[p]), _NN) for p in pairs]
    u = [_mm2(tinv[p], bd(pq[p]), _NN) for p in pairs]
    ys = [_mm2(rt[p], s0[p], _NT)
          + _mm2(jnp.concatenate([a_rb[p], a_rk[p]], axis=1),
                 jnp.concatenate([bd(u[p]), bd(vp[p])], axis=0), _NN) for p in pairs]
    for p in pairs:
        upd = _mm2(jnp.concatenate([u[p], vp[p]], axis=0),
                   jnp.concatenate([bt[p], kt[p]], axis=0), _TN)
        state_ref[p] = (s0[p] + jnp.where(bd_mask, upd, 0.0)) * g_last[p]
    y = jnp.concatenate([jnp.concatenate(ys[bi * RWKV_PAIRS:(bi + 1) * RWKV_PAIRS], axis=1)
                         for bi in range(NB)], axis=0)

    inv_n = 1.0 / RWKV_HEAD
    mean = head_sum(y) * inv_n
    yc = y - mean
    var = head_sum(yc * yc) * inv_n
    y = yc * lax.rsqrt(var + RWKV_GN_EPS) * lng_ref[...] + lnb_ref[...]
    y = y + head_sum(r * k * rk_ref[...]) * v
    out = y * g
    for bi in range(NB):
        o_ref[bi] = out[bi * L:(bi + 1) * L]


def _rwkv_pallas(p_r, mu, w0, w_up, a0, a_up, g_up, k_k, k_a, r_k, ln_g, ln_b):
    B, S, C = p_r.shape
    L, W, NB = RWKV_CHUNK, RWKV_WIDTH, RWKV_BATCH_PER_STEP
    bf16 = jnp.bfloat16
    vecw = lambda t: t.reshape(1, W)
    ones_bd = jnp.kron(jnp.eye(RWKV_HEADS, dtype=bf16), jnp.ones((RWKV_HEAD, RWKV_HEAD), bf16))
    full = lambda a: pl.BlockSpec(a.shape, lambda b, c: (0,) * a.ndim)
    args = [mu.reshape(1, C), vecw(w0), w_up.astype(bf16), vecw(a0), a_up.astype(bf16),
            g_up.astype(bf16), vecw(k_k), vecw(k_a), vecw(r_k), vecw(ln_g), vecw(ln_b), ones_bd]
    return pl.pallas_call(
        _rwkv_kernel,
        grid=(B // NB, S // L),
        in_specs=[pl.BlockSpec((NB, L, C), lambda b, c: (b, c, 0)),
                  pl.BlockSpec((NB, 8, C), lambda b, c: (b, jnp.maximum(c * (L // 8) - 1, 0), 0))]
                 + [full(a) for a in args],
        out_specs=pl.BlockSpec((NB, L, W), lambda b, c: (b, c, 0)),
        out_shape=jax.ShapeDtypeStruct((B, S, W), jnp.float32),
        scratch_shapes=[pltpu.VMEM((NB * RWKV_PAIRS, LANES, LANES), jnp.float32)],
        compiler_params=pltpu.CompilerParams(dimension_semantics=("arbitrary", "arbitrary")),
        name="rwkv7",
    )(p_r, p_r, *args)


MLSTM_CONV = 4
MLSTM_GATE_LANES = LANES


def _mlstm_kernel(p_ref, prev_ref, cw_ref, cb_ref, igb_ref, fgb_ref, hng_ref, hnb_ref,
                  o_ref, c_ref, n_ref, m_ref):
    f32, bf16 = jnp.float32, jnp.bfloat16
    L, W, H, D = MLSTM_CHUNK, MLSTM_WIDTH, MLSTM_HEADS, MLSTM_HEAD
    c_idx = pl.program_id(1)

    @pl.when(c_idx == 0)
    def _():
        c_ref[...] = jnp.zeros_like(c_ref)
        n_ref[...] = jnp.zeros_like(n_ref)
        m_ref[...] = jnp.zeros_like(m_ref)

    p = p_ref[0]
    z = p[:, 0:2 * W]
    prev = jnp.where(c_idx == 0, 0.0, prev_ref[0, :, 0:2 * W])
    ext = jnp.concatenate([prev, z], axis=0)
    conv = cb_ref[...] + cw_ref[MLSTM_CONV - 1:MLSTM_CONV, :] * z
    for j in range(MLSTM_CONV - 1):
        d = MLSTM_CONV - 1 - j
        conv = conv + cw_ref[j:j + 1, :] * ext[8 - d:8 - d + L, :]
    qk = conv * jax.nn.sigmoid(conv)
    q_all, k_all = qk[:, 0:W] * (D ** -0.5), qk[:, W:2 * W]
    v_all, o_all = p[:, 2 * W:3 * W], p[:, 3 * W:4 * W]
    gates = p[:, 4 * W:]
    igl = gates + igb_ref[...]
    lfl = -_softplus(-(gates + fgb_ref[...]))
    tril = (lax.broadcasted_iota(jnp.int32, (L, L), 1)
            <= lax.broadcasted_iota(jnp.int32, (L, L), 0))
    bcum = _mm_exact_lhs(tril.astype(bf16), lfl, _NN)
    e_all = igl - pltpu.roll(bcum, LANES - H, axis=1)
    lane = lax.broadcasted_iota(jnp.int32, (L, LANES), 1)
    hd = range(H)
    cols = lambda t: [t[:, h * D:(h + 1) * D] for h in hd]
    q, k, v, o = cols(q_all), cols(k_all), cols(v_all), cols(o_all)
    qb, kb, vb = ([t.astype(bf16) for t in ts] for ts in (q, k, v))
    b_col = [bcum[:, H + h:H + h + 1] for h in hd]
    ig_col = [igl[:, h:h + 1] for h in hd]
    e_row = [_mm_exact_lhs((lane == h).astype(bf16), e_all, _NT) for h in hd]
    qk = [lax.dot_general(qb[h], kb[h], _NT, preferred_element_type=f32) for h in hd]
    m_prev = [m_ref[h, 0:1, 0:1] for h in hd]
    c_prev = [c_ref[h] for h in hd]
    n_prev = [n_ref[h, 0:1, :] for h in hd]
    qc = [jnp.dot(qb[h], c_prev[h].astype(bf16), preferred_element_type=f32) for h in hd]
    dmat = [jnp.where(tril, b_col[h] + e_row[h], -jnp.inf) for h in hd]
    inter = [b_col[h] + m_prev[h] for h in hd]
    m_t = [jnp.maximum(inter[h], jnp.max(dmat[h], axis=-1, keepdims=True)) for h in hd]
    sc = [qk[h] * jnp.exp(dmat[h] - m_t[h]) for h in hd]
    carry_in = [jnp.exp(inter[h] - m_t[h]) for h in hd]
    sv = [jnp.dot(sc[h].astype(bf16), vb[h], preferred_element_type=f32) for h in hd]
    den = [jnp.sum(sc[h], axis=-1, keepdims=True)
           + carry_in[h] * jnp.sum(q[h] * n_prev[h], axis=-1, keepdims=True) for h in hd]
    hval = [(sv[h] + carry_in[h] * qc[h]) / jnp.maximum(jnp.abs(den[h]), jnp.exp(-m_t[h])) for h in hd]
    b_last = [b_col[h][L - 1:L, :] for h in hd]
    gs = [b_last[h] - b_col[h] + ig_col[h] for h in hd]
    m_new = [jnp.maximum(b_last[h] + m_prev[h], jnp.max(gs[h], axis=0, keepdims=True)) for h in hd]
    keep = [jnp.exp(b_last[h] + m_prev[h] - m_new[h]) for h in hd]
    wk = [jnp.exp(gs[h] - m_new[h]) * k[h] for h in hd]
    kv = [lax.dot_general(wk[h].astype(bf16), vb[h], _TN, preferred_element_type=f32) for h in hd]
    outs = []
    for h in hd:
        c_ref[h] = keep[h] * c_prev[h] + kv[h]
        n_ref[h] = jnp.broadcast_to(keep[h] * n_prev[h] + jnp.sum(wk[h], axis=0, keepdims=True), (8, D))
        m_ref[h] = jnp.broadcast_to(m_new[h], (8, LANES))
        mu = jnp.mean(hval[h], axis=-1, keepdims=True)
        hc = hval[h] - mu
        var = jnp.mean(hc * hc, axis=-1, keepdims=True)
        hs = slice(h * D, (h + 1) * D)
        hn = hc * lax.rsqrt(var + LN_EPS) * hng_ref[:, hs] + hnb_ref[:, hs]
        outs.append(jax.nn.sigmoid(o[h]) * hn)
    o_ref[0] = jnp.concatenate(outs, axis=1)


def _mlstm_pallas(p_m, conv_w, conv_b, ig_b, fg_b, hn_g, hn_b):
    B, S, C = p_m.shape
    L, W, H, D = MLSTM_CHUNK, MLSTM_WIDTH, MLSTM_HEADS, MLSTM_HEAD
    pad = lambda t, off: jnp.zeros((1, LANES), jnp.float32).at[0, off:off + H].set(t)
    args = [conv_w, conv_b.reshape(1, 2 * W), pad(ig_b, 0), pad(fg_b, H),
            hn_g.reshape(1, W), hn_b.reshape(1, W)]
    full = lambda a: pl.BlockSpec(a.shape, lambda b, c: (0,) * a.ndim)
    return pl.pallas_call(
        _mlstm_kernel,
        grid=(B, S // L),
        in_specs=[pl.BlockSpec((1, L, C), lambda b, c: (b, c, 0)),
                  pl.BlockSpec((1, 8, C), lambda b, c: (b, jnp.maximum(c * (L // 8) - 1, 0), 0))]
                 + [full(a) for a in args],
        out_specs=pl.BlockSpec((1, L, W), lambda b, c: (b, c, 0)),
        out_shape=jax.ShapeDtypeStruct((B, S, W), jnp.float32),
        scratch_shapes=[pltpu.VMEM((H, D, D), jnp.float32),
                        pltpu.VMEM((H, 8, D), jnp.float32),
                        pltpu.VMEM((H, 8, LANES), jnp.float32)],
        compiler_params=pltpu.CompilerParams(dimension_semantics=("arbitrary", "arbitrary")),
        name="mlstm",
    )(p_m, p_m, *args)


FOX_TILE = 256
FOX_KEY_GROUP = 4
FOX_HEADS_PER_STEP = 8
FOX_AUG = LANES
FOX_GATE_LANES = LANES


def _fox_prep_kernel(p_ref, qg_ref, kg_ref, fb_ref, ones_ref, qa_ref, ka_ref, vt_ref, carry_ref):
    f32, bf16 = jnp.float32, jnp.bfloat16
    T, W, H, D = FOX_TILE, FOX_WIDTH, FOX_HEADS, FOX_HEAD
    i = pl.program_id(1)

    @pl.when(i == 0)
    def _():
        carry_ref[...] = jnp.zeros_like(carry_ref)

    p = p_ref[0]
    ones_bd = ones_ref[...]
    ms = lambda t: _mm_exact_rhs(t * t, ones_bd, _NN) * (1.0 / D)
    q = p[:, 0:W]
    q = q * lax.rsqrt(ms(q) + 1e-6) * qg_ref[...]
    k = p[:, W:2 * W]
    k = k * lax.rsqrt(ms(k) + 1e-6) * kg_ref[...] * (D ** -0.5)
    vt = p[:, 2 * W:3 * W].T.astype(bf16)
    logf = -_softplus(-(p[:, 4 * W:] + fb_ref[...]))
    tril = (lax.broadcasted_iota(jnp.int32, (T, T), 1)
            <= lax.broadcasted_iota(jnp.int32, (T, T), 0)).astype(bf16)
    c = _mm_exact_lhs(tril, logf, _NN) + carry_ref[0:1, :]
    carry_ref[...] = jnp.broadcast_to(c[T - 1:T, :], carry_ref.shape)
    c1 = c.astype(bf16).astype(f32)
    c2 = (c - c1).astype(bf16).astype(f32)
    c3 = (c - c1 - c2).astype(bf16).astype(f32)
    lane = lax.broadcasted_iota(jnp.int32, (T, D), 1)
    for h in range(H):
        hs = slice(h * D, (h + 1) * D)
        c1h, c2h, c3h = c1[:, h:h + 1], c2[:, h:h + 1], c3[:, h:h + 1]
        pieces = jnp.where(lane % 3 == 0, c1h, jnp.where(lane % 3 == 1, c2h, c3h))
        q_aug = jnp.where(lane < 3, 1.0, jnp.where(lane < 6, pieces, 0.0))
        k_aug = jnp.where(lane < 3, -pieces, jnp.where(lane < 6, 1.0, 0.0))
        qa_ref[0, h] = jnp.concatenate([q[:, hs], q_aug], axis=1).astype(bf16)
        ka_ref[0, h] = jnp.concatenate([k[:, hs], k_aug], axis=1).astype(bf16)
        vt_ref[0, h, 0] = vt[hs, :]


def _fox_prep(p_f, qn_g, kn_g, f_b):
    B, S, C = p_f.shape
    T, W, H, D = FOX_TILE, FOX_WIDTH, FOX_HEADS, FOX_HEAD
    bf16 = jnp.bfloat16
    ones_bd = jnp.kron(jnp.eye(H, dtype=bf16), jnp.ones((D, D), bf16))
    fb = jnp.zeros((1, FOX_GATE_LANES), jnp.float32).at[0, :H].set(f_b)
    args = [qn_g.reshape(1, W), kn_g.reshape(1, W), fb, ones_bd]
    full = lambda a: pl.BlockSpec(a.shape, lambda b, i: (0,) * a.ndim)
    return pl.pallas_call(
        _fox_prep_kernel,
        grid=(B, S // T),
        in_specs=[pl.BlockSpec((1, T, C), lambda b, i: (b, i, 0))] + [full(a) for a in args],
        out_specs=[pl.BlockSpec((1, H, T, FOX_AUG), lambda b, i: (b, 0, i, 0)),
                   pl.BlockSpec((1, H, T, FOX_AUG), lambda b, i: (b, 0, i, 0)),
                   pl.BlockSpec((1, H, 1, D, T), lambda b, i: (b, 0, i, 0, 0))],
        out_shape=[jax.ShapeDtypeStruct((B, H, S, FOX_AUG), bf16),
                   jax.ShapeDtypeStruct((B, H, S, FOX_AUG), bf16),
                   jax.ShapeDtypeStruct((B, H, S // T, D, T), bf16)],
        scratch_shapes=[pltpu.VMEM((8, FOX_GATE_LANES), jnp.float32)],
        compiler_params=pltpu.CompilerParams(dimension_semantics=("arbitrary", "arbitrary"),
                                             vmem_limit_bytes=VMEM_LIMIT_BYTES),
        name="fox_prep",
    )(p_f, *args)


def _fox_attn_kernel(qa_ref, ka_ref, vt_ref, og_ref, o_ref, m_ref, l_ref, acc_ref):
    f32, bf16 = jnp.float32, jnp.bfloat16
    T, D = FOX_TILE, FOX_HEAD
    i = pl.program_id(2)
    heads = range(FOX_HEADS_PER_STEP)
    n_blocks = ka_ref.shape[2] // T
    GK = FOX_KEY_GROUP
    rel = (lax.broadcasted_iota(jnp.int32, (GK * T, T), 0)
           - lax.broadcasted_iota(jnp.int32, (GK * T, T), 1))
    qas = [qa_ref[0, j] for j in heads]

    def key_group(g, first):
        visible = rel <= (i - g * GK) * T

        def scores(j):
            ka = ka_ref[0, j, g * GK * T:(g + 1) * GK * T, :]
            return lax.dot_general(ka, qas[j], _NT, preferred_element_type=f32)

        nxt = scores(0)
        for j in heads:
            s = nxt
            if j + 1 < len(heads):
                nxt = scores(j + 1)
            s = jnp.where(visible, s, -jnp.inf)
            m_g = jnp.max(s, axis=0, keepdims=True)
            m_new = m_g if first else jnp.maximum(m_ref[j], m_g)
            pr = jnp.exp(s - m_new)
            l_g = jnp.sum(pr, axis=0, keepdims=True)
            vt = jnp.concatenate([vt_ref[0, j, g * GK + b] for b in range(GK)], axis=1)
            pv = jnp.dot(vt, pr.astype(bf16), preferred_element_type=f32)
            if first:
                l_ref[j], acc_ref[j] = l_g, pv
            else:
                alpha = jnp.exp(m_ref[j] - m_new)
                l_ref[j] = l_ref[j] * alpha + l_g
                acc_ref[j] = acc_ref[j] * alpha + pv
            m_ref[j] = m_new

    key_group(0, True)
    for g in range(1, n_blocks // GK):
        pl.when(i >= g * GK)(functools.partial(key_group, g, False))
    outs = [(acc_ref[j] / l_ref[j]).T for j in heads]
    o_ref[0] = jax.nn.sigmoid(og_ref[0]) * jnp.concatenate(outs, axis=1)


def _fox_attn(qa, ka, vt, p_f):
    B, H, S, A = qa.shape
    T, D = FOX_TILE, FOX_HEAD
    hp = FOX_HEADS_PER_STEP
    width = hp * D
    og_block0 = 3 * FOX_WIDTH // width
    return pl.pallas_call(
        _fox_attn_kernel,
        grid=(B, H // hp, S // T),
        in_specs=[pl.BlockSpec((1, hp, T, A), lambda b, g, i: (b, g, i, 0)),
                  pl.BlockSpec((1, hp, S, A), lambda b, g, i: (b, g, 0, 0)),
                  pl.BlockSpec((1, hp, S // T, D, T), lambda b, g, i: (b, g, 0, 0, 0)),
                  pl.BlockSpec((1, T, width), lambda b, g, i: (b, i, og_block0 + g))],
        out_specs=pl.BlockSpec((1, T, width), lambda b, g, i: (b, i, g)),
        out_shape=jax.ShapeDtypeStruct((B, S, FOX_WIDTH), jnp.float32),
        scratch_shapes=[pltpu.VMEM((hp, 1, T), jnp.float32), pltpu.VMEM((hp, 1, T), jnp.float32),
                        pltpu.VMEM((hp, D, T), jnp.float32)],
        name="fox_attn",
    )(qa, ka, vt, p_f)


def kernel(x, l0_w_in, l0_rwkv_mu, l0_rwkv_w0, l0_rwkv_w_up, l0_rwkv_a0, l0_rwkv_a_up,
           l0_rwkv_g_up, l0_rwkv_k_k, l0_rwkv_k_a, l0_rwkv_r_k, l0_rwkv_ln_g, l0_rwkv_ln_b,
           l0_mlstm_conv_w, l0_mlstm_conv_b, l0_mlstm_ig_b, l0_mlstm_fg_b,
           l0_mlstm_hn_g, l0_mlstm_hn_b, l0_w_out, l0_ln1_g, l0_ln1_b,
           l0_peer_wq, l0_peer_k1, l0_peer_k2, l0_peer_u, l0_peer_v, l0_ln2_g, l0_ln2_b,
           l1_w_in, l1_fox_qn_g, l1_fox_kn_g, l1_fox_f_b, l1_w_out, l1_ln1_g, l1_ln1_b,
           l1_peer_wq, l1_peer_k1, l1_peer_k2, l1_peer_u, l1_peer_v, l1_ln2_g, l1_ln2_b):
    B, S, D = x.shape
    n = B * S
    bf16 = jnp.bfloat16
    x2 = x.reshape(n, D)

    def pad_lanes(w):
        return jnp.pad(w, ((0, 0), (0, LANES - w.shape[1])))

    m_main = 4 * MLSTM_WIDTH
    w0 = jnp.concatenate([l0_w_in[:, :RWKV_COLS + m_main],
                          pad_lanes(l0_w_in[:, RWKV_COLS + m_main:])], axis=1).astype(bf16)
    p_r, p_m = _proj(x2, w0, (RWKV_COLS, m_main + LANES))
    y_a = _rwkv_pallas(p_r.reshape(B, S, -1), l0_rwkv_mu, l0_rwkv_w0, l0_rwkv_w_up, l0_rwkv_a0,
                       l0_rwkv_a_up, l0_rwkv_g_up, l0_rwkv_k_k, l0_rwkv_k_a, l0_rwkv_r_k,
                       l0_rwkv_ln_g, l0_rwkv_ln_b)
    y_b = _mlstm_pallas(p_m.reshape(B, S, -1), l0_mlstm_conv_w, l0_mlstm_conv_b, l0_mlstm_ig_b,
                        l0_mlstm_fg_b, l0_mlstm_hn_g, l0_mlstm_hn_b)
    x2, xt = _out_proj_ln([y_a.reshape(n, -1), y_b.reshape(n, -1)], l0_w_out.astype(bf16), x2,
                          l0_ln1_g, l0_ln1_b)
    y = _peer_pallas(x2, xt, l0_peer_wq, l0_peer_k1, l0_peer_k2, l0_peer_u, l0_peer_v)
    x2 = _resid_ln(x2, y, l0_ln2_g, l0_ln2_b)

    f_main = 4 * FOX_WIDTH
    w1 = jnp.concatenate([l1_w_in[:, :f_main], pad_lanes(l1_w_in[:, f_main:])], axis=1).astype(bf16)
    (p_f,) = _proj(x2, w1, (f_main + LANES,))
    p_f = p_f.reshape(B, S, -1)
    qa, ka, vt = _fox_prep(p_f, l1_fox_qn_g, l1_fox_kn_g, l1_fox_f_b)
    o = _fox_attn(qa, ka, vt, p_f)
    x2, xt = _out_proj_ln([o.reshape(n, -1)], l1_w_out.astype(bf16), x2, l1_ln1_g, l1_ln1_b)
    y = _peer_pallas(x2, xt, l1_peer_wq, l1_peer_k1, l1_peer_k2, l1_peer_u, l1_peer_v)
    x2 = _resid_ln(x2, y, l1_ln2_g, l1_ln2_b)
    return x2.reshape(B, S, D)
```

```python
import functools

import jax
import jax.numpy as jnp
from jax import lax
from jax.experimental import pallas as pl
from jax.experimental.pallas import tpu as pltpu

D_MODEL = 1024
DEPTH = 2
DN_ALPHA = (2.0 * DEPTH) ** 0.25
LN_EPS = 1e-5

RWKV_WIDTH = D_MODEL // 2
RWKV_HEAD = 64
RWKV_HEADS = RWKV_WIDTH // RWKV_HEAD
RWKV_W_RANK = 64
RWKV_A_RANK = 64
RWKV_G_RANK = 128
RWKV_GN_EPS = 1e-5 * RWKV_HEAD
RWKV_COLS = 3 * RWKV_WIDTH + RWKV_W_RANK + RWKV_A_RANK + RWKV_G_RANK

MLSTM_WIDTH = D_MODEL // 2
MLSTM_HEAD = 128
MLSTM_HEADS = MLSTM_WIDTH // MLSTM_HEAD
MLSTM_CHUNK = 64

FOX_HEAD = 64
FOX_HEADS = D_MODEL // FOX_HEAD
FOX_WIDTH = FOX_HEADS * FOX_HEAD
FOX_QBLOCK = 128

PEER_HEADS = 8
PEER_NKEYS = 128
PEER_TOPK = 16
PEER_DKEY = 256
PEER_DHALF = PEER_DKEY // 2
PEER_TOKBLOCK = 128


def _split_cols(p, sizes):
    out, start = [], 0
    for s in sizes:
        out.append(p[..., start:start + s])
        start += s
    return out


def _resid_ln_kernel(x_ref, y_ref, g_ref, b_ref, o_ref):
    z = DN_ALPHA * x_ref[...] + y_ref[...]
    mu = jnp.mean(z, axis=-1, keepdims=True)
    zc = z - mu
    var = jnp.mean(zc * zc, axis=-1, keepdims=True)
    o_ref[...] = zc * lax.rsqrt(var + LN_EPS) * g_ref[...] + b_ref[...]


def _resid_ln(x2, y2, g, b, tm=512):
    n, d = x2.shape
    row = pl.BlockSpec((tm, d), lambda i: (i, 0))
    vec = pl.BlockSpec((1, d), lambda i: (0, 0))
    return pl.pallas_call(
        _resid_ln_kernel,
        grid=(n // tm,),
        in_specs=[row, row, vec, vec],
        out_specs=row,
        out_shape=jax.ShapeDtypeStruct((n, d), jnp.float32),
        name="resid_ln",
    )(x2, y2, g.reshape(1, d), b.reshape(1, d))


VMEM_LIMIT_BYTES = 56 * 1024 * 1024


def _proj_kernel(x_ref, w_ref, *o_refs):
    xb = x_ref[...].astype(jnp.bfloat16)
    start = 0
    for o_ref in o_refs:
        width = o_ref.shape[1]
        o_ref[...] = jnp.dot(xb, w_ref[:, start:start + width], preferred_element_type=jnp.float32)
        start += width


def _proj(x2, w_bf16, widths, tm=256):
    n, d = x2.shape
    assert sum(widths) == w_bf16.shape[1]
    return pl.pallas_call(
        _proj_kernel,
        grid=(n // tm,),
        in_specs=[pl.BlockSpec((tm, d), lambda i: (i, 0)),
                  pl.BlockSpec(w_bf16.shape, lambda i: (0, 0))],
        out_specs=[pl.BlockSpec((tm, w), lambda i: (i, 0)) for w in widths],
        out_shape=[jax.ShapeDtypeStruct((n, w), jnp.float32) for w in widths],
        compiler_params=pltpu.CompilerParams(vmem_limit_bytes=VMEM_LIMIT_BYTES),
        name="in_proj",
    )(x2, w_bf16)


def _out_proj_ln_kernel(*refs, n_parts):
    y_refs, (w_ref, x_ref, g_ref, b_ref, o_ref, ot_ref) = refs[:n_parts], refs[n_parts:]
    acc, start = None, 0
    for y_ref in y_refs:
        width = y_ref.shape[1]
        part = jnp.dot(y_ref[...].astype(jnp.bfloat16), w_ref[start:start + width, :],
                       preferred_element_type=jnp.float32)
        acc = part if acc is None else acc + part
        start += width
    z = DN_ALPHA * x_ref[...] + acc
    mu = jnp.mean(z, axis=-1, keepdims=True)
    zc = z - mu
    var = jnp.mean(zc * zc, axis=-1, keepdims=True)
    out = zc * lax.rsqrt(var + LN_EPS) * g_ref[...] + b_ref[...]
    o_ref[...] = out
    ot_ref[...] = out.T.astype(jnp.bfloat16)


def _out_proj_ln(ys, w_bf16, x2, g, b, tm=256):
    n, d = x2.shape
    row = lambda width: pl.BlockSpec((tm, width), lambda i: (i, 0))
    vec = pl.BlockSpec((1, d), lambda i: (0, 0))
    return pl.pallas_call(
        functools.partial(_out_proj_ln_kernel, n_parts=len(ys)),
        grid=(n // tm,),
        in_specs=[row(y.shape[1]) for y in ys]
                 + [pl.BlockSpec(w_bf16.shape, lambda i: (0, 0)), row(d), vec, vec],
        out_specs=[row(d), pl.BlockSpec((d, tm), lambda i: (0, i))],
        out_shape=[jax.ShapeDtypeStruct((n, d), jnp.float32), jax.ShapeDtypeStruct((d, n), jnp.bfloat16)],
        name="out_proj_ln",
    )(*ys, w_bf16, x2, g.reshape(1, d), b.reshape(1, d))


def _head_norm(y, g, b, eps):
    mu = jnp.mean(y, -1, keepdims=True)
    var = jnp.mean(jnp.square(y - mu), -1, keepdims=True)
    return (y - mu) * lax.rsqrt(var + eps) * g + b


def _rms_norm(y, g):
    return y * lax.rsqrt(jnp.mean(y * y, -1, keepdims=True) + 1e-6) * g


def _token_shift(z):
    return jnp.pad(z, ((0, 0), (1, 0), (0, 0)))[:, :-1]


def _causal_conv(z, w, b):
    c = z.shape[-1]
    out = lax.conv_general_dilated(z, w[:, None, :], window_strides=(1,),
                                   padding=((w.shape[0] - 1, 0),),
                                   dimension_numbers=('NWC', 'WIO', 'NWC'),
                                   feature_group_count=c)
    return out + b


def _rwkv7_mix(p, mu, w0, w_up, a0, a_up, g_up, k_k, k_a, r_k, ln_g, ln_b):
    B, S, _ = p.shape
    H, N = RWKV_HEADS, RWKV_HEAD
    p = p + (_token_shift(p) - p) * mu
    r, k, v, xw, xa, xg = _split_cols(p, (RWKV_WIDTH, RWKV_WIDTH, RWKV_WIDTH,
                                          RWKV_W_RANK, RWKV_A_RANK, RWKV_G_RANK))
    log_w = -jnp.exp(-jax.nn.softplus(-(w0 + jnp.tanh(xw) @ w_up)) - 0.5)
    a = jax.nn.sigmoid(a0 + xa @ a_up)
    g = jax.nn.sigmoid(xg) @ g_up
    heads = lambda t: t.reshape(B, S, H, N)
    kk = heads(k * k_k)
    kk = kk / jnp.maximum(jnp.sqrt(jnp.sum(kk * kk, -1, keepdims=True)), 1e-12)
    k = k * (1.0 + (a - 1.0) * k_a)
    r_h, k_h, v_h, a_h = heads(r), heads(k), heads(v), heads(a)
    w_h = jnp.exp(heads(log_w))

    def step(state, inp):
        r_t, w_t, k_t, v_t, kk_t, a_t = inp
        s_kk = jnp.einsum('bhvk,bhk->bhv', state, kk_t)
        state = (state * w_t[:, :, None, :]
                 - s_kk[..., None] * (kk_t * a_t)[:, :, None, :]
                 + v_t[..., None] * k_t[:, :, None, :])
        return state, jnp.einsum('bhvk,bhk->bhv', state, r_t)

    seq_first = lambda t: jnp.moveaxis(t, 1, 0)
    state0 = jnp.zeros((B, H, N, N), jnp.float32)
    _, y = lax.scan(step, state0, (seq_first(r_h), seq_first(w_h), seq_first(k_h),
                                   seq_first(v_h), seq_first(kk), seq_first(a_h)))
    y = jnp.moveaxis(y, 0, 1)
    y = _head_norm(y, ln_g, ln_b, RWKV_GN_EPS)
    y = y + jnp.sum(r_h * k_h * r_k, -1, keepdims=True) * v_h
    return y.reshape(B, S, RWKV_WIDTH) * g


def _mlstm_chunkwise(q, k, v, ig, lf):
    B, S, H, D = q.shape
    L = MLSTM_CHUNK
    NC = S // L

    def to_chunks(t):
        t = t.reshape((B, NC, L, H) + t.shape[3:])
        return jnp.moveaxis(t, (1, 3), (0, 2))

    causal = jnp.tril(jnp.ones((L, L), dtype=bool))

    def body(carry, inp):
        C, n, m = carry
        qc, kc, vc, igc, lfc = inp
        b = jnp.cumsum(lfc, axis=-1)
        dmat = jnp.where(causal, b[..., :, None] - b[..., None, :] + igc[..., None, :], -jnp.inf)
        inter = b + m[..., None]
        m_t = jnp.maximum(inter, jnp.max(dmat, -1))
        weights = jnp.exp(dmat - m_t[..., None])
        sc = jnp.einsum('bhtd,bhsd->bhts', qc, kc) * weights
        carry_in = jnp.exp(inter - m_t)
        num = (jnp.einsum('bhts,bhsd->bhtd', sc, vc)
               + carry_in[..., None] * jnp.einsum('bhtk,bhkv->bhtv', qc, C))
        den = jnp.sum(sc, -1) + carry_in * jnp.einsum('bhtk,bhk->bht', qc, n)
        h = num / jnp.maximum(jnp.abs(den), jnp.exp(-m_t))[..., None]
        b_last = b[..., -1]
        gs = b_last[..., None] - b + igc
        m_new = jnp.maximum(b_last + m, jnp.max(gs, -1))
        ws = jnp.exp(gs - m_new[..., None])
        keep = jnp.exp(b_last + m - m_new)
        C = keep[..., None, None] * C + jnp.einsum('bhs,bhsk,bhsv->bhkv', ws, kc, vc)
        n = keep[..., None] * n + jnp.einsum('bhs,bhsk->bhk', ws, kc)
        return (C, n, m_new), h

    f32 = jnp.float32
    init = (jnp.zeros((B, H, D, D), f32), jnp.zeros((B, H, D), f32), jnp.zeros((B, H), f32))
    _, h = lax.scan(body, init, (to_chunks(q), to_chunks(k), to_chunks(v),
                                 to_chunks(ig), to_chunks(lf)))
    return jnp.moveaxis(h, (0, 2), (1, 3)).reshape(B, S, H, D)


def _mlstm_mix(p, conv_w, conv_b, ig_b, fg_b, hn_g, hn_b):
    B, S, _ = p.shape
    H, N = MLSTM_HEADS, MLSTM_HEAD
    q, k, v, o, ig, fg = _split_cols(p, (MLSTM_WIDTH, MLSTM_WIDTH, MLSTM_WIDTH, MLSTM_WIDTH, H, H))
    qk = jax.nn.silu(_causal_conv(jnp.concatenate([q, k], -1), conv_w, conv_b))
    q, k = qk[..., :MLSTM_WIDTH], qk[..., MLSTM_WIDTH:]
    heads = lambda t: t.reshape(B, S, H, N)
    q = heads(q) * (N ** -0.5)
    ig = ig + ig_b
    lf = jax.nn.log_sigmoid(fg + fg_b)
    h = _mlstm_chunkwise(q, heads(k), heads(v), ig, lf)
    h = _head_norm(h, hn_g, hn_b, LN_EPS).reshape(B, S, MLSTM_WIDTH)
    return jax.nn.sigmoid(o) * h


def _even_mixer(x, w_in, mu, w0, w_up, a0, a_up, g_up, k_k, k_a, r_k, rln_g, rln_b,
                conv_w, conv_b, ig_b, fg_b, hn_g, hn_b, w_out):
    p = x @ w_in
    y_a = _rwkv_pallas(p[..., :RWKV_COLS], mu, w0, w_up, a0, a_up, g_up, k_k, k_a, r_k, rln_g, rln_b)
    y_b = _mlstm_mix(p[..., RWKV_COLS:], conv_w, conv_b, ig_b, fg_b, hn_g, hn_b)
    return jnp.concatenate([y_a, y_b], -1) @ w_out


def _fox_attention(q, k, v, logf):
    B, S, H, D = q.shape
    NB = S // FOX_QBLOCK
    c = jnp.cumsum(logf, axis=1)
    c_key = jnp.transpose(c, (0, 2, 1))[:, :, None, :]
    q_blocks = jnp.moveaxis(q.reshape(B, NB, FOX_QBLOCK, H, D), 1, 0)
    c_blocks = jnp.moveaxis(c.reshape(B, NB, FOX_QBLOCK, H), 1, 0)
    key_pos = jnp.arange(S)
    scale = D ** -0.5

    def one_block(args):
        qb, cb, blk = args
        q_pos = blk * FOX_QBLOCK + jnp.arange(FOX_QBLOCK)
        logits = jnp.einsum('bqhd,bkhd->bhqk', qb, k) * scale
        logits = logits + jnp.transpose(cb, (0, 2, 1))[..., None] - c_key
        logits = jnp.where(key_pos[None, :] <= q_pos[:, None], logits, -jnp.inf)
        probs = jax.nn.softmax(logits, axis=-1)
        return jnp.einsum('bhqk,bkhd->bqhd', probs, v)

    out = lax.map(one_block, (q_blocks, c_blocks, jnp.arange(NB)))
    return jnp.moveaxis(out, 0, 1).reshape(B, S, H, D)


def _odd_mixer(x, w_in, qn_g, kn_g, f_b, w_out):
    B, S, _ = x.shape
    p = x @ w_in
    q, k, v, og, fl = _split_cols(p, (FOX_WIDTH, FOX_WIDTH, FOX_WIDTH, FOX_WIDTH, FOX_HEADS))
    heads = lambda t: t.reshape(B, S, FOX_HEADS, FOX_HEAD)
    q = _rms_norm(heads(q), qn_g)
    k = _rms_norm(heads(k), kn_g)
    logf = jax.nn.log_sigmoid(fl + f_b)
    o = _fox_attention(q, k, heads(v), logf).reshape(B, S, FOX_WIDTH)
    return (jax.nn.sigmoid(og) * o) @ w_out


def _peer_ffn(x, w_q, sub_k1, sub_k2, u_tab, v_tab):
    B, S, Dm = x.shape
    K = PEER_TOPK
    xt = x.reshape((B * S) // PEER_TOKBLOCK, PEER_TOKBLOCK, Dm)

    def block(xb):
        T = xb.shape[0]
        q = (xb @ w_q).reshape(T, PEER_HEADS, 2, PEER_DHALF)
        s1 = jnp.einsum('thd,hnd->thn', q[:, :, 0], sub_k1)
        s2 = jnp.einsum('thd,hnd->thn', q[:, :, 1], sub_k2)
        v1, i1 = lax.top_k(s1, K)
        v2, i2 = lax.top_k(s2, K)
        cand = (v1[..., :, None] + v2[..., None, :]).reshape(T, PEER_HEADS, K * K)
        sc, ci = lax.top_k(cand, K)
        e1 = jnp.take_along_axis(i1, ci // K, axis=-1)
        e2 = jnp.take_along_axis(i2, ci % K, axis=-1)
        eid = e1 * PEER_NKEYS + e2
        gate = jax.nn.softmax(sc, axis=-1)
        act = jax.nn.gelu(jnp.einsum('td,thkd->thk', xb, u_tab[eid]), approximate=False)
        return jnp.einsum('thk,thkd->td', gate * act, v_tab[eid])

    return lax.map(block, xt).reshape(B, S, Dm)


PEER_EXPERTS = PEER_NKEYS * PEER_NKEYS
PEER_SELECT_TILE = 256
PEER_TOK_TILE = 512
PEER_EXPERT_CHUNK = 1024
PEER_E1_GROUP = 4
PEER_E2_SUB = 64
BF16_SUBLANES = 16
PEER_A_PIECES = 8
PEER_C_PIECES = 4
LANES = 128
MXU_DIM = 256
INV_SQRT2 = 0.7071067811865476
_NT = (((1,), (1,)), ((), ()))


def _topk_desc(work, k):
    vals = []
    for _ in range(k):
        m = jnp.max(work, axis=0, keepdims=True)
        vals.append(m)
        work = jnp.where(work >= m, -jnp.inf, work)
    return vals


def _oddeven_merge(lo, hi, r):
    step = r * 2
    if step < hi - lo:
        yield from _oddeven_merge(lo, hi, step)
        yield from _oddeven_merge(lo + r, hi, step)
        yield from [(i, i + r) for i in range(lo + r, hi - r, step)]
    else:
        yield (lo, lo + r)


def _oddeven_sort(lo, hi):
    if hi - lo >= 1:
        mid = lo + (hi - lo) // 2
        yield from _oddeven_sort(lo, mid)
        yield from _oddeven_sort(mid + 1, hi)
        yield from _oddeven_merge(lo, hi, 1)


SUBLANES = 8


def _sorted_top(s, k):
    v = [s[i * SUBLANES:(i + 1) * SUBLANES, :] for i in range(k)]

    def exchange(i, j):
        v[i], v[j] = jnp.maximum(v[i], v[j]), jnp.minimum(v[i], v[j])

    for i, j in _oddeven_sort(0, k - 1):
        exchange(i, j)
    shift = SUBLANES // 2
    while shift >= 1:
        v = [jnp.maximum(v[i], pltpu.roll(v[k - 1 - i], shift, axis=0)) for i in range(k)]
        d = k // 2
        while d >= 1:
            for i in range(k):
                if i & d == 0:
                    exchange(i, i + d)
            d //= 2
        shift //= 2
    return [t[0:1, :] for t in v]


def _peer_select_kernel(x_ref, wqt_ref, k1_ref, k2_ref,
                        phi_ref, a1_ref, a2_ref, qt_ref):
    f32, bf16 = jnp.float32, jnp.bfloat16
    K = PEER_TOPK
    qt_ref[...] = lax.dot_general(wqt_ref[...], x_ref[...].astype(bf16), _NT, preferred_element_type=f32)
    n_groups = x_ref.shape[0] // LANES

    def head_pair(hp, carry):
        hs = [2 * hp, 2 * hp + 1]
        sc = []
        for h in hs:
            base = pl.multiple_of(h * PEER_DKEY, PEER_DKEY)
            q1 = qt_ref[pl.ds(base, PEER_DHALF), :].astype(bf16)
            q2 = qt_ref[pl.ds(base + PEER_DHALF, PEER_DHALF), :].astype(bf16)
            sc.append((jnp.dot(k1_ref[h], q1, preferred_element_type=f32),
                       jnp.dot(k2_ref[h], q2, preferred_element_type=f32)))
        items = [(j, c) for j in range(len(hs)) for c in range(n_groups)]
        lanes = lambda c: slice(c * LANES, (c + 1) * LANES)
        s1c = [sc[j][0][:, lanes(c)] for j, c in items]
        s2c = [sc[j][1][:, lanes(c)] for j, c in items]
        it = range(len(items))
        v1 = [_sorted_top(s1c[i], K) for i in it]
        v2 = [_sorted_top(s2c[i], K) for i in it]
        pairs = [(a, b) for a in range(K) for b in range(K) if (a + 1) * (b + 1) <= K + 1]
        pad = [jnp.full_like(v1[0][0], -jnp.inf)] * (-len(pairs) % SUBLANES)
        cand = [jnp.concatenate([v1[i][a] + v2[i][b] for a, b in pairs] + pad, axis=0) for i in it]
        vc = [_topk_desc(cand[i], K + 1) for i in it]
        inv_z = [1.0 / jnp.sum(jnp.where(cand[i] >= vc[i][K - 1], jnp.exp(cand[i] - vc[i][0]), 0.0),
                               axis=0, keepdims=True) for i in it]
        for i, (j, c) in enumerate(items):
            h, sl = hs[j], lanes(c)
            cut = 0.5 * (vc[i][K - 1] + vc[i][K])
            s1m = jnp.where(s1c[i] >= v1[i][K - 1], s1c[i], -jnp.inf)
            s2m = jnp.where(s2c[i] >= v2[i][K - 1], s2c[i], -jnp.inf)
            phi_ref[h, :, sl] = jnp.exp(cut - s1m - v2[i][0]) * inv_z[i]
            a1_ref[h, :, sl] = jnp.exp(s1m - v1[i][0])
            a2_ref[h, :, sl] = jnp.exp(s2m - v2[i][0]) * inv_z[i]
        return carry

    lax.fori_loop(0, PEER_HEADS // 2, head_pair, 0)


def _peer_select(xb, wqt, k1, k2):
    n, d = xb.shape
    T = PEER_SELECT_TILE
    H, NK = PEER_HEADS, PEER_NKEYS
    full = lambda shape: pl.BlockSpec(shape, lambda i: (0,) * len(shape))
    tok3 = lambda rows: pl.BlockSpec((H, rows, T), lambda i: (0, 0, i))
    f32 = jnp.float32
    return pl.pallas_call(
        _peer_select_kernel,
        grid=(n // T,),
        in_specs=[pl.BlockSpec((T, d), lambda i: (i, 0)), full(wqt.shape), full(k1.shape), full(k2.shape)],
        out_specs=[tok3(NK)] * 3,
        out_shape=[jax.ShapeDtypeStruct((H, NK, n), f32)] * 3,
        scratch_shapes=[pltpu.VMEM((H * PEER_DKEY, T), f32)],
        name="peer_select",
    )(xb, wqt, k1, k2)


def _peer_expert_kernel(xt_ref, u_ref, v_ref, phi_ref, a1_ref, a2_ref,
                        o_ref, acc_ref, h0_ref, h1_ref, w0_ref, w1_ref, a2b_ref, phib_ref, a1b_ref):
    f32, bf16 = jnp.float32, jnp.bfloat16
    s = pl.program_id(0)
    n_chunks = PEER_EXPERTS // PEER_EXPERT_CHUNK
    c_chunk = jnp.maximum(s - 2, 0) % n_chunks

    @pl.when(s == 0)
    def _():
        for ref in (h0_ref, h1_ref, w0_ref, w1_ref):
            ref[...] = jnp.zeros_like(ref)

    @pl.when(c_chunk == 0)
    def _():
        acc_ref[...] = jnp.zeros_like(acc_ref)

    T = xt_ref.shape[1]
    NK, G, SB = PEER_NKEYS, PEER_E1_GROUP, PEER_E2_SUB
    CE = PEER_EXPERT_CHUNK
    PA, PC = PEER_A_PIECES, PEER_C_PIECES

    @pl.when(jnp.maximum(s - 1, 0) % n_chunks == 0)
    def _():
        a2b_ref[...] = a2_ref[...].astype(bf16)

    def step(ha_ref, hb_ref, wb_ref, wc_ref):
        def expand_rows():
            for hd in range(PEER_HEADS):
                for r in range(CE // NK):
                    phib_ref[hd, r] = jnp.broadcast_to(phi_ref[hd, r:r + 1, :],
                                                       (BF16_SUBLANES, T)).astype(bf16)
                    a1b_ref[hd, r] = jnp.broadcast_to(a1_ref[hd, r:r + 1, :],
                                                      (BF16_SUBLANES, T)).astype(bf16)

        def stage_a(k):
            rows = CE // PA
            u_blk = pltpu.bitcast(u_ref[k * rows // 2:(k + 1) * rows // 2, :], bf16)
            ha_ref[k * rows:(k + 1) * rows, :] = jnp.dot(u_blk, xt_ref[...], preferred_element_type=f32)

        def stage_b(q, c, sb):
            ts = slice(c * LANES, (c + 1) * LANES)
            e2s = slice(sb * SB, (sb + 1) * SB)
            g = [None] * G
            zero = jnp.zeros((SB, LANES), bf16)
            for hd in range(PEER_HEADS):
                a2 = a2b_ref[hd, e2s, ts]
                for e in range(G):
                    r = q * G + e
                    phi = jnp.concatenate([phib_ref[hd, r, :, ts]] * (SB // BF16_SUBLANES), axis=0)
                    a1 = jnp.concatenate([a1b_ref[hd, r, :, ts]] * (SB // BF16_SUBLANES), axis=0)
                    term = jnp.where(a2 > phi, a1 * a2, zero)
                    g[e] = term if g[e] is None else g[e] + term
            for e in range(G):
                lo = (q * G + e) * NK + sb * SB
                hh = hb_ref[lo:lo + SB, ts]
                act = 0.5 * hh * (1.0 + lax.erf(hh * INV_SQRT2))
                wb_ref[lo:lo + SB, ts] = (g[e].astype(f32) * act).astype(bf16)

        b_blocks = [(q, c, sb) for q in range(CE // (G * NK)) for c in range(T // LANES)
                    for sb in range(NK // SB)]
        def stage_c(k):
            rows = slice(k * CE // PC, (k + 1) * CE // PC)
            acc_ref[...] = lax.dot_general(wc_ref[rows, :], v_ref[rows, :], _TN,
                                           preferred_element_type=f32) + acc_ref[...]

        nb = len(b_blocks)
        mxu_pieces = [functools.partial(stage_a, k) for k in range(PA)]
        mxu_pieces[1:1] = [functools.partial(stage_c, k) for k in range(PC)]
        mxu_pieces[0]()
        expand_rows()
        issued = 1
        for t, blk in enumerate(b_blocks):
            stage_b(*blk)
            while issued < len(mxu_pieces) and (issued - 1) * nb <= t * (len(mxu_pieces) - 1):
                mxu_pieces[issued]()
                issued += 1
        assert issued == len(mxu_pieces)

    pl.when(s % 2 == 0)(functools.partial(step, h0_ref, h1_ref, w1_ref, w0_ref))
    pl.when(s % 2 == 1)(functools.partial(step, h1_ref, h0_ref, w0_ref, w1_ref))

    @pl.when(c_chunk == n_chunks - 1)
    def _():
        o_ref[...] = acc_ref[...]


def _peer_experts(xtb, ub, vb, phi, a1, a2):
    d, n = xtb.shape
    T, CE = PEER_TOK_TILE, PEER_EXPERT_CHUNK
    H, NK = PEER_HEADS, PEER_NKEYS
    e1_per_chunk = CE // NK
    nc = PEER_EXPERTS // CE
    last = (n // T) * nc - 1
    at = lambda s, lag: jnp.clip(s - lag, 0, last)
    return pl.pallas_call(
        _peer_expert_kernel,
        grid=(last + 3,),
        in_specs=[
            pl.BlockSpec((d, T), lambda s: (0, at(s, 0) // nc)),
            pl.BlockSpec((CE // 2, d), lambda s: (at(s, 0) % nc, 0)),
            pl.BlockSpec((CE, d), lambda s: (at(s, 2) % nc, 0)),
            pl.BlockSpec((H, e1_per_chunk, T), lambda s: (0, at(s, 1) % nc, at(s, 1) // nc)),
            pl.BlockSpec((H, e1_per_chunk, T), lambda s: (0, at(s, 1) % nc, at(s, 1) // nc)),
            pl.BlockSpec((H, NK, T), lambda s: (0, 0, at(s, 1) // nc)),
        ],
        out_specs=pl.BlockSpec((T, d), lambda s: (at(s, 2) // nc, 0)),
        out_shape=jax.ShapeDtypeStruct((n, d), jnp.float32),
        scratch_shapes=[pltpu.VMEM((T, d), jnp.float32),
                        pltpu.VMEM((CE, T), jnp.float32), pltpu.VMEM((CE, T), jnp.float32),
                        pltpu.VMEM((CE, T), jnp.bfloat16), pltpu.VMEM((CE, T), jnp.bfloat16),
                        pltpu.VMEM((H, NK, T), jnp.bfloat16),
                        pltpu.VMEM((H, e1_per_chunk, BF16_SUBLANES, T), jnp.bfloat16),
                        pltpu.VMEM((H, e1_per_chunk, BF16_SUBLANES, T), jnp.bfloat16)],
        compiler_params=pltpu.CompilerParams(dimension_semantics=("arbitrary",),
                                             vmem_limit_bytes=VMEM_LIMIT_BYTES),
        name="peer_experts",
    )(xtb, ub, vb, phi, a1, a2)


def _pack_row_pairs(t):
    rows, cols = t.shape
    half = lambda part: lax.bitcast_convert_type(part.astype(jnp.bfloat16), jnp.uint16).astype(jnp.uint32)
    pairs = t.reshape(rows // 2, 2 * cols)
    return half(pairs[:, :cols]) | (half(pairs[:, cols:]) << 16)


def _peer_pallas(x2, xt_bf16, w_q, sub_k1, sub_k2, u_tab, v_tab):
    bf16 = jnp.bfloat16
    sel = _peer_select(x2, w_q.T.astype(bf16), sub_k1.astype(bf16), sub_k2.astype(bf16))
    return _peer_experts(xt_bf16, _pack_row_pairs(u_tab), v_tab.astype(bf16), *sel)


RWKV_CHUNK = 64
RWKV_BATCH_PER_STEP = 2
RWKV_PAIR = LANES // RWKV_HEAD
RWKV_PAIRS = RWKV_HEADS // RWKV_PAIR
_NN = (((1,), (0,)), ((), ()))
_TN = (((0,), (0,)), ((), ()))


def _split_bf16(a):
    hi = a.astype(jnp.bfloat16)
    lo = (a - hi.astype(jnp.float32)).astype(jnp.bfloat16)
    return hi, lo


def _mm2(a, b, dims):
    d = lambda u, v: lax.dot_general(u, v, dims, preferred_element_type=jnp.float32)
    (ah, al), (bh, bl) = _split_bf16(a), _split_bf16(b)
    return d(ah, bh) + d(ah, bl) + d(al, bh)


def _mm_exact_rhs(a, b_exact, dims, terms=2):
    d = lambda u: lax.dot_general(u, b_exact, dims, preferred_element_type=jnp.float32)
    out, rem = None, a
    for _ in range(terms):
        piece = rem.astype(jnp.bfloat16)
        rem = rem - piece.astype(jnp.float32)
        out = d(piece) if out is None else out + d(piece)
    return out


def _mm_exact_lhs(a_exact, b, dims, terms=3):
    d = lambda v: lax.dot_general(a_exact, v, dims, preferred_element_type=jnp.float32)
    out, rem = None, b
    for _ in range(terms):
        piece = rem.astype(jnp.bfloat16)
        rem = rem - piece.astype(jnp.float32)
        out = d(piece) if out is None else out + d(piece)
    return out


def _softplus(z):
    return jnp.maximum(z, 0.0) + jnp.log1p(jnp.exp(-jnp.abs(z)))


def _rwkv_kernel(p_ref, prev_ref, mu_ref, w0_ref, wup_ref, a0_ref, aup_ref, gup_ref,
                 kk_ref, ka_ref, rk_ref, lng_ref, lnb_ref, ones_ref,
                 o_ref, state_ref):
    f32, bf16 = jnp.float32, jnp.bfloat16
    L, W = RWKV_CHUNK, RWKV_WIDTH
    c_idx = pl.program_id(1)

    @pl.when(c_idx == 0)
    def _():
        state_ref[...] = jnp.zeros_like(state_ref)

    NB = p_ref.shape[0]
    parts = []
    for bi in range(NB):
        pb = p_ref[bi]
        row = lax.broadcasted_iota(jnp.int32, pb.shape, 0)
        prev_row = jnp.where(c_idx == 0, 0.0, prev_ref[bi, 7:8, :])
        shifted = jnp.where(row == 0, prev_row, pltpu.roll(pb, 1, axis=0))
        parts.append(pb + (shifted - pb) * mu_ref[...])
    p = jnp.concatenate(parts, axis=0)
    r, k, v = p[:, 0:W], p[:, W:2 * W], p[:, 2 * W:3 * W]
    o = 3 * W
    xw = p[:, o:o + RWKV_W_RANK]
    xa = p[:, o + RWKV_W_RANK:o + RWKV_W_RANK + RWKV_A_RANK]
    xg = p[:, o + RWKV_W_RANK + RWKV_A_RANK:]
    dotd = lambda u, m: jnp.dot(u.astype(bf16), m, preferred_element_type=f32)
    lw = -jnp.exp(-_softplus(-(w0_ref[...] + dotd(jnp.tanh(xw), wup_ref[...]))) - 0.5)
    a = jax.nn.sigmoid(a0_ref[...] + dotd(xa, aup_ref[...]))
    g = dotd(jax.nn.sigmoid(xg), gup_ref[...])
    ones_bd = ones_ref[...]
    head_sum = lambda t: _mm_exact_rhs(t, ones_bd, _NN)
    kk = k * kk_ref[...]
    kk = kk / jnp.maximum(jnp.sqrt(head_sum(kk * kk)), 1e-12)
    k = k * (1.0 + (a - 1.0) * ka_ref[...])

    lane = lax.broadcasted_iota(jnp.int32, (L, LANES), 1)
    trow = lax.broadcasted_iota(jnp.int32, (L, LANES), 0)
    s_in = lane % RWKV_HEAD
    strict, incl = s_in < trow, s_in <= trow
    eye_pair = (s_in == trow).astype(f32)
    m0 = lane < RWKV_HEAD
    tril = (lax.broadcasted_iota(jnp.int32, (L, L), 1)
            <= lax.broadcasted_iota(jnp.int32, (L, L), 0)).astype(bf16)
    r2 = lax.broadcasted_iota(jnp.int32, (LANES, LANES), 0) // RWKV_HEAD
    c2 = lax.broadcasted_iota(jnp.int32, (LANES, LANES), 1) // RWKV_HEAD
    bd_mask = r2 == c2

    def bd(t):
        return jnp.concatenate([jnp.where(m0, t, 0.0), jnp.where(m0, 0.0, t)], axis=0)

    pairs = range(NB * RWKV_PAIRS)
    cut = lambda t: [t[bi * L:(bi + 1) * L, pr * LANES:(pr + 1) * LANES]
                     for bi in range(NB) for pr in range(RWKV_PAIRS)]
    rp, kp, vp, kkp, ap, lwp = cut(r), cut(k), cut(v), cut(kk), cut(a), cut(lw)
    c = [_mm_exact_lhs(tril, lwp[p], _NN) for p in pairs]
    at = [-kkp[p] * jnp.exp(c[p] - lwp[p]) for p in pairs]
    eni = [jnp.exp(-c[p]) for p in pairs]
    bt = [kkp[p] * ap[p] * eni[p] for p in pairs]
    kt = [kp[p] * eni[p] for p in pairs]
    rt = [rp[p] * jnp.exp(c[p]) for p in pairs]
    g_last = [jnp.exp(c[p][L - 1:L, :]) for p in pairs]
    gram = [_mm2(jnp.concatenate([at[p], rt[p]], axis=0),
                 jnp.concatenate([bd(bt[p]), bd(kt[p])], axis=0), _NT) for p in pairs]
    n_ab = [jnp.where(strict, gram[p][0:L, 0:LANES], 0.0) for p in pairs]
    a_ak = [jnp.where(strict, gram[p][0:L, LANES:], 0.0) for p in pairs]
    a_rb = [jnp.where(incl, gram[p][L:, 0:LANES], 0.0) for p in pairs]
    a_rk = [jnp.where(incl, gram[p][L:, LANES:], 0.0) for p in pairs]
    tinv, m = [eye_pair + n_ab[p] for p in pairs], n_ab
    for _ in range(5):
        m = [_mm2(m[p], bd(m[p]), _NN) for p in pairs]
        tinv = [tinv[p] + _mm2(m[p], bd(tinv[p]), _NN) for p in pairs]
    s0 = [state_ref[p] for p in pairs]
    pq = [_mm2(at[p], s0[p], _NT) + _mm2(a_ak[p], bd(vp[p]), _NN) for p in pairs]
    u = [_mm2(tinv[p], bd(pq[p]), _NN) for p in pairs]
    ys = [_mm2(rt[p], s0[p], _NT)
          + _mm2(jnp.concatenate([a_rb[p], a_rk[p]], axis=1),
                 jnp.concatenate([bd(u[p]), bd(vp[p])], axis=0), _NN) for p in pairs]
    for p in pairs:
        upd = _mm2(jnp.concatenate([u[p], vp[p]], axis=0),
                   jnp.concatenate([bt[p], kt[p]], axis=0), _TN)
        state_ref[p] = (s0[p] + jnp.where(bd_mask, upd, 0.0)) * g_last[p]
    y = jnp.concatenate([jnp.concatenate(ys[bi * RWKV_PAIRS:(bi + 1) * RWKV_PAIRS], axis=1)
                         for bi in range(NB)], axis=0)

    inv_n = 1.0 / RWKV_HEAD
    mean = head_sum(y) * inv_n
    yc = y - mean
    var = head_sum(yc * yc) * inv_n
    y = yc * lax.rsqrt(var + RWKV_GN_EPS) * lng_ref[...] + lnb_ref[...]
    y = y + head_sum(r * k * rk_ref[...]) * v
    out = y * g
    for bi in range(NB):
        o_ref[bi] = out[bi * L:(bi + 1) * L]


def _rwkv_pallas(p_r, mu, w0, w_up, a0, a_up, g_up, k_k, k_a, r_k, ln_g, ln_b):
    B, S, C = p_r.shape
    L, W, NB = RWKV_CHUNK, RWKV_WIDTH, RWKV_BATCH_PER_STEP
    bf16 = jnp.bfloat16
    vecw = lambda t: t.reshape(1, W)
    ones_bd = jnp.kron(jnp.eye(RWKV_HEADS, dtype=bf16), jnp.ones((RWKV_HEAD, RWKV_HEAD), bf16))
    full = lambda a: pl.BlockSpec(a.shape, lambda b, c: (0,) * a.ndim)
    args = [mu.reshape(1, C), vecw(w0), w_up.astype(bf16), vecw(a0), a_up.astype(bf16),
            g_up.astype(bf16), vecw(k_k), vecw(k_a), vecw(r_k), vecw(ln_g), vecw(ln_b), ones_bd]
    return pl.pallas_call(
        _rwkv_kernel,
        grid=(B // NB, S // L),
        in_specs=[pl.BlockSpec((NB, L, C), lambda b, c: (b, c, 0)),
                  pl.BlockSpec((NB, 8, C), lambda b, c: (b, jnp.maximum(c * (L // 8) - 1, 0), 0))]
                 + [full(a) for a in args],
        out_specs=pl.BlockSpec((NB, L, W), lambda b, c: (b, c, 0)),
        out_shape=jax.ShapeDtypeStruct((B, S, W), jnp.float32),
        scratch_shapes=[pltpu.VMEM((NB * RWKV_PAIRS, LANES, LANES), jnp.float32)],
        compiler_params=pltpu.CompilerParams(dimension_semantics=("arbitrary", "arbitrary")),
        name="rwkv7",
    )(p_r, p_r, *args)


MLSTM_CONV = 4
MLSTM_GATE_LANES = LANES


def _mlstm_kernel(p_ref, prev_ref, cw_ref, cb_ref, igb_ref, fgb_ref, hng_ref, hnb_ref,
                  o_ref, c_ref, n_ref, m_ref):
    f32, bf16 = jnp.float32, jnp.bfloat16
    L, W, H, D = MLSTM_CHUNK, MLSTM_WIDTH, MLSTM_HEADS, MLSTM_HEAD
    c_idx = pl.program_id(1)

    @pl.when(c_idx == 0)
    def _():
        c_ref[...] = jnp.zeros_like(c_ref)
        n_ref[...] = jnp.zeros_like(n_ref)
        m_ref[...] = jnp.zeros_like(m_ref)

    p = p_ref[0]
    z = p[:, 0:2 * W]
    prev = jnp.where(c_idx == 0, 0.0, prev_ref[0, :, 0:2 * W])
    ext = jnp.concatenate([prev, z], axis=0)
    conv = cb_ref[...] + cw_ref[MLSTM_CONV - 1:MLSTM_CONV, :] * z
    for j in range(MLSTM_CONV - 1):
        d = MLSTM_CONV - 1 - j
        conv = conv + cw_ref[j:j + 1, :] * ext[8 - d:8 - d + L, :]
    qk = conv * jax.nn.sigmoid(conv)
    q_all, k_all = qk[:, 0:W] * (D ** -0.5), qk[:, W:2 * W]
    v_all, o_all = p[:, 2 * W:3 * W], p[:, 3 * W:4 * W]
    gates = p[:, 4 * W:]
    igl = gates + igb_ref[...]
    lfl = -_softplus(-(gates + fgb_ref[...]))
    tril = (lax.broadcasted_iota(jnp.int32, (L, L), 1)
            <= lax.broadcasted_iota(jnp.int32, (L, L), 0))
    bcum = _mm_exact_lhs(tril.astype(bf16), lfl, _NN)
    e_all = igl - pltpu.roll(bcum, LANES - H, axis=1)
    lane = lax.broadcasted_iota(jnp.int32, (L, LANES), 1)
    hd = range(H)
    cols = lambda t: [t[:, h * D:(h + 1) * D] for h in hd]
    q, k, v, o = cols(q_all), cols(k_all), cols(v_all), cols(o_all)
    qb, kb, vb = ([t.astype(bf16) for t in ts] for ts in (q, k, v))
    b_col = [bcum[:, H + h:H + h + 1] for h in hd]
    ig_col = [igl[:, h:h + 1] for h in hd]
    e_row = [_mm_exact_lhs((lane == h).astype(bf16), e_all, _NT) for h in hd]
    qk = [lax.dot_general(qb[h], kb[h], _NT, preferred_element_type=f32) for h in hd]
    m_prev = [m_ref[h, 0:1, 0:1] for h in hd]
    c_prev = [c_ref[h] for h in hd]
    n_prev = [n_ref[h, 0:1, :] for h in hd]
    qc = [jnp.dot(qb[h], c_prev[h].astype(bf16), preferred_element_type=f32) for h in hd]
    dmat = [jnp.where(tril, b_col[h] + e_row[h], -jnp.inf) for h in hd]
    inter = [b_col[h] + m_prev[h] for h in hd]
    m_t = [jnp.maximum(inter[h], jnp.max(dmat[h], axis=-1, keepdims=True)) for h in hd]
    sc = [qk[h] * jnp.exp(dmat[h] - m_t[h]) for h in hd]
    carry_in = [jnp.exp(inter[h] - m_t[h]) for h in hd]
    sv = [jnp.dot(sc[h].astype(bf16), vb[h], preferred_element_type=f32) for h in hd]
    den = [jnp.sum(sc[h], axis=-1, keepdims=True)
           + carry_in[h] * jnp.sum(q[h] * n_prev[h], axis=-1, keepdims=True) for h in hd]
    hval = [(sv[h] + carry_in[h] * qc[h]) / jnp.maximum(jnp.abs(den[h]), jnp.exp(-m_t[h])) for h in hd]
    b_last = [b_col[h][L - 1:L, :] for h in hd]
    gs = [b_last[h] - b_col[h] + ig_col[h] for h in hd]
    m_new = [jnp.maximum(b_last[h] + m_prev[h], jnp.max(gs[h], axis=0, keepdims=True)) for h in hd]
    keep = [jnp.exp(b_last[h] + m_prev[h] - m_new[h]) for h in hd]
    wk = [jnp.exp(gs[h] - m_new[h]) * k[h] for h in hd]
    kv = [lax.dot_general(wk[h].astype(bf16), vb[h], _TN, preferred_element_type=f32) for h in hd]
    outs = []
    for h in hd:
        c_ref[h] = keep[h] * c_prev[h] + kv[h]
        n_ref[h] = jnp.broadcast_to(keep[h] * n_prev[h] + jnp.sum(wk[h], axis=0, keepdims=True), (8, D))
        m_ref[h] = jnp.broadcast_to(m_new[h], (8, LANES))
        mu = jnp.mean(hval[h], axis=-1, keepdims=True)
        hc = hval[h] - mu
        var = jnp.mean(hc * hc, axis=-1, keepdims=True)
        hs = slice(h * D, (h + 1) * D)
        hn = hc * lax.rsqrt(var + LN_EPS) * hng_ref[:, hs] + hnb_ref[:, hs]
        outs.append(jax.nn.sigmoid(o[h]) * hn)
    o_ref[0] = jnp.concatenate(outs, axis=1)


def _mlstm_pallas(p_m, conv_w, conv_b, ig_b, fg_b, hn_g, hn_b):
    B, S, C = p_m.shape
    L, W, H, D = MLSTM_CHUNK, MLSTM_WIDTH, MLSTM_HEADS, MLSTM_HEAD
    pad = lambda t, off: jnp.zeros((1, LANES), jnp.float32).at[0, off:off + H].set(t)
    args = [conv_w, conv_b.reshape(1, 2 * W), pad(ig_b, 0), pad(fg_b, H),
            hn_g.reshape(1, W), hn_b.reshape(1, W)]
    full = lambda a: pl.BlockSpec(a.shape, lambda b, c: (0,) * a.ndim)
    return pl.pallas_call(
        _mlstm_kernel,
        grid=(B, S // L),
        in_specs=[pl.BlockSpec((1, L, C), lambda b, c: (b, c, 0)),
                  pl.BlockSpec((1, 8, C), lambda b, c: (b, jnp.maximum(c * (L // 8) - 1, 0), 0))]
                 + [full(a) for a in args],
        out_specs=pl.BlockSpec((1, L, W), lambda b, c: (b, c, 0)),
        out_shape=jax.ShapeDtypeStruct((B, S, W), jnp.float32),
        scratch_shapes=[pltpu.VMEM((H, D, D), jnp.float32),
                        pltpu.VMEM((H, 8, D), jnp.float32),
                        pltpu.VMEM((H, 8, LANES), jnp.float32)],
        compiler_params=pltpu.CompilerParams(dimension_semantics=("arbitrary", "arbitrary")),
        name="mlstm",
    )(p_m, p_m, *args)


FOX_TILE = 256
FOX_KEY_GROUP = 4
FOX_HEADS_PER_STEP = 8
FOX_AUG = LANES
FOX_GATE_LANES = LANES


def _fox_prep_kernel(p_ref, qg_ref, kg_ref, fb_ref, ones_ref, qa_ref, ka_ref, vt_ref, carry_ref):
    f32, bf16 = jnp.float32, jnp.bfloat16
    T, W, H, D = FOX_TILE, FOX_WIDTH, FOX_HEADS, FOX_HEAD
    i = pl.program_id(1)

    @pl.when(i == 0)
    def _():
        carry_ref[...] = jnp.zeros_like(carry_ref)

    p = p_ref[0]
    ones_bd = ones_ref[...]
    ms = lambda t: _mm_exact_rhs(t * t, ones_bd, _NN) * (1.0 / D)
    q = p[:, 0:W]
    q = q * lax.rsqrt(ms(q) + 1e-6) * qg_ref[...]
    k = p[:, W:2 * W]
    k = k * lax.rsqrt(ms(k) + 1e-6) * kg_ref[...] * (D ** -0.5)
    vt = p[:, 2 * W:3 * W].T.astype(bf16)
    logf = -_softplus(-(p[:, 4 * W:] + fb_ref[...]))
    tril = (lax.broadcasted_iota(jnp.int32, (T, T), 1)
            <= lax.broadcasted_iota(jnp.int32, (T, T), 0)).astype(bf16)
    c = _mm_exact_lhs(tril, logf, _NN) + carry_ref[0:1, :]
    carry_ref[...] = jnp.broadcast_to(c[T - 1:T, :], carry_ref.shape)
    c1 = c.astype(bf16).astype(f32)
    c2 = (c - c1).astype(bf16).astype(f32)
    c3 = (c - c1 - c2).astype(bf16).astype(f32)
    lane = lax.broadcasted_iota(jnp.int32, (T, D), 1)
    for h in range(H):
        hs = slice(h * D, (h + 1) * D)
        c1h, c2h, c3h = c1[:, h:h + 1], c2[:, h:h + 1], c3[:, h:h + 1]
        pieces = jnp.where(lane % 3 == 0, c1h, jnp.where(lane % 3 == 1, c2h, c3h))
        q_aug = jnp.where(lane < 3, 1.0, jnp.where(lane < 6, pieces, 0.0))
        k_aug = jnp.where(lane < 3, -pieces, jnp.where(lane < 6, 1.0, 0.0))
        qa_ref[0, h] = jnp.concatenate([q[:, hs], q_aug], axis=1).astype(bf16)
        ka_ref[0, h] = jnp.concatenate([k[:, hs], k_aug], axis=1).astype(bf16)
        vt_ref[0, h, 0] = vt[hs, :]


def _fox_prep(p_f, qn_g, kn_g, f_b):
    B, S, C = p_f.shape
    T, W, H, D = FOX_TILE, FOX_WIDTH, FOX_HEADS, FOX_HEAD
    bf16 = jnp.bfloat16
    ones_bd = jnp.kron(jnp.eye(H, dtype=bf16), jnp.ones((D, D), bf16))
    fb = jnp.zeros((1, FOX_GATE_LANES), jnp.float32).at[0, :H].set(f_b)
    args = [qn_g.reshape(1, W), kn_g.reshape(1, W), fb, ones_bd]
    full = lambda a: pl.BlockSpec(a.shape, lambda b, i: (0,) * a.ndim)
    return pl.pallas_call(
        _fox_prep_kernel,
        grid=(B, S // T),
        in_specs=[pl.BlockSpec((1, T, C), lambda b, i: (b, i, 0))] + [full(a) for a in args],
        out_specs=[pl.BlockSpec((1, H, T, FOX_AUG), lambda b, i: (b, 0, i, 0)),
                   pl.BlockSpec((1, H, T, FOX_AUG), lambda b, i: (b, 0, i, 0)),
                   pl.BlockSpec((1, H, 1, D, T), lambda b, i: (b, 0, i, 0, 0))],
        out_shape=[jax.ShapeDtypeStruct((B, H, S, FOX_AUG), bf16),
                   jax.ShapeDtypeStruct((B, H, S, FOX_AUG), bf16),
                   jax.ShapeDtypeStruct((B, H, S // T, D, T), bf16)],
        scratch_shapes=[pltpu.VMEM((8, FOX_GATE_LANES), jnp.float32)],
        compiler_params=pltpu.CompilerParams(dimension_semantics=("arbitrary", "arbitrary"),
                                             vmem_limit_bytes=VMEM_LIMIT_BYTES),
        name="fox_prep",
    )(p_f, *args)


def _fox_attn_kernel(qa_ref, ka_ref, vt_ref, og_ref, o_ref, m_ref, l_ref, acc_ref):
    f32, bf16 = jnp.float32, jnp.bfloat16
    T, D = FOX_TILE, FOX_HEAD
    i = pl.program_id(2)
    heads = range(FOX_HEADS_PER_STEP)
    n_blocks = ka_ref.shape[2] // T
    GK = FOX_KEY_GROUP
    rel = (lax.broadcasted_iota(jnp.int32, (GK * T, T), 0)
           - lax.broadcasted_iota(jnp.int32, (GK * T, T), 1))
    qas = [qa_ref[0, j] for j in heads]

    def key_group(g, first):
        visible = rel <= (i - g * GK) * T

        def scores(j):
            ka = ka_ref[0, j, g * GK * T:(g + 1) * GK * T, :]
            return lax.dot_general(ka, qas[j], _NT, preferred_element_type=f32)

        nxt = scores(0)
        for j in heads:
            s = nxt
            if j + 1 < len(heads):
                nxt = scores(j + 1)
            s = jnp.where(visible, s, -jnp.inf)
            m_g = jnp.max(s, axis=0, keepdims=True)
            m_new = m_g if first else jnp.maximum(m_ref[j], m_g)
            pr = jnp.exp(s - m_new)
            l_g = jnp.sum(pr, axis=0, keepdims=True)
            vt = jnp.concatenate([vt_ref[0, j, g * GK + b] for b in range(GK)], axis=1)
            pv = jnp.dot(vt, pr.astype(bf16), preferred_element_type=f32)
            if first:
                l_ref[j], acc_ref[j] = l_g, pv
            else:
                alpha = jnp.exp(m_ref[j] - m_new)
                l_ref[j] = l_ref[j] * alpha + l_g
                acc_ref[j] = acc_ref[j] * alpha + pv
            m_ref[j] = m_new

    key_group(0, True)
    for g in range(1, n_blocks // GK):
        pl.when(i >= g * GK)(functools.partial(key_group, g, False))
    outs = [(acc_ref[j] / l_ref[j]).T for j in heads]
    o_ref[0] = jax.nn.sigmoid(og_ref[0]) * jnp.concatenate(outs, axis=1)


def _fox_attn(qa, ka, vt, p_f):
    B, H, S, A = qa.shape
    T, D = FOX_TILE, FOX_HEAD
    hp = FOX_HEADS_PER_STEP
    width = hp * D
    og_block0 = 3 * FOX_WIDTH // width
    return pl.pallas_call(
        _fox_attn_kernel,
        grid=(B, H // hp, S // T),
        in_specs=[pl.BlockSpec((1, hp, T, A), lambda b, g, i: (b, g, i, 0)),
                  pl.BlockSpec((1, hp, S, A), lambda b, g, i: (b, g, 0, 0)),
                  pl.BlockSpec((1, hp, S // T, D, T), lambda b, g, i: (b, g, 0, 0, 0)),
                  pl.BlockSpec((1, T, width), lambda b, g, i: (b, i, og_block0 + g))],
        out_specs=pl.BlockSpec((1, T, width), lambda b, g, i: (b, i, g)),
        out_shape=jax.ShapeDtypeStruct((B, S, FOX_WIDTH), jnp.float32),
        scratch_shapes=[pltpu.VMEM((hp, 1, T), jnp.float32), pltpu.VMEM((hp, 1, T), jnp.float32),
                        pltpu.VMEM((hp, D, T), jnp.float32)],
        name="fox_attn",
    )(qa, ka, vt, p_f)


def kernel(x, l0_w_in, l0_rwkv_mu, l0_rwkv_w0, l0_rwkv_w_up, l0_rwkv_a0, l0_rwkv_a_up,
           l0_rwkv_g_up, l0_rwkv_k_k, l0_rwkv_k_a, l0_rwkv_r_k, l0_rwkv_ln_g, l0_rwkv_ln_b,
           l0_mlstm_conv_w, l0_mlstm_conv_b, l0_mlstm_ig_b, l0_mlstm_fg_b,
           l0_mlstm_hn_g, l0_mlstm_hn_b, l0_w_out, l0_ln1_g, l0_ln1_b,
           l0_peer_wq, l0_peer_k1, l0_peer_k2, l0_peer_u, l0_peer_v, l0_ln2_g, l0_ln2_b,
           l1_w_in, l1_fox_qn_g, l1_fox_kn_g, l1_fox_f_b, l1_w_out, l1_ln1_g, l1_ln1_b,
           l1_peer_wq, l1_peer_k1, l1_peer_k2, l1_peer_u, l1_peer_v, l1_ln2_g, l1_ln2_b):
    B, S, D = x.shape
    n = B * S
    bf16 = jnp.bfloat16
    x2 = x.reshape(n, D)

    def pad_lanes(w):
        return jnp.pad(w, ((0, 0), (0, LANES - w.shape[1])))

    m_main = 4 * MLSTM_WIDTH
    w0 = jnp.concatenate([l0_w_in[:, :RWKV_COLS + m_main],
                          pad_lanes(l0_w_in[:, RWKV_COLS + m_main:])], axis=1).astype(bf16)
    p_r, p_m = _proj(x2, w0, (RWKV_COLS, m_main + LANES))
    y_a = _rwkv_pallas(p_r.reshape(B, S, -1), l0_rwkv_mu, l0_rwkv_w0, l0_rwkv_w_up, l0_rwkv_a0,
                       l0_rwkv_a_up, l0_rwkv_g_up, l0_rwkv_k_k, l0_rwkv_k_a, l0_rwkv_r_k,
                       l0_rwkv_ln_g, l0_rwkv_ln_b)
    y_b = _mlstm_pallas(p_m.reshape(B, S, -1), l0_mlstm_conv_w, l0_mlstm_conv_b, l0_mlstm_ig_b,
                        l0_mlstm_fg_b, l0_mlstm_hn_g, l0_mlstm_hn_b)
    x2, xt = _out_proj_ln([y_a.reshape(n, -1), y_b.reshape(n, -1)], l0_w_out.astype(bf16), x2,
                          l0_ln1_g, l0_ln1_b)
    y = _peer_pallas(x2, xt, l0_peer_wq, l0_peer_k1, l0_peer_k2, l0_peer_u, l0_peer_v)
    x2 = _resid_ln(x2, y, l0_ln2_g, l0_ln2_b)

    f_main = 4 * FOX_WIDTH
    w1 = jnp.concatenate([l1_w_in[:, :f_main], pad_lanes(l1_w_in[:, f_main:])], axis=1).astype(bf16)
    (p_f,) = _proj(x2, w1, (f_main + LANES,))
    p_f = p_f.reshape(B, S, -1)
    qa, ka, vt = _fox_prep(p_f, l1_fox_qn_g, l1_fox_kn_g, l1_fox_f_b)
    o = _fox_attn(qa, ka, vt, p_f)
    x2, xt = _out_proj_ln([o.reshape(n, -1)], l1_w_out.astype(bf16), x2, l1_ln1_g, l1_ln1_b)
    y = _peer_pallas(x2, xt, l1_peer_wq, l1_peer_k1, l1_peer_k2, l1_peer_u, l1_peer_v)
    x2 = _resid_ln(x2, y, l1_ln2_g, l1_ln2_b)
    return x2.reshape(B, S, D)
```

```python
import functools

import jax
import jax.numpy as jnp
from jax import lax
from jax.experimental import pallas as pl
from jax.experimental.pallas import tpu as pltpu

D_MODEL = 1024
DEPTH = 2
DN_ALPHA = (2.0 * DEPTH) ** 0.25
LN_EPS = 1e-5

RWKV_WIDTH = D_MODEL // 2
RWKV_HEAD = 64
RWKV_HEADS = RWKV_WIDTH // RWKV_HEAD
RWKV_W_RANK = 64
RWKV_A_RANK = 64
RWKV_G_RANK = 128
RWKV_GN_EPS = 1e-5 * RWKV_HEAD
RWKV_COLS = 3 * RWKV_WIDTH + RWKV_W_RANK + RWKV_A_RANK + RWKV_G_RANK

MLSTM_WIDTH = D_MODEL // 2
MLSTM_HEAD = 128
MLSTM_HEADS = MLSTM_WIDTH // MLSTM_HEAD
MLSTM_CHUNK = 64

FOX_HEAD = 64
FOX_HEADS = D_MODEL // FOX_HEAD
FOX_WIDTH = FOX_HEADS * FOX_HEAD

PEER_HEADS = 8
PEER_NKEYS = 128
PEER_TOPK = 16
PEER_DKEY = 256
PEER_DHALF = PEER_DKEY // 2


def _resid_ln_kernel(x_ref, y_ref, g_ref, b_ref, o_ref):
    z = DN_ALPHA * x_ref[...] + y_ref[...]
    mu = jnp.mean(z, axis=-1, keepdims=True)
    zc = z - mu
    var = jnp.mean(zc * zc, axis=-1, keepdims=True)
    o_ref[...] = zc * lax.rsqrt(var + LN_EPS) * g_ref[...] + b_ref[...]


def _resid_ln(x2, y2, g, b, tm=512):
    n, d = x2.shape
    row = pl.BlockSpec((tm, d), lambda i: (i, 0))
    vec = pl.BlockSpec((1, d), lambda i: (0, 0))
    return pl.pallas_call(
        _resid_ln_kernel,
        grid=(n // tm,),
        in_specs=[row, row, vec, vec],
        out_specs=row,
        out_shape=jax.ShapeDtypeStruct((n, d), jnp.float32),
        name="resid_ln",
    )(x2, y2, g.reshape(1, d), b.reshape(1, d))


VMEM_LIMIT_BYTES = 56 * 1024 * 1024


def _proj_kernel(x_ref, w_ref, *o_refs):
    xb = x_ref[...].astype(jnp.bfloat16)
    start = 0
    for o_ref in o_refs:
        width = o_ref.shape[1]
        o_ref[...] = jnp.dot(xb, w_ref[:, start:start + width], preferred_element_type=jnp.float32)
        start += width


def _proj(x2, w_bf16, widths, tm=256):
    n, d = x2.shape
    assert sum(widths) == w_bf16.shape[1]
    return pl.pallas_call(
        _proj_kernel,
        grid=(n // tm,),
        in_specs=[pl.BlockSpec((tm, d), lambda i: (i, 0)),
                  pl.BlockSpec(w_bf16.shape, lambda i: (0, 0))],
        out_specs=[pl.BlockSpec((tm, w), lambda i: (i, 0)) for w in widths],
        out_shape=[jax.ShapeDtypeStruct((n, w), jnp.float32) for w in widths],
        compiler_params=pltpu.CompilerParams(vmem_limit_bytes=VMEM_LIMIT_BYTES),
        name="in_proj",
    )(x2, w_bf16)


def _out_proj_ln_kernel(*refs, n_parts):
    y_refs, (w_ref, x_ref, g_ref, b_ref, o_ref, ot_ref) = refs[:n_parts], refs[n_parts:]
    acc, start = None, 0
    for y_ref in y_refs:
        width = y_ref.shape[1]
        part = jnp.dot(y_ref[...].astype(jnp.bfloat16), w_ref[start:start + width, :],
                       preferred_element_type=jnp.float32)
        acc = part if acc is None else acc + part
        start += width
    z = DN_ALPHA * x_ref[...] + acc
    mu = jnp.mean(z, axis=-1, keepdims=True)
    zc = z - mu
    var = jnp.mean(zc * zc, axis=-1, keepdims=True)
    out = zc * lax.rsqrt(var + LN_EPS) * g_ref[...] + b_ref[...]
    o_ref[...] = out
    ot_ref[...] = out.T.astype(jnp.bfloat16)


def _out_proj_ln(ys, w_bf16, x2, g, b, tm=256):
    n, d = x2.shape
    row = lambda width: pl.BlockSpec((tm, width), lambda i: (i, 0))
    vec = pl.BlockSpec((1, d), lambda i: (0, 0))
    return pl.pallas_call(
        functools.partial(_out_proj_ln_kernel, n_parts=len(ys)),
        grid=(n // tm,),
        in_specs=[row(y.shape[1]) for y in ys]
                 + [pl.BlockSpec(w_bf16.shape, lambda i: (0, 0)), row(d), vec, vec],
        out_specs=[row(d), pl.BlockSpec((d, tm), lambda i: (0, i))],
        out_shape=[jax.ShapeDtypeStruct((n, d), jnp.float32), jax.ShapeDtypeStruct((d, n), jnp.bfloat16)],
        name="out_proj_ln",
    )(*ys, w_bf16, x2, g.reshape(1, d), b.reshape(1, d))


PEER_EXPERTS = PEER_NKEYS * PEER_NKEYS
PEER_SELECT_TILE = 256
PEER_TOK_TILE = 512
PEER_EXPERT_CHUNK = 1024
PEER_E1_GROUP = 4
PEER_E2_SUB = 32
PEER_A_PIECES = 8
PEER_C_PIECES = 4
LANES = 128
MXU_DIM = 256
INV_SQRT2 = 0.7071067811865476
_NT = (((1,), (1,)), ((), ()))


def _topk_desc(work, k):
    vals = []
    for _ in range(k):
        m = jnp.max(work, axis=0, keepdims=True)
        vals.append(m)
        work = jnp.where(work >= m, -jnp.inf, work)
    return vals


def _oddeven_merge(lo, hi, r):
    step = r * 2
    if step < hi - lo:
        yield from _oddeven_merge(lo, hi, step)
        yield from _oddeven_merge(lo + r, hi, step)
        yield from [(i, i + r) for i in range(lo + r, hi - r, step)]
    else:
        yield (lo, lo + r)


def _oddeven_sort(lo, hi):
    if hi - lo >= 1:
        mid = lo + (hi - lo) // 2
        yield from _oddeven_sort(lo, mid)
        yield from _oddeven_sort(mid + 1, hi)
        yield from _oddeven_merge(lo, hi, 1)


SUBLANES = 8


def _sorted_top(s, k):
    v = [s[i * SUBLANES:(i + 1) * SUBLANES, :] for i in range(k)]

    def exchange(i, j):
        v[i], v[j] = jnp.maximum(v[i], v[j]), jnp.minimum(v[i], v[j])

    for i, j in _oddeven_sort(0, k - 1):
        exchange(i, j)
    shift = SUBLANES // 2
    while shift >= 1:
        v = [jnp.maximum(v[i], pltpu.roll(v[k - 1 - i], shift, axis=0)) for i in range(k)]
        d = k // 2
        while d >= 1:
            for i in range(k):
                if i & d == 0:
                    exchange(i, i + d)
            d //= 2
        shift //= 2
    return [t[0:1, :] for t in v]


def _peer_select_kernel(x_ref, wqt_ref, k1_ref, k2_ref,
                        phi_ref, a1_ref, a2_ref, qt_ref):
    f32, bf16 = jnp.float32, jnp.bfloat16
    K = PEER_TOPK
    qt_ref[...] = lax.dot_general(wqt_ref[...], x_ref[...].astype(bf16), _NT, preferred_element_type=f32)
    n_groups = x_ref.shape[0] // LANES

    def head_pair(hp, carry):
        hs = [2 * hp, 2 * hp + 1]
        sc = []
        for h in hs:
            base = pl.multiple_of(h * PEER_DKEY, PEER_DKEY)
            q1 = qt_ref[pl.ds(base, PEER_DHALF), :].astype(bf16)
            q2 = qt_ref[pl.ds(base + PEER_DHALF, PEER_DHALF), :].astype(bf16)
            sc.append((jnp.dot(k1_ref[h], q1, preferred_element_type=f32),
                       jnp.dot(k2_ref[h], q2, preferred_element_type=f32)))
        items = [(j, c) for j in range(len(hs)) for c in range(n_groups)]
        lanes = lambda c: slice(c * LANES, (c + 1) * LANES)
        s1c = [sc[j][0][:, lanes(c)] for j, c in items]
        s2c = [sc[j][1][:, lanes(c)] for j, c in items]
        it = range(len(items))
        v1 = [_sorted_top(s1c[i], K) for i in it]
        v2 = [_sorted_top(s2c[i], K) for i in it]
        pairs = [(a, b) for a in range(K) for b in range(K) if (a + 1) * (b + 1) <= K + 1]
        pad = [jnp.full_like(v1[0][0], -jnp.inf)] * (-len(pairs) % SUBLANES)
        cand = [jnp.concatenate([v1[i][a] + v2[i][b] for a, b in pairs] + pad, axis=0) for i in it]
        vc = [_topk_desc(cand[i], K + 1) for i in it]
        inv_z = [1.0 / jnp.sum(jnp.where(cand[i] >= vc[i][K - 1], jnp.exp(cand[i] - vc[i][0]), 0.0),
                               axis=0, keepdims=True) for i in it]
        for i, (j, c) in enumerate(items):
            h, sl = hs[j], lanes(c)
            cut = 0.5 * (vc[i][K - 1] + vc[i][K])
            s1m = jnp.where(s1c[i] >= v1[i][K - 1], s1c[i], -jnp.inf)
            s2m = jnp.where(s2c[i] >= v2[i][K - 1], s2c[i], -jnp.inf)
            phi_ref[h, :, sl] = jnp.exp(cut - s1m - v2[i][0]) * inv_z[i]
            a1_ref[h, :, sl] = jnp.exp(s1m - v1[i][0])
            a2_ref[h, :, sl] = jnp.exp(s2m - v2[i][0]) * inv_z[i]
        return carry

    lax.fori_loop(0, PEER_HEADS // 2, head_pair, 0)


def _peer_select(xb, wqt, k1, k2):
    n, d = xb.shape
    T = PEER_SELECT_TILE
    H, NK = PEER_HEADS, PEER_NKEYS
    full = lambda shape: pl.BlockSpec(shape, lambda i: (0,) * len(shape))
    tok3 = lambda rows: pl.BlockSpec((H, rows, T), lambda i: (0, 0, i))
    f32 = jnp.float32
    return pl.pallas_call(
        _peer_select_kernel,
        grid=(n // T,),
        in_specs=[pl.BlockSpec((T, d), lambda i: (i, 0)), full(wqt.shape), full(k1.shape), full(k2.shape)],
        out_specs=[tok3(NK)] * 3,
        out_shape=[jax.ShapeDtypeStruct((H, NK, n), f32)] * 3,
        scratch_shapes=[pltpu.VMEM((H * PEER_DKEY, T), f32)],
        name="peer_select",
    )(xb, wqt, k1, k2)


def _peer_expert_kernel(xt_ref, u_ref, v_ref, phi_ref, a1_ref, a2_ref,
                        o_ref, acc_ref, h0_ref, h1_ref, w0_ref, w1_ref):
    f32, bf16 = jnp.float32, jnp.bfloat16
    s = pl.program_id(0)
    n_chunks = PEER_EXPERTS // PEER_EXPERT_CHUNK
    c_chunk = jnp.maximum(s - 2, 0) % n_chunks

    @pl.when(s == 0)
    def _():
        for ref in (h0_ref, h1_ref, w0_ref, w1_ref):
            ref[...] = jnp.zeros_like(ref)

    @pl.when(c_chunk == 0)
    def _():
        acc_ref[...] = jnp.zeros_like(acc_ref)

    T = xt_ref.shape[1]
    NK, G, SB = PEER_NKEYS, PEER_E1_GROUP, PEER_E2_SUB
    CE = PEER_EXPERT_CHUNK
    PA, PC = PEER_A_PIECES, PEER_C_PIECES

    def step(ha_ref, hb_ref, wb_ref, wc_ref):
        def stage_a(k):
            rows = CE // PA
            u_blk = u_ref[k * rows:(k + 1) * rows, :]
            ha_ref[k * rows:(k + 1) * rows, :] = jnp.dot(u_blk, xt_ref[...], preferred_element_type=f32)

        def stage_b(q, c, sb):
            ts = slice(c * LANES, (c + 1) * LANES)
            e2s = slice(sb * SB, (sb + 1) * SB)
            g = [None] * G
            for hd in range(PEER_HEADS):
                a2 = a2_ref[hd, e2s, ts]
                for e in range(G):
                    r = q * G + e
                    term = jnp.where(a2 > phi_ref[hd, r:r + 1, ts], a1_ref[hd, r:r + 1, ts] * a2, 0.0)
                    g[e] = term if g[e] is None else g[e] + term
            for e in range(G):
                lo = (q * G + e) * NK + sb * SB
                hh = hb_ref[lo:lo + SB, ts]
                wb_ref[lo:lo + SB, ts] = (g[e] * (0.5 * hh * (1.0 + lax.erf(hh * INV_SQRT2)))).astype(bf16)

        b_blocks = [(q, c, sb) for q in range(CE // (G * NK)) for c in range(T // LANES)
                    for sb in range(NK // SB)]
        def stage_c(k):
            rows = slice(k * CE // PC, (k + 1) * CE // PC)
            acc_ref[...] += lax.dot_general(wc_ref[rows, :], v_ref[rows, :], _TN,
                                            preferred_element_type=f32)

        nb = len(b_blocks)
        mxu_pieces = [functools.partial(stage_a, k) for k in range(PA)]
        mxu_pieces[1:1] = [functools.partial(stage_c, k) for k in range(PC)]
        every = nb // len(mxu_pieces)
        for t, blk in enumerate(b_blocks):
            stage_b(*blk)
            if t % every == 0 and t // every < len(mxu_pieces):
                mxu_pieces[t // every]()

    pl.when(s % 2 == 0)(functools.partial(step, h0_ref, h1_ref, w1_ref, w0_ref))
    pl.when(s % 2 == 1)(functools.partial(step, h1_ref, h0_ref, w0_ref, w1_ref))

    @pl.when(c_chunk == n_chunks - 1)
    def _():
        o_ref[...] = acc_ref[...]


def _peer_experts(xtb, ub, vb, phi, a1, a2):
    d, n = xtb.shape
    T, CE = PEER_TOK_TILE, PEER_EXPERT_CHUNK
    H, NK = PEER_HEADS, PEER_NKEYS
    e1_per_chunk = CE // NK
    nc = PEER_EXPERTS // CE
    last = (n // T) * nc - 1
    at = lambda s, lag: jnp.clip(s - lag, 0, last)
    return pl.pallas_call(
        _peer_expert_kernel,
        grid=(last + 3,),
        in_specs=[
            pl.BlockSpec((d, T), lambda s: (0, at(s, 0) // nc)),
            pl.BlockSpec((CE, d), lambda s: (at(s, 0) % nc, 0)),
            pl.BlockSpec((CE, d), lambda s: (at(s, 2) % nc, 0)),
            pl.BlockSpec((H, e1_per_chunk, T), lambda s: (0, at(s, 1) % nc, at(s, 1) // nc)),
            pl.BlockSpec((H, e1_per_chunk, T), lambda s: (0, at(s, 1) % nc, at(s, 1) // nc)),
            pl.BlockSpec((H, NK, T), lambda s: (0, 0, at(s, 1) // nc)),
        ],
        out_specs=pl.BlockSpec((T, d), lambda s: (at(s, 2) // nc, 0)),
        out_shape=jax.ShapeDtypeStruct((n, d), jnp.float32),
        scratch_shapes=[pltpu.VMEM((T, d), jnp.float32),
                        pltpu.VMEM((CE, T), jnp.float32), pltpu.VMEM((CE, T), jnp.float32),
                        pltpu.VMEM((CE, T), jnp.bfloat16), pltpu.VMEM((CE, T), jnp.bfloat16)],
        compiler_params=pltpu.CompilerParams(dimension_semantics=("arbitrary",),
                                             vmem_limit_bytes=VMEM_LIMIT_BYTES),
        name="peer_experts",
    )(xtb, ub, vb, phi, a1, a2)


def _peer_pallas(x2, xt_bf16, w_q, sub_k1, sub_k2, u_tab, v_tab):
    bf16 = jnp.bfloat16
    sel = _peer_select(x2, w_q.T.astype(bf16), sub_k1.astype(bf16), sub_k2.astype(bf16))
    return _peer_experts(xt_bf16, u_tab.astype(bf16), v_tab.astype(bf16), *sel)


RWKV_CHUNK = 64
RWKV_BATCH_PER_STEP = 2
RWKV_PAIR = LANES // RWKV_HEAD
RWKV_PAIRS = RWKV_HEADS // RWKV_PAIR
_NN = (((1,), (0,)), ((), ()))
_TN = (((0,), (0,)), ((), ()))


def _split_bf16(a):
    hi = a.astype(jnp.bfloat16)
    lo = (a - hi.astype(jnp.float32)).astype(jnp.bfloat16)
    return hi, lo


def _mm2(a, b, dims):
    d = lambda u, v: lax.dot_general(u, v, dims, preferred_element_type=jnp.float32)
    (ah, al), (bh, bl) = _split_bf16(a), _split_bf16(b)
    return d(ah, bh) + d(ah, bl) + d(al, bh)


def _mm_exact_rhs(a, b_exact, dims, terms=2):
    d = lambda u: lax.dot_general(u, b_exact, dims, preferred_element_type=jnp.float32)
    out, rem = None, a
    for _ in range(terms):
        piece = rem.astype(jnp.bfloat16)
        rem = rem - piece.astype(jnp.float32)
        out = d(piece) if out is None else out + d(piece)
    return out


def _mm_exact_lhs(a_exact, b, dims, terms=3):
    d = lambda v: lax.dot_general(a_exact, v, dims, preferred_element_type=jnp.float32)
    out, rem = None, b
    for _ in range(terms):
        piece = rem.astype(jnp.bfloat16)
        rem = rem - piece.astype(jnp.float32)
        out = d(piece) if out is None else out + d(piece)
    return out


def _softplus(z):
    return jnp.maximum(z, 0.0) + jnp.log1p(jnp.exp(-jnp.abs(z)))


def _rwkv_kernel(p_ref, prev_ref, mu_ref, w0_ref, wup_ref, a0_ref, aup_ref, gup_ref,
                 kk_ref, ka_ref, rk_ref, lng_ref, lnb_ref, ones_ref,
                 o_ref, state_ref):
    f32, bf16 = jnp.float32, jnp.bfloat16
    L, W = RWKV_CHUNK, RWKV_WIDTH
    c_idx = pl.program_id(1)

    @pl.when(c_idx == 0)
    def _():
        state_ref[...] = jnp.zeros_like(state_ref)

    NB = p_ref.shape[0]
    parts = []
    for bi in range(NB):
        pb = p_ref[bi]
        row = lax.broadcasted_iota(jnp.int32, pb.shape, 0)
        prev_row = jnp.where(c_idx == 0, 0.0, prev_ref[bi, 7:8, :])
        shifted = jnp.where(row == 0, prev_row, pltpu.roll(pb, 1, axis=0))
        parts.append(pb + (shifted - pb) * mu_ref[...])
    p = jnp.concatenate(parts, axis=0)
    r, k, v = p[:, 0:W], p[:, W:2 * W], p[:, 2 * W:3 * W]
    o = 3 * W
    xw = p[:, o:o + RWKV_W_RANK]
    xa = p[:, o + RWKV_W_RANK:o + RWKV_W_RANK + RWKV_A_RANK]
    xg = p[:, o + RWKV_W_RANK + RWKV_A_RANK:]
    dotd = lambda u, m: jnp.dot(u.astype(bf16), m, preferred_element_type=f32)
    lw = -jnp.exp(-_softplus(-(w0_ref[...] + dotd(jnp.tanh(xw), wup_ref[...]))) - 0.5)
    a = jax.nn.sigmoid(a0_ref[...] + dotd(xa, aup_ref[...]))
    g = dotd(jax.nn.sigmoid(xg), gup_ref[...])
    ones_bd = ones_ref[...]
    head_sum = lambda t: _mm_exact_rhs(t, ones_bd, _NN)
    kk = k * kk_ref[...]
    kk = kk / jnp.maximum(jnp.sqrt(head_sum(kk * kk)), 1e-12)
    k = k * (1.0 + (a - 1.0) * ka_ref[...])

    lane = lax.broadcasted_iota(jnp.int32, (L, LANES), 1)
    trow = lax.broadcasted_iota(jnp.int32, (L, LANES), 0)
    s_in = lane % RWKV_HEAD
    strict, incl = s_in < trow, s_in <= trow
    eye_pair = (s_in == trow).astype(f32)
    m0 = lane < RWKV_HEAD
    tril = (lax.broadcasted_iota(jnp.int32, (L, L), 1)
            <= lax.broadcasted_iota(jnp.int32, (L, L), 0)).astype(bf16)
    r2 = lax.broadcasted_iota(jnp.int32, (LANES, LANES), 0) // RWKV_HEAD
    c2 = lax.broadcasted_iota(jnp.int32, (LANES, LANES), 1) // RWKV_HEAD
    bd_mask = r2 == c2

    def bd(t):
        return jnp.concatenate([jnp.where(m0, t, 0.0), jnp.where(m0, 0.0, t)], axis=0)

    pairs = range(NB * RWKV_PAIRS)
    cut = lambda t: [t[bi * L:(bi + 1) * L, pr * LANES:(pr + 1) * LANES]
                     for bi in range(NB) for pr in range(RWKV_PAIRS)]
    rp, kp, vp, kkp, ap, lwp = cut(r), cut(k), cut(v), cut(kk), cut(a), cut(lw)
    c = [_mm_exact_lhs(tril, lwp[p], _NN) for p in pairs]
    at = [-kkp[p] * jnp.exp(c[p] - lwp[p]) for p in pairs]
    eni = [jnp.exp(-c[p]) for p in pairs]
    bt = [kkp[p] * ap[p] * eni[p] for p in pairs]
    kt = [kp[p] * eni[p] for p in pairs]
    rt = [rp[p] * jnp.exp(c[p]) for p in pairs]
    g_last = [jnp.exp(c[p][L - 1:L, :]) for p in pairs]
    gram = [_mm2(jnp.concatenate([at[p], rt[p]], axis=0),
                 jnp.concatenate([bd(bt[p]), bd(kt[p])], axis=0), _NT) for p in pairs]
    n_ab = [jnp.where(strict, gram[p][0:L, 0:LANES], 0.0) for p in pairs]
    a_ak = [jnp.where(strict, gram[p][0:L, LANES:], 0.0) for p in pairs]
    a_rb = [jnp.where(incl, gram[p][L:, 0:LANES], 0.0) for p in pairs]
    a_rk = [jnp.where(incl, gram[p][L:, LANES:], 0.0) for p in pairs]
    tinv, m = [eye_pair + n_ab[p] for p in pairs], n_ab
    for _ in range(5):
        m = [_mm2(m[p], bd(m[p]), _NN) for p in pairs]
        tinv = [tinv[p] + _mm2(m[p], bd(tinv[p]), _NN) for p in pairs]
    s0 = [state_ref[p] for p in pairs]
    pq = [_mm2(at[p], s0[p], _NT) + _mm2(a_ak[p], bd(vp[p]), _NN) for p in pairs]
    u = [_mm2(tinv[p], bd(pq[p]), _NN) for p in pairs]
    ys = [_mm2(rt[p], s0[p], _NT)
          + _mm2(jnp.concatenate([a_rb[p], a_rk[p]], axis=1),
                 jnp.concatenate([bd(u[p]), bd(vp[p])], axis=0), _NN) for p in pairs]
    for p in pairs:
        upd = _mm2(jnp.concatenate([u[p], vp[p]], axis=0),
                   jnp.concatenate([bt[p], kt[p]], axis=0), _TN)
        state_ref[p] = (s0[p] + jnp.where(bd_mask, upd, 0.0)) * g_last[p]
    y = jnp.concatenate([jnp.concatenate(ys[bi * RWKV_PAIRS:(bi + 1) * RWKV_PAIRS], axis=1)
                         for bi in range(NB)], axis=0)

    inv_n = 1.0 / RWKV_HEAD
    mean = head_sum(y) * inv_n
    yc = y - mean
    var = head_sum(yc * yc) * inv_n
    y = yc * lax.rsqrt(var + RWKV_GN_EPS) * lng_ref[...] + lnb_ref[...]
    y = y + head_sum(r * k * rk_ref[...]) * v
    out = y * g
    for bi in range(NB):
        o_ref[bi] = out[bi * L:(bi + 1) * L]


def _rwkv_pallas(p_r, mu, w0, w_up, a0, a_up, g_up, k_k, k_a, r_k, ln_g, ln_b):
    B, S, C = p_r.shape
    L, W, NB = RWKV_CHUNK, RWKV_WIDTH, RWKV_BATCH_PER_STEP
    bf16 = jnp.bfloat16
    vecw = lambda t: t.reshape(1, W)
    ones_bd = jnp.kron(jnp.eye(RWKV_HEADS, dtype=bf16), jnp.ones((RWKV_HEAD, RWKV_HEAD), bf16))
    full = lambda a: pl.BlockSpec(a.shape, lambda b, c: (0,) * a.ndim)
    args = [mu.reshape(1, C), vecw(w0), w_up.astype(bf16), vecw(a0), a_up.astype(bf16),
            g_up.astype(bf16), vecw(k_k), vecw(k_a), vecw(r_k), vecw(ln_g), vecw(ln_b), ones_bd]
    return pl.pallas_call(
        _rwkv_kernel,
        grid=(B // NB, S // L),
        in_specs=[pl.BlockSpec((NB, L, C), lambda b, c: (b, c, 0)),
                  pl.BlockSpec((NB, 8, C), lambda b, c: (b, jnp.maximum(c * (L // 8) - 1, 0), 0))]
                 + [full(a) for a in args],
        out_specs=pl.BlockSpec((NB, L, W), lambda b, c: (b, c, 0)),
        out_shape=jax.ShapeDtypeStruct((B, S, W), jnp.float32),
        scratch_shapes=[pltpu.VMEM((NB * RWKV_PAIRS, LANES, LANES), jnp.float32)],
        compiler_params=pltpu.CompilerParams(dimension_semantics=("arbitrary", "arbitrary")),
        name="rwkv7",
    )(p_r, p_r, *args)


MLSTM_CONV = 4
MLSTM_GATE_LANES = LANES


def _mlstm_kernel(p_ref, prev_ref, cw_ref, cb_ref, igb_ref, fgb_ref, hng_ref, hnb_ref,
                  o_ref, c_ref, n_ref, m_ref):
    f32, bf16 = jnp.float32, jnp.bfloat16
    L, W, H, D = MLSTM_CHUNK, MLSTM_WIDTH, MLSTM_HEADS, MLSTM_HEAD
    c_idx = pl.program_id(1)

    @pl.when(c_idx == 0)
    def _():
        c_ref[...] = jnp.zeros_like(c_ref)
        n_ref[...] = jnp.zeros_like(n_ref)
        m_ref[...] = jnp.zeros_like(m_ref)

    p = p_ref[0]
    z = p[:, 0:2 * W]
    prev = jnp.where(c_idx == 0, 0.0, prev_ref[0, :, 0:2 * W])
    ext = jnp.concatenate([prev, z], axis=0)
    conv = cb_ref[...] + cw_ref[MLSTM_CONV - 1:MLSTM_CONV, :] * z
    for j in range(MLSTM_CONV - 1):
        d = MLSTM_CONV - 1 - j
        conv = conv + cw_ref[j:j + 1, :] * ext[8 - d:8 - d + L, :]
    qk = conv * jax.nn.sigmoid(conv)
    q_all, k_all = qk[:, 0:W] * (D ** -0.5), qk[:, W:2 * W]
    v_all, o_all = p[:, 2 * W:3 * W], p[:, 3 * W:4 * W]
    gates = p[:, 4 * W:]
    igl = gates + igb_ref[...]
    lfl = -_softplus(-(gates + fgb_ref[...]))
    tril = (lax.broadcasted_iota(jnp.int32, (L, L), 1)
            <= lax.broadcasted_iota(jnp.int32, (L, L), 0))
    bcum = _mm_exact_lhs(tril.astype(bf16), lfl, _NN)
    e_all = igl - pltpu.roll(bcum, LANES - H, axis=1)
    lane = lax.broadcasted_iota(jnp.int32, (L, LANES), 1)
    hd = range(H)
    cols = lambda t: [t[:, h * D:(h + 1) * D] for h in hd]
    q, k, v, o = cols(q_all), cols(k_all), cols(v_all), cols(o_all)
    qb, kb, vb = ([t.astype(bf16) for t in ts] for ts in (q, k, v))
    b_col = [bcum[:, H + h:H + h + 1] for h in hd]
    ig_col = [igl[:, h:h + 1] for h in hd]
    e_row = [_mm_exact_lhs((lane == h).astype(bf16), e_all, _NT) for h in hd]
    qk = [lax.dot_general(qb[h], kb[h], _NT, preferred_element_type=f32) for h in hd]
    m_prev = [m_ref[h, 0:1, 0:1] for h in hd]
    c_prev = [c_ref[h] for h in hd]
    n_prev = [n_ref[h, 0:1, :] for h in hd]
    qc = [jnp.dot(qb[h], c_prev[h].astype(bf16), preferred_element_type=f32) for h in hd]
    dmat = [jnp.where(tril, b_col[h] + e_row[h], -jnp.inf) for h in hd]
    inter = [b_col[h] + m_prev[h] for h in hd]
    m_t = [jnp.maximum(inter[h], jnp.max(dmat[h], axis=-1, keepdims=True)) for h in hd]
    sc = [qk[h] * jnp.exp(dmat[h] - m_t[h]) for h in hd]
    carry_in = [jnp.exp(inter[h] - m_t[h]) for h in hd]
    sv = [jnp.dot(sc[h].astype(bf16), vb[h], preferred_element_type=f32) for h in hd]
    den = [jnp.sum(sc[h], axis=-1, keepdims=True)
           + carry_in[h] * jnp.sum(q[h] * n_prev[h], axis=-1, keepdims=True) for h in hd]
    hval = [(sv[h] + carry_in[h] * qc[h]) / jnp.maximum(jnp.abs(den[h]), jnp.exp(-m_t[h])) for h in hd]
    b_last = [b_col[h][L - 1:L, :] for h in hd]
    gs = [b_last[h] - b_col[h] + ig_col[h] for h in hd]
    m_new = [jnp.maximum(b_last[h] + m_prev[h], jnp.max(gs[h], axis=0, keepdims=True)) for h in hd]
    keep = [jnp.exp(b_last[h] + m_prev[h] - m_new[h]) for h in hd]
    wk = [jnp.exp(gs[h] - m_new[h]) * k[h] for h in hd]
    kv = [lax.dot_general(wk[h].astype(bf16), vb[h], _TN, preferred_element_type=f32) for h in hd]
    outs = []
    for h in hd:
        c_ref[h] = keep[h] * c_prev[h] + kv[h]
        n_ref[h] = jnp.broadcast_to(keep[h] * n_prev[h] + jnp.sum(wk[h], axis=0, keepdims=True), (8, D))
        m_ref[h] = jnp.broadcast_to(m_new[h], (8, LANES))
        mu = jnp.mean(hval[h], axis=-1, keepdims=True)
        hc = hval[h] - mu
        var = jnp.mean(hc * hc, axis=-1, keepdims=True)
        hs = slice(h * D, (h + 1) * D)
        hn = hc * lax.rsqrt(var + LN_EPS) * hng_ref[:, hs] + hnb_ref[:, hs]
        outs.append(jax.nn.sigmoid(o[h]) * hn)
    o_ref[0] = jnp.concatenate(outs, axis=1)


def _mlstm_pallas(p_m, conv_w, conv_b, ig_b, fg_b, hn_g, hn_b):
    B, S, C = p_m.shape
    L, W, H, D = MLSTM_CHUNK, MLSTM_WIDTH, MLSTM_HEADS, MLSTM_HEAD
    pad = lambda t, off: jnp.zeros((1, LANES), jnp.float32).at[0, off:off + H].set(t)
    args = [conv_w, conv_b.reshape(1, 2 * W), pad(ig_b, 0), pad(fg_b, H),
            hn_g.reshape(1, W), hn_b.reshape(1, W)]
    full = lambda a: pl.BlockSpec(a.shape, lambda b, c: (0,) * a.ndim)
    return pl.pallas_call(
        _mlstm_kernel,
        grid=(B, S // L),
        in_specs=[pl.BlockSpec((1, L, C), lambda b, c: (b, c, 0)),
                  pl.BlockSpec((1, 8, C), lambda b, c: (b, jnp.maximum(c * (L // 8) - 1, 0), 0))]
                 + [full(a) for a in args],
        out_specs=pl.BlockSpec((1, L, W), lambda b, c: (b, c, 0)),
        out_shape=jax.ShapeDtypeStruct((B, S, W), jnp.float32),
        scratch_shapes=[pltpu.VMEM((H, D, D), jnp.float32),
                        pltpu.VMEM((H, 8, D), jnp.float32),
                        pltpu.VMEM((H, 8, LANES), jnp.float32)],
        compiler_params=pltpu.CompilerParams(dimension_semantics=("arbitrary", "arbitrary")),
        name="mlstm",
    )(p_m, p_m, *args)


FOX_TILE = 256
FOX_KEY_GROUP = 4
FOX_HEADS_PER_STEP = 8
FOX_AUG = LANES
FOX_GATE_LANES = LANES


def _fox_prep_kernel(p_ref, qg_ref, kg_ref, fb_ref, sel_ref, selt_ref, qa_ref, ka_ref, vt_ref, carry_ref):
    f32, bf16 = jnp.float32, jnp.bfloat16
    T, W, H, D = FOX_TILE, FOX_WIDTH, FOX_HEADS, FOX_HEAD
    i = pl.program_id(1)

    @pl.when(i == 0)
    def _():
        carry_ref[...] = jnp.zeros_like(carry_ref)

    p = p_ref[0]
    ms = lambda t: _mm_exact_rhs(_mm_exact_rhs(t * t, sel_ref[...], _NN), selt_ref[...], _NN) * (1.0 / D)
    q = p[:, 0:W]
    q = q * lax.rsqrt(ms(q) + 1e-6) * qg_ref[...]
    k = p[:, W:2 * W]
    k = k * lax.rsqrt(ms(k) + 1e-6) * kg_ref[...] * (D ** -0.5)
    vt = p[:, 2 * W:3 * W].T.astype(bf16)
    logf = -_softplus(-(p[:, 4 * W:] + fb_ref[...]))
    tril = (lax.broadcasted_iota(jnp.int32, (T, T), 1)
            <= lax.broadcasted_iota(jnp.int32, (T, T), 0)).astype(bf16)
    c = _mm_exact_lhs(tril, logf, _NN) + carry_ref[0:1, :]
    carry_ref[...] = jnp.broadcast_to(c[T - 1:T, :], carry_ref.shape)
    c1 = c.astype(bf16).astype(f32)
    c2 = (c - c1).astype(bf16).astype(f32)
    c3 = (c - c1 - c2).astype(bf16).astype(f32)
    lane = lax.broadcasted_iota(jnp.int32, (T, D), 1)
    for h in range(H):
        hs = slice(h * D, (h + 1) * D)
        c1h, c2h, c3h = c1[:, h:h + 1], c2[:, h:h + 1], c3[:, h:h + 1]
        pieces = jnp.where(lane % 3 == 0, c1h, jnp.where(lane % 3 == 1, c2h, c3h))
        q_aug = jnp.where(lane < 3, 1.0, jnp.where(lane < 6, pieces, 0.0))
        k_aug = jnp.where(lane < 3, -pieces, jnp.where(lane < 6, 1.0, 0.0))
        qa_ref[0, h] = jnp.concatenate([q[:, hs], q_aug], axis=1).astype(bf16)
        ka_ref[0, h] = jnp.concatenate([k[:, hs], k_aug], axis=1).astype(bf16)
        vt_ref[0, h, 0] = vt[hs, :]


def _fox_prep(p_f, qn_g, kn_g, f_b):
    B, S, C = p_f.shape
    T, W, H, D = FOX_TILE, FOX_WIDTH, FOX_HEADS, FOX_HEAD
    bf16 = jnp.bfloat16
    sel = jnp.pad(jnp.kron(jnp.eye(H, dtype=bf16), jnp.ones((D, 1), bf16)), ((0, 0), (0, LANES - H)))
    fb = jnp.zeros((1, FOX_GATE_LANES), jnp.float32).at[0, :H].set(f_b)
    args = [qn_g.reshape(1, W), kn_g.reshape(1, W), fb, sel, sel.T]
    full = lambda a: pl.BlockSpec(a.shape, lambda b, i: (0,) * a.ndim)
    return pl.pallas_call(
        _fox_prep_kernel,
        grid=(B, S // T),
        in_specs=[pl.BlockSpec((1, T, C), lambda b, i: (b, i, 0))] + [full(a) for a in args],
        out_specs=[pl.BlockSpec((1, H, T, FOX_AUG), lambda b, i: (b, 0, i, 0)),
                   pl.BlockSpec((1, H, T, FOX_AUG), lambda b, i: (b, 0, i, 0)),
                   pl.BlockSpec((1, H, 1, D, T), lambda b, i: (b, 0, i, 0, 0))],
        out_shape=[jax.ShapeDtypeStruct((B, H, S, FOX_AUG), bf16),
                   jax.ShapeDtypeStruct((B, H, S, FOX_AUG), bf16),
                   jax.ShapeDtypeStruct((B, H, S // T, D, T), bf16)],
        scratch_shapes=[pltpu.VMEM((8, FOX_GATE_LANES), jnp.float32)],
        compiler_params=pltpu.CompilerParams(dimension_semantics=("arbitrary", "arbitrary"),
                                             vmem_limit_bytes=VMEM_LIMIT_BYTES),
        name="fox_prep",
    )(p_f, *args)


def _fox_attn_kernel(qa_ref, ka_ref, vt_ref, og_ref, o_ref, m_ref, l_ref, acc_ref):
    f32, bf16 = jnp.float32, jnp.bfloat16
    T, D = FOX_TILE, FOX_HEAD
    i = pl.program_id(2)
    heads = range(FOX_HEADS_PER_STEP)
    n_blocks = ka_ref.shape[2] // T
    GK = FOX_KEY_GROUP
    rel = (lax.broadcasted_iota(jnp.int32, (GK * T, T), 0)
           - lax.broadcasted_iota(jnp.int32, (GK * T, T), 1))
    qas = [qa_ref[0, j] for j in heads]

    def key_group(g, first):
        visible = rel <= (i - g * GK) * T

        def scores(j):
            ka = ka_ref[0, j, g * GK * T:(g + 1) * GK * T, :]
            return lax.dot_general(ka, qas[j], _NT, preferred_element_type=f32)

        nxt = scores(0)
        for j in heads:
            s = nxt
            if j + 1 < len(heads):
                nxt = scores(j + 1)
            s = jnp.where(visible, s, -jnp.inf)
            m_g = jnp.max(s, axis=0, keepdims=True)
            m_new = m_g if first else jnp.maximum(m_ref[j], m_g)
            pr = jnp.exp(s - m_new)
            l_g = jnp.sum(pr, axis=0, keepdims=True)
            vt = jnp.concatenate([vt_ref[0, j, g * GK + b] for b in range(GK)], axis=1)
            pv = jnp.dot(vt, pr.astype(bf16), preferred_element_type=f32)
            if first:
                l_ref[j], acc_ref[j] = l_g, pv
            else:
                alpha = jnp.exp(m_ref[j] - m_new)
                l_ref[j] = l_ref[j] * alpha + l_g
                acc_ref[j] = acc_ref[j] * alpha + pv
            m_ref[j] = m_new

    key_group(0, True)
    for g in range(1, n_blocks // GK):
        pl.when(i >= g * GK)(functools.partial(key_group, g, False))
    outs = [(acc_ref[j] / l_ref[j]).T for j in heads]
    o_ref[0] = jax.nn.sigmoid(og_ref[0]) * jnp.concatenate(outs, axis=1)


def _fox_attn(qa, ka, vt, p_f):
    B, H, S, A = qa.shape
    T, D = FOX_TILE, FOX_HEAD
    hp = FOX_HEADS_PER_STEP
    width = hp * D
    og_block0 = 3 * FOX_WIDTH // width
    return pl.pallas_call(
        _fox_attn_kernel,
        grid=(B, H // hp, S // T),
        in_specs=[pl.BlockSpec((1, hp, T, A), lambda b, g, i: (b, g, i, 0)),
                  pl.BlockSpec((1, hp, S, A), lambda b, g, i: (b, g, 0, 0)),
                  pl.BlockSpec((1, hp, S // T, D, T), lambda b, g, i: (b, g, 0, 0, 0)),
                  pl.BlockSpec((1, T, width), lambda b, g, i: (b, i, og_block0 + g))],
        out_specs=pl.BlockSpec((1, T, width), lambda b, g, i: (b, i, g)),
        out_shape=jax.ShapeDtypeStruct((B, S, FOX_WIDTH), jnp.float32),
        scratch_shapes=[pltpu.VMEM((hp, 1, T), jnp.float32), pltpu.VMEM((hp, 1, T), jnp.float32),
                        pltpu.VMEM((hp, D, T), jnp.float32)],
        name="fox_attn",
    )(qa, ka, vt, p_f)


def kernel(x, l0_w_in, l0_rwkv_mu, l0_rwkv_w0, l0_rwkv_w_up, l0_rwkv_a0, l0_rwkv_a_up,
           l0_rwkv_g_up, l0_rwkv_k_k, l0_rwkv_k_a, l0_rwkv_r_k, l0_rwkv_ln_g, l0_rwkv_ln_b,
           l0_mlstm_conv_w, l0_mlstm_conv_b, l0_mlstm_ig_b, l0_mlstm_fg_b,
           l0_mlstm_hn_g, l0_mlstm_hn_b, l0_w_out, l0_ln1_g, l0_ln1_b,
           l0_peer_wq, l0_peer_k1, l0_peer_k2, l0_peer_u, l0_peer_v, l0_ln2_g, l0_ln2_b,
           l1_w_in, l1_fox_qn_g, l1_fox_kn_g, l1_fox_f_b, l1_w_out, l1_ln1_g, l1_ln1_b,
           l1_peer_wq, l1_peer_k1, l1_peer_k2, l1_peer_u, l1_peer_v, l1_ln2_g, l1_ln2_b):
    B, S, D = x.shape
    n = B * S
    bf16 = jnp.bfloat16
    x2 = x.reshape(n, D)

    def pad_lanes(w):
        return jnp.pad(w, ((0, 0), (0, LANES - w.shape[1])))

    m_main = 4 * MLSTM_WIDTH
    w0 = jnp.concatenate([l0_w_in[:, :RWKV_COLS + m_main],
                          pad_lanes(l0_w_in[:, RWKV_COLS + m_main:])], axis=1).astype(bf16)
    p_r, p_m = _proj(x2, w0, (RWKV_COLS, m_main + LANES))
    y_a = _rwkv_pallas(p_r.reshape(B, S, -1), l0_rwkv_mu, l0_rwkv_w0, l0_rwkv_w_up, l0_rwkv_a0,
                       l0_rwkv_a_up, l0_rwkv_g_up, l0_rwkv_k_k, l0_rwkv_k_a, l0_rwkv_r_k,
                       l0_rwkv_ln_g, l0_rwkv_ln_b)
    y_b = _mlstm_pallas(p_m.reshape(B, S, -1), l0_mlstm_conv_w, l0_mlstm_conv_b, l0_mlstm_ig_b,
                        l0_mlstm_fg_b, l0_mlstm_hn_g, l0_mlstm_hn_b)
    x2, xt = _out_proj_ln([y_a.reshape(n, -1), y_b.reshape(n, -1)], l0_w_out.astype(bf16), x2,
                          l0_ln1_g, l0_ln1_b)
    y = _peer_pallas(x2, xt, l0_peer_wq, l0_peer_k1, l0_peer_k2, l0_peer_u, l0_peer_v)
    x2 = _resid_ln(x2, y, l0_ln2_g, l0_ln2_b)

    f_main = 4 * FOX_WIDTH
    w1 = jnp.concatenate([l1_w_in[:, :f_main], pad_lanes(l1_w_in[:, f_main:])], axis=1).astype(bf16)
    (p_f,) = _proj(x2, w1, (f_main + LANES,))
    p_f = p_f.reshape(B, S, -1)
    qa, ka, vt = _fox_prep(p_f, l1_fox_qn_g, l1_fox_kn_g, l1_fox_f_b)
    o = _fox_attn(qa, ka, vt, p_f)
    x2, xt = _out_proj_ln([o.reshape(n, -1)], l1_w_out.astype(bf16), x2, l1_ln1_g, l1_ln1_b)
    y = _peer_pallas(x2, xt, l1_peer_wq, l1_peer_k1, l1_peer_k2, l1_peer_u, l1_peer_v)
    x2 = _resid_ln(x2, y, l1_ln2_g, l1_ln2_b)
    return x2.reshape(B, S, D)
```
